```python
import math
import jax, jax.numpy as jnp
from jax import lax
import numpy as np

D_MODEL = 1024
BATCH = 8
SEQ = 16384
DEPTH = 4

N_MIXERS = 2
N_SSD_LAYERS = (DEPTH + 1) // 2
N_SC_LAYERS = DEPTH // 2
RMS_EPS = 1e-5
D_FF = 2816
SSD_EXPAND = 2
SSD_D_INNER = SSD_EXPAND * D_MODEL
SSD_HEAD_DIM = 64
SSD_N_HEADS = SSD_D_INNER // SSD_HEAD_DIM
SSD_N_GROUPS = 4
SSD_HEADS_PER_GROUP = SSD_N_HEADS // SSD_N_GROUPS
SSD_D_STATE = 128
SSD_CONV_W = 4
SSD_CHUNK = 128
SSD_CONV_DIM = SSD_D_INNER + 2 * SSD_N_GROUPS * SSD_D_STATE
SSD_IN_DIM = SSD_D_INNER + SSD_CONV_DIM + SSD_N_HEADS
SSD_DT_MIN = 1e-3
SSD_DT_MAX = 1e-1
SC_CONV_W = 3

kernel_name = "hybrid_ssd_shortconv_macaron"


def rmsnorm(x, w):
    xf = x.astype(jnp.float32)
    inv = lax.rsqrt(jnp.mean(xf * xf, axis=-1, keepdims=True) + RMS_EPS)
    return (xf * inv).astype(x.dtype) * w


def swiglu(h, w_gate, w_up, w_down):
    return (jax.nn.silu(h @ w_gate) * (h @ w_up)) @ w_down


def causal_dwconv(u, w):
    k_w = w.shape[0]
    s = u.shape[1]
    upad = jnp.pad(u, ((0, 0), (k_w - 1, 0), (0, 0)))
    out = upad[:, 0:s] * w[0]
    for k in range(1, k_w):
        out = out + upad[:, k:k + s] * w[k]
    return out


def causal_decay(a_cs):
    t = a_cs.shape[-1]
    seg = a_cs[..., :, None] - a_cs[..., None, :]
    mask = jnp.tril(jnp.ones((t, t), dtype=bool))
    return jnp.exp(jnp.where(mask, seg, -jnp.inf))


def ssd_chunked(x, dt, a, bm, cm):
    b, s = x.shape[0], x.shape[1]
    nc = s // SSD_CHUNK
    G, E, P, L = SSD_N_GROUPS, SSD_HEADS_PER_GROUP, SSD_HEAD_DIM, SSD_CHUNK
    x_dt = x.astype(jnp.float32) * dt[..., None]
    xc = x_dt.reshape(b, nc, L, G, E, P)
    ac = (dt * a).reshape(b, nc, L, G, E).transpose(0, 3, 4, 1, 2)
    bc = bm.astype(jnp.float32).reshape(b, nc, L, G, SSD_D_STATE)
    cc = cm.astype(jnp.float32).reshape(b, nc, L, G, SSD_D_STATE)
    a_cs = jnp.cumsum(ac, axis=-1)
    lmat = causal_decay(a_cs)
    cb = jnp.einsum("bclgn,bcsgn->bgcls", cc, bc)
    y_diag = jnp.einsum("bgcls,bgecls,bcsgep->bclgep", cb, lmat, xc)
    decay_states = jnp.exp(a_cs[..., -1:] - a_cs)
    states = jnp.einsum("bclgn,bgecl,bclgep->bcgepn", bc, decay_states, xc)
    states = jnp.concatenate([jnp.zeros_like(states[:, :1]), states], axis=1)
    chunk_tot = jnp.pad(a_cs[..., -1], ((0, 0), (0, 0), (0, 0), (1, 0)))
    decay_chunk = causal_decay(jnp.cumsum(chunk_tot, axis=-1))
    new_states = jnp.einsum("bgezc,bcgepn->bzgepn", decay_chunk, states)
    prev_states = new_states[:, :-1]
    y_off = jnp.einsum("bclgn,bcgepn,bgecl->bclgep", cc, prev_states, jnp.exp(a_cs))
    return (y_diag + y_off).reshape(b, s, SSD_N_HEADS, P)


def ssd_mixer(h, w_in, conv_w, conv_b, dt_bias, a_log, d_skip, norm_w, w_out):
    b, s, _ = h.shape
    zxbcdt = h @ w_in
    z = zxbcdt[..., :SSD_D_INNER]
    xbc = zxbcdt[..., SSD_D_INNER:SSD_D_INNER + SSD_CONV_DIM]
    dt_raw = zxbcdt[..., SSD_D_INNER + SSD_CONV_DIM:]
    xbc = jax.nn.silu(causal_dwconv(xbc, conv_w) + conv_b)
    gn = SSD_N_GROUPS * SSD_D_STATE
    xs = xbc[..., :SSD_D_INNER].reshape(b, s, SSD_N_HEADS, SSD_HEAD_DIM)
    bm = xbc[..., SSD_D_INNER:SSD_D_INNER + gn].reshape(b, s, SSD_N_GROUPS, SSD_D_STATE)
    cm = xbc[..., SSD_D_INNER + gn:].reshape(b, s, SSD_N_GROUPS, SSD_D_STATE)
    dt = jax.nn.softplus(dt_raw.astype(jnp.float32) + dt_bias.astype(jnp.float32))
    a = -jnp.exp(a_log.astype(jnp.float32))
    y = ssd_chunked(xs, dt, a, bm, cm)
    y = y + xs.astype(jnp.float32) * d_skip.astype(jnp.float32)[:, None]
    y = y.reshape(b, s, SSD_D_INNER)
    g = y * jax.nn.silu(z.astype(jnp.float32))
    gg = g.reshape(b, s, SSD_N_GROUPS, SSD_D_INNER // SSD_N_GROUPS)
    gg = gg * lax.rsqrt(jnp.mean(gg * gg, axis=-1, keepdims=True) + RMS_EPS)
    g = gg.reshape(b, s, SSD_D_INNER).astype(h.dtype) * norm_w
    return g @ w_out


def shortconv_mixer(h, w_in, conv_w, w_out):
    bcu = h @ w_in
    bg = bcu[..., :D_MODEL]
    cg = bcu[..., D_MODEL:2 * D_MODEL]
    u = bcu[..., 2 * D_MODEL:]
    v = causal_dwconv(cg * u, conv_w)
    return (bg * v) @ w_out


def _fwd_setup_inputs(seed: int = 0) -> dict:
    key = jax.random.key(seed)
    ks = jax.random.split(key, 20)
    f32 = jnp.float32
    nrm = lambda k, shape, fan_in: jax.random.normal(k, shape, f32) * (fan_in ** -0.5)
    x = jax.random.normal(ks[0], (BATCH, SEQ, D_MODEL), f32)
    norm_w = 1.0 + 0.01 * jax.random.normal(ks[1], (DEPTH, 3, D_MODEL), f32)
    ffn_w_gate = nrm(ks[2], (DEPTH, 2, D_MODEL, D_FF), D_MODEL)
    ffn_w_up = nrm(ks[3], (DEPTH, 2, D_MODEL, D_FF), D_MODEL)
    ffn_w_down = nrm(ks[4], (DEPTH, 2, D_FF, D_MODEL), D_FF)
    ssd_w_in = nrm(ks[5], (N_SSD_LAYERS, D_MODEL, SSD_IN_DIM), D_MODEL)
    ssd_conv_w = nrm(ks[6], (N_SSD_LAYERS, SSD_CONV_W, SSD_CONV_DIM), SSD_CONV_W)
    ssd_conv_b = 0.01 * jax.random.normal(ks[7], (N_SSD_LAYERS, SSD_CONV_DIM), f32)
    dt0 = jnp.exp(jax.random.uniform(ks[8], (N_SSD_LAYERS, SSD_N_HEADS), f32)
                  * (math.log(SSD_DT_MAX) - math.log(SSD_DT_MIN)) + math.log(SSD_DT_MIN))
    ssd_dt_bias = dt0 + jnp.log(-jnp.expm1(-dt0))
    ssd_a_log = jnp.log(jax.random.uniform(ks[9], (N_SSD_LAYERS, SSD_N_HEADS), f32, 1.0, 16.0))
    ssd_d = 1.0 + 0.1 * jax.random.normal(ks[10], (N_SSD_LAYERS, SSD_N_HEADS), f32)
    ssd_norm_w = 1.0 + 0.01 * jax.random.normal(ks[11], (N_SSD_LAYERS, SSD_D_INNER), f32)
    ssd_w_out = nrm(ks[12], (N_SSD_LAYERS, SSD_D_INNER, D_MODEL), SSD_D_INNER)
    sc_w_in = nrm(ks[13], (N_SC_LAYERS, D_MODEL, 3 * D_MODEL), D_MODEL)
    sc_conv_w = nrm(ks[14], (N_SC_LAYERS, SC_CONV_W, D_MODEL), SC_CONV_W)
    sc_w_out = nrm(ks[15], (N_SC_LAYERS, D_MODEL, D_MODEL), D_MODEL)
    final_norm_w = 1.0 + 0.01 * jax.random.normal(ks[16], (D_MODEL,), f32)
    return {"x": x, "norm_w": norm_w, "ffn_w_gate": ffn_w_gate, "ffn_w_up": ffn_w_up,
            "ffn_w_down": ffn_w_down, "ssd_w_in": ssd_w_in, "ssd_conv_w": ssd_conv_w,
            "ssd_conv_b": ssd_conv_b, "ssd_dt_bias": ssd_dt_bias, "ssd_a_log": ssd_a_log,
            "ssd_d": ssd_d, "ssd_norm_w": ssd_norm_w, "ssd_w_out": ssd_w_out,
            "sc_w_in": sc_w_in, "sc_conv_w": sc_conv_w, "sc_w_out": sc_w_out,
            "final_norm_w": final_norm_w}


def _fwd_reference(x, norm_w, ffn_w_gate, ffn_w_up, ffn_w_down, ssd_w_in, ssd_conv_w, ssd_conv_b,
              ssd_dt_bias, ssd_a_log, ssd_d, ssd_norm_w, ssd_w_out, sc_w_in, sc_conv_w,
              sc_w_out, final_norm_w):
    for i in range(DEPTH):
        x = x + 0.5 * swiglu(rmsnorm(x, norm_w[i, 0]), ffn_w_gate[i, 0], ffn_w_up[i, 0], ffn_w_down[i, 0])
        h = rmsnorm(x, norm_w[i, 1])
        j = i // N_MIXERS
        if i % N_MIXERS == 0:
            mix = ssd_mixer(h, ssd_w_in[j], ssd_conv_w[j], ssd_conv_b[j], ssd_dt_bias[j],
                            ssd_a_log[j], ssd_d[j], ssd_norm_w[j], ssd_w_out[j])
        else:
            mix = shortconv_mixer(h, sc_w_in[j], sc_conv_w[j], sc_w_out[j])
        x = x + mix
        x = x + 0.5 * swiglu(rmsnorm(x, norm_w[i, 2]), ffn_w_gate[i, 1], ffn_w_up[i, 1], ffn_w_down[i, 1])
    return rmsnorm(x, final_norm_w)


import jax as _jax
import jax.numpy as _jnp

TWIN_FORMAT = 'train_step'
FWD_PARAMS = ['x', 'norm_w', 'ffn_w_gate', 'ffn_w_up', 'ffn_w_down', 'ssd_w_in', 'ssd_conv_w', 'ssd_conv_b', 'ssd_dt_bias', 'ssd_a_log', 'ssd_d', 'ssd_norm_w', 'ssd_w_out', 'sc_w_in', 'sc_conv_w', 'sc_w_out', 'final_norm_w']
TWIN_WEIGHTS = ['norm_w', 'ffn_w_gate', 'ffn_w_up', 'ffn_w_down', 'ssd_w_in', 'ssd_conv_w', 'ssd_conv_b', 'ssd_dt_bias', 'ssd_a_log', 'ssd_d', 'ssd_norm_w', 'ssd_w_out', 'sc_w_in', 'sc_conv_w', 'sc_w_out', 'final_norm_w']
TWIN_DIFF_INPUT = 'x'
TWIN_INPUTS = ['x', 'norm_w', 'ffn_w_gate', 'ffn_w_up', 'ffn_w_down', 'ssd_w_in', 'ssd_conv_w', 'ssd_conv_b', 'ssd_dt_bias', 'ssd_a_log', 'ssd_d', 'ssd_norm_w', 'ssd_w_out', 'sc_w_in', 'sc_conv_w', 'sc_w_out', 'final_norm_w', 'loss_target', 'm_norm_w', 'm_ffn_w_gate', 'm_ffn_w_up', 'm_ffn_w_down', 'm_ssd_w_in', 'm_ssd_conv_w', 'm_ssd_conv_b', 'm_ssd_dt_bias', 'm_ssd_a_log', 'm_ssd_d', 'm_ssd_norm_w', 'm_ssd_w_out', 'm_sc_w_in', 'm_sc_conv_w', 'm_sc_w_out', 'm_final_norm_w', 'v_norm_w', 'v_ffn_w_gate', 'v_ffn_w_up', 'v_ffn_w_down', 'v_ssd_w_in', 'v_ssd_conv_w', 'v_ssd_conv_b', 'v_ssd_dt_bias', 'v_ssd_a_log', 'v_ssd_d', 'v_ssd_norm_w', 'v_ssd_w_out', 'v_sc_w_in', 'v_sc_conv_w', 'v_sc_w_out', 'v_final_norm_w']
TWIN_OUTPUTS = ['loss', 'grad_x', 'grad_norm_w', 'grad_ffn_w_gate', 'grad_ffn_w_up', 'grad_ffn_w_down', 'grad_ssd_w_in', 'grad_ssd_conv_w', 'grad_ssd_conv_b', 'grad_ssd_dt_bias', 'grad_ssd_a_log', 'grad_ssd_d', 'grad_ssd_norm_w', 'grad_ssd_w_out', 'grad_sc_w_in', 'grad_sc_conv_w', 'grad_sc_w_out', 'grad_final_norm_w', 'delta_norm_w', 'delta_ffn_w_gate', 'delta_ffn_w_up', 'delta_ffn_w_down', 'delta_ssd_w_in', 'delta_ssd_conv_w', 'delta_ssd_conv_b', 'delta_ssd_dt_bias', 'delta_ssd_a_log', 'delta_ssd_d', 'delta_ssd_norm_w', 'delta_ssd_w_out', 'delta_sc_w_in', 'delta_sc_conv_w', 'delta_sc_w_out', 'delta_final_norm_w', 'new_m_norm_w', 'new_m_ffn_w_gate', 'new_m_ffn_w_up', 'new_m_ffn_w_down', 'new_m_ssd_w_in', 'new_m_ssd_conv_w', 'new_m_ssd_conv_b', 'new_m_ssd_dt_bias', 'new_m_ssd_a_log', 'new_m_ssd_d', 'new_m_ssd_norm_w', 'new_m_ssd_w_out', 'new_m_sc_w_in', 'new_m_sc_conv_w', 'new_m_sc_w_out', 'new_m_final_norm_w', 'new_v_norm_w', 'new_v_ffn_w_gate', 'new_v_ffn_w_up', 'new_v_ffn_w_down', 'new_v_ssd_w_in', 'new_v_ssd_conv_w', 'new_v_ssd_conv_b', 'new_v_ssd_dt_bias', 'new_v_ssd_a_log', 'new_v_ssd_d', 'new_v_ssd_norm_w', 'new_v_ssd_w_out', 'new_v_sc_w_in', 'new_v_sc_conv_w', 'new_v_sc_w_out', 'new_v_final_norm_w']
TWIN_LEAF_KINDS = {'loss': 'loss', 'grad_x': 'grad_x', 'grad_norm_w': 'grad_w', 'grad_ffn_w_gate': 'grad_w', 'grad_ffn_w_up': 'grad_w', 'grad_ffn_w_down': 'grad_w', 'grad_ssd_w_in': 'grad_w', 'grad_ssd_conv_w': 'grad_w', 'grad_ssd_conv_b': 'grad_w', 'grad_ssd_dt_bias': 'grad_w', 'grad_ssd_a_log': 'grad_w', 'grad_ssd_d': 'grad_w', 'grad_ssd_norm_w': 'grad_w', 'grad_ssd_w_out': 'grad_w', 'grad_sc_w_in': 'grad_w', 'grad_sc_conv_w': 'grad_w', 'grad_sc_w_out': 'grad_w', 'grad_final_norm_w': 'grad_w', 'delta_norm_w': 'delta_w', 'delta_ffn_w_gate': 'delta_w', 'delta_ffn_w_up': 'delta_w', 'delta_ffn_w_down': 'delta_w', 'delta_ssd_w_in': 'delta_w', 'delta_ssd_conv_w': 'delta_w', 'delta_ssd_conv_b': 'delta_w', 'delta_ssd_dt_bias': 'delta_w', 'delta_ssd_a_log': 'delta_w', 'delta_ssd_d': 'delta_w', 'delta_ssd_norm_w': 'delta_w', 'delta_ssd_w_out': 'delta_w', 'delta_sc_w_in': 'delta_w', 'delta_sc_conv_w': 'delta_w', 'delta_sc_w_out': 'delta_w', 'delta_final_norm_w': 'delta_w', 'new_m_norm_w': 'new_m', 'new_m_ffn_w_gate': 'new_m', 'new_m_ffn_w_up': 'new_m', 'new_m_ffn_w_down': 'new_m', 'new_m_ssd_w_in': 'new_m', 'new_m_ssd_conv_w': 'new_m', 'new_m_ssd_conv_b': 'new_m', 'new_m_ssd_dt_bias': 'new_m', 'new_m_ssd_a_log': 'new_m', 'new_m_ssd_d': 'new_m', 'new_m_ssd_norm_w': 'new_m', 'new_m_ssd_w_out': 'new_m', 'new_m_sc_w_in': 'new_m', 'new_m_sc_conv_w': 'new_m', 'new_m_sc_w_out': 'new_m', 'new_m_final_norm_w': 'new_m', 'new_v_norm_w': 'new_v', 'new_v_ffn_w_gate': 'new_v', 'new_v_ffn_w_up': 'new_v', 'new_v_ffn_w_down': 'new_v', 'new_v_ssd_w_in': 'new_v', 'new_v_ssd_conv_w': 'new_v', 'new_v_ssd_conv_b': 'new_v', 'new_v_ssd_dt_bias': 'new_v', 'new_v_ssd_a_log': 'new_v', 'new_v_ssd_d': 'new_v', 'new_v_ssd_norm_w': 'new_v', 'new_v_ssd_w_out': 'new_v', 'new_v_sc_w_in': 'new_v', 'new_v_sc_conv_w': 'new_v', 'new_v_sc_w_out': 'new_v', 'new_v_final_norm_w': 'new_v'}


def _forward(args):
    return _fwd_reference(*[args[k] for k in FWD_PARAMS])


def _output_shape():
    def fwd():
        inp = _fwd_setup_inputs(0)
        return _fwd_reference(*[inp[k] for k in FWD_PARAMS])
    out = _jax.eval_shape(fwd)
    return out.shape, out.dtype

N_MICROBATCH = 1
ADAM_LR = 0.001
ADAM_B1 = 0.9
ADAM_B2 = 0.999
ADAM_EPS = 1e-08
ADAM_WD = 0.01
ADAM_STEP = 10
PER_EXAMPLE_BATCH_AXIS = {'x': 0, 'loss_target': 0}
SHARED_INPUTS = []
_WEIGHT_DTYPES = {'norm_w': _jnp.float32, 'ffn_w_gate': _jnp.float32, 'ffn_w_up': _jnp.float32, 'ffn_w_down': _jnp.float32, 'ssd_w_in': _jnp.float32, 'ssd_conv_w': _jnp.float32, 'ssd_conv_b': _jnp.float32, 'ssd_dt_bias': _jnp.float32, 'ssd_a_log': _jnp.float32, 'ssd_d': _jnp.float32, 'ssd_norm_w': _jnp.float32, 'ssd_w_out': _jnp.float32, 'sc_w_in': _jnp.float32, 'sc_conv_w': _jnp.float32, 'sc_w_out': _jnp.float32, 'final_norm_w': _jnp.float32}
MOMENT_SCALE = {'norm_w': 3.007456e-01, 'ffn_w_gate': 7.232845e-02, 'ffn_w_up': 7.000625e-02, 'ffn_w_down': 1.162599e-01, 'ssd_w_in': 2.323106e-01, 'ssd_conv_w': 2.128758e-01, 'ssd_conv_b': 3.391987e-01, 'ssd_dt_bias': 5.417418e-01, 'ssd_a_log': 2.887793e+00, 'ssd_d': 1.436105e+00, 'ssd_norm_w': 2.781326e-01, 'ssd_w_out': 3.524815e-01, 'sc_w_in': 2.210666e-01, 'sc_conv_w': 2.262032e-01, 'sc_w_out': 2.199693e-01, 'final_norm_w': 1.281173e+02}


def _to_microbatches(a, axis):
    t = _jnp.moveaxis(a, axis, 0)
    t = t.reshape((N_MICROBATCH, t.shape[0] // N_MICROBATCH) + t.shape[1:])
    return _jnp.moveaxis(t, 1, axis + 1)


def setup_inputs(seed: int = 0) -> dict:
    inp = _fwd_setup_inputs(seed)
    key = _jax.random.fold_in(_jax.random.key(seed), 7919)
    shape, _ = _output_shape()
    out = dict(inp)
    out["loss_target"] = _jax.random.normal(_jax.random.fold_in(key, 0), shape, _jnp.float32)
    for i, name in enumerate(TWIN_WEIGHTS):
        w = inp[name].astype(_jnp.float32)
        if MOMENT_SCALE is None:
            s = _jnp.sqrt(_jnp.mean(_jnp.square(w)) + 1e-30)
        else:
            s = MOMENT_SCALE[name]
        km, kv = _jax.random.split(_jax.random.fold_in(key, i + 1))
        out[name] = w
        out["m_" + name] = s * _jax.random.normal(km, w.shape, _jnp.float32)
        out["v_" + name] = (s * s) * _jax.random.uniform(kv, w.shape, _jnp.float32, 0.5, 1.5)
    if N_MICROBATCH > 1:
        for name, axis in PER_EXAMPLE_BATCH_AXIS.items():
            out[name] = _to_microbatches(out[name], axis)
    return {'x': out['x'], 'norm_w': out['norm_w'], 'ffn_w_gate': out['ffn_w_gate'], 'ffn_w_up': out['ffn_w_up'], 'ffn_w_down': out['ffn_w_down'], 'ssd_w_in': out['ssd_w_in'], 'ssd_conv_w': out['ssd_conv_w'], 'ssd_conv_b': out['ssd_conv_b'], 'ssd_dt_bias': out['ssd_dt_bias'], 'ssd_a_log': out['ssd_a_log'], 'ssd_d': out['ssd_d'], 'ssd_norm_w': out['ssd_norm_w'], 'ssd_w_out': out['ssd_w_out'], 'sc_w_in': out['sc_w_in'], 'sc_conv_w': out['sc_conv_w'], 'sc_w_out': out['sc_w_out'], 'final_norm_w': out['final_norm_w'], 'loss_target': out['loss_target'], 'm_norm_w': out['m_norm_w'], 'm_ffn_w_gate': out['m_ffn_w_gate'], 'm_ffn_w_up': out['m_ffn_w_up'], 'm_ffn_w_down': out['m_ffn_w_down'], 'm_ssd_w_in': out['m_ssd_w_in'], 'm_ssd_conv_w': out['m_ssd_conv_w'], 'm_ssd_conv_b': out['m_ssd_conv_b'], 'm_ssd_dt_bias': out['m_ssd_dt_bias'], 'm_ssd_a_log': out['m_ssd_a_log'], 'm_ssd_d': out['m_ssd_d'], 'm_ssd_norm_w': out['m_ssd_norm_w'], 'm_ssd_w_out': out['m_ssd_w_out'], 'm_sc_w_in': out['m_sc_w_in'], 'm_sc_conv_w': out['m_sc_conv_w'], 'm_sc_w_out': out['m_sc_w_out'], 'm_final_norm_w': out['m_final_norm_w'], 'v_norm_w': out['v_norm_w'], 'v_ffn_w_gate': out['v_ffn_w_gate'], 'v_ffn_w_up': out['v_ffn_w_up'], 'v_ffn_w_down': out['v_ffn_w_down'], 'v_ssd_w_in': out['v_ssd_w_in'], 'v_ssd_conv_w': out['v_ssd_conv_w'], 'v_ssd_conv_b': out['v_ssd_conv_b'], 'v_ssd_dt_bias': out['v_ssd_dt_bias'], 'v_ssd_a_log': out['v_ssd_a_log'], 'v_ssd_d': out['v_ssd_d'], 'v_ssd_norm_w': out['v_ssd_norm_w'], 'v_ssd_w_out': out['v_ssd_w_out'], 'v_sc_w_in': out['v_sc_w_in'], 'v_sc_conv_w': out['v_sc_conv_w'], 'v_sc_w_out': out['v_sc_w_out'], 'v_final_norm_w': out['v_final_norm_w']}


def _loss(weights, diff, rest, loss_target):
    with _jax.named_scope("forward"):
        args = {**rest, TWIN_DIFF_INPUT: diff, **{k: w.astype(_WEIGHT_DTYPES[k]) for k, w in weights.items()}}
        y = _forward(args)
    with _jax.named_scope("loss_head"):
        err = _jnp.square(y.astype(_jnp.float32) - loss_target)
        return 0.5 * _jnp.sum(_jnp.mean(err, axis=-1)) if err.ndim else 0.5 * err


def _adamw(w, g, m, v):
    m = ADAM_B1 * m + (1.0 - ADAM_B1) * g
    v = ADAM_B2 * v + (1.0 - ADAM_B2) * _jnp.square(g)
    m_hat = m / (1.0 - ADAM_B1 ** ADAM_STEP)
    v_hat = v / (1.0 - ADAM_B2 ** ADAM_STEP)
    delta = -ADAM_LR * (m_hat / (_jnp.sqrt(v_hat) + ADAM_EPS) + ADAM_WD * w)
    return delta, m, v


def reference(x, norm_w, ffn_w_gate, ffn_w_up, ffn_w_down, ssd_w_in, ssd_conv_w, ssd_conv_b, ssd_dt_bias, ssd_a_log, ssd_d, ssd_norm_w, ssd_w_out, sc_w_in, sc_conv_w, sc_w_out, final_norm_w, loss_target, m_norm_w, m_ffn_w_gate, m_ffn_w_up, m_ffn_w_down, m_ssd_w_in, m_ssd_conv_w, m_ssd_conv_b, m_ssd_dt_bias, m_ssd_a_log, m_ssd_d, m_ssd_norm_w, m_ssd_w_out, m_sc_w_in, m_sc_conv_w, m_sc_w_out, m_final_norm_w, v_norm_w, v_ffn_w_gate, v_ffn_w_up, v_ffn_w_down, v_ssd_w_in, v_ssd_conv_w, v_ssd_conv_b, v_ssd_dt_bias, v_ssd_a_log, v_ssd_d, v_ssd_norm_w, v_ssd_w_out, v_sc_w_in, v_sc_conv_w, v_sc_w_out, v_final_norm_w):
    given = dict(x=x, norm_w=norm_w, ffn_w_gate=ffn_w_gate, ffn_w_up=ffn_w_up, ffn_w_down=ffn_w_down, ssd_w_in=ssd_w_in, ssd_conv_w=ssd_conv_w, ssd_conv_b=ssd_conv_b, ssd_dt_bias=ssd_dt_bias, ssd_a_log=ssd_a_log, ssd_d=ssd_d, ssd_norm_w=ssd_norm_w, ssd_w_out=ssd_w_out, sc_w_in=sc_w_in, sc_conv_w=sc_conv_w, sc_w_out=sc_w_out, final_norm_w=final_norm_w, loss_target=loss_target, m_norm_w=m_norm_w, m_ffn_w_gate=m_ffn_w_gate, m_ffn_w_up=m_ffn_w_up, m_ffn_w_down=m_ffn_w_down, m_ssd_w_in=m_ssd_w_in, m_ssd_conv_w=m_ssd_conv_w, m_ssd_conv_b=m_ssd_conv_b, m_ssd_dt_bias=m_ssd_dt_bias, m_ssd_a_log=m_ssd_a_log, m_ssd_d=m_ssd_d, m_ssd_norm_w=m_ssd_norm_w, m_ssd_w_out=m_ssd_w_out, m_sc_w_in=m_sc_w_in, m_sc_conv_w=m_sc_conv_w, m_sc_w_out=m_sc_w_out, m_final_norm_w=m_final_norm_w, v_norm_w=v_norm_w, v_ffn_w_gate=v_ffn_w_gate, v_ffn_w_up=v_ffn_w_up, v_ffn_w_down=v_ffn_w_down, v_ssd_w_in=v_ssd_w_in, v_ssd_conv_w=v_ssd_conv_w, v_ssd_conv_b=v_ssd_conv_b, v_ssd_dt_bias=v_ssd_dt_bias, v_ssd_a_log=v_ssd_a_log, v_ssd_d=v_ssd_d, v_ssd_norm_w=v_ssd_norm_w, v_ssd_w_out=v_ssd_w_out, v_sc_w_in=v_sc_w_in, v_sc_conv_w=v_sc_conv_w, v_sc_w_out=v_sc_w_out, v_final_norm_w=v_final_norm_w)
    weights = {n: given[n] for n in TWIN_WEIGHTS}
    shared = {n: given[n] for n in SHARED_INPUTS}
    per_example = {n: given[n] for n in ['x']}
    grad_fn = _jax.value_and_grad(_loss, argnums=(0, 1))

    def one_microbatch(ex, loss_target):
        ex = dict(ex)
        diff = ex.pop(TWIN_DIFF_INPUT)
        return grad_fn(weights, diff, {**shared, **ex}, loss_target)

    if N_MICROBATCH == 1:
        loss, (grad_w, grad_x) = one_microbatch(per_example, given["loss_target"])
    else:
        def body(carry, xs):
            loss_sum, grad_sum = carry
            l_k, (gw_k, gx_k) = one_microbatch(xs[0], xs[1])
            with _jax.named_scope("update"):
                return (loss_sum + l_k, _jax.tree.map(_jnp.add, grad_sum, gw_k)), gx_k

        init = (_jnp.zeros((), _jnp.float32), _jax.tree.map(_jnp.zeros_like, weights))
        (loss, grad_w), grad_x = _jax.lax.scan(body, init, (per_example, given["loss_target"]))
    with _jax.named_scope("update"):
        delta_w, new_m, new_v = {}, {}, {}
        for n in TWIN_WEIGHTS:
            delta_w[n], new_m[n], new_v[n] = _adamw(weights[n], grad_w[n], given["m_" + n], given["v_" + n])
    return (loss, grad_x, *[grad_w[n] for n in TWIN_WEIGHTS], *[delta_w[n] for n in TWIN_WEIGHTS],
            *[new_m[n] for n in TWIN_WEIGHTS], *[new_v[n] for n in TWIN_WEIGHTS])
```

```python
import functools

import jax
import jax.numpy as jnp
from jax import lax
from jax.experimental import pallas as pl
from jax.experimental.pallas import tpu as pltpu

F32 = jnp.float32
BF16 = jnp.bfloat16
MESH = pl.DeviceIdType.MESH

RMS_EPS = 1e-5
D_MODEL = 1024
D_FF = 2816
N_LAYERS = 4
SSD_D_INNER = 2048
SSD_N_HEADS = 32
SSD_N_GROUPS = 4
SSD_D_STATE = 128
SSD_CHUNK = 128
SSD_CONV_W = 4
SSD_CONV_DIM = 3072
SSD_IN_DIM = 5152
SC_CONV_W = 3
LANES = 128
SUBLANES = 8
N_XS_BLK = SSD_D_INNER // LANES
SSD_IN_PAD = SSD_D_INNER + SSD_CONV_DIM + LANES
VMEM_LIMIT = 56 * 2**20
TOKEN_TILE = 512
FF_CHUNK = 256
N_SHARDS = 4

ADAM_LR = 0.001
ADAM_B1 = 0.9
ADAM_B2 = 0.999
ADAM_EPS = 1e-08
ADAM_WD = 0.01
ADAM_STEP = 10


def _params(*sem):
    return pltpu.CompilerParams(dimension_semantics=sem if sem else None, vmem_limit_bytes=VMEM_LIMIT)


def _dot(a, b):
    return jnp.dot(a, b, preferred_element_type=F32)


def _dot_nt(a, b):
    return lax.dot_general(a, b, (((1,), (1,)), ((), ())), preferred_element_type=F32)


def _dot_tn(a, b):
    return lax.dot_general(a, b, (((0,), (0,)), ((), ())), preferred_element_type=F32)


def _resident(shape):
    n = len(shape)
    return pl.BlockSpec(shape, lambda *_: (0,) * n, pipeline_mode=pl.Buffered(1))


def _split3(v):
    hi = v.astype(BF16)
    r1 = v - hi.astype(F32)
    mid = r1.astype(BF16)
    lo = (r1 - mid.astype(F32)).astype(BF16)
    return hi, mid, lo


def _sel_right(v3, sel):
    return _dot(v3[0], sel) + _dot(v3[1], sel) + _dot(v3[2], sel)


def _sel_left(sel, v3):
    return _dot(sel, v3[0]) + _dot(sel, v3[1]) + _dot(sel, v3[2])


def _sel_right_t(v3, sel):
    return _dot_nt(v3[0], sel) + _dot_nt(v3[1], sel) + _dot_nt(v3[2], sel)


def _sigmoid(v):
    return 1.0 / (1.0 + jnp.exp(-v))


def _rms_fwd(x, w):
    inv = lax.rsqrt(jnp.mean(x * x, axis=-1, keepdims=True) + RMS_EPS)
    xhat = x * inv
    return xhat * w, xhat, inv


def _rms_bwd(dh, xhat, inv, w):
    dxhat = dh * w
    dx = inv * (dxhat - xhat * jnp.mean(dxhat * xhat, axis=-1, keepdims=True))
    return dx, jnp.sum(dh * xhat, axis=0, keepdims=True)


def _ffn_fwd(x, nw, wg, wu, wd):
    t, d = x.shape
    f = wg.shape[1]
    tm = min(TOKEN_TILE, t)

    def body(x_ref, nw_ref, wg_ref, wu_ref, wd_ref, o_ref):
        xv = x_ref[...]
        h = _rms_fwd(xv, nw_ref[...])[0].astype(BF16)
        acc = jnp.zeros((tm, d), F32)
        for j in range(f // FF_CHUNK):
            sl = slice(j * FF_CHUNK, (j + 1) * FF_CHUNK)
            g = _dot(h, wg_ref[:, sl])
            u = _dot(h, wu_ref[:, sl])
            a = (g * _sigmoid(g) * u).astype(BF16)
            acc = acc + _dot(a, wd_ref[sl, :])
        o_ref[...] = xv + 0.5 * acc

    return pl.pallas_call(
        body, name="ffn_fwd", grid=(t // tm,),
        in_specs=[pl.BlockSpec((tm, d), lambda i: (i, 0)), _resident((1, d)), _resident((d, f)),
                  _resident((d, f)), _resident((f, d))],
        out_specs=pl.BlockSpec((tm, d), lambda i: (i, 0)),
        out_shape=jax.ShapeDtypeStruct((t, d), F32),
        compiler_params=_params("parallel"),
    )(x, nw, wg, wu, wd)


def _ffn_bwd(x, dy, nw, wg, wu, wd):
    t, d = x.shape
    f = wg.shape[1]
    tm = min(TOKEN_TILE, t)

    def body(x_ref, dy_ref, nw_ref, wg_ref, wu_ref, wd_ref, dx_ref, dnw_ref, h_ref, a_ref, dg_ref, du_ref):
        xv = x_ref[...]
        dyv = dy_ref[...]
        nwv = nw_ref[...]
        hf, xhat, inv = _rms_fwd(xv, nwv)
        h = hf.astype(BF16)
        h_ref[...] = h
        dob = (0.5 * dyv).astype(BF16)
        dh = jnp.zeros((tm, d), F32)
        for j in range(f // FF_CHUNK):
            sl = slice(j * FF_CHUNK, (j + 1) * FF_CHUNK)
            g = _dot(h, wg_ref[:, sl])
            u = _dot(h, wu_ref[:, sl])
            sig = _sigmoid(g)
            s = g * sig
            a_ref[:, sl] = (s * u).astype(BF16)
            da = _dot_nt(dob, wd_ref[sl, :])
            dgb = (da * u * (sig * (1.0 + g * (1.0 - sig)))).astype(BF16)
            dub = (da * s).astype(BF16)
            dg_ref[:, sl] = dgb
            du_ref[:, sl] = dub
            dh = dh + _dot_nt(dgb, wg_ref[:, sl]) + _dot_nt(dub, wu_ref[:, sl])
        dx, dw = _rms_bwd(dh, xhat, inv, nwv)
        dx_ref[...] = dyv + dx

        @pl.when(pl.program_id(0) == 0)
        def _():
            dnw_ref[...] = jnp.zeros_like(dnw_ref)

        dnw_ref[...] += dw

    tok = lambda n: pl.BlockSpec((tm, n), lambda i: (i, 0))
    return pl.pallas_call(
        body, name="ffn_bwd", grid=(t // tm,),
        in_specs=[tok(d), tok(d), _resident((1, d)), _resident((d, f)), _resident((d, f)), _resident((f, d))],
        out_specs=[tok(d), pl.BlockSpec((1, d), lambda i: (0, 0)), tok(d), tok(f), tok(f), tok(f)],
        out_shape=[jax.ShapeDtypeStruct((t, d), F32), jax.ShapeDtypeStruct((1, d), F32),
                   jax.ShapeDtypeStruct((t, d), BF16), jax.ShapeDtypeStruct((t, f), BF16),
                   jax.ShapeDtypeStruct((t, f), BF16), jax.ShapeDtypeStruct((t, f), BF16)],
        compiler_params=_params("arbitrary"),
    )(x, dy, nw, wg, wu, wd)


def _pick_bn(m, n, unit):
    best = unit
    for k in range(1, n // unit + 1):
        bn = k * unit
        if n % bn == 0 and m * bn * 4 <= 8 * 2**20:
            best = bn
    return best


def _matmul_tn(a, b, scale=1.0, name="wgrad"):
    t, m = a.shape
    n = b.shape[1]
    bt = min(1024, t)
    bn = _pick_bn(m, n, LANES)
    nt = t // bt

    def body(a_ref, b_ref, o_ref):
        @pl.when(pl.program_id(1) == 0)
        def _():
            o_ref[...] = jnp.zeros_like(o_ref)

        o_ref[...] += _dot_tn(a_ref[...].astype(BF16), b_ref[...].astype(BF16))
        if scale != 1.0:
            @pl.when(pl.program_id(1) == nt - 1)
            def _():
                o_ref[...] *= scale

    return pl.pallas_call(
        body, name=name, grid=(n // bn, nt),
        in_specs=[pl.BlockSpec((bt, m), lambda j, k: (k, 0)), pl.BlockSpec((bt, bn), lambda j, k: (k, j))],
        out_specs=pl.BlockSpec((m, bn), lambda j, k: (0, j)),
        out_shape=jax.ShapeDtypeStruct((m, n), F32),
        compiler_params=_params("parallel", "arbitrary"),
    )(a, b)


def _matmul_tn_blocked(a, b, name="wgrad_blk"):
    t, m = a.shape
    nb = b.shape[0]
    bt = min(1024, t)
    nbt = _pick_bn(m, nb * LANES, LANES) // LANES
    while nb % nbt:
        nbt -= 1

    def body(a_ref, b_ref, o_ref):
        @pl.when(pl.program_id(1) == 0)
        def _():
            o_ref[...] = jnp.zeros_like(o_ref)

        bv = jnp.concatenate([b_ref[i] for i in range(nbt)], axis=1) if nbt > 1 else b_ref[0]
        o_ref[...] += _dot_tn(a_ref[...], bv)

    return pl.pallas_call(
        body, name=name, grid=(nb // nbt, t // bt),
        in_specs=[pl.BlockSpec((bt, m), lambda j, k: (k, 0)), pl.BlockSpec((nbt, bt, LANES), lambda j, k: (j, k, 0))],
        out_specs=pl.BlockSpec((m, nbt * LANES), lambda j, k: (0, j)),
        out_shape=jax.ShapeDtypeStruct((m, nb * LANES), F32),
        compiler_params=_params("parallel", "arbitrary"),
    )(a, b)


def _norm_mm(x, nw, w):
    t, d = x.shape
    n = w.shape[1]
    tm = min(TOKEN_TILE, t)
    cn = 1024 if n % 1024 == 0 else n

    def body(x_ref, nw_ref, w_ref, o_ref):
        h = _rms_fwd(x_ref[...], nw_ref[...])[0].astype(BF16)
        for j in range(n // cn):
            sl = slice(j * cn, (j + 1) * cn)
            o_ref[:, sl] = _dot(h, w_ref[:, sl])

    return pl.pallas_call(
        body, name="norm_mm", grid=(t // tm,),
        in_specs=[pl.BlockSpec((tm, d), lambda i: (i, 0)), _resident((1, d)), _resident((d, n))],
        out_specs=pl.BlockSpec((tm, n), lambda i: (i, 0)),
        out_shape=jax.ShapeDtypeStruct((t, n), F32),
        compiler_params=_params("parallel"),
    )(x, nw, w)


def _ssd_inproj(x, nw, w):
    t, d = x.shape
    tm = min(TOKEN_TILE, t)
    nz, nx = SSD_D_INNER // LANES, SSD_CONV_DIM // LANES
    cn = 1024

    def body(x_ref, nw_ref, w_ref, z_ref, xr_ref, dt_ref):
        h = _rms_fwd(x_ref[...], nw_ref[...])[0].astype(BF16)
        for j in range((SSD_D_INNER + SSD_CONV_DIM) // cn):
            r = _dot(h, w_ref[:, j * cn:(j + 1) * cn])
            for i in range(cn // LANES):
                blk = j * (cn // LANES) + i
                v = r[:, i * LANES:(i + 1) * LANES]
                if blk < nz:
                    z_ref[blk] = v
                else:
                    xr_ref[blk - nz] = v
        dt_ref[...] = _dot(h, w_ref[:, SSD_D_INNER + SSD_CONV_DIM:])

    return pl.pallas_call(
        body, name="ssd_inproj", grid=(t // tm,),
        in_specs=[pl.BlockSpec((tm, d), lambda i: (i, 0)), _resident((1, d)), _resident((d, SSD_IN_PAD))],
        out_specs=[pl.BlockSpec((nz, tm, LANES), lambda i: (0, i, 0)), pl.BlockSpec((nx, tm, LANES), lambda i: (0, i, 0)),
                   pl.BlockSpec((tm, LANES), lambda i: (i, 0))],
        out_shape=[jax.ShapeDtypeStruct((nz, t, LANES), F32), jax.ShapeDtypeStruct((nx, t, LANES), F32),
                   jax.ShapeDtypeStruct((t, LANES), F32)],
        compiler_params=_params("parallel"),
    )(x, nw, w)


def _inproj_bwd(x, dy, nw, w, pieces):
    t, d = x.shape
    n = w.shape[1]
    tm = min(TOKEN_TILE, t)
    npc = len(pieces)

    def body(*refs):
        x_ref, dy_ref, nw_ref, w_ref = refs[:4]
        p_refs = refs[4:4 + npc]
        dx_ref, dnw_ref, h_ref = refs[4 + npc:]
        nwv = nw_ref[...]
        hf, xhat, inv = _rms_fwd(x_ref[...], nwv)
        h_ref[...] = hf.astype(BF16)
        parts = []
        for p in p_refs:
            if len(p.shape) == 3:
                parts += [p[i] for i in range(p.shape[0])]
            else:
                parts.append(p[...])
        dz = jnp.concatenate(parts, axis=1) if len(parts) > 1 else parts[0]
        dh = _dot_nt(dz, w_ref[...])
        dx, dw = _rms_bwd(dh, xhat, inv, nwv)
        dx_ref[...] = dy_ref[...] + dx

        @pl.when(pl.program_id(0) == 0)
        def _():
            dnw_ref[...] = jnp.zeros_like(dnw_ref)

        dnw_ref[...] += dw

    tok = lambda m: pl.BlockSpec((tm, m), lambda i: (i, 0))
    p_specs = [pl.BlockSpec((p.shape[0], tm, LANES), lambda i: (0, i, 0)) if p.ndim == 3 else tok(p.shape[1])
               for p in pieces]
    return pl.pallas_call(
        body, name="inproj_bwd", grid=(t // tm,),
        in_specs=[tok(d), tok(d), _resident((1, d)), _resident((d, n))] + p_specs,
        out_specs=[tok(d), pl.BlockSpec((1, d), lambda i: (0, 0)), tok(d)],
        out_shape=[jax.ShapeDtypeStruct((t, d), F32), jax.ShapeDtypeStruct((1, d), F32),
                   jax.ShapeDtypeStruct((t, d), BF16)],
        compiler_params=_params("arbitrary"),
    )(x, dy, nw, w, *pieces)


def _shift_down(v, j, prev8):
    if j == 0:
        return v
    r = pltpu.roll(v, j, 0)
    p = pltpu.roll(prev8, j, 0)
    rows = lax.broadcasted_iota(jnp.int32, prev8.shape, 0)
    first = jnp.where(rows < j, p, r[0:SUBLANES])
    return jnp.concatenate([first, r[SUBLANES:]], axis=0)


def _shift_up(v, j, next8):
    if j == 0:
        return v
    n = v.shape[0]
    r = pltpu.roll(v, n - j, 0)
    p = pltpu.roll(next8, SUBLANES - j, 0)
    rows = lax.broadcasted_iota(jnp.int32, next8.shape, 0)
    last = jnp.where(rows >= SUBLANES - j, p, r[n - SUBLANES:])
    return jnp.concatenate([r[:n - SUBLANES], last], axis=0)


def _sc_fwd(x, bcu, cw, wo):
    t, d = x.shape
    tm = min(TOKEN_TILE, t)
    hb = tm // SUBLANES

    def body(x_ref, bcu_ref, prev_ref, cw_ref, wo_ref, o_ref):
        bg, cg, u = bcu_ref[:, 0:d], bcu_ref[:, d:2 * d], bcu_ref[:, 2 * d:3 * d]
        q = cg * u
        qp = jnp.where(pl.program_id(0) == 0, 0.0, prev_ref[:, d:2 * d] * prev_ref[:, 2 * d:3 * d])
        cwv = cw_ref[...]
        v = cwv[2:3] * q + cwv[1:2] * _shift_down(q, 1, qp) + cwv[0:1] * _shift_down(q, 2, qp)
        o_ref[...] = x_ref[...] + _dot((bg * v).astype(BF16), wo_ref[...])

    return pl.pallas_call(
        body, name="sc_fwd", grid=(t // tm,),
        in_specs=[pl.BlockSpec((tm, d), lambda i: (i, 0)), pl.BlockSpec((tm, 3 * d), lambda i: (i, 0)),
                  pl.BlockSpec((SUBLANES, 3 * d), lambda i: (jnp.maximum(i * hb - 1, 0), 0)),
                  _resident((SUBLANES, d)), _resident((d, d))],
        out_specs=pl.BlockSpec((tm, d), lambda i: (i, 0)),
        out_shape=jax.ShapeDtypeStruct((t, d), F32),
        compiler_params=_params("parallel"),
    )(x, bcu, bcu, cw, wo)


def _sc_bwd(dy, bcu, cw, wo):
    t, d = dy.shape
    tm = min(TOKEN_TILE, t)
    hb = tm // SUBLANES
    nt = t // tm

    def body(dy_ref, dyn_ref, bcu_ref, prev_ref, next_ref, cw_ref, wo_ref, dbcu_ref, p_ref, dcw_ref):
        i = pl.program_id(0)
        bg, cg, u = bcu_ref[:, 0:d], bcu_ref[:, d:2 * d], bcu_ref[:, 2 * d:3 * d]
        q = cg * u
        qp = jnp.where(i == 0, 0.0, prev_ref[:, d:2 * d] * prev_ref[:, 2 * d:3 * d])
        cwv = cw_ref[...]
        q1 = _shift_down(q, 1, qp)
        q2 = _shift_down(q, 2, qp)
        v = cwv[2:3] * q + cwv[1:2] * q1 + cwv[0:1] * q2
        p_ref[...] = (bg * v).astype(BF16)
        wov = wo_ref[...]
        dp = _dot_nt(dy_ref[...].astype(BF16), wov)
        dpn = _dot_nt(dyn_ref[...].astype(BF16), wov)
        dv = dp * bg
        dvn = jnp.where(i == nt - 1, 0.0, dpn * next_ref[:, 0:d])
        dq = cwv[2:3] * dv + cwv[1:2] * _shift_up(dv, 1, dvn) + cwv[0:1] * _shift_up(dv, 2, dvn)
        dbcu_ref[:, 0:d] = (dp * v).astype(BF16)
        dbcu_ref[:, d:2 * d] = (dq * u).astype(BF16)
        dbcu_ref[:, 2 * d:3 * d] = (dq * cg).astype(BF16)

        @pl.when(i == 0)
        def _():
            dcw_ref[...] = jnp.zeros_like(dcw_ref)

        dcw_ref[0:1, :] += jnp.sum(dv * q2, axis=0, keepdims=True)
        dcw_ref[1:2, :] += jnp.sum(dv * q1, axis=0, keepdims=True)
        dcw_ref[2:3, :] += jnp.sum(dv * q, axis=0, keepdims=True)

    last8 = t // SUBLANES - 1
    return pl.pallas_call(
        body, name="sc_bwd", grid=(nt,),
        in_specs=[pl.BlockSpec((tm, d), lambda i: (i, 0)),
                  pl.BlockSpec((SUBLANES, d), lambda i: (jnp.minimum((i + 1) * hb, last8), 0)),
                  pl.BlockSpec((tm, 3 * d), lambda i: (i, 0)),
                  pl.BlockSpec((SUBLANES, 3 * d), lambda i: (jnp.maximum(i * hb - 1, 0), 0)),
                  pl.BlockSpec((SUBLANES, 3 * d), lambda i: (jnp.minimum((i + 1) * hb, last8), 0)),
                  _resident((SUBLANES, d)), _resident((d, d))],
        out_specs=[pl.BlockSpec((tm, 3 * d), lambda i: (i, 0)), pl.BlockSpec((tm, d), lambda i: (i, 0)),
                   pl.BlockSpec((SUBLANES, d), lambda i: (0, 0))],
        out_shape=[jax.ShapeDtypeStruct((t, 3 * d), BF16), jax.ShapeDtypeStruct((t, d), BF16),
                   jax.ShapeDtypeStruct((SUBLANES, d), F32)],
        compiler_params=_params("arbitrary"),
    )(dy, dy, bcu, bcu, bcu, cw, wo)


NEG_BIG = -1e30


def _ssd_consts():
    r = lax.broadcasted_iota(jnp.int32, (LANES, LANES), 0)
    c = lax.broadcasted_iota(jnp.int32, (LANES, LANES), 1)
    p = jnp.arange(N_XS_BLK, dtype=jnp.int32)[:, None, None]
    e = jnp.arange(SSD_N_HEADS, dtype=jnp.int32)[:, None, None]
    pair_sel = (r[None] == 2 * p + (c[None] >= LANES // 2)).astype(BF16)
    head_sel = jnp.broadcast_to(r[None] == e, (SSD_N_HEADS, LANES, LANES)).astype(BF16)
    head_sel_t = jnp.broadcast_to(c[None] == e, (SSD_N_HEADS, LANES, LANES)).astype(BF16)
    tril = (c <= r).astype(BF16)
    triu = (c >= r).astype(BF16)
    return pair_sel, head_sel, head_sel_t, tril, triu


def _ssd_decay(dtr_ref, dtb_ref, alog_ref, tril_ref):
    shape = (SSD_CHUNK, LANES)
    lanes = lax.broadcasted_iota(jnp.int32, shape, 1)
    rows = lax.broadcasted_iota(jnp.int32, shape, 0)
    pre = dtr_ref[...] + dtb_ref[...]
    valid = lanes < SSD_N_HEADS
    dt = jnp.where(valid, jnp.maximum(pre, 0.0) + jnp.log(1.0 + jnp.exp(-jnp.abs(pre))), 0.0)
    a = -jnp.exp(alog_ref[...])
    acs = _sel_left(tril_ref[...], _split3(dt * a))
    return dt, a, acs, pre, valid, rows, lanes


def _ssd_pair_terms(k3, dt3, acs3, rows, xs):
    dtp = _sel_right(dt3, k3)
    acsp = _sel_right(acs3, k3)
    lastp = jnp.sum(jnp.where(rows == SSD_CHUNK - 1, acsp, 0.0), axis=0, keepdims=True)
    eap = jnp.exp(acsp)
    decp = jnp.exp(lastp - acsp)
    etp = jnp.exp(lastp)
    xdt = xs * dtp
    return dtp, eap, decp, etp, xdt


def _ssd_conv_taps(cwb, xr, prev8):
    sh = [_shift_down(xr, j, prev8) for j in range(SSD_CONV_W)]
    xc = cwb[4:5]
    for j in range(SSD_CONV_W):
        xc = xc + cwb[3 - j:4 - j] * sh[j]
    return xc, sh


def _ssd_specs(nc, rev):
    ch = (lambda i: nc - 1 - i) if rev else (lambda i: i)
    L = SSD_CHUNK
    xs = pl.BlockSpec((4, L, LANES), lambda g, i: (g, ch(i), 0))
    bb = pl.BlockSpec((1, L, LANES), lambda g, i: (N_XS_BLK + g, ch(i), 0))
    cc = pl.BlockSpec((1, L, LANES), lambda g, i: (N_XS_BLK + SSD_N_GROUPS + g, ch(i), 0))
    dt = pl.BlockSpec((L, LANES), lambda g, i: (ch(i), 0))
    cw_xs = pl.BlockSpec((4, SUBLANES, LANES), lambda g, i: (g, 0, 0))
    cw_b = pl.BlockSpec((1, SUBLANES, LANES), lambda g, i: (N_XS_BLK + g, 0, 0))
    cw_c = pl.BlockSpec((1, SUBLANES, LANES), lambda g, i: (N_XS_BLK + SSD_N_GROUPS + g, 0, 0))
    st = pl.BlockSpec((1, 4, LANES, LANES), lambda g, i: (ch(i), g, 0, 0))
    grp4 = pl.BlockSpec((4, L, LANES), lambda g, i: (g, ch(i), 0))
    return xs, bb, cc, dt, cw_xs, cw_b, cw_c, st, grp4


def _ssd_scan_fwd(xr, dtr, cwb, dtb, alog, dskip, consts):
    t = xr.shape[1]
    L = SSD_CHUNK
    nc = t // L
    pair_sel, head_sel, head_sel_t, tril, triu = consts
    xs_s, b_s, c_s, dt_s, cwx_s, cwb_s, cwc_s, st_s, grp4 = _ssd_specs(nc, False)

    def body(xs_ref, b_ref, c_ref, dtr_ref, cwx_ref, cwbb_ref, cwc_ref, dtb_ref, alog_ref, dsk_ref,
             ps_ref, hs_ref, hst_ref, tril_ref, y_ref, sp_ref, state, tail, xa_s):
        g = pl.program_id(0)

        @pl.when(pl.program_id(1) == 0)
        def _():
            state[...] = jnp.zeros_like(state)
            tail[...] = jnp.zeros_like(tail)

        for b in range(6):
            xrb = xs_ref[b] if b < 4 else (b_ref[0] if b == 4 else c_ref[0])
            cw = cwx_ref[b] if b < 4 else (cwbb_ref[0] if b == 4 else cwc_ref[0])
            xc, _ = _ssd_conv_taps(cw, xrb, tail[b])
            tail[b] = xrb[L - SUBLANES:]
            xa_s[b] = xc * _sigmoid(xc)

        dt, a, acs, _, _, rows, lanes = _ssd_decay(dtr_ref, dtb_ref, alog_ref, tril_ref)
        dt3, acs3, acst3 = _split3(dt), _split3(acs), _split3(acs.T)
        bb = xa_s[4].astype(BF16)
        cb_ = xa_s[5].astype(BF16)
        cbm = _dot_nt(cb_, bb)
        causal = rows >= lanes
        low_half = lanes < LANES // 2

        def pair(k, carry):
            kk = 4 * g + k
            xs = xa_s[k]
            dtp, eap, decp, etp, xdt = _ssd_pair_terms(ps_ref[kk], dt3, acs3, rows, xs)
            xdtb = xdt.astype(BF16)
            wb = (decp * xdt).astype(BF16)
            yd = []
            for hh in range(2):
                e = 2 * kk + hh
                diff = _sel_right(acs3, hs_ref[e]) - _sel_left(hst_ref[e], acst3)
                lm = jnp.exp(jnp.where(causal, diff, NEG_BIG))
                yd.append(_dot((cbm * lm).astype(BF16), xdtb))
            sp = state[k]
            yo = eap * _dot(cb_, sp.astype(BF16))
            y_ref[k] = jnp.where(low_half, yd[0], yd[1]) + yo + dsk_ref[k][0:1] * xs
            sp_ref[0, k] = sp
            state[k] = etp * sp + _dot_tn(bb, wb)
            return carry

        lax.fori_loop(0, 4, pair, 0)

    row = _resident((1, LANES))
    return pl.pallas_call(
        body, name="ssd_scan_fwd", grid=(SSD_N_GROUPS, nc),
        in_specs=[xs_s, b_s, c_s, dt_s, cwx_s, cwb_s, cwc_s, row, row, cwx_s,
                  _resident(pair_sel.shape), _resident(head_sel.shape), _resident(head_sel_t.shape),
                  _resident(tril.shape)],
        out_specs=[grp4, st_s],
        out_shape=[jax.ShapeDtypeStruct((N_XS_BLK, t, LANES), F32),
                   jax.ShapeDtypeStruct((nc, N_XS_BLK, LANES, LANES), F32)],
        scratch_shapes=[pltpu.VMEM((4, LANES, LANES), F32), pltpu.VMEM((6, SUBLANES, LANES), F32),
                        pltpu.VMEM((6, L, LANES), F32)],
        compiler_params=_params("arbitrary", "arbitrary"),
    )(xr, xr, xr, dtr, cwb, cwb, cwb, dtb, alog, dskip, pair_sel, head_sel, head_sel_t, tril)


def _ssd_scan_bwd(xr, dtr, dy, sprev, cwb, dtb, alog, dskip, consts):
    t = xr.shape[1]
    L = SSD_CHUNK
    nc = t // L
    hb = L // SUBLANES
    pair_sel, head_sel, head_sel_t, tril, triu = consts
    xs_s, b_s, c_s, dt_s, cwx_s, cwb_s, cwc_s, st_s, grp4 = _ssd_specs(nc, True)
    prev = lambda off: pl.BlockSpec(
        (4 if off is None else 1, SUBLANES, LANES),
        (lambda g, i: (g, jnp.maximum((nc - 1 - i) * hb - 1, 0), 0)) if off is None else
        (lambda g, i: (off + g, jnp.maximum((nc - 1 - i) * hb - 1, 0), 0)))
    grp1 = pl.BlockSpec((1, L, LANES), lambda g, i: (g, nc - 1 - i, 0))
    acc4 = pl.BlockSpec((4, SUBLANES, LANES), lambda g, i: (g, 0, 0))
    acc1 = pl.BlockSpec((1, SUBLANES, LANES), lambda g, i: (g, 0, 0))

    def body(xs_ref, b_ref, c_ref, pxs_ref, pb_ref, pc_ref, dtr_ref, dy_ref, sp_ref,
             cwx_ref, cwbb_ref, cwc_ref, dtb_ref, alog_ref, dsk_ref,
             ps_ref, hs_ref, hst_ref, tril_ref, triu_ref,
             dxs_ref, db_ref, dc_ref, ddtr_ref, dcwx_ref, dcwb_ref, dcwc_ref, dd_ref, dsm_ref,
             dstate, head, xa_s, dsil_s, dxa_s, dcb_s, dcbt_s, dbg_s, dcg_s, dacs_s, dacst_s, ddt_s):
        g = pl.program_id(0)
        step = pl.program_id(1)
        first_chunk = step == nc - 1

        @pl.when(step == 0)
        def _():
            dstate[...] = jnp.zeros_like(dstate)
            head[...] = jnp.zeros_like(head)
            for r in (dcwx_ref, dcwb_ref, dcwc_ref, dd_ref, dsm_ref):
                r[...] = jnp.zeros_like(r)

        def blk(b):
            xrb = xs_ref[b] if b < 4 else (b_ref[0] if b == 4 else c_ref[0])
            cw = cwx_ref[b] if b < 4 else (cwbb_ref[0] if b == 4 else cwc_ref[0])
            p8 = pxs_ref[b] if b < 4 else (pb_ref[0] if b == 4 else pc_ref[0])
            return xrb, cw, jnp.where(first_chunk, 0.0, p8)

        for b in range(6):
            xrb, cw, p8 = blk(b)
            xc, _ = _ssd_conv_taps(cw, xrb, p8)
            sig = _sigmoid(xc)
            xa_s[b] = xc * sig
            dsil_s[b] = sig * (1.0 + xc * (1.0 - sig))

        dt, a, acs, pre, valid, rows, lanes = _ssd_decay(dtr_ref, dtb_ref, alog_ref, tril_ref)
        dt3, acs3, acst3 = _split3(dt), _split3(acs), _split3(acs.T)
        bb = xa_s[4].astype(BF16)
        cb_ = xa_s[5].astype(BF16)
        cbm = _dot_nt(cb_, bb)
        cbmt = _dot_nt(bb, cb_)
        causal = rows >= lanes
        anti = rows <= lanes
        low_half = lanes < LANES // 2
        last_row = rows == L - 1
        for r in (dcb_s, dcbt_s, dbg_s, dcg_s, dacs_s, dacst_s, ddt_s):
            r[...] = jnp.zeros_like(r)

        def pair(k, carry):
            kk = 4 * g + k
            k3 = ps_ref[kk]
            xs = xa_s[k]
            dtp, eap, decp, etp, xdt = _ssd_pair_terms(k3, dt3, acs3, rows, xs)
            xdtb = xdt.astype(BF16)
            w = decp * xdt
            wb = w.astype(BF16)
            dyv = dy_ref[k]
            sp = sp_ref[0, k]
            spb = sp.astype(BF16)
            dsn = dstate[k]
            dsnb = dsn.astype(BF16)
            yoff = eap * _dot(cb_, spb)
            dgb = (eap * dyv).astype(BF16)
            dcg_s[...] += _dot_nt(dgb, spb)
            dsp = _dot_tn(cb_, dgb) + etp * dsn
            last_lane = etp * jnp.sum(dsn * sp, axis=0, keepdims=True)
            dbg_s[...] += _dot_nt(wb, dsnb)
            dw = _dot(bb, dsnb)
            t2 = dw * w
            dxdt = decp * dw
            last_lane = last_lane + jnp.sum(t2, axis=0, keepdims=True)
            lane_acc = dyv * yoff - t2 + jnp.where(last_row, last_lane, 0.0)
            for hh in range(2):
                e = 2 * kk + hh
                diff = _sel_right(acs3, hs_ref[e]) - _sel_left(hst_ref[e], acst3)
                lm = jnp.exp(jnp.where(causal, diff, NEG_BIG))
                lmt = jnp.exp(jnp.where(anti, -diff, NEG_BIG))
                dye = jnp.where(low_half == (hh == 0), dyv, 0.0).astype(BF16)
                dm = _dot_nt(dye, xdtb)
                dmt = _dot_nt(xdtb, dye)
                mt = cbmt * lmt
                seg = dmt * mt - dm * (cbm * lm)
                dacst_s[...] += _dot(hs_ref[e], seg.astype(BF16))
                dcb_s[...] += dm * lm
                dcbt_s[...] += dmt * lmt
                dxdt = dxdt + _dot(mt.astype(BF16), dye)
            dacs_s[...] += _sel_right_t(_split3(lane_acc), k3)
            ddt_s[...] += _sel_right_t(_split3(dxdt * xs), k3)
            dsk = dsk_ref[k][0:1]
            dxa_s[k] = dsk * dyv + dxdt * dtp
            dd_ref[k, 0:1, :] += jnp.sum(dyv * xs, axis=0, keepdims=True)
            dstate[k] = dsp
            return carry

        lax.fori_loop(0, 4, pair, 0)

        dxa_s[4] = dbg_s[...] + _dot(dcbt_s[...].astype(BF16), cb_)
        dxa_s[5] = dcg_s[...] + _dot(dcb_s[...].astype(BF16), bb)
        dacs = dacs_s[...] + dacst_s[...].T
        dac = _sel_left(triu_ref[...], _split3(dacs))
        ddt = ddt_s[...] + dac * a
        ddtr = jnp.where(valid, ddt * _sigmoid(pre), 0.0)
        ddtr_ref[0] = ddtr
        dsm_ref[0, 0:1, :] += jnp.sum(ddtr, axis=0, keepdims=True)
        dsm_ref[0, 1:2, :] += jnp.sum(dac * dt, axis=0, keepdims=True) * a

        for b in range(6):
            xrb, cw, p8 = blk(b)
            sh = [_shift_down(xrb, j, p8) for j in range(SSD_CONV_W)]
            dxc = dxa_s[b] * dsil_s[b]
            acc = dcwx_ref.at[b] if b < 4 else (dcwb_ref.at[0] if b == 4 else dcwc_ref.at[0])
            acc[4:5, :] += jnp.sum(dxc, axis=0, keepdims=True)
            dxr = jnp.zeros_like(dxc)
            for j in range(SSD_CONV_W):
                acc[3 - j:4 - j, :] += jnp.sum(dxc * sh[j], axis=0, keepdims=True)
                dxr = dxr + cw[3 - j:4 - j] * _shift_up(dxc, j, head[b])
            head[b] = dxc[0:SUBLANES]
            out = dxs_ref.at[b] if b < 4 else (db_ref.at[0] if b == 4 else dc_ref.at[0])
            out[...] = dxr.astype(BF16)

    row = _resident((1, LANES))
    mat = lambda: pltpu.VMEM((LANES, LANES), F32)
    return pl.pallas_call(
        body, name="ssd_scan_bwd", grid=(SSD_N_GROUPS, nc),
        in_specs=[xs_s, b_s, c_s, prev(None), prev(N_XS_BLK), prev(N_XS_BLK + SSD_N_GROUPS), dt_s, grp4, st_s,
                  cwx_s, cwb_s, cwc_s, row, row, cwx_s,
                  _resident(pair_sel.shape), _resident(head_sel.shape), _resident(head_sel_t.shape),
                  _resident(tril.shape), _resident(triu.shape)],
        out_specs=[grp4, grp1, grp1, grp1, acc4, acc1, acc1, acc4, acc1],
        out_shape=[jax.ShapeDtypeStruct((N_XS_BLK, t, LANES), BF16),
                   jax.ShapeDtypeStruct((SSD_N_GROUPS, t, LANES), BF16),
                   jax.ShapeDtypeStruct((SSD_N_GROUPS, t, LANES), BF16),
                   jax.ShapeDtypeStruct((SSD_N_GROUPS, t, LANES), F32),
                   jax.ShapeDtypeStruct((N_XS_BLK, SUBLANES, LANES), F32),
                   jax.ShapeDtypeStruct((SSD_N_GROUPS, SUBLANES, LANES), F32),
                   jax.ShapeDtypeStruct((SSD_N_GROUPS, SUBLANES, LANES), F32),
                   jax.ShapeDtypeStruct((N_XS_BLK, SUBLANES, LANES), F32),
                   jax.ShapeDtypeStruct((SSD_N_GROUPS, SUBLANES, LANES), F32)],
        scratch_shapes=[pltpu.VMEM((4, LANES, LANES), F32), pltpu.VMEM((6, SUBLANES, LANES), F32),
                        pltpu.VMEM((6, L, LANES), F32), pltpu.VMEM((6, L, LANES), F32), pltpu.VMEM((6, L, LANES), F32),
                        mat(), mat(), mat(), mat(), mat(), mat(), mat()],
        compiler_params=_params("arbitrary", "arbitrary"),
    )(xr, xr, xr, xr, xr, xr, dtr, dy, sprev, cwb, cwb, cwb, dtb, alog, dskip,
      pair_sel, head_sel, head_sel_t, tril, triu)


def _ssd_gate_fwd(x, y, z, gnw, wo):
    t, d = x.shape
    tm = min(TOKEN_TILE, t)
    nb = N_XS_BLK
    per = nb // SSD_N_GROUPS

    def body(x_ref, y_ref, z_ref, gnw_ref, wo_ref, o_ref, gn_ref):
        gs = []
        for j in range(nb):
            zv = z_ref[j]
            gs.append(y_ref[j] * (zv * _sigmoid(zv)))
        for q in range(SSD_N_GROUPS):
            ss = sum(jnp.sum(gs[j] * gs[j], axis=1, keepdims=True) for j in range(q * per, (q + 1) * per))
            inv = lax.rsqrt(ss / (per * LANES) + RMS_EPS)
            for j in range(q * per, (q + 1) * per):
                gn_ref[:, j * LANES:(j + 1) * LANES] = ((gs[j] * inv) * gnw_ref[j]).astype(BF16)
        o_ref[...] = x_ref[...] + _dot(gn_ref[...], wo_ref[...])

    blk = pl.BlockSpec((nb, tm, LANES), lambda i: (0, i, 0))
    return pl.pallas_call(
        body, name="ssd_gate_fwd", grid=(t // tm,),
        in_specs=[pl.BlockSpec((tm, d), lambda i: (i, 0)), blk, blk, _resident((nb, 1, LANES)),
                  _resident((SSD_D_INNER, d))],
        out_specs=[pl.BlockSpec((tm, d), lambda i: (i, 0)), pl.BlockSpec((tm, SSD_D_INNER), lambda i: (i, 0))],
        out_shape=[jax.ShapeDtypeStruct((t, d), F32), jax.ShapeDtypeStruct((t, SSD_D_INNER), BF16)],
        compiler_params=_params("parallel"),
    )(x, y, z, gnw, wo)


def _ssd_gate_bwd(dy, y, z, gnw, wo):
    t, d = dy.shape
    tm = min(TOKEN_TILE, t)
    nb = N_XS_BLK
    per = nb // SSD_N_GROUPS

    def body(dy_ref, y_ref, z_ref, gnw_ref, wo_ref, dys_ref, dz_ref, dgnw_ref):
        @pl.when(pl.program_id(0) == 0)
        def _():
            dgnw_ref[...] = jnp.zeros_like(dgnw_ref)

        dgn = _dot_nt(dy_ref[...].astype(BF16), wo_ref[...])
        for q in range(SSD_N_GROUPS):
            js = range(q * per, (q + 1) * per)
            gs, sil, dsil = {}, {}, {}
            for j in js:
                zv = z_ref[j]
                sig = _sigmoid(zv)
                sil[j] = zv * sig
                dsil[j] = sig * (1.0 + zv * (1.0 - sig))
                gs[j] = y_ref[j] * sil[j]
            ss = sum(jnp.sum(gs[j] * gs[j], axis=1, keepdims=True) for j in js)
            inv = lax.rsqrt(ss / (per * LANES) + RMS_EPS)
            ghat = {j: gs[j] * inv for j in js}
            dgh = {}
            for j in js:
                dj = dgn[:, j * LANES:(j + 1) * LANES]
                dgnw_ref[j] += jnp.sum(dj * ghat[j], axis=0, keepdims=True)
                dgh[j] = dj * gnw_ref[j]
            mean = sum(jnp.sum(dgh[j] * ghat[j], axis=1, keepdims=True) for j in js) / (per * LANES)
            for j in js:
                dg = inv * (dgh[j] - ghat[j] * mean)
                dys_ref[j] = dg * sil[j]
                dz_ref[j] = (dg * y_ref[j] * dsil[j]).astype(BF16)

    blk = pl.BlockSpec((nb, tm, LANES), lambda i: (0, i, 0))
    return pl.pallas_call(
        body, name="ssd_gate_bwd", grid=(t // tm,),
        in_specs=[pl.BlockSpec((tm, d), lambda i: (i, 0)), blk, blk, _resident((nb, 1, LANES)),
                  _resident((SSD_D_INNER, d))],
        out_specs=[blk, blk, pl.BlockSpec((nb, 1, LANES), lambda i: (0, 0, 0))],
        out_shape=[jax.ShapeDtypeStruct((nb, t, LANES), F32), jax.ShapeDtypeStruct((nb, t, LANES), BF16),
                   jax.ShapeDtypeStruct((nb, 1, LANES), F32)],
        compiler_params=_params("arbitrary"),
    )(dy, y, z, gnw, wo)


def _lane_blocks(v):
    r, n = v.shape[0], v.shape[1] // LANES
    return v.reshape(r, n, LANES).transpose(1, 0, 2)


def _ssd_prep(w_in, conv_w, conv_b, dt_bias, a_log, d_skip, norm_w):
    w_in_pad = jnp.pad(w_in, ((0, 0), (0, SSD_IN_PAD - SSD_IN_DIM)))
    taps = jnp.concatenate([conv_w, conv_b[None], jnp.zeros((SUBLANES - SSD_CONV_W - 1, SSD_CONV_DIM), F32)], axis=0)
    cwb = _lane_blocks(taps)
    row = lambda v: jnp.pad(v, (0, LANES - SSD_N_HEADS))[None]
    dskip = jnp.broadcast_to(jnp.repeat(d_skip, SSD_D_INNER // SSD_N_HEADS).reshape(N_XS_BLK, 1, LANES),
                             (N_XS_BLK, SUBLANES, LANES))
    gnw = norm_w.reshape(N_XS_BLK, 1, LANES)
    return w_in_pad, cwb, row(dt_bias), row(a_log), dskip, gnw


def _ssd_layer_fwd(x, nw, prm, wo, consts):
    w_in_pad, cwb, dtb, alog, dskip, gnw = prm
    z, xr, dtr = _ssd_inproj(x, nw, w_in_pad)
    y, sprev = _ssd_scan_fwd(xr, dtr, cwb, dtb, alog, dskip, consts)
    out, gn = _ssd_gate_fwd(x, y, z, gnw, wo)
    return out, (z, xr, dtr, y, sprev, gn)


def _ssd_layer_bwd(x, dy, nw, prm, wo, consts, saved):
    w_in_pad, cwb, dtb, alog, dskip, gnw = prm
    z, xr, dtr, y, sprev, gn = saved
    dys, dz, dgnw = _ssd_gate_bwd(dy, y, z, gnw, wo)
    dwo = _matmul_tn(gn, dy, name="wgrad_ssd_out")
    dxs, db, dc, ddtr4, dcwx, dcwb, dcwc, dd, dsm = _ssd_scan_bwd(xr, dtr, dys, sprev, cwb, dtb, alog, dskip, consts)
    ddtr = jnp.sum(ddtr4, axis=0).astype(BF16)[None]
    pieces = [dz, dxs, db, dc, ddtr]
    dx, dnw, h = _inproj_bwd(x, dy, nw, w_in_pad, pieces)
    dw_in = jnp.concatenate([_matmul_tn_blocked(h, p, name=f"wgrad_ssd_in{i}") for i, p in enumerate(pieces)],
                            axis=1)[:, :SSD_IN_DIM]
    dtaps = jnp.concatenate([dcwx, dcwb, dcwc], axis=0).transpose(1, 0, 2).reshape(SUBLANES, SSD_CONV_DIM)
    dsm = jnp.sum(dsm, axis=0)
    d_d = jnp.sum(dd[:, 0, :].reshape(SSD_N_HEADS, SSD_D_INNER // SSD_N_HEADS), axis=1)
    return dx, (dnw, dw_in, dtaps[:SSD_CONV_W], dtaps[SSD_CONV_W], dsm[0, :SSD_N_HEADS], dsm[1, :SSD_N_HEADS],
                d_d, dgnw.reshape(SSD_D_INNER), dwo)


def _loss_head(x, fw, target):
    t, d = x.shape
    tm = min(TOKEN_TILE, t)

    def body(x_ref, fw_ref, tgt_ref, loss_ref, dx_ref, dfw_ref):
        fwv = fw_ref[...]
        y, xhat, inv = _rms_fwd(x_ref[...], fwv)
        err = y - tgt_ref[...]
        tot = jnp.sum(jnp.sum(err * err, axis=1, keepdims=True), axis=0, keepdims=True)
        dx, dw = _rms_bwd(err * (1.0 / d), xhat, inv, fwv)
        dx_ref[...] = dx

        @pl.when(pl.program_id(0) == 0)
        def _():
            loss_ref[...] = jnp.zeros_like(loss_ref)
            dfw_ref[...] = jnp.zeros_like(dfw_ref)

        loss_ref[...] += jnp.broadcast_to(tot * (0.5 / d), loss_ref.shape)
        dfw_ref[...] += dw

    tok = pl.BlockSpec((tm, d), lambda i: (i, 0))
    return pl.pallas_call(
        body, name="loss_head", grid=(t // tm,),
        in_specs=[tok, _resident((1, d)), tok],
        out_specs=[pl.BlockSpec((1, LANES), lambda i: (0, 0)), tok, pl.BlockSpec((1, d), lambda i: (0, 0))],
        out_shape=[jax.ShapeDtypeStruct((1, LANES), F32), jax.ShapeDtypeStruct((t, d), F32),
                   jax.ShapeDtypeStruct((1, d), F32)],
        compiler_params=_params("arbitrary"),
    )(x, fw, target)


def _row_tile(rows, cap):
    best = SUBLANES
    for r in range(SUBLANES, min(rows, cap) + 1, SUBLANES):
        if rows % r == 0:
            best = r
    return best


def _adamw(w, g, m, v, name):
    rows, cols = w.shape
    br = _row_tile(rows, 256)
    c1 = 1.0 - ADAM_B1 ** ADAM_STEP
    c2 = 1.0 - ADAM_B2 ** ADAM_STEP

    def body(w_ref, g_ref, m_ref, v_ref, d_ref, nm_ref, nv_ref):
        gv = g_ref[...]
        nm = ADAM_B1 * m_ref[...] + (1.0 - ADAM_B1) * gv
        nv = ADAM_B2 * v_ref[...] + (1.0 - ADAM_B2) * (gv * gv)
        nm_ref[...] = nm
        nv_ref[...] = nv
        d_ref[...] = -ADAM_LR * ((nm / c1) / (jnp.sqrt(nv / c2) + ADAM_EPS) + ADAM_WD * w_ref[...])

    blk = pl.BlockSpec((br, cols), lambda i: (i, 0))
    shp = jax.ShapeDtypeStruct((rows, cols), F32)
    return pl.pallas_call(
        body, name=name, grid=(rows // br,), in_specs=[blk] * 4, out_specs=[blk] * 3, out_shape=[shp] * 3,
        compiler_params=_params("parallel"),
    )(w, g, m, v)


def _sum_leading(a, name):
    k, rows, cols = a.shape
    br = _row_tile(rows, 1024)

    def body(a_ref, o_ref):
        acc = a_ref[0]
        for i in range(1, k):
            acc = acc + a_ref[i]
        o_ref[...] = acc

    return pl.pallas_call(
        body, name=name, grid=(rows // br,),
        in_specs=[pl.BlockSpec((k, br, cols), lambda i: (0, i, 0))],
        out_specs=pl.BlockSpec((br, cols), lambda i: (i, 0)),
        out_shape=jax.ShapeDtypeStruct((rows, cols), F32),
        compiler_params=_params("parallel"),
    )(a)


_HBM = pl.BlockSpec(memory_space=pl.ANY)


def _place():
    x, y, c = lax.axis_index("x"), lax.axis_index("y"), lax.axis_index("c")
    return x, y, c, [(1 - x, y), (x, 1 - y), (1 - x, 1 - y)]


def _remote(src, dst, send_sems, recv_sems, k, to):
    return pltpu.make_async_remote_copy(src_ref=src, dst_ref=dst, send_sem=send_sems.at[k], recv_sem=recv_sems.at[k],
                                        device_id=to, device_id_type=MESH)


def _all_gather_shards(big, small):
    ops = (big, small)

    def body(big_ref, small_ref, obig_ref, osmall_ref, send_sems, recv_sems, local_sems):
        x, y, c, chips = _place()
        me = 2 * x + y
        sibling = (x, y, 1 - c)
        pairs = ((big_ref, obig_ref), (small_ref, osmall_ref))

        def rows(dst, shard, which):
            half = dst.shape[1] // 2
            return dst.at[shard, pl.ds(which * half, half)]

        own, sent = [], []
        for oi, (src, dst) in enumerate(pairs):
            own.append(pltpu.make_async_copy(src, dst.at[me], local_sems.at[oi]))
            own[-1].start()
            half = src.shape[0] // 2
            for j, chip in enumerate(chips):
                sent.append(_remote(src.at[pl.ds(c * half, half)], rows(dst, me, c), send_sems, recv_sems,
                                    6 * oi + j, (*chip, c)))
                sent[-1].start()
        for oi, (src, dst) in enumerate(pairs):
            for j, chip in enumerate(chips):
                landed = rows(dst, 2 * chip[0] + chip[1], c)
                _remote(landed, landed, send_sems, recv_sems, 6 * oi + j, (*chip, c)).wait_recv()
                sent.append(_remote(landed, landed, send_sems, recv_sems, 6 * oi + 3 + j, sibling))
                sent[-1].start()
        for oi, (src, dst) in enumerate(pairs):
            for j, chip in enumerate(chips):
                landed = rows(dst, 2 * chip[0] + chip[1], 1 - c)
                _remote(landed, landed, send_sems, recv_sems, 6 * oi + 3 + j, sibling).wait_recv()
        for cp in sent:
            cp.wait_send()
        for cp in own:
            cp.wait()

    return pl.pallas_call(
        body, name="all_gather_shards",
        in_specs=[_HBM, _HBM], out_specs=[_HBM, _HBM],
        out_shape=[jax.ShapeDtypeStruct((N_SHARDS,) + o.shape, o.dtype) for o in ops],
        scratch_shapes=[pltpu.SemaphoreType.DMA((12,)), pltpu.SemaphoreType.DMA((12,)), pltpu.SemaphoreType.DMA((2,))],
    )(big, small)


def _swap_with_sibling(v):
    def body(v_ref, o_ref, send_sems, recv_sems):
        x, y, c, _ = _place()
        cp = _remote(v_ref, o_ref, send_sems, recv_sems, 0, (x, y, 1 - c))
        cp.start()
        cp.wait()

    return pl.pallas_call(
        body, name="swap_with_sibling", in_specs=[_HBM], out_specs=_HBM,
        out_shape=jax.ShapeDtypeStruct(v.shape, v.dtype),
        scratch_shapes=[pltpu.SemaphoreType.DMA((1,)), pltpu.SemaphoreType.DMA((1,))],
    )(v)


def _scatter_to_chips(p):
    def body(p_ref, o_ref, send_sems, recv_sems, local_sem):
        x, y, c, chips = _place()
        me = 2 * x + y
        own = pltpu.make_async_copy(p_ref.at[me], o_ref.at[me], local_sem.at[0])
        own.start()
        sent = []
        for j, chip in enumerate(chips):
            cp = _remote(p_ref.at[2 * chip[0] + chip[1]], o_ref.at[me], send_sems, recv_sems, j, (*chip, c))
            cp.start()
            sent.append(cp)
        for j, chip in enumerate(chips):
            landed = o_ref.at[2 * chip[0] + chip[1]]
            _remote(landed, landed, send_sems, recv_sems, j, (*chip, c)).wait_recv()
        for cp in sent:
            cp.wait_send()
        own.wait()

    return pl.pallas_call(
        body, name="scatter_to_chips", in_specs=[_HBM], out_specs=_HBM,
        out_shape=jax.ShapeDtypeStruct(p.shape, p.dtype),
        scratch_shapes=[pltpu.SemaphoreType.DMA((3,)), pltpu.SemaphoreType.DMA((3,)), pltpu.SemaphoreType.DMA((1,))],
    )(p)


def _join_halves(r):
    def body(r_ref, o_ref, send_sems, recv_sems, local_sem):
        x, y, c, _ = _place()
        own = pltpu.make_async_copy(r_ref, o_ref.at[c], local_sem.at[0])
        own.start()
        cp = _remote(r_ref, o_ref.at[c], send_sems, recv_sems, 0, (x, y, 1 - c))
        cp.start()
        other = o_ref.at[1 - c]
        _remote(other, other, send_sems, recv_sems, 0, (x, y, 1 - c)).wait_recv()
        cp.wait_send()
        own.wait()

    return pl.pallas_call(
        body, name="join_halves", in_specs=[_HBM], out_specs=_HBM,
        out_shape=jax.ShapeDtypeStruct((2,) + r.shape, r.dtype),
        scratch_shapes=[pltpu.SemaphoreType.DMA((1,)), pltpu.SemaphoreType.DMA((1,)), pltpu.SemaphoreType.DMA((1,))],
    )(r)


def _add_pair(a, b):
    shape = a.shape
    a2, b2 = a.reshape(-1, shape[-1]), b.reshape(-1, shape[-1])
    rows, cols = a2.shape
    br = _row_tile(rows, 1024)

    def body(a_ref, b_ref, o_ref):
        o_ref[...] = a_ref[...] + b_ref[...]

    blk = pl.BlockSpec((br, cols), lambda i: (i, 0))
    out = pl.pallas_call(
        body, name="add_pair", grid=(rows // br,), in_specs=[blk, blk], out_specs=blk,
        out_shape=jax.ShapeDtypeStruct((rows, cols), F32), compiler_params=_params("parallel"),
    )(a2, b2)
    return out.reshape(shape)


WEIGHTS = ("norm_w", "ffn_w_gate", "ffn_w_up", "ffn_w_down", "ssd_w_in", "ssd_conv_w", "ssd_conv_b", "ssd_dt_bias",
           "ssd_a_log", "ssd_d", "ssd_norm_w", "ssd_w_out", "sc_w_in", "sc_conv_w", "sc_w_out", "final_norm_w")
BIG = (("ffn_w_gate", 3), ("ffn_w_up", 3), ("ffn_w_down", 2), ("ssd_w_in", 2), ("ssd_w_out", 1), ("sc_w_in", 2),
       ("sc_w_out", 1))
SMALL_SHARDED = (("norm_w", 2), ("ssd_conv_w", 2), ("sc_conv_w", 2))
REPLICATED = ("ssd_conv_b", "ssd_dt_bias", "ssd_a_log", "ssd_d", "ssd_norm_w", "final_norm_w")
FLAT_COLS = 1024


def _pack(arrays, row_multiple, lead=()):
    flat = jnp.concatenate([a.reshape(lead + (-1,)) for a in arrays], axis=len(lead))
    unit = row_multiple * FLAT_COLS
    n = flat.shape[-1]
    pad = (-n) % unit
    if pad:
        flat = jnp.pad(flat, [(0, 0)] * len(lead) + [(0, pad)])
    return flat.reshape(lead + (-1, FLAT_COLS))


def _unpack(flat, shapes, lead=()):
    flat = flat.reshape(lead + (-1,))
    out, off = [], 0
    for shp in shapes:
        n = 1
        for s in shp:
            n *= s
        out.append(flat[..., off:off + n].reshape(lead + tuple(shp)))
        off += n
    return out


def _to_shards(full, axis):
    shp = full.shape
    r = full.reshape(shp[:axis] + (N_SHARDS, shp[axis] // N_SHARDS) + shp[axis + 1:])
    return jnp.moveaxis(r, axis, 0)


def _from_shards(sh, axis):
    r = jnp.moveaxis(sh, 0, axis)
    shp = r.shape
    return r.reshape(shp[:axis] + (shp[axis] * shp[axis + 1],) + shp[axis + 2:])


def _forward_backward(x, target, p):
    consts = _ssd_consts()
    nw = p["norm_w"]
    row = lambda v: v[None]
    ssd_prm = [_ssd_prep(p["ssd_w_in"][j], p["ssd_conv_w"][j], p["ssd_conv_b"][j], p["ssd_dt_bias"][j],
                         p["ssd_a_log"][j], p["ssd_d"][j], p["ssd_norm_w"][j]) for j in range(2)]
    sc_cw = [jnp.pad(p["sc_conv_w"][j], ((0, SUBLANES - SC_CONV_W), (0, 0))) for j in range(2)]
    ffn = lambda i, k: (p["ffn_w_gate"][i, k], p["ffn_w_up"][i, k], p["ffn_w_down"][i, k])

    xin, saved = [], []
    for i in range(N_LAYERS):
        j = i // 2
        xin.append(x)
        x = _ffn_fwd(x, row(nw[i, 0]), *ffn(i, 0))
        xin.append(x)
        if i % 2 == 0:
            x, sv = _ssd_layer_fwd(x, row(nw[i, 1]), ssd_prm[j], p["ssd_w_out"][j], consts)
        else:
            sv = _norm_mm(x, row(nw[i, 1]), p["sc_w_in"][j])
            x = _sc_fwd(x, sv, sc_cw[j], p["sc_w_out"][j])
        saved.append(sv)
        xin.append(x)
        x = _ffn_fwd(x, row(nw[i, 2]), *ffn(i, 1))
    loss, dx, dfw = _loss_head(x, row(p["final_norm_w"]), target)

    g_nw = [[None] * 3 for _ in range(N_LAYERS)]
    g_ffn = {n: [[None] * 2 for _ in range(N_LAYERS)] for n in ("ffn_w_gate", "ffn_w_up", "ffn_w_down")}
    g_ssd = [None, None]
    g_sc = [None, None]

    def ffn_bwd(i, k, slot, dy):
        wg, wu, wd = ffn(i, k)
        dxn, dnw, h, a, dg, du = _ffn_bwd(xin[3 * i + slot], dy, row(nw[i, slot]), wg, wu, wd)
        g_nw[i][slot] = dnw[0]
        g_ffn["ffn_w_gate"][i][k] = _matmul_tn(h, dg, name="wgrad_ffn_gate")
        g_ffn["ffn_w_up"][i][k] = _matmul_tn(h, du, name="wgrad_ffn_up")
        g_ffn["ffn_w_down"][i][k] = _matmul_tn(a, dy, scale=0.5, name="wgrad_ffn_down")
        return dxn

    for i in reversed(range(N_LAYERS)):
        j = i // 2
        dx = ffn_bwd(i, 1, 2, dx)
        xm = xin[3 * i + 1]
        if i % 2 == 0:
            dx, gs = _ssd_layer_bwd(xm, dx, row(nw[i, 1]), ssd_prm[j], p["ssd_w_out"][j], consts, saved[i])
            g_nw[i][1] = gs[0][0]
            g_ssd[j] = gs[1:]
        else:
            bcu = saved[i]
            dbcu, pin, dcw = _sc_bwd(dx, bcu, sc_cw[j], p["sc_w_out"][j])
            dwo = _matmul_tn(pin, dx, name="wgrad_sc_out")
            dx, dnw, h = _inproj_bwd(xm, dx, row(nw[i, 1]), p["sc_w_in"][j], [dbcu])
            g_nw[i][1] = dnw[0]
            g_sc[j] = (_matmul_tn(h, dbcu, name="wgrad_sc_in"), dcw[:SC_CONV_W], dwo)
        dx = ffn_bwd(i, 0, 0, dx)

    g = {"norm_w": jnp.stack([jnp.stack(r) for r in g_nw]), "final_norm_w": dfw[0]}
    for n in g_ffn:
        g[n] = jnp.stack([jnp.stack(r) for r in g_ffn[n]])
    for k, n in enumerate(("ssd_w_in", "ssd_conv_w", "ssd_conv_b", "ssd_dt_bias", "ssd_a_log", "ssd_d", "ssd_norm_w",
                           "ssd_w_out")):
        g[n] = jnp.stack([g_ssd[0][k], g_ssd[1][k]])
    for k, n in enumerate(("sc_w_in", "sc_conv_w", "sc_w_out")):
        g[n] = jnp.stack([g_sc[0][k], g_sc[1][k]])
    return loss, dx, g


def kernel(x, norm_w, ffn_w_gate, ffn_w_up, ffn_w_down, ssd_w_in, ssd_conv_w, ssd_conv_b, ssd_dt_bias, ssd_a_log, ssd_d, ssd_norm_w, ssd_w_out, sc_w_in, sc_conv_w, sc_w_out, final_norm_w, loss_target, m_norm_w, m_ffn_w_gate, m_ffn_w_up, m_ffn_w_down, m_ssd_w_in, m_ssd_conv_w, m_ssd_conv_b, m_ssd_dt_bias, m_ssd_a_log, m_ssd_d, m_ssd_norm_w, m_ssd_w_out, m_sc_w_in, m_sc_conv_w, m_sc_w_out, m_final_norm_w, v_norm_w, v_ffn_w_gate, v_ffn_w_up, v_ffn_w_down, v_ssd_w_in, v_ssd_conv_w, v_ssd_conv_b, v_ssd_dt_bias, v_ssd_a_log, v_ssd_d, v_ssd_norm_w, v_ssd_w_out, v_sc_w_in, v_sc_conv_w, v_sc_w_out, v_final_norm_w):
    w = dict(zip(WEIGHTS, (norm_w, ffn_w_gate, ffn_w_up, ffn_w_down, ssd_w_in, ssd_conv_w, ssd_conv_b, ssd_dt_bias,
                           ssd_a_log, ssd_d, ssd_norm_w, ssd_w_out, sc_w_in, sc_conv_w, sc_w_out, final_norm_w)))
    m = dict(zip(WEIGHTS, (m_norm_w, m_ffn_w_gate, m_ffn_w_up, m_ffn_w_down, m_ssd_w_in, m_ssd_conv_w, m_ssd_conv_b,
                           m_ssd_dt_bias, m_ssd_a_log, m_ssd_d, m_ssd_norm_w, m_ssd_w_out, m_sc_w_in, m_sc_conv_w,
                           m_sc_w_out, m_final_norm_w)))
    v = dict(zip(WEIGHTS, (v_norm_w, v_ffn_w_gate, v_ffn_w_up, v_ffn_w_down, v_ssd_w_in, v_ssd_conv_w, v_ssd_conv_b,
                           v_ssd_dt_bias, v_ssd_a_log, v_ssd_d, v_ssd_norm_w, v_ssd_w_out, v_sc_w_in, v_sc_conv_w,
                           v_sc_w_out, v_final_norm_w)))
    c = lax.axis_index("c")
    big_names = [n for n, _ in BIG]
    small_names = [n for n, _ in SMALL_SHARDED] + list(REPLICATED)

    gbig, gsmall = _all_gather_shards(_pack([w[n].astype(BF16) for n in big_names], 32),
                                      _pack([w[n] for n, _ in SMALL_SHARDED], 16))
    p = {n: w[n] for n in REPLICATED}
    for (n, ax), sh in zip(BIG, _unpack(gbig, [w[n].shape for n in big_names], lead=(N_SHARDS,))):
        p[n] = _from_shards(sh, ax)
    for (n, ax), sh in zip(SMALL_SHARDED, _unpack(gsmall, [w[n].shape for n, _ in SMALL_SHARDED], lead=(N_SHARDS,))):
        p[n] = _from_shards(sh, ax)

    t, d = x.shape[-2:]
    loss, dx, g = _forward_backward(x.reshape(t, d), loss_target.reshape(t, d), p)

    big_part = _pack([_to_shards(g[n], ax) for n, ax in BIG], 1, lead=(N_SHARDS,))
    small_part = _pack([_to_shards(g[n], ax) for n, ax in SMALL_SHARDED]
                       + [jnp.broadcast_to(g[n][None], (N_SHARDS,) + g[n].shape) for n in REPLICATED],
                       SUBLANES, lead=(N_SHARDS,))
    big_rows, small_rows = big_part.shape[1], small_part.shape[1]
    flat = _pack([big_part, small_part], 2 * SUBLANES, lead=(N_SHARDS,))
    half_rows = flat.shape[1] // 2
    halves = flat.reshape(N_SHARDS, 2, half_rows, FLAT_COLS)
    mine = lax.dynamic_index_in_dim(halves, c, axis=1, keepdims=False)
    theirs = lax.dynamic_index_in_dim(halves, 1 - c, axis=1, keepdims=False)
    chip_sum = _add_pair(mine, _swap_with_sibling(theirs))
    reduced = _join_halves(_sum_leading(_scatter_to_chips(chip_sum), name="sum_chips")).reshape(-1, FLAT_COLS)

    grad = dict(zip(big_names, _unpack(reduced[:big_rows], [w[n].shape for n in big_names])))
    g_small = reduced[big_rows:big_rows + small_rows]
    grad.update(zip(small_names, _unpack(g_small, [w[n].shape for n in small_names])))

    delta, new_m, new_v = {}, {}, {}
    for n in big_names:
        shp = w[n].shape
        as2d = lambda a: a.reshape(-1, shp[-1])
        out = _adamw(as2d(w[n]), as2d(grad[n]), as2d(m[n]), as2d(v[n]), name="adamw_" + n)
        delta[n], new_m[n], new_v[n] = (o.reshape(shp) for o in out)
    packed = [_pack([s[n] for n in small_names], SUBLANES) for s in (w, m, v)]
    out = _adamw(packed[0], g_small, packed[1], packed[2], name="adamw_small")
    shapes = [w[n].shape for n in small_names]
    for dst, o in zip((delta, new_m, new_v), out):
        dst.update(zip(small_names, _unpack(o, shapes)))

    loss = lax.psum(loss[0, 0], ("x", "y", "c"))
    return (loss, dx.reshape(x.shape), *[grad[n] for n in WEIGHTS], *[delta[n] for n in WEIGHTS],
            *[new_m[n] for n in WEIGHTS], *[new_v[n] for n in WEIGHTS])
```

```python
import functools

import jax
import jax.numpy as jnp
from jax import lax
from jax.experimental import pallas as pl
from jax.experimental.pallas import tpu as pltpu

F32 = jnp.float32
BF16 = jnp.bfloat16
MESH = pl.DeviceIdType.MESH

RMS_EPS = 1e-5
D_MODEL = 1024
D_FF = 2816
N_LAYERS = 4
SSD_D_INNER = 2048
SSD_N_HEADS = 32
SSD_N_GROUPS = 4
SSD_D_STATE = 128
SSD_CHUNK = 128
SSD_CONV_W = 4
SSD_CONV_DIM = 3072
SSD_IN_DIM = 5152
SC_CONV_W = 3
LANES = 128
SUBLANES = 8
N_XS_BLK = SSD_D_INNER // LANES
SSD_IN_PAD = SSD_D_INNER + SSD_CONV_DIM + LANES
VMEM_LIMIT = 56 * 2**20
TOKEN_TILE = 512
FF_CHUNK = 256
N_SHARDS = 4

ADAM_LR = 0.001
ADAM_B1 = 0.9
ADAM_B2 = 0.999
ADAM_EPS = 1e-08
ADAM_WD = 0.01
ADAM_STEP = 10


_HBM = pl.BlockSpec(memory_space=pl.ANY)


def _params(*sem):
    return pltpu.CompilerParams(dimension_semantics=sem if sem else None, vmem_limit_bytes=VMEM_LIMIT)


def _dot(a, b):
    return jnp.dot(a, b, preferred_element_type=F32)


def _dot_nt(a, b):
    return lax.dot_general(a, b, (((1,), (1,)), ((), ())), preferred_element_type=F32)


def _dot_tn(a, b):
    return lax.dot_general(a, b, (((0,), (0,)), ((), ())), preferred_element_type=F32)


def _resident(shape):
    n = len(shape)
    return pl.BlockSpec(shape, lambda *_: (0,) * n, pipeline_mode=pl.Buffered(1))


def _split3(v):
    hi = v.astype(BF16)
    r1 = v - hi.astype(F32)
    mid = r1.astype(BF16)
    lo = (r1 - mid.astype(F32)).astype(BF16)
    return hi, mid, lo


def _sel_right(v3, sel):
    return _dot(v3[0], sel) + _dot(v3[1], sel) + _dot(v3[2], sel)


def _sel_left(sel, v3):
    return _dot(sel, v3[0]) + _dot(sel, v3[1]) + _dot(sel, v3[2])


def _sel_right_t(v3, sel):
    return _dot_nt(v3[0], sel) + _dot_nt(v3[1], sel) + _dot_nt(v3[2], sel)


def _sigmoid(v):
    return 1.0 / (1.0 + jnp.exp(-v))


def _rms_fwd(x, w):
    inv = lax.rsqrt(jnp.mean(x * x, axis=-1, keepdims=True) + RMS_EPS)
    xhat = x * inv
    return xhat * w, xhat, inv


def _rms_bwd(dh, xhat, inv, w):
    dxhat = dh * w
    dx = inv * (dxhat - xhat * jnp.mean(dxhat * xhat, axis=-1, keepdims=True))
    return dx, jnp.sum(dh * xhat, axis=0, keepdims=True)


def _ffn_fwd(x, nw, wg, wu, wd):
    t, d = x.shape
    f = wg.shape[1]
    tm = min(TOKEN_TILE, t)

    def body(x_ref, nw_ref, wg_ref, wu_ref, wd_ref, o_ref):
        xv = x_ref[...]
        h = _rms_fwd(xv, nw_ref[...])[0].astype(BF16)
        acc = jnp.zeros((tm, d), F32)
        for j in range(f // FF_CHUNK):
            sl = slice(j * FF_CHUNK, (j + 1) * FF_CHUNK)
            g = _dot(h, wg_ref[:, sl])
            u = _dot(h, wu_ref[:, sl])
            a = (g * _sigmoid(g) * u).astype(BF16)
            acc = acc + _dot(a, wd_ref[sl, :])
        o_ref[...] = xv + 0.5 * acc

    return pl.pallas_call(
        body, name="ffn_fwd", grid=(t // tm,),
        in_specs=[pl.BlockSpec((tm, d), lambda i: (i, 0)), _resident((1, d)), _resident((d, f)),
                  _resident((d, f)), _resident((f, d))],
        out_specs=pl.BlockSpec((tm, d), lambda i: (i, 0)),
        out_shape=jax.ShapeDtypeStruct((t, d), F32),
        compiler_params=_params("parallel"),
    )(x, nw, wg, wu, wd)


def _ffn_bwd(x, dy, nw, wg, wu, wd):
    t, d = x.shape
    f = wg.shape[1]
    tm = min(TOKEN_TILE, t)

    def body(x_ref, dy_ref, nw_ref, wg_ref, wu_ref, wd_ref, dx_ref, dnw_ref, h_ref, a_ref, dg_ref, du_ref):
        xv = x_ref[...]
        dyv = dy_ref[...]
        nwv = nw_ref[...]
        hf, xhat, inv = _rms_fwd(xv, nwv)
        h = hf.astype(BF16)
        h_ref[...] = h
        dob = (0.5 * dyv).astype(BF16)
        dh = jnp.zeros((tm, d), F32)
        for j in range(f // FF_CHUNK):
            sl = slice(j * FF_CHUNK, (j + 1) * FF_CHUNK)
            g = _dot(h, wg_ref[:, sl])
            u = _dot(h, wu_ref[:, sl])
            sig = _sigmoid(g)
            s = g * sig
            a_ref[:, sl] = (s * u).astype(BF16)
            da = _dot_nt(dob, wd_ref[sl, :])
            dgb = (da * u * (sig * (1.0 + g * (1.0 - sig)))).astype(BF16)
            dub = (da * s).astype(BF16)
            dg_ref[:, sl] = dgb
            du_ref[:, sl] = dub
            dh = dh + _dot_nt(dgb, wg_ref[:, sl]) + _dot_nt(dub, wu_ref[:, sl])
        dx, dw = _rms_bwd(dh, xhat, inv, nwv)
        dx_ref[...] = dyv + dx

        @pl.when(pl.program_id(0) == 0)
        def _():
            dnw_ref[...] = jnp.zeros_like(dnw_ref)

        dnw_ref[...] += dw

    tok = lambda n: pl.BlockSpec((tm, n), lambda i: (i, 0))
    return pl.pallas_call(
        body, name="ffn_bwd", grid=(t // tm,),
        in_specs=[tok(d), tok(d), _resident((1, d)), _resident((d, f)), _resident((d, f)), _resident((f, d))],
        out_specs=[tok(d), pl.BlockSpec((1, d), lambda i: (0, 0)), tok(d), tok(f), tok(f), tok(f)],
        out_shape=[jax.ShapeDtypeStruct((t, d), F32), jax.ShapeDtypeStruct((1, d), F32),
                   jax.ShapeDtypeStruct((t, d), BF16), jax.ShapeDtypeStruct((t, f), BF16),
                   jax.ShapeDtypeStruct((t, f), BF16), jax.ShapeDtypeStruct((t, f), BF16)],
        compiler_params=_params("arbitrary"),
    )(x, dy, nw, wg, wu, wd)


def _pick_bn(m, n, unit):
    best = unit
    for k in range(1, n // unit + 1):
        bn = k * unit
        if n % bn == 0 and m * bn * 4 <= 8 * 2**20:
            best = bn
    return best


def _matmul_tn(a, b, scale=1.0, name="wgrad", slab=None, stack=None, buf=None):
    t, m = a.shape
    n = b.shape[1]
    bt = min(1024, t)
    bn = _pick_bn(m, n, LANES)
    nt = t // bt
    lead = tuple(slab) if slab is not None else ()

    def body(a_ref, b_ref, *rest):
        o_ref = rest[-1]

        @pl.when(pl.program_id(1) == 0)
        def _():
            o_ref[...] = jnp.zeros_like(o_ref)

        o_ref[...] += _dot_tn(a_ref[...].astype(BF16), b_ref[...].astype(BF16))
        if scale != 1.0:
            @pl.when(pl.program_id(1) == nt - 1)
            def _():
                o_ref[...] *= scale

    in_specs = [pl.BlockSpec((bt, m), lambda j, k: (k, 0)), pl.BlockSpec((bt, bn), lambda j, k: (k, j))]
    args = [a, b]
    if buf is not None:
        in_specs.append(_HBM)
        args.append(buf)
    return pl.pallas_call(
        body, name=name, grid=(n // bn, nt),
        in_specs=in_specs,
        out_specs=pl.BlockSpec((None,) * len(lead) + (m, bn), lambda j, k: lead + (0, j)),
        out_shape=jax.ShapeDtypeStruct(tuple(stack or ()) + (m, n), F32),
        input_output_aliases={2: 0} if buf is not None else {},
        compiler_params=_params("parallel", "arbitrary"),
    )(*args)


def _matmul_tn_blocked(a, b, name="wgrad_blk"):
    t, m = a.shape
    nb = b.shape[0]
    bt = min(1024, t)
    nbt = _pick_bn(m, nb * LANES, LANES) // LANES
    while nb % nbt:
        nbt -= 1

    def body(a_ref, b_ref, o_ref):
        @pl.when(pl.program_id(1) == 0)
        def _():
            o_ref[...] = jnp.zeros_like(o_ref)

        bv = jnp.concatenate([b_ref[i] for i in range(nbt)], axis=1) if nbt > 1 else b_ref[0]
        o_ref[...] += _dot_tn(a_ref[...], bv)

    return pl.pallas_call(
        body, name=name, grid=(nb // nbt, t // bt),
        in_specs=[pl.BlockSpec((bt, m), lambda j, k: (k, 0)), pl.BlockSpec((nbt, bt, LANES), lambda j, k: (j, k, 0))],
        out_specs=pl.BlockSpec((m, nbt * LANES), lambda j, k: (0, j)),
        out_shape=jax.ShapeDtypeStruct((m, nb * LANES), F32),
        compiler_params=_params("parallel", "arbitrary"),
    )(a, b)


def _norm_mm(x, nw, w):
    t, d = x.shape
    n = w.shape[1]
    tm = min(TOKEN_TILE, t)
    cn = 1024 if n % 1024 == 0 else n

    def body(x_ref, nw_ref, w_ref, o_ref):
        h = _rms_fwd(x_ref[...], nw_ref[...])[0].astype(BF16)
        for j in range(n // cn):
            sl = slice(j * cn, (j + 1) * cn)
            o_ref[:, sl] = _dot(h, w_ref[:, sl])

    return pl.pallas_call(
        body, name="norm_mm", grid=(t // tm,),
        in_specs=[pl.BlockSpec((tm, d), lambda i: (i, 0)), _resident((1, d)), _resident((d, n))],
        out_specs=pl.BlockSpec((tm, n), lambda i: (i, 0)),
        out_shape=jax.ShapeDtypeStruct((t, n), F32),
        compiler_params=_params("parallel"),
    )(x, nw, w)


def _ssd_inproj(x, nw, w):
    t, d = x.shape
    tm = min(TOKEN_TILE, t)
    nz, nx = SSD_D_INNER // LANES, SSD_CONV_DIM // LANES
    cn = 1024

    def body(x_ref, nw_ref, w_ref, z_ref, xr_ref, dt_ref):
        h = _rms_fwd(x_ref[...], nw_ref[...])[0].astype(BF16)
        for j in range((SSD_D_INNER + SSD_CONV_DIM) // cn):
            r = _dot(h, w_ref[:, j * cn:(j + 1) * cn])
            for i in range(cn // LANES):
                blk = j * (cn // LANES) + i
                v = r[:, i * LANES:(i + 1) * LANES]
                if blk < nz:
                    z_ref[blk] = v
                else:
                    xr_ref[blk - nz] = v
        dt_ref[...] = _dot(h, w_ref[:, SSD_D_INNER + SSD_CONV_DIM:])

    return pl.pallas_call(
        body, name="ssd_inproj", grid=(t // tm,),
        in_specs=[pl.BlockSpec((tm, d), lambda i: (i, 0)), _resident((1, d)), _resident((d, SSD_IN_PAD))],
        out_specs=[pl.BlockSpec((nz, tm, LANES), lambda i: (0, i, 0)), pl.BlockSpec((nx, tm, LANES), lambda i: (0, i, 0)),
                   pl.BlockSpec((tm, LANES), lambda i: (i, 0))],
        out_shape=[jax.ShapeDtypeStruct((nz, t, LANES), F32), jax.ShapeDtypeStruct((nx, t, LANES), F32),
                   jax.ShapeDtypeStruct((t, LANES), F32)],
        compiler_params=_params("parallel"),
    )(x, nw, w)


def _inproj_bwd(x, dy, nw, w, pieces):
    t, d = x.shape
    n = w.shape[1]
    tm = min(TOKEN_TILE, t)
    npc = len(pieces)

    def body(*refs):
        x_ref, dy_ref, nw_ref, w_ref = refs[:4]
        p_refs = refs[4:4 + npc]
        dx_ref, dnw_ref, h_ref = refs[4 + npc:]
        nwv = nw_ref[...]
        hf, xhat, inv = _rms_fwd(x_ref[...], nwv)
        h_ref[...] = hf.astype(BF16)
        parts = []
        for p in p_refs:
            if len(p.shape) == 3:
                parts += [p[i] for i in range(p.shape[0])]
            else:
                parts.append(p[...])
        dz = jnp.concatenate(parts, axis=1) if len(parts) > 1 else parts[0]
        dh = _dot_nt(dz, w_ref[...])
        dx, dw = _rms_bwd(dh, xhat, inv, nwv)
        dx_ref[...] = dy_ref[...] + dx

        @pl.when(pl.program_id(0) == 0)
        def _():
            dnw_ref[...] = jnp.zeros_like(dnw_ref)

        dnw_ref[...] += dw

    tok = lambda m: pl.BlockSpec((tm, m), lambda i: (i, 0))
    p_specs = [pl.BlockSpec((p.shape[0], tm, LANES), lambda i: (0, i, 0)) if p.ndim == 3 else tok(p.shape[1])
               for p in pieces]
    return pl.pallas_call(
        body, name="inproj_bwd", grid=(t // tm,),
        in_specs=[tok(d), tok(d), _resident((1, d)), _resident((d, n))] + p_specs,
        out_specs=[tok(d), pl.BlockSpec((1, d), lambda i: (0, 0)), tok(d)],
        out_shape=[jax.ShapeDtypeStruct((t, d), F32), jax.ShapeDtypeStruct((1, d), F32),
                   jax.ShapeDtypeStruct((t, d), BF16)],
        compiler_params=_params("arbitrary"),
    )(x, dy, nw, w, *pieces)


def _shift_down(v, j, prev8):
    if j == 0:
        return v
    r = pltpu.roll(v, j, 0)
    p = pltpu.roll(prev8, j, 0)
    rows = lax.broadcasted_iota(jnp.int32, prev8.shape, 0)
    first = jnp.where(rows < j, p, r[0:SUBLANES])
    return jnp.concatenate([first, r[SUBLANES:]], axis=0)


def _shift_up(v, j, next8):
    if j == 0:
        return v
    n = v.shape[0]
    r = pltpu.roll(v, n - j, 0)
    p = pltpu.roll(next8, SUBLANES - j, 0)
    rows = lax.broadcasted_iota(jnp.int32, next8.shape, 0)
    last = jnp.where(rows >= SUBLANES - j, p, r[n - SUBLANES:])
    return jnp.concatenate([r[:n - SUBLANES], last], axis=0)


def _sc_fwd(x, bcu, cw, wo):
    t, d = x.shape
    tm = min(TOKEN_TILE, t)
    hb = tm // SUBLANES

    def body(x_ref, bcu_ref, prev_ref, cw_ref, wo_ref, o_ref):
        bg, cg, u = bcu_ref[:, 0:d], bcu_ref[:, d:2 * d], bcu_ref[:, 2 * d:3 * d]
        q = cg * u
        qp = jnp.where(pl.program_id(0) == 0, 0.0, prev_ref[:, d:2 * d] * prev_ref[:, 2 * d:3 * d])
        cwv = cw_ref[...]
        v = cwv[2:3] * q + cwv[1:2] * _shift_down(q, 1, qp) + cwv[0:1] * _shift_down(q, 2, qp)
        o_ref[...] = x_ref[...] + _dot((bg * v).astype(BF16), wo_ref[...])

    return pl.pallas_call(
        body, name="sc_fwd", grid=(t // tm,),
        in_specs=[pl.BlockSpec((tm, d), lambda i: (i, 0)), pl.BlockSpec((tm, 3 * d), lambda i: (i, 0)),
                  pl.BlockSpec((SUBLANES, 3 * d), lambda i: (jnp.maximum(i * hb - 1, 0), 0)),
                  _resident((SUBLANES, d)), _resident((d, d))],
        out_specs=pl.BlockSpec((tm, d), lambda i: (i, 0)),
        out_shape=jax.ShapeDtypeStruct((t, d), F32),
        compiler_params=_params("parallel"),
    )(x, bcu, bcu, cw, wo)


def _sc_bwd(dy, bcu, cw, wo):
    t, d = dy.shape
    tm = min(TOKEN_TILE, t)
    hb = tm // SUBLANES
    nt = t // tm

    def body(dy_ref, dyn_ref, bcu_ref, prev_ref, next_ref, cw_ref, wo_ref, dbcu_ref, p_ref, dcw_ref):
        i = pl.program_id(0)
        bg, cg, u = bcu_ref[:, 0:d], bcu_ref[:, d:2 * d], bcu_ref[:, 2 * d:3 * d]
        q = cg * u
        qp = jnp.where(i == 0, 0.0, prev_ref[:, d:2 * d] * prev_ref[:, 2 * d:3 * d])
        cwv = cw_ref[...]
        q1 = _shift_down(q, 1, qp)
        q2 = _shift_down(q, 2, qp)
        v = cwv[2:3] * q + cwv[1:2] * q1 + cwv[0:1] * q2
        p_ref[...] = (bg * v).astype(BF16)
        wov = wo_ref[...]
        dp = _dot_nt(dy_ref[...].astype(BF16), wov)
        dpn = _dot_nt(dyn_ref[...].astype(BF16), wov)
        dv = dp * bg
        dvn = jnp.where(i == nt - 1, 0.0, dpn * next_ref[:, 0:d])
        dq = cwv[2:3] * dv + cwv[1:2] * _shift_up(dv, 1, dvn) + cwv[0:1] * _shift_up(dv, 2, dvn)
        dbcu_ref[:, 0:d] = (dp * v).astype(BF16)
        dbcu_ref[:, d:2 * d] = (dq * u).astype(BF16)
        dbcu_ref[:, 2 * d:3 * d] = (dq * cg).astype(BF16)

        @pl.when(i == 0)
        def _():
            dcw_ref[...] = jnp.zeros_like(dcw_ref)

        dcw_ref[0:1, :] += jnp.sum(dv * q2, axis=0, keepdims=True)
        dcw_ref[1:2, :] += jnp.sum(dv * q1, axis=0, keepdims=True)
        dcw_ref[2:3, :] += jnp.sum(dv * q, axis=0, keepdims=True)

    last8 = t // SUBLANES - 1
    return pl.pallas_call(
        body, name="sc_bwd", grid=(nt,),
        in_specs=[pl.BlockSpec((tm, d), lambda i: (i, 0)),
                  pl.BlockSpec((SUBLANES, d), lambda i: (jnp.minimum((i + 1) * hb, last8), 0)),
                  pl.BlockSpec((tm, 3 * d), lambda i: (i, 0)),
                  pl.BlockSpec((SUBLANES, 3 * d), lambda i: (jnp.maximum(i * hb - 1, 0), 0)),
                  pl.BlockSpec((SUBLANES, 3 * d), lambda i: (jnp.minimum((i + 1) * hb, last8), 0)),
                  _resident((SUBLANES, d)), _resident((d, d))],
        out_specs=[pl.BlockSpec((tm, 3 * d), lambda i: (i, 0)), pl.BlockSpec((tm, d), lambda i: (i, 0)),
                   pl.BlockSpec((SUBLANES, d), lambda i: (0, 0))],
        out_shape=[jax.ShapeDtypeStruct((t, 3 * d), BF16), jax.ShapeDtypeStruct((t, d), BF16),
                   jax.ShapeDtypeStruct((SUBLANES, d), F32)],
        compiler_params=_params("arbitrary"),
    )(dy, dy, bcu, bcu, bcu, cw, wo)


NEG_BIG = -1e30


def _ssd_consts():
    r = lax.broadcasted_iota(jnp.int32, (LANES, LANES), 0)
    c = lax.broadcasted_iota(jnp.int32, (LANES, LANES), 1)
    p = jnp.arange(N_XS_BLK, dtype=jnp.int32)[:, None, None]
    e = jnp.arange(SSD_N_HEADS, dtype=jnp.int32)[:, None, None]
    pair_sel = (r[None] == 2 * p + (c[None] >= LANES // 2)).astype(BF16)
    head_sel = jnp.broadcast_to(r[None] == e, (SSD_N_HEADS, LANES, LANES)).astype(BF16)
    head_sel_t = jnp.broadcast_to(c[None] == e, (SSD_N_HEADS, LANES, LANES)).astype(BF16)
    tril = (c <= r).astype(BF16)
    triu = (c >= r).astype(BF16)
    return pair_sel, head_sel, head_sel_t, tril, triu


def _ssd_decay(dtr_ref, dtb_ref, alog_ref, tril_ref):
    shape = (SSD_CHUNK, LANES)
    lanes = lax.broadcasted_iota(jnp.int32, shape, 1)
    rows = lax.broadcasted_iota(jnp.int32, shape, 0)
    pre = dtr_ref[...] + dtb_ref[...]
    valid = lanes < SSD_N_HEADS
    dt = jnp.where(valid, jnp.maximum(pre, 0.0) + jnp.log(1.0 + jnp.exp(-jnp.abs(pre))), 0.0)
    a = -jnp.exp(alog_ref[...])
    acs = _sel_left(tril_ref[...], _split3(dt * a))
    return dt, a, acs, pre, valid, rows, lanes


def _ssd_pair_terms(k3, dt3, acs3, rows, xs):
    dtp = _sel_right(dt3, k3)
    acsp = _sel_right(acs3, k3)
    lastp = jnp.sum(jnp.where(rows == SSD_CHUNK - 1, acsp, 0.0), axis=0, keepdims=True)
    eap = jnp.exp(acsp)
    decp = jnp.exp(lastp - acsp)
    etp = jnp.exp(lastp)
    xdt = xs * dtp
    return dtp, eap, decp, etp, xdt


def _ssd_conv_taps(cwb, xr, prev8):
    sh = [_shift_down(xr, j, prev8) for j in range(SSD_CONV_W)]
    xc = cwb[4:5]
    for j in range(SSD_CONV_W):
        xc = xc + cwb[3 - j:4 - j] * sh[j]
    return xc, sh


def _ssd_specs(nc, rev):
    ch = (lambda i: nc - 1 - i) if rev else (lambda i: i)
    L = SSD_CHUNK
    xs = pl.BlockSpec((4, L, LANES), lambda g, i: (g, ch(i), 0))
    bb = pl.BlockSpec((1, L, LANES), lambda g, i: (N_XS_BLK + g, ch(i), 0))
    cc = pl.BlockSpec((1, L, LANES), lambda g, i: (N_XS_BLK + SSD_N_GROUPS + g, ch(i), 0))
    dt = pl.BlockSpec((L, LANES), lambda g, i: (ch(i), 0))
    cw_xs = pl.BlockSpec((4, SUBLANES, LANES), lambda g, i: (g, 0, 0))
    cw_b = pl.BlockSpec((1, SUBLANES, LANES), lambda g, i: (N_XS_BLK + g, 0, 0))
    cw_c = pl.BlockSpec((1, SUBLANES, LANES), lambda g, i: (N_XS_BLK + SSD_N_GROUPS + g, 0, 0))
    st = pl.BlockSpec((1, 4, LANES, LANES), lambda g, i: (ch(i), g, 0, 0))
    grp4 = pl.BlockSpec((4, L, LANES), lambda g, i: (g, ch(i), 0))
    return xs, bb, cc, dt, cw_xs, cw_b, cw_c, st, grp4


def _ssd_scan_fwd(xr, dtr, cwb, dtb, alog, dskip, consts):
    t = xr.shape[1]
    L = SSD_CHUNK
    nc = t // L
    pair_sel, head_sel, head_sel_t, tril, triu = consts
    xs_s, b_s, c_s, dt_s, cwx_s, cwb_s, cwc_s, st_s, grp4 = _ssd_specs(nc, False)

    def body(xs_ref, b_ref, c_ref, dtr_ref, cwx_ref, cwbb_ref, cwc_ref, dtb_ref, alog_ref, dsk_ref,
             ps_ref, hs_ref, hst_ref, tril_ref, y_ref, sp_ref, state, tail, xa_s):
        g = pl.program_id(0)

        @pl.when(pl.program_id(1) == 0)
        def _():
            state[...] = jnp.zeros_like(state)
            tail[...] = jnp.zeros_like(tail)

        for b in range(6):
            xrb = xs_ref[b] if b < 4 else (b_ref[0] if b == 4 else c_ref[0])
            cw = cwx_ref[b] if b < 4 else (cwbb_ref[0] if b == 4 else cwc_ref[0])
            xc, _ = _ssd_conv_taps(cw, xrb, tail[b])
            tail[b] = xrb[L - SUBLANES:]
            xa_s[b] = xc * _sigmoid(xc)

        dt, a, acs, _, _, rows, lanes = _ssd_decay(dtr_ref, dtb_ref, alog_ref, tril_ref)
        dt3, acs3, acst3 = _split3(dt), _split3(acs), _split3(acs.T)
        bb = xa_s[4].astype(BF16)
        cb_ = xa_s[5].astype(BF16)
        cbm = _dot_nt(cb_, bb)
        causal = rows >= lanes
        low_half = lanes < LANES // 2

        def pair(k, carry):
            kk = 4 * g + k
            xs = xa_s[k]
            dtp, eap, decp, etp, xdt = _ssd_pair_terms(ps_ref[kk], dt3, acs3, rows, xs)
            xdtb = xdt.astype(BF16)
            wb = (decp * xdt).astype(BF16)
            yd = []
            for hh in range(2):
                e = 2 * kk + hh
                diff = _sel_right(acs3, hs_ref[e]) - _sel_left(hst_ref[e], acst3)
                lm = jnp.exp(jnp.where(causal, diff, NEG_BIG))
                yd.append(_dot((cbm * lm).astype(BF16), xdtb))
            sp = state[k]
            yo = eap * _dot(cb_, sp.astype(BF16))
            y_ref[k] = jnp.where(low_half, yd[0], yd[1]) + yo + dsk_ref[k][0:1] * xs
            sp_ref[0, k] = sp
            state[k] = etp * sp + _dot_tn(bb, wb)
            return carry

        lax.fori_loop(0, 4, pair, 0)

    row = _resident((1, LANES))
    return pl.pallas_call(
        body, name="ssd_scan_fwd", grid=(SSD_N_GROUPS, nc),
        in_specs=[xs_s, b_s, c_s, dt_s, cwx_s, cwb_s, cwc_s, row, row, cwx_s,
                  _resident(pair_sel.shape), _resident(head_sel.shape), _resident(head_sel_t.shape),
                  _resident(tril.shape)],
        out_specs=[grp4, st_s],
        out_shape=[jax.ShapeDtypeStruct((N_XS_BLK, t, LANES), F32),
                   jax.ShapeDtypeStruct((nc, N_XS_BLK, LANES, LANES), F32)],
        scratch_shapes=[pltpu.VMEM((4, LANES, LANES), F32), pltpu.VMEM((6, SUBLANES, LANES), F32),
                        pltpu.VMEM((6, L, LANES), F32)],
        compiler_params=_params("arbitrary", "arbitrary"),
    )(xr, xr, xr, dtr, cwb, cwb, cwb, dtb, alog, dskip, pair_sel, head_sel, head_sel_t, tril)


def _ssd_scan_bwd(xr, dtr, dy, sprev, cwb, dtb, alog, dskip, consts):
    t = xr.shape[1]
    L = SSD_CHUNK
    nc = t // L
    hb = L // SUBLANES
    pair_sel, head_sel, head_sel_t, tril, triu = consts
    xs_s, b_s, c_s, dt_s, cwx_s, cwb_s, cwc_s, st_s, grp4 = _ssd_specs(nc, True)
    prev = lambda off: pl.BlockSpec(
        (4 if off is None else 1, SUBLANES, LANES),
        (lambda g, i: (g, jnp.maximum((nc - 1 - i) * hb - 1, 0), 0)) if off is None else
        (lambda g, i: (off + g, jnp.maximum((nc - 1 - i) * hb - 1, 0), 0)))
    grp1 = pl.BlockSpec((1, L, LANES), lambda g, i: (g, nc - 1 - i, 0))
    acc4 = pl.BlockSpec((4, SUBLANES, LANES), lambda g, i: (g, 0, 0))
    acc1 = pl.BlockSpec((1, SUBLANES, LANES), lambda g, i: (g, 0, 0))

    def body(xs_ref, b_ref, c_ref, pxs_ref, pb_ref, pc_ref, dtr_ref, dy_ref, sp_ref,
             cwx_ref, cwbb_ref, cwc_ref, dtb_ref, alog_ref, dsk_ref,
             ps_ref, hs_ref, hst_ref, tril_ref, triu_ref,
             dxs_ref, db_ref, dc_ref, ddtr_ref, dcwx_ref, dcwb_ref, dcwc_ref, dd_ref, dsm_ref,
             dstate, head, xa_s, dsil_s, dxa_s, dcb_s, dcbt_s, dbg_s, dcg_s, dacs_s, dacst_s, ddt_s):
        g = pl.program_id(0)
        step = pl.program_id(1)
        first_chunk = step == nc - 1

        @pl.when(step == 0)
        def _():
            dstate[...] = jnp.zeros_like(dstate)
            head[...] = jnp.zeros_like(head)
            for r in (dcwx_ref, dcwb_ref, dcwc_ref, dd_ref, dsm_ref):
                r[...] = jnp.zeros_like(r)

        def blk(b):
            xrb = xs_ref[b] if b < 4 else (b_ref[0] if b == 4 else c_ref[0])
            cw = cwx_ref[b] if b < 4 else (cwbb_ref[0] if b == 4 else cwc_ref[0])
            p8 = pxs_ref[b] if b < 4 else (pb_ref[0] if b == 4 else pc_ref[0])
            return xrb, cw, jnp.where(first_chunk, 0.0, p8)

        for b in range(6):
            xrb, cw, p8 = blk(b)
            xc, _ = _ssd_conv_taps(cw, xrb, p8)
            sig = _sigmoid(xc)
            xa_s[b] = xc * sig
            dsil_s[b] = sig * (1.0 + xc * (1.0 - sig))

        dt, a, acs, pre, valid, rows, lanes = _ssd_decay(dtr_ref, dtb_ref, alog_ref, tril_ref)
        dt3, acs3, acst3 = _split3(dt), _split3(acs), _split3(acs.T)
        bb = xa_s[4].astype(BF16)
        cb_ = xa_s[5].astype(BF16)
        cbm = _dot_nt(cb_, bb)
        cbmt = _dot_nt(bb, cb_)
        causal = rows >= lanes
        anti = rows <= lanes
        low_half = lanes < LANES // 2
        last_row = rows == L - 1
        for r in (dcb_s, dcbt_s, dbg_s, dcg_s, dacs_s, dacst_s, ddt_s):
            r[...] = jnp.zeros_like(r)

        def pair(k, carry):
            kk = 4 * g + k
            k3 = ps_ref[kk]
            xs = xa_s[k]
            dtp, eap, decp, etp, xdt = _ssd_pair_terms(k3, dt3, acs3, rows, xs)
            xdtb = xdt.astype(BF16)
            w = decp * xdt
            wb = w.astype(BF16)
            dyv = dy_ref[k]
            sp = sp_ref[0, k]
            spb = sp.astype(BF16)
            dsn = dstate[k]
            dsnb = dsn.astype(BF16)
            yoff = eap * _dot(cb_, spb)
            dgb = (eap * dyv).astype(BF16)
            dcg_s[...] += _dot_nt(dgb, spb)
            dsp = _dot_tn(cb_, dgb) + etp * dsn
            last_lane = etp * jnp.sum(dsn * sp, axis=0, keepdims=True)
            dbg_s[...] += _dot_nt(wb, dsnb)
            dw = _dot(bb, dsnb)
            t2 = dw * w
            dxdt = decp * dw
            last_lane = last_lane + jnp.sum(t2, axis=0, keepdims=True)
            lane_acc = dyv * yoff - t2 + jnp.where(last_row, last_lane, 0.0)
            for hh in range(2):
                e = 2 * kk + hh
                diff = _sel_right(acs3, hs_ref[e]) - _sel_left(hst_ref[e], acst3)
                lm = jnp.exp(jnp.where(causal, diff, NEG_BIG))
                lmt = jnp.exp(jnp.where(anti, -diff, NEG_BIG))
                dye = jnp.where(low_half == (hh == 0), dyv, 0.0).astype(BF16)
                dm = _dot_nt(dye, xdtb)
                dmt = _dot_nt(xdtb, dye)
                mt = cbmt * lmt
                seg = dmt * mt - dm * (cbm * lm)
                dacst_s[...] += _dot(hs_ref[e], seg.astype(BF16))
                dcb_s[...] += dm * lm
                dcbt_s[...] += dmt * lmt
                dxdt = dxdt + _dot(mt.astype(BF16), dye)
            dacs_s[...] += _sel_right_t(_split3(lane_acc), k3)
            ddt_s[...] += _sel_right_t(_split3(dxdt * xs), k3)
            dsk = dsk_ref[k][0:1]
            dxa_s[k] = dsk * dyv + dxdt * dtp
            dd_ref[k, 0:1, :] += jnp.sum(dyv * xs, axis=0, keepdims=True)
            dstate[k] = dsp
            return carry

        lax.fori_loop(0, 4, pair, 0)

        dxa_s[4] = dbg_s[...] + _dot(dcbt_s[...].astype(BF16), cb_)
        dxa_s[5] = dcg_s[...] + _dot(dcb_s[...].astype(BF16), bb)
        dacs = dacs_s[...] + dacst_s[...].T
        dac = _sel_left(triu_ref[...], _split3(dacs))
        ddt = ddt_s[...] + dac * a
        ddtr = jnp.where(valid, ddt * _sigmoid(pre), 0.0)
        ddtr_ref[0] = ddtr
        dsm_ref[0, 0:1, :] += jnp.sum(ddtr, axis=0, keepdims=True)
        dsm_ref[0, 1:2, :] += jnp.sum(dac * dt, axis=0, keepdims=True) * a

        for b in range(6):
            xrb, cw, p8 = blk(b)
            sh = [_shift_down(xrb, j, p8) for j in range(SSD_CONV_W)]
            dxc = dxa_s[b] * dsil_s[b]
            acc = dcwx_ref.at[b] if b < 4 else (dcwb_ref.at[0] if b == 4 else dcwc_ref.at[0])
            acc[4:5, :] += jnp.sum(dxc, axis=0, keepdims=True)
            dxr = jnp.zeros_like(dxc)
            for j in range(SSD_CONV_W):
                acc[3 - j:4 - j, :] += jnp.sum(dxc * sh[j], axis=0, keepdims=True)
                dxr = dxr + cw[3 - j:4 - j] * _shift_up(dxc, j, head[b])
            head[b] = dxc[0:SUBLANES]
            out = dxs_ref.at[b] if b < 4 else (db_ref.at[0] if b == 4 else dc_ref.at[0])
            out[...] = dxr.astype(BF16)

    row = _resident((1, LANES))
    mat = lambda: pltpu.VMEM((LANES, LANES), F32)
    return pl.pallas_call(
        body, name="ssd_scan_bwd", grid=(SSD_N_GROUPS, nc),
        in_specs=[xs_s, b_s, c_s, prev(None), prev(N_XS_BLK), prev(N_XS_BLK + SSD_N_GROUPS), dt_s, grp4, st_s,
                  cwx_s, cwb_s, cwc_s, row, row, cwx_s,
                  _resident(pair_sel.shape), _resident(head_sel.shape), _resident(head_sel_t.shape),
                  _resident(tril.shape), _resident(triu.shape)],
        out_specs=[grp4, grp1, grp1, grp1, acc4, acc1, acc1, acc4, acc1],
        out_shape=[jax.ShapeDtypeStruct((N_XS_BLK, t, LANES), BF16),
                   jax.ShapeDtypeStruct((SSD_N_GROUPS, t, LANES), BF16),
                   jax.ShapeDtypeStruct((SSD_N_GROUPS, t, LANES), BF16),
                   jax.ShapeDtypeStruct((SSD_N_GROUPS, t, LANES), F32),
                   jax.ShapeDtypeStruct((N_XS_BLK, SUBLANES, LANES), F32),
                   jax.ShapeDtypeStruct((SSD_N_GROUPS, SUBLANES, LANES), F32),
                   jax.ShapeDtypeStruct((SSD_N_GROUPS, SUBLANES, LANES), F32),
                   jax.ShapeDtypeStruct((N_XS_BLK, SUBLANES, LANES), F32),
                   jax.ShapeDtypeStruct((SSD_N_GROUPS, SUBLANES, LANES), F32)],
        scratch_shapes=[pltpu.VMEM((4, LANES, LANES), F32), pltpu.VMEM((6, SUBLANES, LANES), F32),
                        pltpu.VMEM((6, L, LANES), F32), pltpu.VMEM((6, L, LANES), F32), pltpu.VMEM((6, L, LANES), F32),
                        mat(), mat(), mat(), mat(), mat(), mat(), mat()],
        compiler_params=_params("arbitrary", "arbitrary"),
    )(xr, xr, xr, xr, xr, xr, dtr, dy, sprev, cwb, cwb, cwb, dtb, alog, dskip,
      pair_sel, head_sel, head_sel_t, tril, triu)


def _ssd_gate_fwd(x, y, z, gnw, wo):
    t, d = x.shape
    tm = min(TOKEN_TILE, t)
    nb = N_XS_BLK
    per = nb // SSD_N_GROUPS

    def body(x_ref, y_ref, z_ref, gnw_ref, wo_ref, o_ref, gn_ref):
        gs = []
        for j in range(nb):
            zv = z_ref[j]
            gs.append(y_ref[j] * (zv * _sigmoid(zv)))
        for q in range(SSD_N_GROUPS):
            ss = sum(jnp.sum(gs[j] * gs[j], axis=1, keepdims=True) for j in range(q * per, (q + 1) * per))
            inv = lax.rsqrt(ss / (per * LANES) + RMS_EPS)
            for j in range(q * per, (q + 1) * per):
                gn_ref[:, j * LANES:(j + 1) * LANES] = ((gs[j] * inv) * gnw_ref[j]).astype(BF16)
        o_ref[...] = x_ref[...] + _dot(gn_ref[...], wo_ref[...])

    blk = pl.BlockSpec((nb, tm, LANES), lambda i: (0, i, 0))
    return pl.pallas_call(
        body, name="ssd_gate_fwd", grid=(t // tm,),
        in_specs=[pl.BlockSpec((tm, d), lambda i: (i, 0)), blk, blk, _resident((nb, 1, LANES)),
                  _resident((SSD_D_INNER, d))],
        out_specs=[pl.BlockSpec((tm, d), lambda i: (i, 0)), pl.BlockSpec((tm, SSD_D_INNER), lambda i: (i, 0))],
        out_shape=[jax.ShapeDtypeStruct((t, d), F32), jax.ShapeDtypeStruct((t, SSD_D_INNER), BF16)],
        compiler_params=_params("parallel"),
    )(x, y, z, gnw, wo)


def _ssd_gate_bwd(dy, y, z, gnw, wo):
    t, d = dy.shape
    tm = min(TOKEN_TILE, t)
    nb = N_XS_BLK
    per = nb // SSD_N_GROUPS

    def body(dy_ref, y_ref, z_ref, gnw_ref, wo_ref, dys_ref, dz_ref, dgnw_ref):
        @pl.when(pl.program_id(0) == 0)
        def _():
            dgnw_ref[...] = jnp.zeros_like(dgnw_ref)

        dgn = _dot_nt(dy_ref[...].astype(BF16), wo_ref[...])
        for q in range(SSD_N_GROUPS):
            js = range(q * per, (q + 1) * per)
            gs, sil, dsil = {}, {}, {}
            for j in js:
                zv = z_ref[j]
                sig = _sigmoid(zv)
                sil[j] = zv * sig
                dsil[j] = sig * (1.0 + zv * (1.0 - sig))
                gs[j] = y_ref[j] * sil[j]
            ss = sum(jnp.sum(gs[j] * gs[j], axis=1, keepdims=True) for j in js)
            inv = lax.rsqrt(ss / (per * LANES) + RMS_EPS)
            ghat = {j: gs[j] * inv for j in js}
            dgh = {}
            for j in js:
                dj = dgn[:, j * LANES:(j + 1) * LANES]
                dgnw_ref[j] += jnp.sum(dj * ghat[j], axis=0, keepdims=True)
                dgh[j] = dj * gnw_ref[j]
            mean = sum(jnp.sum(dgh[j] * ghat[j], axis=1, keepdims=True) for j in js) / (per * LANES)
            for j in js:
                dg = inv * (dgh[j] - ghat[j] * mean)
                dys_ref[j] = dg * sil[j]
                dz_ref[j] = (dg * y_ref[j] * dsil[j]).astype(BF16)

    blk = pl.BlockSpec((nb, tm, LANES), lambda i: (0, i, 0))
    return pl.pallas_call(
        body, name="ssd_gate_bwd", grid=(t // tm,),
        in_specs=[pl.BlockSpec((tm, d), lambda i: (i, 0)), blk, blk, _resident((nb, 1, LANES)),
                  _resident((SSD_D_INNER, d))],
        out_specs=[blk, blk, pl.BlockSpec((nb, 1, LANES), lambda i: (0, 0, 0))],
        out_shape=[jax.ShapeDtypeStruct((nb, t, LANES), F32), jax.ShapeDtypeStruct((nb, t, LANES), BF16),
                   jax.ShapeDtypeStruct((nb, 1, LANES), F32)],
        compiler_params=_params("arbitrary"),
    )(dy, y, z, gnw, wo)


def _lane_blocks(v):
    r, n = v.shape[0], v.shape[1] // LANES
    return v.reshape(r, n, LANES).transpose(1, 0, 2)


def _ssd_prep(w_in, conv_w, conv_b, dt_bias, a_log, d_skip, norm_w):
    w_in_pad = jnp.pad(w_in, ((0, 0), (0, SSD_IN_PAD - SSD_IN_DIM)))
    taps = jnp.concatenate([conv_w, conv_b[None], jnp.zeros((SUBLANES - SSD_CONV_W - 1, SSD_CONV_DIM), F32)], axis=0)
    cwb = _lane_blocks(taps)
    row = lambda v: jnp.pad(v, (0, LANES - SSD_N_HEADS))[None]
    dskip = jnp.broadcast_to(jnp.repeat(d_skip, SSD_D_INNER // SSD_N_HEADS).reshape(N_XS_BLK, 1, LANES),
                             (N_XS_BLK, SUBLANES, LANES))
    gnw = norm_w.reshape(N_XS_BLK, 1, LANES)
    return w_in_pad, cwb, row(dt_bias), row(a_log), dskip, gnw


def _ssd_layer_fwd(x, nw, prm, wo, consts):
    w_in_pad, cwb, dtb, alog, dskip, gnw = prm
    z, xr, dtr = _ssd_inproj(x, nw, w_in_pad)
    y, sprev = _ssd_scan_fwd(xr, dtr, cwb, dtb, alog, dskip, consts)
    out, gn = _ssd_gate_fwd(x, y, z, gnw, wo)
    return out, (z, xr, dtr, y, sprev, gn)


def _ssd_layer_bwd(x, dy, nw, prm, wo, consts, saved):
    w_in_pad, cwb, dtb, alog, dskip, gnw = prm
    z, xr, dtr, y, sprev, gn = saved
    dys, dz, dgnw = _ssd_gate_bwd(dy, y, z, gnw, wo)
    dwo = _matmul_tn(gn, dy, name="wgrad_ssd_out")
    dxs, db, dc, ddtr4, dcwx, dcwb, dcwc, dd, dsm = _ssd_scan_bwd(xr, dtr, dys, sprev, cwb, dtb, alog, dskip, consts)
    ddtr = jnp.sum(ddtr4, axis=0).astype(BF16)[None]
    pieces = [dz, dxs, db, dc, ddtr]
    dx, dnw, h = _inproj_bwd(x, dy, nw, w_in_pad, pieces)
    dw_in = jnp.concatenate([_matmul_tn_blocked(h, p, name=f"wgrad_ssd_in{i}") for i, p in enumerate(pieces)],
                            axis=1)[:, :SSD_IN_DIM]
    dtaps = jnp.concatenate([dcwx, dcwb, dcwc], axis=0).transpose(1, 0, 2).reshape(SUBLANES, SSD_CONV_DIM)
    dsm = jnp.sum(dsm, axis=0)
    d_d = jnp.sum(dd[:, 0, :].reshape(SSD_N_HEADS, SSD_D_INNER // SSD_N_HEADS), axis=1)
    return dx, (dnw, dw_in, dtaps[:SSD_CONV_W], dtaps[SSD_CONV_W], dsm[0, :SSD_N_HEADS], dsm[1, :SSD_N_HEADS],
                d_d, dgnw.reshape(SSD_D_INNER), dwo)


def _loss_head(x, fw, target):
    t, d = x.shape
    tm = min(TOKEN_TILE, t)

    def body(x_ref, fw_ref, tgt_ref, loss_ref, dx_ref, dfw_ref):
        fwv = fw_ref[...]
        y, xhat, inv = _rms_fwd(x_ref[...], fwv)
        err = y - tgt_ref[...]
        tot = jnp.sum(jnp.sum(err * err, axis=1, keepdims=True), axis=0, keepdims=True)
        dx, dw = _rms_bwd(err * (1.0 / d), xhat, inv, fwv)
        dx_ref[...] = dx

        @pl.when(pl.program_id(0) == 0)
        def _():
            loss_ref[...] = jnp.zeros_like(loss_ref)
            dfw_ref[...] = jnp.zeros_like(dfw_ref)

        loss_ref[...] += jnp.broadcast_to(tot * (0.5 / d), loss_ref.shape)
        dfw_ref[...] += dw

    tok = pl.BlockSpec((tm, d), lambda i: (i, 0))
    return pl.pallas_call(
        body, name="loss_head", grid=(t // tm,),
        in_specs=[tok, _resident((1, d)), tok],
        out_specs=[pl.BlockSpec((1, LANES), lambda i: (0, 0)), tok, pl.BlockSpec((1, d), lambda i: (0, 0))],
        out_shape=[jax.ShapeDtypeStruct((1, LANES), F32), jax.ShapeDtypeStruct((t, d), F32),
                   jax.ShapeDtypeStruct((1, d), F32)],
        compiler_params=_params("arbitrary"),
    )(x, fw, target)


def _row_tile(rows, cap):
    best = SUBLANES
    for r in range(SUBLANES, min(rows, cap) + 1, SUBLANES):
        if rows % r == 0:
            best = r
    return best


def _adamw(w, g, m, v, name):
    rows, cols = w.shape
    br = _row_tile(rows, 256)
    c1 = 1.0 - ADAM_B1 ** ADAM_STEP
    c2 = 1.0 - ADAM_B2 ** ADAM_STEP

    def body(w_ref, g_ref, m_ref, v_ref, d_ref, nm_ref, nv_ref):
        gv = g_ref[...]
        nm = ADAM_B1 * m_ref[...] + (1.0 - ADAM_B1) * gv
        nv = ADAM_B2 * v_ref[...] + (1.0 - ADAM_B2) * (gv * gv)
        nm_ref[...] = nm
        nv_ref[...] = nv
        d_ref[...] = -ADAM_LR * ((nm / c1) / (jnp.sqrt(nv / c2) + ADAM_EPS) + ADAM_WD * w_ref[...])

    blk = pl.BlockSpec((br, cols), lambda i: (i, 0))
    shp = jax.ShapeDtypeStruct((rows, cols), F32)
    return pl.pallas_call(
        body, name=name, grid=(rows // br,), in_specs=[blk] * 4, out_specs=[blk] * 3, out_shape=[shp] * 3,
        compiler_params=_params("parallel"),
    )(w, g, m, v)


def _sum_leading(a, name):
    k, rows, cols = a.shape
    br = _row_tile(rows, 512)

    def body(a_ref, o_ref):
        acc = a_ref[0]
        for i in range(1, k):
            acc = acc + a_ref[i]
        o_ref[...] = acc

    return pl.pallas_call(
        body, name=name, grid=(rows // br,),
        in_specs=[pl.BlockSpec((k, br, cols), lambda i: (0, i, 0))],
        out_specs=pl.BlockSpec((br, cols), lambda i: (i, 0)),
        out_shape=jax.ShapeDtypeStruct((rows, cols), F32),
        compiler_params=_params("parallel"),
    )(a)


def _place():
    x, y, c = lax.axis_index("x"), lax.axis_index("y"), lax.axis_index("c")
    return x, y, c, [(1 - x, y), (x, 1 - y), (1 - x, 1 - y)]


def _remote(src, dst, send_sems, recv_sems, k, to):
    return pltpu.make_async_remote_copy(src_ref=src, dst_ref=dst, send_sem=send_sems.at[k], recv_sem=recv_sems.at[k],
                                        device_id=to, device_id_type=MESH)


def _all_gather_shards(arrs):
    n = len(arrs)

    def body(*refs):
        srcs, dsts = refs[:n], refs[n:2 * n]
        send_sems, recv_sems, local_sems = refs[2 * n:]
        x, y, c, chips = _place()
        me = 2 * x + y
        sibling = (x, y, 1 - c)
        own, sent = [], []
        for oi, (src, dst) in enumerate(zip(srcs, dsts)):
            own.append(pltpu.make_async_copy(src, dst.at[me], local_sems.at[oi]))
            own[-1].start()
            for j, chip in enumerate(chips):
                sent.append(_remote(src.at[c], dst.at[me, c], send_sems, recv_sems, 6 * oi + j, (*chip, c)))
                sent[-1].start()
        for oi, dst in enumerate(dsts):
            for j, chip in enumerate(chips):
                landed = dst.at[2 * chip[0] + chip[1], c]
                _remote(landed, landed, send_sems, recv_sems, 6 * oi + j, (*chip, c)).wait_recv()
                sent.append(_remote(landed, landed, send_sems, recv_sems, 6 * oi + 3 + j, sibling))
                sent[-1].start()
        for oi, dst in enumerate(dsts):
            for j, chip in enumerate(chips):
                landed = dst.at[2 * chip[0] + chip[1], 1 - c]
                _remote(landed, landed, send_sems, recv_sems, 6 * oi + 3 + j, sibling).wait_recv()
        for cp in sent:
            cp.wait_send()
        for cp in own:
            cp.wait()

    return pl.pallas_call(
        body, name="all_gather_shards",
        in_specs=[_HBM] * n, out_specs=[_HBM] * n,
        out_shape=[jax.ShapeDtypeStruct((N_SHARDS,) + a.shape, a.dtype) for a in arrs],
        scratch_shapes=[pltpu.SemaphoreType.DMA((6 * n,)), pltpu.SemaphoreType.DMA((6 * n,)),
                        pltpu.SemaphoreType.DMA((n,))],
    )(*arrs)


def _swap_halves(arrs):
    n = len(arrs)

    def body(*refs):
        srcs, dsts = refs[:n], refs[n:2 * n]
        send_sems, recv_sems = refs[2 * n:]
        x, y, c, _ = _place()
        cps = [_remote(src.at[:, 1 - c], dst, send_sems, recv_sems, oi, (x, y, 1 - c))
               for oi, (src, dst) in enumerate(zip(srcs, dsts))]
        for cp in cps:
            cp.start()
        for cp in cps:
            cp.wait()

    return pl.pallas_call(
        body, name="swap_halves", in_specs=[_HBM] * n, out_specs=[_HBM] * n,
        out_shape=[jax.ShapeDtypeStruct((a.shape[0],) + a.shape[2:], a.dtype) for a in arrs],
        scratch_shapes=[pltpu.SemaphoreType.DMA((n,)), pltpu.SemaphoreType.DMA((n,))],
    )(*arrs)


def _scatter_to_chips(arrs):
    n = len(arrs)

    def body(*refs):
        srcs, dsts = refs[:n], refs[n:2 * n]
        send_sems, recv_sems, local_sems = refs[2 * n:]
        x, y, c, chips = _place()
        me = 2 * x + y
        own, sent = [], []
        for oi, (src, dst) in enumerate(zip(srcs, dsts)):
            own.append(pltpu.make_async_copy(src.at[me], dst.at[me], local_sems.at[oi]))
            own[-1].start()
            for j, chip in enumerate(chips):
                sent.append(_remote(src.at[2 * chip[0] + chip[1]], dst.at[me], send_sems, recv_sems, 3 * oi + j,
                                    (*chip, c)))
                sent[-1].start()
        for oi, dst in enumerate(dsts):
            for j, chip in enumerate(chips):
                landed = dst.at[2 * chip[0] + chip[1]]
                _remote(landed, landed, send_sems, recv_sems, 3 * oi + j, (*chip, c)).wait_recv()
        for cp in sent:
            cp.wait_send()
        for cp in own:
            cp.wait()

    return pl.pallas_call(
        body, name="scatter_to_chips", in_specs=[_HBM] * n, out_specs=[_HBM] * n,
        out_shape=[jax.ShapeDtypeStruct(a.shape, a.dtype) for a in arrs],
        scratch_shapes=[pltpu.SemaphoreType.DMA((3 * n,)), pltpu.SemaphoreType.DMA((3 * n,)),
                        pltpu.SemaphoreType.DMA((n,))],
    )(*arrs)


def _join_halves(arrs):
    n = len(arrs)

    def body(*refs):
        srcs, dsts = refs[:n], refs[n:2 * n]
        send_sems, recv_sems, local_sems = refs[2 * n:]
        x, y, c, _ = _place()
        sibling = (x, y, 1 - c)
        own, sent = [], []
        for oi, (src, dst) in enumerate(zip(srcs, dsts)):
            own.append(pltpu.make_async_copy(src, dst.at[c], local_sems.at[oi]))
            own[-1].start()
            sent.append(_remote(src, dst.at[c], send_sems, recv_sems, oi, sibling))
            sent[-1].start()
        for oi, dst in enumerate(dsts):
            other = dst.at[1 - c]
            _remote(other, other, send_sems, recv_sems, oi, sibling).wait_recv()
        for cp in sent:
            cp.wait_send()
        for cp in own:
            cp.wait()

    return pl.pallas_call(
        body, name="join_halves", in_specs=[_HBM] * n, out_specs=[_HBM] * n,
        out_shape=[jax.ShapeDtypeStruct((2,) + a.shape, a.dtype) for a in arrs],
        scratch_shapes=[pltpu.SemaphoreType.DMA((n,)), pltpu.SemaphoreType.DMA((n,)), pltpu.SemaphoreType.DMA((n,))],
    )(*arrs)


def _add_halves(full, recv, core):
    n, _, rows, cols = full.shape
    br = _row_tile(rows, 512)

    def body(c_ref, a_ref, b_ref, o_ref):
        o_ref[...] = a_ref[...] + b_ref[...]

    grid_spec = pltpu.PrefetchScalarGridSpec(
        num_scalar_prefetch=1, grid=(n, rows // br),
        in_specs=[pl.BlockSpec((None, None, br, cols), lambda s, i, c_ref: (s, c_ref[0], i, 0)),
                  pl.BlockSpec((None, br, cols), lambda s, i, c_ref: (s, i, 0))],
        out_specs=pl.BlockSpec((None, br, cols), lambda s, i, c_ref: (s, i, 0)))
    return pl.pallas_call(
        body, name="add_halves", grid_spec=grid_spec, out_shape=jax.ShapeDtypeStruct((n, rows, cols), F32),
        compiler_params=_params("parallel", "parallel"),
    )(core, full, recv)


WEIGHTS = ("norm_w", "ffn_w_gate", "ffn_w_up", "ffn_w_down", "ssd_w_in", "ssd_conv_w", "ssd_conv_b", "ssd_dt_bias",
           "ssd_a_log", "ssd_d", "ssd_norm_w", "ssd_w_out", "sc_w_in", "sc_conv_w", "sc_w_out", "final_norm_w")
BIG = (("ffn_w_gate", 3), ("ffn_w_up", 3), ("ffn_w_down", 2), ("ssd_w_in", 2), ("ssd_w_out", 1), ("sc_w_in", 2),
       ("sc_w_out", 1))
SMALL_SHARDED = (("norm_w", 2), ("ssd_conv_w", 2), ("sc_conv_w", 2))
REPLICATED = ("ssd_conv_b", "ssd_dt_bias", "ssd_a_log", "ssd_d", "ssd_norm_w", "final_norm_w")
FLAT_COLS = 1024


def _pack(arrays, row_multiple, lead=()):
    flat = jnp.concatenate([a.reshape(lead + (-1,)) for a in arrays], axis=len(lead))
    unit = row_multiple * FLAT_COLS
    n = flat.shape[-1]
    pad = (-n) % unit
    if pad:
        flat = jnp.pad(flat, [(0, 0)] * len(lead) + [(0, pad)])
    return flat.reshape(lead + (-1, FLAT_COLS))


def _unpack(flat, shapes, lead=()):
    flat = flat.reshape(lead + (-1,))
    out, off = [], 0
    for shp in shapes:
        n = 1
        for s in shp:
            n *= s
        out.append(flat[..., off:off + n].reshape(lead + tuple(shp)))
        off += n
    return out


def _to_shards(full, axis):
    shp = full.shape
    r = full.reshape(shp[:axis] + (N_SHARDS, shp[axis] // N_SHARDS) + shp[axis + 1:])
    return jnp.moveaxis(r, axis, 0)


def _from_shards(sh, axis):
    r = jnp.moveaxis(sh, 0, axis)
    shp = r.shape
    return r.reshape(shp[:axis] + (shp[axis] * shp[axis + 1],) + shp[axis + 2:])


def _forward_backward(x, target, p):
    consts = _ssd_consts()
    nw = p["norm_w"]
    row = lambda v: v[None]
    ssd_prm = [_ssd_prep(p["ssd_w_in"][j], p["ssd_conv_w"][j], p["ssd_conv_b"][j], p["ssd_dt_bias"][j],
                         p["ssd_a_log"][j], p["ssd_d"][j], p["ssd_norm_w"][j]) for j in range(2)]
    sc_cw = [jnp.pad(p["sc_conv_w"][j], ((0, SUBLANES - SC_CONV_W), (0, 0))) for j in range(2)]
    ffn = lambda i, k: (p["ffn_w_gate"][i, k], p["ffn_w_up"][i, k], p["ffn_w_down"][i, k])

    xin, saved = [], []
    for i in range(N_LAYERS):
        j = i // 2
        xin.append(x)
        x = _ffn_fwd(x, row(nw[i, 0]), *ffn(i, 0))
        xin.append(x)
        if i % 2 == 0:
            x, sv = _ssd_layer_fwd(x, row(nw[i, 1]), ssd_prm[j], p["ssd_w_out"][j], consts)
        else:
            sv = _norm_mm(x, row(nw[i, 1]), p["sc_w_in"][j])
            x = _sc_fwd(x, sv, sc_cw[j], p["sc_w_out"][j])
        saved.append(sv)
        xin.append(x)
        x = _ffn_fwd(x, row(nw[i, 2]), *ffn(i, 1))
    loss, dx, dfw = _loss_head(x, row(p["final_norm_w"]), target)

    g_nw = [[None] * 3 for _ in range(N_LAYERS)]
    g_ffn = {}
    g_ssd = [None, None]
    g_sc = [None, None]

    def ffn_bwd(i, k, slot, dy):
        wg, wu, wd = ffn(i, k)
        dxn, dnw, h, a, dg, du = _ffn_bwd(xin[3 * i + slot], dy, row(nw[i, slot]), wg, wu, wd)
        g_nw[i][slot] = dnw[0]
        for n, lhs, rhs, scale in (("ffn_w_gate", h, dg, 1.0), ("ffn_w_up", h, du, 1.0), ("ffn_w_down", a, dy, 0.5)):
            g_ffn[n] = _matmul_tn(lhs, rhs, scale=scale, name="wgrad_" + n, slab=(i, k), stack=(N_LAYERS, 2),
                                  buf=g_ffn.get(n))
        return dxn

    for i in reversed(range(N_LAYERS)):
        j = i // 2
        dx = ffn_bwd(i, 1, 2, dx)
        xm = xin[3 * i + 1]
        if i % 2 == 0:
            dx, gs = _ssd_layer_bwd(xm, dx, row(nw[i, 1]), ssd_prm[j], p["ssd_w_out"][j], consts, saved[i])
            g_nw[i][1] = gs[0][0]
            g_ssd[j] = gs[1:]
        else:
            bcu = saved[i]
            dbcu, pin, dcw = _sc_bwd(dx, bcu, sc_cw[j], p["sc_w_out"][j])
            dwo = _matmul_tn(pin, dx, name="wgrad_sc_out")
            dx, dnw, h = _inproj_bwd(xm, dx, row(nw[i, 1]), p["sc_w_in"][j], [dbcu])
            g_nw[i][1] = dnw[0]
            g_sc[j] = (_matmul_tn(h, dbcu, name="wgrad_sc_in"), dcw[:SC_CONV_W], dwo)
        dx = ffn_bwd(i, 0, 0, dx)

    g = {"norm_w": jnp.stack([jnp.stack(r) for r in g_nw]), "final_norm_w": dfw[0], **g_ffn}
    for k, n in enumerate(("ssd_w_in", "ssd_conv_w", "ssd_conv_b", "ssd_dt_bias", "ssd_a_log", "ssd_d", "ssd_norm_w",
                           "ssd_w_out")):
        g[n] = jnp.stack([g_ssd[0][k], g_ssd[1][k]])
    for k, n in enumerate(("sc_w_in", "sc_conv_w", "sc_w_out")):
        g[n] = jnp.stack([g_sc[0][k], g_sc[1][k]])
    return loss, dx, g


def kernel(x, norm_w, ffn_w_gate, ffn_w_up, ffn_w_down, ssd_w_in, ssd_conv_w, ssd_conv_b, ssd_dt_bias, ssd_a_log, ssd_d, ssd_norm_w, ssd_w_out, sc_w_in, sc_conv_w, sc_w_out, final_norm_w, loss_target, m_norm_w, m_ffn_w_gate, m_ffn_w_up, m_ffn_w_down, m_ssd_w_in, m_ssd_conv_w, m_ssd_conv_b, m_ssd_dt_bias, m_ssd_a_log, m_ssd_d, m_ssd_norm_w, m_ssd_w_out, m_sc_w_in, m_sc_conv_w, m_sc_w_out, m_final_norm_w, v_norm_w, v_ffn_w_gate, v_ffn_w_up, v_ffn_w_down, v_ssd_w_in, v_ssd_conv_w, v_ssd_conv_b, v_ssd_dt_bias, v_ssd_a_log, v_ssd_d, v_ssd_norm_w, v_ssd_w_out, v_sc_w_in, v_sc_conv_w, v_sc_w_out, v_final_norm_w):
    w = dict(zip(WEIGHTS, (norm_w, ffn_w_gate, ffn_w_up, ffn_w_down, ssd_w_in, ssd_conv_w, ssd_conv_b, ssd_dt_bias,
                           ssd_a_log, ssd_d, ssd_norm_w, ssd_w_out, sc_w_in, sc_conv_w, sc_w_out, final_norm_w)))
    m = dict(zip(WEIGHTS, (m_norm_w, m_ffn_w_gate, m_ffn_w_up, m_ffn_w_down, m_ssd_w_in, m_ssd_conv_w, m_ssd_conv_b,
                           m_ssd_dt_bias, m_ssd_a_log, m_ssd_d, m_ssd_norm_w, m_ssd_w_out, m_sc_w_in, m_sc_conv_w,
                           m_sc_w_out, m_final_norm_w)))
    v = dict(zip(WEIGHTS, (v_norm_w, v_ffn_w_gate, v_ffn_w_up, v_ffn_w_down, v_ssd_w_in, v_ssd_conv_w, v_ssd_conv_b,
                           v_ssd_dt_bias, v_ssd_a_log, v_ssd_d, v_ssd_norm_w, v_ssd_w_out, v_sc_w_in, v_sc_conv_w,
                           v_sc_w_out, v_final_norm_w)))
    core = lax.axis_index("c").astype(jnp.int32).reshape(1)
    big_names = [n for n, _ in BIG]
    small_names = [n for n, _ in SMALL_SHARDED] + list(REPLICATED)
    halved = lambda a, lead=(): a.reshape(lead + (2, -1, a.shape[-1]))

    gathered = _all_gather_shards([halved(w[n].astype(BF16)) for n in big_names]
                                  + [halved(_pack([w[n] for n, _ in SMALL_SHARDED], 2 * SUBLANES))])
    p = {n: w[n] for n in REPLICATED}
    for (n, ax), sh in zip(BIG, gathered):
        p[n] = _from_shards(sh.reshape((N_SHARDS,) + w[n].shape), ax)
    for (n, ax), sh in zip(SMALL_SHARDED, _unpack(gathered[-1], [w[n].shape for n, _ in SMALL_SHARDED], lead=(N_SHARDS,))):
        p[n] = _from_shards(sh, ax)

    t, d = x.shape[-2:]
    loss, dx, g = _forward_backward(x.reshape(t, d), loss_target.reshape(t, d), p)

    small_part = _pack([_to_shards(g[n], ax) for n, ax in SMALL_SHARDED]
                       + [jnp.broadcast_to(g[n][None], (N_SHARDS,) + g[n].shape) for n in REPLICATED],
                       4 * SUBLANES, lead=(N_SHARDS,))
    parts = [halved(_to_shards(g[n], ax), lead=(N_SHARDS,)) for n, ax in BIG] + [halved(small_part, lead=(N_SHARDS,))]
    chip_sums = [_add_halves(a, r, core) for a, r in zip(parts, _swap_halves(parts))]
    reduced = _join_halves([_sum_leading(s, name="sum_chips") for s in _scatter_to_chips(chip_sums)])

    grad = {n: r.reshape(w[n].shape) for n, r in zip(big_names, reduced)}
    g_small = reduced[-1].reshape(-1, FLAT_COLS)
    grad.update(zip(small_names, _unpack(g_small, [w[n].shape for n in small_names])))

    delta, new_m, new_v = {}, {}, {}
    for n in big_names:
        shp = w[n].shape
        as2d = lambda a: a.reshape(-1, shp[-1])
        out = _adamw(as2d(w[n]), as2d(grad[n]), as2d(m[n]), as2d(v[n]), name="adamw_" + n)
        delta[n], new_m[n], new_v[n] = (o.reshape(shp) for o in out)
    packed = [_pack([s[n] for n in small_names], 4 * SUBLANES) for s in (w, m, v)]
    out = _adamw(packed[0], g_small, packed[1], packed[2], name="adamw_small")
    shapes = [w[n].shape for n in small_names]
    for dst, o in zip((delta, new_m, new_v), out):
        dst.update(zip(small_names, _unpack(o, shapes)))

    loss = lax.psum(loss[0, 0], ("x", "y", "c"))
    return (loss, dx.reshape(x.shape), *[grad[n] for n in WEIGHTS], *[delta[n] for n in WEIGHTS],
            *[new_m[n] for n in WEIGHTS], *[new_v[n] for n in WEIGHTS])
```

```python
import functools

import jax
import jax.numpy as jnp
from jax import lax
from jax.experimental import pallas as pl
from jax.experimental.pallas import tpu as pltpu

F32 = jnp.float32
BF16 = jnp.bfloat16
MESH = pl.DeviceIdType.MESH

RMS_EPS = 1e-5
D_MODEL = 1024
D_FF = 2816
N_LAYERS = 4
SSD_D_INNER = 2048
SSD_N_HEADS = 32
SSD_N_GROUPS = 4
SSD_D_STATE = 128
SSD_CHUNK = 128
SSD_CONV_W = 4
SSD_CONV_DIM = 3072
SSD_IN_DIM = 5152
SC_CONV_W = 3
LANES = 128
SUBLANES = 8
N_XS_BLK = SSD_D_INNER // LANES
SSD_IN_PAD = SSD_D_INNER + SSD_CONV_DIM + SSD_N_GROUPS * LANES
VMEM_LIMIT = 56 * 2**20
TOKEN_TILE = 512
FF_CHUNK = 256
N_SHARDS = 4

ADAM_LR = 0.001
ADAM_B1 = 0.9
ADAM_B2 = 0.999
ADAM_EPS = 1e-08
ADAM_WD = 0.01
ADAM_STEP = 10


_HBM = pl.BlockSpec(memory_space=pl.ANY)


def _params(*sem):
    return pltpu.CompilerParams(dimension_semantics=sem if sem else None, vmem_limit_bytes=VMEM_LIMIT)


def _dot(a, b):
    return jnp.dot(a, b, preferred_element_type=F32)


def _dot_nt(a, b):
    return lax.dot_general(a, b, (((1,), (1,)), ((), ())), preferred_element_type=F32)


def _dot_tn(a, b):
    return lax.dot_general(a, b, (((0,), (0,)), ((), ())), preferred_element_type=F32)


def _resident(shape):
    n = len(shape)
    return pl.BlockSpec(shape, lambda *_: (0,) * n, pipeline_mode=pl.Buffered(1))


def _split3(v):
    hi = v.astype(BF16)
    r1 = v - hi.astype(F32)
    mid = r1.astype(BF16)
    lo = (r1 - mid.astype(F32)).astype(BF16)
    return hi, mid, lo


def _sel_left(sel, v3):
    return _dot(sel, v3[0]) + _dot(sel, v3[1]) + _dot(sel, v3[2])


def _sigmoid(v):
    return 1.0 / (1.0 + jnp.exp(-v))


def _rms_fwd(x, w):
    inv = lax.rsqrt(jnp.mean(x * x, axis=-1, keepdims=True) + RMS_EPS)
    xhat = x * inv
    return xhat * w, xhat, inv


def _rms_bwd(dh, xhat, inv, w):
    dxhat = dh * w
    dx = inv * (dxhat - xhat * jnp.mean(dxhat * xhat, axis=-1, keepdims=True))
    return dx, jnp.sum(dh * xhat, axis=0, keepdims=True)


def _ffn_fwd(x, nw, wg, wu, wd):
    t, d = x.shape
    f = wg.shape[1]
    tm = min(TOKEN_TILE, t)

    def body(x_ref, nw_ref, wg_ref, wu_ref, wd_ref, o_ref):
        xv = x_ref[...]
        h = _rms_fwd(xv, nw_ref[...])[0].astype(BF16)
        acc = jnp.zeros((tm, d), F32)
        for j in range(f // FF_CHUNK):
            sl = slice(j * FF_CHUNK, (j + 1) * FF_CHUNK)
            g = _dot(h, wg_ref[:, sl])
            u = _dot(h, wu_ref[:, sl])
            a = (g * _sigmoid(g) * u).astype(BF16)
            acc = acc + _dot(a, wd_ref[sl, :])
        o_ref[...] = xv + 0.5 * acc

    return pl.pallas_call(
        body, name="ffn_fwd", grid=(t // tm,),
        in_specs=[pl.BlockSpec((tm, d), lambda i: (i, 0)), _resident((1, d)), _resident((d, f)),
                  _resident((d, f)), _resident((f, d))],
        out_specs=pl.BlockSpec((tm, d), lambda i: (i, 0)),
        out_shape=jax.ShapeDtypeStruct((t, d), F32),
        compiler_params=_params("parallel"),
    )(x, nw, wg, wu, wd)


def _ffn_bwd(x, dy, nw, wg, wu, wd):
    t, d = x.shape
    f = wg.shape[1]
    tm = min(TOKEN_TILE, t)

    def body(x_ref, dy_ref, nw_ref, wg_ref, wu_ref, wd_ref, dx_ref, dnw_ref, h_ref, a_ref, dg_ref, du_ref):
        xv = x_ref[...]
        dyv = dy_ref[...]
        nwv = nw_ref[...]
        hf, xhat, inv = _rms_fwd(xv, nwv)
        h = hf.astype(BF16)
        h_ref[...] = h
        dob = (0.5 * dyv).astype(BF16)
        dh = jnp.zeros((tm, d), F32)
        for j in range(f // FF_CHUNK):
            sl = slice(j * FF_CHUNK, (j + 1) * FF_CHUNK)
            g = _dot(h, wg_ref[:, sl])
            u = _dot(h, wu_ref[:, sl])
            sig = _sigmoid(g)
            s = g * sig
            a_ref[:, sl] = (s * u).astype(BF16)
            da = _dot_nt(dob, wd_ref[sl, :])
            dgb = (da * u * (sig * (1.0 + g * (1.0 - sig)))).astype(BF16)
            dub = (da * s).astype(BF16)
            dg_ref[:, sl] = dgb
            du_ref[:, sl] = dub
            dh = dh + _dot_nt(dgb, wg_ref[:, sl]) + _dot_nt(dub, wu_ref[:, sl])
        dx, dw = _rms_bwd(dh, xhat, inv, nwv)
        dx_ref[...] = dyv + dx

        @pl.when(pl.program_id(0) == 0)
        def _():
            dnw_ref[...] = jnp.zeros_like(dnw_ref)

        dnw_ref[...] += dw

    tok = lambda n: pl.BlockSpec((tm, n), lambda i: (i, 0))
    return pl.pallas_call(
        body, name="ffn_bwd", grid=(t // tm,),
        in_specs=[tok(d), tok(d), _resident((1, d)), _resident((d, f)), _resident((d, f)), _resident((f, d))],
        out_specs=[tok(d), pl.BlockSpec((1, d), lambda i: (0, 0)), tok(d), tok(f), tok(f), tok(f)],
        out_shape=[jax.ShapeDtypeStruct((t, d), F32), jax.ShapeDtypeStruct((1, d), F32),
                   jax.ShapeDtypeStruct((t, d), BF16), jax.ShapeDtypeStruct((t, f), BF16),
                   jax.ShapeDtypeStruct((t, f), BF16), jax.ShapeDtypeStruct((t, f), BF16)],
        compiler_params=_params("arbitrary"),
    )(x, dy, nw, wg, wu, wd)


def _pick_bn(m, n, unit):
    best = unit
    for k in range(1, n // unit + 1):
        bn = k * unit
        if n % bn == 0 and m * bn * 4 <= 8 * 2**20:
            best = bn
    return best


def _matmul_tn(a, b, scale=1.0, name="wgrad", slab=None, stack=None, buf=None):
    t, m = a.shape
    n = b.shape[1]
    bt = min(1024, t)
    bn = _pick_bn(m, n, LANES)
    nt = t // bt
    lead = tuple(slab) if slab is not None else ()

    def body(a_ref, b_ref, *rest):
        o_ref = rest[-1]

        @pl.when(pl.program_id(1) == 0)
        def _():
            o_ref[...] = jnp.zeros_like(o_ref)

        o_ref[...] += _dot_tn(a_ref[...].astype(BF16), b_ref[...].astype(BF16))
        if scale != 1.0:
            @pl.when(pl.program_id(1) == nt - 1)
            def _():
                o_ref[...] *= scale

    in_specs = [pl.BlockSpec((bt, m), lambda j, k: (k, 0)), pl.BlockSpec((bt, bn), lambda j, k: (k, j))]
    args = [a, b]
    if buf is not None:
        in_specs.append(_HBM)
        args.append(buf)
    return pl.pallas_call(
        body, name=name, grid=(n // bn, nt),
        in_specs=in_specs,
        out_specs=pl.BlockSpec((None,) * len(lead) + (m, bn), lambda j, k: lead + (0, j)),
        out_shape=jax.ShapeDtypeStruct(tuple(stack or ()) + (m, n), F32),
        input_output_aliases={2: 0} if buf is not None else {},
        compiler_params=_params("parallel", "arbitrary"),
    )(*args)


def _matmul_tn_blocked(a, b, name="wgrad_blk"):
    t, m = a.shape
    nb = b.shape[0]
    bt = min(1024, t)
    nbt = _pick_bn(m, nb * LANES, LANES) // LANES
    while nb % nbt:
        nbt -= 1

    def body(a_ref, b_ref, o_ref):
        @pl.when(pl.program_id(1) == 0)
        def _():
            o_ref[...] = jnp.zeros_like(o_ref)

        bv = jnp.concatenate([b_ref[i] for i in range(nbt)], axis=1) if nbt > 1 else b_ref[0]
        o_ref[...] += _dot_tn(a_ref[...], bv)

    return pl.pallas_call(
        body, name=name, grid=(nb // nbt, t // bt),
        in_specs=[pl.BlockSpec((bt, m), lambda j, k: (k, 0)), pl.BlockSpec((nbt, bt, LANES), lambda j, k: (j, k, 0))],
        out_specs=pl.BlockSpec((m, nbt * LANES), lambda j, k: (0, j)),
        out_shape=jax.ShapeDtypeStruct((m, nb * LANES), F32),
        compiler_params=_params("parallel", "arbitrary"),
    )(a, b)


def _norm_mm(x, nw, w):
    t, d = x.shape
    n = w.shape[1]
    tm = min(TOKEN_TILE, t)
    cn = 1024 if n % 1024 == 0 else n

    def body(x_ref, nw_ref, w_ref, o_ref):
        h = _rms_fwd(x_ref[...], nw_ref[...])[0].astype(BF16)
        for j in range(n // cn):
            sl = slice(j * cn, (j + 1) * cn)
            o_ref[:, sl] = _dot(h, w_ref[:, sl])

    return pl.pallas_call(
        body, name="norm_mm", grid=(t // tm,),
        in_specs=[pl.BlockSpec((tm, d), lambda i: (i, 0)), _resident((1, d)), _resident((d, n))],
        out_specs=pl.BlockSpec((tm, n), lambda i: (i, 0)),
        out_shape=jax.ShapeDtypeStruct((t, n), F32),
        compiler_params=_params("parallel"),
    )(x, nw, w)


def _ssd_inproj(x, nw, w):
    t, d = x.shape
    tm = min(TOKEN_TILE, t)
    nz, nx, ng = SSD_D_INNER // LANES, SSD_CONV_DIM // LANES, SSD_N_GROUPS
    cn = 1024

    def body(x_ref, nw_ref, w_ref, z_ref, xr_ref, dt_ref):
        h = _rms_fwd(x_ref[...], nw_ref[...])[0].astype(BF16)
        for j in range(-(-SSD_IN_PAD // cn)):
            lo, hi = j * cn, min((j + 1) * cn, SSD_IN_PAD)
            r = _dot(h, w_ref[:, lo:hi])
            for i in range((hi - lo) // LANES):
                blk = j * (cn // LANES) + i
                v = r[:, i * LANES:(i + 1) * LANES]
                if blk < nz:
                    z_ref[blk] = v
                elif blk < nz + nx:
                    xr_ref[blk - nz] = v
                else:
                    dt_ref[blk - nz - nx] = v

    out = lambda n: pl.BlockSpec((n, tm, LANES), lambda i: (0, i, 0))
    return pl.pallas_call(
        body, name="ssd_inproj", grid=(t // tm,),
        in_specs=[pl.BlockSpec((tm, d), lambda i: (i, 0)), _resident((1, d)), _resident((d, SSD_IN_PAD))],
        out_specs=[out(nz), out(nx), out(ng)],
        out_shape=[jax.ShapeDtypeStruct((n, t, LANES), F32) for n in (nz, nx, ng)],
        compiler_params=_params("parallel"),
    )(x, nw, w)


def _inproj_bwd(x, dy, nw, w, pieces):
    t, d = x.shape
    n = w.shape[1]
    tm = min(TOKEN_TILE, t)
    npc = len(pieces)

    def body(*refs):
        x_ref, dy_ref, nw_ref, w_ref = refs[:4]
        p_refs = refs[4:4 + npc]
        dx_ref, dnw_ref, h_ref = refs[4 + npc:]
        nwv = nw_ref[...]
        hf, xhat, inv = _rms_fwd(x_ref[...], nwv)
        h_ref[...] = hf.astype(BF16)
        parts = []
        for p in p_refs:
            if len(p.shape) == 3:
                parts += [p[i] for i in range(p.shape[0])]
            else:
                parts.append(p[...])
        dz = jnp.concatenate(parts, axis=1) if len(parts) > 1 else parts[0]
        dh = _dot_nt(dz, w_ref[...])
        dx, dw = _rms_bwd(dh, xhat, inv, nwv)
        dx_ref[...] = dy_ref[...] + dx

        @pl.when(pl.program_id(0) == 0)
        def _():
            dnw_ref[...] = jnp.zeros_like(dnw_ref)

        dnw_ref[...] += dw

    tok = lambda m: pl.BlockSpec((tm, m), lambda i: (i, 0))
    p_specs = [pl.BlockSpec((p.shape[0], tm, LANES), lambda i: (0, i, 0)) if p.ndim == 3 else tok(p.shape[1])
               for p in pieces]
    return pl.pallas_call(
        body, name="inproj_bwd", grid=(t // tm,),
        in_specs=[tok(d), tok(d), _resident((1, d)), _resident((d, n))] + p_specs,
        out_specs=[tok(d), pl.BlockSpec((1, d), lambda i: (0, 0)), tok(d)],
        out_shape=[jax.ShapeDtypeStruct((t, d), F32), jax.ShapeDtypeStruct((1, d), F32),
                   jax.ShapeDtypeStruct((t, d), BF16)],
        compiler_params=_params("arbitrary"),
    )(x, dy, nw, w, *pieces)


def _shift_down(v, j, prev8):
    if j == 0:
        return v
    r = pltpu.roll(v, j, 0)
    p = pltpu.roll(prev8, j, 0)
    rows = lax.broadcasted_iota(jnp.int32, prev8.shape, 0)
    first = jnp.where(rows < j, p, r[0:SUBLANES])
    return jnp.concatenate([first, r[SUBLANES:]], axis=0)


def _shift_up(v, j, next8):
    if j == 0:
        return v
    n = v.shape[0]
    r = pltpu.roll(v, n - j, 0)
    p = pltpu.roll(next8, SUBLANES - j, 0)
    rows = lax.broadcasted_iota(jnp.int32, next8.shape, 0)
    last = jnp.where(rows >= SUBLANES - j, p, r[n - SUBLANES:])
    return jnp.concatenate([r[:n - SUBLANES], last], axis=0)


def _sc_fwd(x, bcu, cw, wo):
    t, d = x.shape
    tm = min(TOKEN_TILE, t)
    hb = tm // SUBLANES

    def body(x_ref, bcu_ref, prev_ref, cw_ref, wo_ref, o_ref):
        bg, cg, u = bcu_ref[:, 0:d], bcu_ref[:, d:2 * d], bcu_ref[:, 2 * d:3 * d]
        q = cg * u
        qp = jnp.where(pl.program_id(0) == 0, 0.0, prev_ref[:, d:2 * d] * prev_ref[:, 2 * d:3 * d])
        cwv = cw_ref[...]
        v = cwv[2:3] * q + cwv[1:2] * _shift_down(q, 1, qp) + cwv[0:1] * _shift_down(q, 2, qp)
        o_ref[...] = x_ref[...] + _dot((bg * v).astype(BF16), wo_ref[...])

    return pl.pallas_call(
        body, name="sc_fwd", grid=(t // tm,),
        in_specs=[pl.BlockSpec((tm, d), lambda i: (i, 0)), pl.BlockSpec((tm, 3 * d), lambda i: (i, 0)),
                  pl.BlockSpec((SUBLANES, 3 * d), lambda i: (jnp.maximum(i * hb - 1, 0), 0)),
                  _resident((SUBLANES, d)), _resident((d, d))],
        out_specs=pl.BlockSpec((tm, d), lambda i: (i, 0)),
        out_shape=jax.ShapeDtypeStruct((t, d), F32),
        compiler_params=_params("parallel"),
    )(x, bcu, bcu, cw, wo)


def _sc_bwd(dy, bcu, cw, wo):
    t, d = dy.shape
    tm = min(TOKEN_TILE, t)
    hb = tm // SUBLANES
    nt = t // tm

    def body(dy_ref, dyn_ref, bcu_ref, prev_ref, next_ref, cw_ref, wo_ref, dbcu_ref, p_ref, dcw_ref):
        i = pl.program_id(0)
        bg, cg, u = bcu_ref[:, 0:d], bcu_ref[:, d:2 * d], bcu_ref[:, 2 * d:3 * d]
        q = cg * u
        qp = jnp.where(i == 0, 0.0, prev_ref[:, d:2 * d] * prev_ref[:, 2 * d:3 * d])
        cwv = cw_ref[...]
        q1 = _shift_down(q, 1, qp)
        q2 = _shift_down(q, 2, qp)
        v = cwv[2:3] * q + cwv[1:2] * q1 + cwv[0:1] * q2
        p_ref[...] = (bg * v).astype(BF16)
        wov = wo_ref[...]
        dp = _dot_nt(dy_ref[...].astype(BF16), wov)
        dpn = _dot_nt(dyn_ref[...].astype(BF16), wov)
        dv = dp * bg
        dvn = jnp.where(i == nt - 1, 0.0, dpn * next_ref[:, 0:d])
        dq = cwv[2:3] * dv + cwv[1:2] * _shift_up(dv, 1, dvn) + cwv[0:1] * _shift_up(dv, 2, dvn)
        dbcu_ref[:, 0:d] = (dp * v).astype(BF16)
        dbcu_ref[:, d:2 * d] = (dq * u).astype(BF16)
        dbcu_ref[:, 2 * d:3 * d] = (dq * cg).astype(BF16)

        @pl.when(i == 0)
        def _():
            dcw_ref[...] = jnp.zeros_like(dcw_ref)

        dcw_ref[0:1, :] += jnp.sum(dv * q2, axis=0, keepdims=True)
        dcw_ref[1:2, :] += jnp.sum(dv * q1, axis=0, keepdims=True)
        dcw_ref[2:3, :] += jnp.sum(dv * q, axis=0, keepdims=True)

    last8 = t // SUBLANES - 1
    return pl.pallas_call(
        body, name="sc_bwd", grid=(nt,),
        in_specs=[pl.BlockSpec((tm, d), lambda i: (i, 0)),
                  pl.BlockSpec((SUBLANES, d), lambda i: (jnp.minimum((i + 1) * hb, last8), 0)),
                  pl.BlockSpec((tm, 3 * d), lambda i: (i, 0)),
                  pl.BlockSpec((SUBLANES, 3 * d), lambda i: (jnp.maximum(i * hb - 1, 0), 0)),
                  pl.BlockSpec((SUBLANES, 3 * d), lambda i: (jnp.minimum((i + 1) * hb, last8), 0)),
                  _resident((SUBLANES, d)), _resident((d, d))],
        out_specs=[pl.BlockSpec((tm, 3 * d), lambda i: (i, 0)), pl.BlockSpec((tm, d), lambda i: (i, 0)),
                   pl.BlockSpec((SUBLANES, d), lambda i: (0, 0))],
        out_shape=[jax.ShapeDtypeStruct((t, 3 * d), BF16), jax.ShapeDtypeStruct((t, d), BF16),
                   jax.ShapeDtypeStruct((SUBLANES, d), F32)],
        compiler_params=_params("arbitrary"),
    )(dy, dy, bcu, bcu, bcu, cw, wo)


NEG_BIG = -1e30


HEADS_PER_GROUP = SSD_N_HEADS // SSD_N_GROUPS
PAIRS_PER_GROUP = HEADS_PER_GROUP // 2


def _ssd_consts():
    r = lax.broadcasted_iota(jnp.int32, (LANES, LANES), 0)
    c = lax.broadcasted_iota(jnp.int32, (LANES, LANES), 1)
    return (c <= r).astype(BF16), (c >= r).astype(BF16)


def _ssd_decay(dtr, dtb, alog, tril):
    shape = (SSD_CHUNK, LANES)
    lanes = lax.broadcasted_iota(jnp.int32, shape, 1)
    rows = lax.broadcasted_iota(jnp.int32, shape, 0)
    pre = dtr + dtb
    valid = lanes < HEADS_PER_GROUP
    dt = jnp.where(valid, jnp.maximum(pre, 0.0) + jnp.log(1.0 + jnp.exp(-jnp.abs(pre))), 0.0)
    a = -jnp.exp(alog)
    acs = _sel_left(tril, _split3(dt * a))
    return dt, a, acs, pre, valid, rows, lanes


def _lane_col(v, j):
    return jnp.broadcast_to(v[:, j:j + 1], v.shape)


def _ssd_pair_terms(k, dt, cols, low_half, xs):
    dtp = jnp.where(low_half, _lane_col(dt, 2 * k), _lane_col(dt, 2 * k + 1))
    acsp = jnp.where(low_half, cols[2 * k], cols[2 * k + 1])
    lastp = acsp[SSD_CHUNK - 1:SSD_CHUNK, :]
    eap = jnp.exp(acsp)
    decp = jnp.exp(lastp - acsp)
    etp = jnp.exp(lastp)
    xdt = xs * dtp
    return dtp, eap, decp, etp, xdt


def _ssd_conv_taps(cwb, xr, prev8):
    sh = [_shift_down(xr, j, prev8) for j in range(SSD_CONV_W)]
    xc = cwb[4:5]
    for j in range(SSD_CONV_W):
        xc = xc + cwb[3 - j:4 - j] * sh[j]
    return xc, sh


def _ssd_specs(nc, rev):
    ch = (lambda i: nc - 1 - i) if rev else (lambda i: i)
    L = SSD_CHUNK
    xs = pl.BlockSpec((4, L, LANES), lambda g, i: (g, ch(i), 0))
    bb = pl.BlockSpec((1, L, LANES), lambda g, i: (N_XS_BLK + g, ch(i), 0))
    cc = pl.BlockSpec((1, L, LANES), lambda g, i: (N_XS_BLK + SSD_N_GROUPS + g, ch(i), 0))
    dt = pl.BlockSpec((1, L, LANES), lambda g, i: (g, ch(i), 0))
    cw_xs = pl.BlockSpec((4, SUBLANES, LANES), lambda g, i: (g, 0, 0))
    cw_b = pl.BlockSpec((1, SUBLANES, LANES), lambda g, i: (N_XS_BLK + g, 0, 0))
    cw_c = pl.BlockSpec((1, SUBLANES, LANES), lambda g, i: (N_XS_BLK + SSD_N_GROUPS + g, 0, 0))
    st = pl.BlockSpec((1, 4, LANES, LANES), lambda g, i: (ch(i), g, 0, 0))
    grp4 = pl.BlockSpec((4, L, LANES), lambda g, i: (g, ch(i), 0))
    return xs, bb, cc, dt, cw_xs, cw_b, cw_c, st, grp4


def _ssd_scan_fwd(xr, dtr, cwb, dtb, alog, dskip, consts):
    t = xr.shape[1]
    L = SSD_CHUNK
    nc = t // L
    tril, _ = consts
    xs_s, b_s, c_s, dt_s, cwx_s, cwb_s, cwc_s, st_s, grp4 = _ssd_specs(nc, False)
    grp_row = pl.BlockSpec((1, 1, LANES), lambda g, i: (g, 0, 0))

    def body(xs_ref, b_ref, c_ref, dtr_ref, cwx_ref, cwbb_ref, cwc_ref, dtb_ref, alog_ref, dsk_ref,
             tril_ref, y_ref, sp_ref, state, tail):
        @pl.when(pl.program_id(1) == 0)
        def _():
            state[...] = jnp.zeros_like(state)
            tail[...] = jnp.zeros_like(tail)

        xa = []
        for b in range(6):
            xrb = xs_ref[b] if b < 4 else (b_ref[0] if b == 4 else c_ref[0])
            cw = cwx_ref[b] if b < 4 else (cwbb_ref[0] if b == 4 else cwc_ref[0])
            xc, _ = _ssd_conv_taps(cw, xrb, tail[b])
            tail[b] = xrb[L - SUBLANES:]
            xa.append(xc * _sigmoid(xc))

        dt, a, acs, _, _, rows, lanes = _ssd_decay(dtr_ref[0], dtb_ref[0], alog_ref[0], tril_ref[...])
        acst = acs.T
        bb = xa[4].astype(BF16)
        cb_ = xa[5].astype(BF16)
        cbm = _dot_nt(cb_, bb)
        causal = rows >= lanes
        low_half = lanes < LANES // 2
        cols = [_lane_col(acs, j) for j in range(HEADS_PER_GROUP)]

        for k in range(PAIRS_PER_GROUP):
            xs = xa[k]
            dtp, eap, decp, etp, xdt = _ssd_pair_terms(k, dt, cols, low_half, xs)
            ms = []
            for j in (2 * k, 2 * k + 1):
                diff = cols[j] - jnp.broadcast_to(acst[j:j + 1, :], (L, L))
                ms.append((cbm * jnp.exp(jnp.where(causal, diff, NEG_BIG))).astype(BF16))
            xcat = jnp.concatenate([jnp.where(low_half, xdt, 0.0).astype(BF16),
                                    jnp.where(low_half, 0.0, xdt).astype(BF16)], axis=0)
            yd = _dot(jnp.concatenate(ms, axis=1), xcat)
            sp = state[k]
            yo = eap * _dot(cb_, sp.astype(BF16))
            y_ref[k] = yd + yo + dsk_ref[k][0:1] * xs
            sp_ref[0, k] = sp
            state[k] = etp * sp + _dot_tn(bb, (decp * xdt).astype(BF16))

    return pl.pallas_call(
        body, name="ssd_scan_fwd", grid=(SSD_N_GROUPS, nc),
        in_specs=[xs_s, b_s, c_s, dt_s, cwx_s, cwb_s, cwc_s, grp_row, grp_row, cwx_s, _resident(tril.shape)],
        out_specs=[grp4, st_s],
        out_shape=[jax.ShapeDtypeStruct((N_XS_BLK, t, LANES), F32),
                   jax.ShapeDtypeStruct((nc, N_XS_BLK, LANES, LANES), F32)],
        scratch_shapes=[pltpu.VMEM((4, LANES, LANES), F32), pltpu.VMEM((6, SUBLANES, LANES), F32)],
        compiler_params=_params("arbitrary", "arbitrary"),
    )(xr, xr, xr, dtr, cwb, cwb, cwb, dtb, alog, dskip, tril)


def _ssd_scan_bwd(xr, dtr, dy, sprev, cwb, dtb, alog, dskip, consts):
    t = xr.shape[1]
    L = SSD_CHUNK
    nc = t // L
    hb = L // SUBLANES
    tril, triu = consts
    xs_s, b_s, c_s, dt_s, cwx_s, cwb_s, cwc_s, st_s, grp4 = _ssd_specs(nc, True)
    grp_row = pl.BlockSpec((1, 1, LANES), lambda g, i: (g, 0, 0))
    prev = lambda off: pl.BlockSpec(
        (4 if off is None else 1, SUBLANES, LANES),
        (lambda g, i: (g, jnp.maximum((nc - 1 - i) * hb - 1, 0), 0)) if off is None else
        (lambda g, i: (off + g, jnp.maximum((nc - 1 - i) * hb - 1, 0), 0)))
    grp1 = pl.BlockSpec((1, L, LANES), lambda g, i: (g, nc - 1 - i, 0))
    acc4 = pl.BlockSpec((4, SUBLANES, LANES), lambda g, i: (g, 0, 0))
    acc1 = pl.BlockSpec((1, SUBLANES, LANES), lambda g, i: (g, 0, 0))

    def body(xs_ref, b_ref, c_ref, pxs_ref, pb_ref, pc_ref, dtr_ref, dy_ref, sp_ref,
             cwx_ref, cwbb_ref, cwc_ref, dtb_ref, alog_ref, dsk_ref, tril_ref, triu_ref,
             dxs_ref, db_ref, dc_ref, ddtr_ref, dcwx_ref, dcwb_ref, dcwc_ref, dd_ref, dsm_ref,
             dstate, head):
        step = pl.program_id(1)
        first_chunk = step == nc - 1

        @pl.when(step == 0)
        def _():
            dstate[...] = jnp.zeros_like(dstate)
            head[...] = jnp.zeros_like(head)
            for r in (dcwx_ref, dcwb_ref, dcwc_ref, dd_ref, dsm_ref):
                r[...] = jnp.zeros_like(r)

        def blk(b):
            xrb = xs_ref[b] if b < 4 else (b_ref[0] if b == 4 else c_ref[0])
            cw = cwx_ref[b] if b < 4 else (cwbb_ref[0] if b == 4 else cwc_ref[0])
            p8 = pxs_ref[b] if b < 4 else (pb_ref[0] if b == 4 else pc_ref[0])
            return xrb, cw, jnp.where(first_chunk, 0.0, p8)

        xa, dsil = [], []
        for b in range(6):
            xrb, cw, p8 = blk(b)
            xc, _ = _ssd_conv_taps(cw, xrb, p8)
            sig = _sigmoid(xc)
            xa.append(xc * sig)
            dsil.append(sig * (1.0 + xc * (1.0 - sig)))

        dt, a, acs, pre, valid, rows, lanes = _ssd_decay(dtr_ref[0], dtb_ref[0], alog_ref[0], tril_ref[...])
        acst = acs.T
        bb = xa[4].astype(BF16)
        cb_ = xa[5].astype(BF16)
        cbm = _dot_nt(cb_, bb)
        cbmt = _dot_nt(bb, cb_)
        causal = rows >= lanes
        anti = rows <= lanes
        low_half = lanes < LANES // 2
        last_row = rows == L - 1
        cols = [_lane_col(acs, j) for j in range(HEADS_PER_GROUP)]
        zeros = jnp.zeros((L, LANES), F32)
        dcb, dcbt, dbg, dcg, dacs, dacst, ddt = zeros, zeros, zeros, zeros, zeros, zeros, zeros
        dxa = []

        for k in range(PAIRS_PER_GROUP):
            xs = xa[k]
            dtp, eap, decp, etp, xdt = _ssd_pair_terms(k, dt, cols, low_half, xs)
            xdtb = xdt.astype(BF16)
            w = decp * xdt
            wb = w.astype(BF16)
            dyv = dy_ref[k]
            sp = sp_ref[0, k]
            spb = sp.astype(BF16)
            dsn = dstate[k]
            dsnb = dsn.astype(BF16)
            yoff = eap * _dot(cb_, spb)
            dgb = (eap * dyv).astype(BF16)
            dcg = dcg + _dot_nt(dgb, spb)
            dstate[k] = _dot_tn(cb_, dgb) + etp * dsn
            last_lane = etp * jnp.sum(dsn * sp, axis=0, keepdims=True)
            dbg = dbg + _dot_nt(wb, dsnb)
            dw = _dot(bb, dsnb)
            t2 = dw * w
            dxdt = decp * dw
            last_lane = last_lane + jnp.sum(t2, axis=0, keepdims=True)
            lane_acc = dyv * yoff - t2 + jnp.where(last_row, last_lane, 0.0)
            for j in (2 * k, 2 * k + 1):
                diff = cols[j] - jnp.broadcast_to(acst[j:j + 1, :], (L, L))
                lm = jnp.exp(jnp.where(causal, diff, NEG_BIG))
                lmt = jnp.exp(jnp.where(anti, -diff, NEG_BIG))
                dye = jnp.where(low_half == (j % 2 == 0), dyv, 0.0).astype(BF16)
                dm = _dot_nt(dye, xdtb)
                dmt = _dot_nt(xdtb, dye)
                mt = cbmt * lmt
                seg = dmt * mt - dm * (cbm * lm)
                dacst = dacst + jnp.where(rows == j, jnp.sum(seg, axis=0, keepdims=True), 0.0)
                dcb = dcb + dm * lm
                dcbt = dcbt + dmt * lmt
                dxdt = dxdt + _dot(mt.astype(BF16), dye)
            ddt_lane = dxdt * xs
            for j, keep in ((2 * k, low_half), (2 * k + 1, jnp.logical_not(low_half))):
                dacs = dacs + jnp.where(lanes == j, jnp.sum(jnp.where(keep, lane_acc, 0.0), axis=1, keepdims=True), 0.0)
                ddt = ddt + jnp.where(lanes == j, jnp.sum(jnp.where(keep, ddt_lane, 0.0), axis=1, keepdims=True), 0.0)
            dxa.append(dsk_ref[k][0:1] * dyv + dxdt * dtp)
            dd_ref[k, 0:1, :] += jnp.sum(dyv * xs, axis=0, keepdims=True)

        dxa.append(dbg + _dot(dcbt.astype(BF16), cb_))
        dxa.append(dcg + _dot(dcb.astype(BF16), bb))
        dac = _sel_left(triu_ref[...], _split3(dacs + dacst.T))
        ddtr = jnp.where(valid, (ddt + dac * a) * _sigmoid(pre), 0.0)
        ddtr_ref[0] = ddtr.astype(BF16)
        dsm_ref[0, 0:1, :] += jnp.sum(ddtr, axis=0, keepdims=True)
        dsm_ref[0, 1:2, :] += jnp.sum(dac * dt, axis=0, keepdims=True) * a

        for b in range(6):
            xrb, cw, p8 = blk(b)
            sh = [_shift_down(xrb, j, p8) for j in range(SSD_CONV_W)]
            dxc = dxa[b] * dsil[b]
            acc = dcwx_ref.at[b] if b < 4 else (dcwb_ref.at[0] if b == 4 else dcwc_ref.at[0])
            acc[4:5, :] += jnp.sum(dxc, axis=0, keepdims=True)
            dxr = jnp.zeros_like(dxc)
            for j in range(SSD_CONV_W):
                acc[3 - j:4 - j, :] += jnp.sum(dxc * sh[j], axis=0, keepdims=True)
                dxr = dxr + cw[3 - j:4 - j] * _shift_up(dxc, j, head[b])
            head[b] = dxc[0:SUBLANES]
            out = dxs_ref.at[b] if b < 4 else (db_ref.at[0] if b == 4 else dc_ref.at[0])
            out[...] = dxr.astype(BF16)

    return pl.pallas_call(
        body, name="ssd_scan_bwd", grid=(SSD_N_GROUPS, nc),
        in_specs=[xs_s, b_s, c_s, prev(None), prev(N_XS_BLK), prev(N_XS_BLK + SSD_N_GROUPS), dt_s, grp4, st_s,
                  cwx_s, cwb_s, cwc_s, grp_row, grp_row, cwx_s, _resident(tril.shape), _resident(triu.shape)],
        out_specs=[grp4, grp1, grp1, grp1, acc4, acc1, acc1, acc4, acc1],
        out_shape=[jax.ShapeDtypeStruct((N_XS_BLK, t, LANES), BF16),
                   jax.ShapeDtypeStruct((SSD_N_GROUPS, t, LANES), BF16),
                   jax.ShapeDtypeStruct((SSD_N_GROUPS, t, LANES), BF16),
                   jax.ShapeDtypeStruct((SSD_N_GROUPS, t, LANES), BF16),
                   jax.ShapeDtypeStruct((N_XS_BLK, SUBLANES, LANES), F32),
                   jax.ShapeDtypeStruct((SSD_N_GROUPS, SUBLANES, LANES), F32),
                   jax.ShapeDtypeStruct((SSD_N_GROUPS, SUBLANES, LANES), F32),
                   jax.ShapeDtypeStruct((N_XS_BLK, SUBLANES, LANES), F32),
                   jax.ShapeDtypeStruct((SSD_N_GROUPS, SUBLANES, LANES), F32)],
        scratch_shapes=[pltpu.VMEM((4, LANES, LANES), F32), pltpu.VMEM((6, SUBLANES, LANES), F32)],
        compiler_params=_params("arbitrary", "arbitrary"),
    )(xr, xr, xr, xr, xr, xr, dtr, dy, sprev, cwb, cwb, cwb, dtb, alog, dskip, tril, triu)


def _ssd_gate_fwd(x, y, z, gnw, wo):
    t, d = x.shape
    tm = min(TOKEN_TILE, t)
    nb = N_XS_BLK
    per = nb // SSD_N_GROUPS

    def body(x_ref, y_ref, z_ref, gnw_ref, wo_ref, o_ref, gn_ref):
        gs = []
        for j in range(nb):
            zv = z_ref[j]
            gs.append(y_ref[j] * (zv * _sigmoid(zv)))
        for q in range(SSD_N_GROUPS):
            ss = sum(jnp.sum(gs[j] * gs[j], axis=1, keepdims=True) for j in range(q * per, (q + 1) * per))
            inv = lax.rsqrt(ss / (per * LANES) + RMS_EPS)
            for j in range(q * per, (q + 1) * per):
                gn_ref[:, j * LANES:(j + 1) * LANES] = ((gs[j] * inv) * gnw_ref[j]).astype(BF16)
        o_ref[...] = x_ref[...] + _dot(gn_ref[...], wo_ref[...])

    blk = pl.BlockSpec((nb, tm, LANES), lambda i: (0, i, 0))
    return pl.pallas_call(
        body, name="ssd_gate_fwd", grid=(t // tm,),
        in_specs=[pl.BlockSpec((tm, d), lambda i: (i, 0)), blk, blk, _resident((nb, 1, LANES)),
                  _resident((SSD_D_INNER, d))],
        out_specs=[pl.BlockSpec((tm, d), lambda i: (i, 0)), pl.BlockSpec((tm, SSD_D_INNER), lambda i: (i, 0))],
        out_shape=[jax.ShapeDtypeStruct((t, d), F32), jax.ShapeDtypeStruct((t, SSD_D_INNER), BF16)],
        compiler_params=_params("parallel"),
    )(x, y, z, gnw, wo)


def _ssd_gate_bwd(dy, y, z, gnw, wo):
    t, d = dy.shape
    tm = min(TOKEN_TILE, t)
    nb = N_XS_BLK
    per = nb // SSD_N_GROUPS

    def body(dy_ref, y_ref, z_ref, gnw_ref, wo_ref, dys_ref, dz_ref, dgnw_ref):
        @pl.when(pl.program_id(0) == 0)
        def _():
            dgnw_ref[...] = jnp.zeros_like(dgnw_ref)

        dgn = _dot_nt(dy_ref[...].astype(BF16), wo_ref[...])
        for q in range(SSD_N_GROUPS):
            js = range(q * per, (q + 1) * per)
            gs, sil, dsil = {}, {}, {}
            for j in js:
                zv = z_ref[j]
                sig = _sigmoid(zv)
                sil[j] = zv * sig
                dsil[j] = sig * (1.0 + zv * (1.0 - sig))
                gs[j] = y_ref[j] * sil[j]
            ss = sum(jnp.sum(gs[j] * gs[j], axis=1, keepdims=True) for j in js)
            inv = lax.rsqrt(ss / (per * LANES) + RMS_EPS)
            ghat = {j: gs[j] * inv for j in js}
            dgh = {}
            for j in js:
                dj = dgn[:, j * LANES:(j + 1) * LANES]
                dgnw_ref[j] += jnp.sum(dj * ghat[j], axis=0, keepdims=True)
                dgh[j] = dj * gnw_ref[j]
            mean = sum(jnp.sum(dgh[j] * ghat[j], axis=1, keepdims=True) for j in js) / (per * LANES)
            for j in js:
                dg = inv * (dgh[j] - ghat[j] * mean)
                dys_ref[j] = dg * sil[j]
                dz_ref[j] = (dg * y_ref[j] * dsil[j]).astype(BF16)

    blk = pl.BlockSpec((nb, tm, LANES), lambda i: (0, i, 0))
    return pl.pallas_call(
        body, name="ssd_gate_bwd", grid=(t // tm,),
        in_specs=[pl.BlockSpec((tm, d), lambda i: (i, 0)), blk, blk, _resident((nb, 1, LANES)),
                  _resident((SSD_D_INNER, d))],
        out_specs=[blk, blk, pl.BlockSpec((nb, 1, LANES), lambda i: (0, 0, 0))],
        out_shape=[jax.ShapeDtypeStruct((nb, t, LANES), F32), jax.ShapeDtypeStruct((nb, t, LANES), BF16),
                   jax.ShapeDtypeStruct((nb, 1, LANES), F32)],
        compiler_params=_params("arbitrary"),
    )(dy, y, z, gnw, wo)


def _lane_blocks(v):
    r, n = v.shape[0], v.shape[1] // LANES
    return v.reshape(r, n, LANES).transpose(1, 0, 2)


def _ssd_prep(w_in, conv_w, conv_b, dt_bias, a_log, d_skip, norm_w):
    n_main = SSD_D_INNER + SSD_CONV_DIM
    w_dt = w_in[:, n_main:].reshape(-1, SSD_N_GROUPS, HEADS_PER_GROUP)
    w_dt = jnp.pad(w_dt, ((0, 0), (0, 0), (0, LANES - HEADS_PER_GROUP))).reshape(-1, SSD_N_GROUPS * LANES)
    w_in_pad = jnp.concatenate([w_in[:, :n_main], w_dt], axis=1)
    taps = jnp.concatenate([conv_w, conv_b[None], jnp.zeros((SUBLANES - SSD_CONV_W - 1, SSD_CONV_DIM), F32)], axis=0)
    cwb = _lane_blocks(taps)
    row = lambda v: jnp.pad(v.reshape(SSD_N_GROUPS, 1, HEADS_PER_GROUP), ((0, 0), (0, 0), (0, LANES - HEADS_PER_GROUP)))
    dskip = jnp.broadcast_to(jnp.repeat(d_skip, SSD_D_INNER // SSD_N_HEADS).reshape(N_XS_BLK, 1, LANES),
                             (N_XS_BLK, SUBLANES, LANES))
    gnw = norm_w.reshape(N_XS_BLK, 1, LANES)
    return w_in_pad, cwb, row(dt_bias), row(a_log), dskip, gnw


def _ssd_layer_fwd(x, nw, prm, wo, consts):
    w_in_pad, cwb, dtb, alog, dskip, gnw = prm
    z, xr, dtr = _ssd_inproj(x, nw, w_in_pad)
    y, sprev = _ssd_scan_fwd(xr, dtr, cwb, dtb, alog, dskip, consts)
    out, gn = _ssd_gate_fwd(x, y, z, gnw, wo)
    return out, (z, xr, dtr, y, sprev, gn)


def _ssd_layer_bwd(x, dy, nw, prm, wo, consts, saved):
    w_in_pad, cwb, dtb, alog, dskip, gnw = prm
    z, xr, dtr, y, sprev, gn = saved
    dys, dz, dgnw = _ssd_gate_bwd(dy, y, z, gnw, wo)
    dwo = _matmul_tn(gn, dy, name="wgrad_ssd_out")
    dxs, db, dc, ddtr, dcwx, dcwb, dcwc, dd, dsm = _ssd_scan_bwd(xr, dtr, dys, sprev, cwb, dtb, alog, dskip, consts)
    pieces = [dz, dxs, db, dc, ddtr]
    dx, dnw, h = _inproj_bwd(x, dy, nw, w_in_pad, pieces)
    dws = [_matmul_tn_blocked(h, p, name=f"wgrad_ssd_in{i}") for i, p in enumerate(pieces)]
    dw_dt = dws[4].reshape(-1, SSD_N_GROUPS, LANES)[:, :, :HEADS_PER_GROUP].reshape(-1, SSD_N_HEADS)
    dw_in = jnp.concatenate(dws[:4] + [dw_dt], axis=1)
    dtaps = jnp.concatenate([dcwx, dcwb, dcwc], axis=0).transpose(1, 0, 2).reshape(SUBLANES, SSD_CONV_DIM)
    by_head = lambda r: dsm[:, r, :HEADS_PER_GROUP].reshape(SSD_N_HEADS)
    d_d = jnp.sum(dd[:, 0, :].reshape(SSD_N_HEADS, SSD_D_INNER // SSD_N_HEADS), axis=1)
    return dx, (dnw, dw_in, dtaps[:SSD_CONV_W], dtaps[SSD_CONV_W], by_head(0), by_head(1),
                d_d, dgnw.reshape(SSD_D_INNER), dwo)


def _loss_head(x, fw, target):
    t, d = x.shape
    tm = min(TOKEN_TILE, t)

    def body(x_ref, fw_ref, tgt_ref, loss_ref, dx_ref, dfw_ref):
        fwv = fw_ref[...]
        y, xhat, inv = _rms_fwd(x_ref[...], fwv)
        err = y - tgt_ref[...]
        tot = jnp.sum(jnp.sum(err * err, axis=1, keepdims=True), axis=0, keepdims=True)
        dx, dw = _rms_bwd(err * (1.0 / d), xhat, inv, fwv)
        dx_ref[...] = dx

        @pl.when(pl.program_id(0) == 0)
        def _():
            loss_ref[...] = jnp.zeros_like(loss_ref)
            dfw_ref[...] = jnp.zeros_like(dfw_ref)

        loss_ref[...] += jnp.broadcast_to(tot * (0.5 / d), loss_ref.shape)
        dfw_ref[...] += dw

    tok = pl.BlockSpec((tm, d), lambda i: (i, 0))
    return pl.pallas_call(
        body, name="loss_head", grid=(t // tm,),
        in_specs=[tok, _resident((1, d)), tok],
        out_specs=[pl.BlockSpec((1, LANES), lambda i: (0, 0)), tok, pl.BlockSpec((1, d), lambda i: (0, 0))],
        out_shape=[jax.ShapeDtypeStruct((1, LANES), F32), jax.ShapeDtypeStruct((t, d), F32),
                   jax.ShapeDtypeStruct((1, d), F32)],
        compiler_params=_params("arbitrary"),
    )(x, fw, target)


def _row_tile(rows, cap):
    best = SUBLANES
    for r in range(SUBLANES, min(rows, cap) + 1, SUBLANES):
        if rows % r == 0:
            best = r
    return best


def _adamw(w, g, m, v, name):
    rows, cols = w.shape
    br = _row_tile(rows, 256)
    c1 = 1.0 - ADAM_B1 ** ADAM_STEP
    c2 = 1.0 - ADAM_B2 ** ADAM_STEP

    def body(w_ref, g_ref, m_ref, v_ref, d_ref, nm_ref, nv_ref):
        gv = g_ref[...]
        nm = ADAM_B1 * m_ref[...] + (1.0 - ADAM_B1) * gv
        nv = ADAM_B2 * v_ref[...] + (1.0 - ADAM_B2) * (gv * gv)
        nm_ref[...] = nm
        nv_ref[...] = nv
        d_ref[...] = -ADAM_LR * ((nm / c1) / (jnp.sqrt(nv / c2) + ADAM_EPS) + ADAM_WD * w_ref[...])

    blk = pl.BlockSpec((br, cols), lambda i: (i, 0))
    shp = jax.ShapeDtypeStruct((rows, cols), F32)
    return pl.pallas_call(
        body, name=name, grid=(rows // br,), in_specs=[blk] * 4, out_specs=[blk] * 3, out_shape=[shp] * 3,
        compiler_params=_params("parallel"),
    )(w, g, m, v)


def _sum_leading(a, name):
    k, rows, cols = a.shape
    br = _row_tile(rows, 512)

    def body(a_ref, o_ref):
        acc = a_ref[0]
        for i in range(1, k):
            acc = acc + a_ref[i]
        o_ref[...] = acc

    return pl.pallas_call(
        body, name=name, grid=(rows // br,),
        in_specs=[pl.BlockSpec((k, br, cols), lambda i: (0, i, 0))],
        out_specs=pl.BlockSpec((br, cols), lambda i: (i, 0)),
        out_shape=jax.ShapeDtypeStruct((rows, cols), F32),
        compiler_params=_params("parallel"),
    )(a)


def _place():
    x, y, c = lax.axis_index("x"), lax.axis_index("y"), lax.axis_index("c")
    return x, y, c, [(1 - x, y), (x, 1 - y), (1 - x, 1 - y)]


def _remote(src, dst, send_sems, recv_sems, k, to):
    return pltpu.make_async_remote_copy(src_ref=src, dst_ref=dst, send_sem=send_sems.at[k], recv_sem=recv_sems.at[k],
                                        device_id=to, device_id_type=MESH)


def _all_gather_shards(arrs):
    n = len(arrs)

    def body(*refs):
        srcs, dsts = refs[:n], refs[n:2 * n]
        send_sems, recv_sems, local_sems = refs[2 * n:]
        x, y, c, chips = _place()
        me = 2 * x + y
        sibling = (x, y, 1 - c)
        own, sent = [], []
        for oi, (src, dst) in enumerate(zip(srcs, dsts)):
            own.append(pltpu.make_async_copy(src, dst.at[me], local_sems.at[oi]))
            own[-1].start()
            for j, chip in enumerate(chips):
                sent.append(_remote(src.at[c], dst.at[me, c], send_sems, recv_sems, 6 * oi + j, (*chip, c)))
                sent[-1].start()
        for oi, dst in enumerate(dsts):
            for j, chip in enumerate(chips):
                landed = dst.at[2 * chip[0] + chip[1], c]
                _remote(landed, landed, send_sems, recv_sems, 6 * oi + j, (*chip, c)).wait_recv()
                sent.append(_remote(landed, landed, send_sems, recv_sems, 6 * oi + 3 + j, sibling))
                sent[-1].start()
        for oi, dst in enumerate(dsts):
            for j, chip in enumerate(chips):
                landed = dst.at[2 * chip[0] + chip[1], 1 - c]
                _remote(landed, landed, send_sems, recv_sems, 6 * oi + 3 + j, sibling).wait_recv()
        for cp in sent:
            cp.wait_send()
        for cp in own:
            cp.wait()

    return pl.pallas_call(
        body, name="all_gather_shards",
        in_specs=[_HBM] * n, out_specs=[_HBM] * n,
        out_shape=[jax.ShapeDtypeStruct((N_SHARDS,) + a.shape, a.dtype) for a in arrs],
        scratch_shapes=[pltpu.SemaphoreType.DMA((6 * n,)), pltpu.SemaphoreType.DMA((6 * n,)),
                        pltpu.SemaphoreType.DMA((n,))],
    )(*arrs)


def _swap_halves(arrs):
    n = len(arrs)

    def body(*refs):
        srcs, dsts = refs[:n], refs[n:2 * n]
        send_sems, recv_sems = refs[2 * n:]
        x, y, c, _ = _place()
        cps = [_remote(src.at[:, 1 - c], dst, send_sems, recv_sems, oi, (x, y, 1 - c))
               for oi, (src, dst) in enumerate(zip(srcs, dsts))]
        for cp in cps:
            cp.start()
        for cp in cps:
            cp.wait()

    return pl.pallas_call(
        body, name="swap_halves", in_specs=[_HBM] * n, out_specs=[_HBM] * n,
        out_shape=[jax.ShapeDtypeStruct((a.shape[0],) + a.shape[2:], a.dtype) for a in arrs],
        scratch_shapes=[pltpu.SemaphoreType.DMA((n,)), pltpu.SemaphoreType.DMA((n,))],
    )(*arrs)


def _scatter_to_chips(arrs):
    n = len(arrs)

    def body(*refs):
        srcs, dsts = refs[:n], refs[n:2 * n]
        send_sems, recv_sems, local_sems = refs[2 * n:]
        x, y, c, chips = _place()
        me = 2 * x + y
        own, sent = [], []
        for oi, (src, dst) in enumerate(zip(srcs, dsts)):
            own.append(pltpu.make_async_copy(src.at[me], dst.at[me], local_sems.at[oi]))
            own[-1].start()
            for j, chip in enumerate(chips):
                sent.append(_remote(src.at[2 * chip[0] + chip[1]], dst.at[me], send_sems, recv_sems, 3 * oi + j,
                                    (*chip, c)))
                sent[-1].start()
        for oi, dst in enumerate(dsts):
            for j, chip in enumerate(chips):
                landed = dst.at[2 * chip[0] + chip[1]]
                _remote(landed, landed, send_sems, recv_sems, 3 * oi + j, (*chip, c)).wait_recv()
        for cp in sent:
            cp.wait_send()
        for cp in own:
            cp.wait()

    return pl.pallas_call(
        body, name="scatter_to_chips", in_specs=[_HBM] * n, out_specs=[_HBM] * n,
        out_shape=[jax.ShapeDtypeStruct(a.shape, a.dtype) for a in arrs],
        scratch_shapes=[pltpu.SemaphoreType.DMA((3 * n,)), pltpu.SemaphoreType.DMA((3 * n,)),
                        pltpu.SemaphoreType.DMA((n,))],
    )(*arrs)


def _join_halves(arrs):
    n = len(arrs)

    def body(*refs):
        srcs, dsts = refs[:n], refs[n:2 * n]
        send_sems, recv_sems, local_sems = refs[2 * n:]
        x, y, c, _ = _place()
        sibling = (x, y, 1 - c)
        own, sent = [], []
        for oi, (src, dst) in enumerate(zip(srcs, dsts)):
            own.append(pltpu.make_async_copy(src, dst.at[c], local_sems.at[oi]))
            own[-1].start()
            sent.append(_remote(src, dst.at[c], send_sems, recv_sems, oi, sibling))
            sent[-1].start()
        for oi, dst in enumerate(dsts):
            other = dst.at[1 - c]
            _remote(other, other, send_sems, recv_sems, oi, sibling).wait_recv()
        for cp in sent:
            cp.wait_send()
        for cp in own:
            cp.wait()

    return pl.pallas_call(
        body, name="join_halves", in_specs=[_HBM] * n, out_specs=[_HBM] * n,
        out_shape=[jax.ShapeDtypeStruct((2,) + a.shape, a.dtype) for a in arrs],
        scratch_shapes=[pltpu.SemaphoreType.DMA((n,)), pltpu.SemaphoreType.DMA((n,)), pltpu.SemaphoreType.DMA((n,))],
    )(*arrs)


def _add_halves(full, recv, core):
    n, _, rows, cols = full.shape
    br = _row_tile(rows, 512)

    def body(c_ref, a_ref, b_ref, o_ref):
        o_ref[...] = a_ref[...] + b_ref[...]

    grid_spec = pltpu.PrefetchScalarGridSpec(
        num_scalar_prefetch=1, grid=(n, rows // br),
        in_specs=[pl.BlockSpec((None, None, br, cols), lambda s, i, c_ref: (s, c_ref[0], i, 0)),
                  pl.BlockSpec((None, br, cols), lambda s, i, c_ref: (s, i, 0))],
        out_specs=pl.BlockSpec((None, br, cols), lambda s, i, c_ref: (s, i, 0)))
    return pl.pallas_call(
        body, name="add_halves", grid_spec=grid_spec, out_shape=jax.ShapeDtypeStruct((n, rows, cols), F32),
        compiler_params=_params("parallel", "parallel"),
    )(core, full, recv)


WEIGHTS = ("norm_w", "ffn_w_gate", "ffn_w_up", "ffn_w_down", "ssd_w_in", "ssd_conv_w", "ssd_conv_b", "ssd_dt_bias",
           "ssd_a_log", "ssd_d", "ssd_norm_w", "ssd_w_out", "sc_w_in", "sc_conv_w", "sc_w_out", "final_norm_w")
BIG = (("ffn_w_gate", 3), ("ffn_w_up", 3), ("ffn_w_down", 2), ("ssd_w_in", 2), ("ssd_w_out", 1), ("sc_w_in", 2),
       ("sc_w_out", 1))
SMALL_SHARDED = (("norm_w", 2), ("ssd_conv_w", 2), ("sc_conv_w", 2))
REPLICATED = ("ssd_conv_b", "ssd_dt_bias", "ssd_a_log", "ssd_d", "ssd_norm_w", "final_norm_w")
FLAT_COLS = 1024


def _pack(arrays, row_multiple, lead=()):
    flat = jnp.concatenate([a.reshape(lead + (-1,)) for a in arrays], axis=len(lead))
    unit = row_multiple * FLAT_COLS
    n = flat.shape[-1]
    pad = (-n) % unit
    if pad:
        flat = jnp.pad(flat, [(0, 0)] * len(lead) + [(0, pad)])
    return flat.reshape(lead + (-1, FLAT_COLS))


def _unpack(flat, shapes, lead=()):
    flat = flat.reshape(lead + (-1,))
    out, off = [], 0
    for shp in shapes:
        n = 1
        for s in shp:
            n *= s
        out.append(flat[..., off:off + n].reshape(lead + tuple(shp)))
        off += n
    return out


def _to_shards(full, axis):
    shp = full.shape
    r = full.reshape(shp[:axis] + (N_SHARDS, shp[axis] // N_SHARDS) + shp[axis + 1:])
    return jnp.moveaxis(r, axis, 0)


def _from_shards(sh, axis):
    r = jnp.moveaxis(sh, 0, axis)
    shp = r.shape
    return r.reshape(shp[:axis] + (shp[axis] * shp[axis + 1],) + shp[axis + 2:])


def _forward_backward(x, target, p):
    consts = _ssd_consts()
    nw = p["norm_w"]
    row = lambda v: v[None]
    ssd_prm = [_ssd_prep(p["ssd_w_in"][j], p["ssd_conv_w"][j], p["ssd_conv_b"][j], p["ssd_dt_bias"][j],
                         p["ssd_a_log"][j], p["ssd_d"][j], p["ssd_norm_w"][j]) for j in range(2)]
    sc_cw = [jnp.pad(p["sc_conv_w"][j], ((0, SUBLANES - SC_CONV_W), (0, 0))) for j in range(2)]
    ffn = lambda i, k: (p["ffn_w_gate"][i, k], p["ffn_w_up"][i, k], p["ffn_w_down"][i, k])

    xin, saved = [], []
    for i in range(N_LAYERS):
        j = i // 2
        xin.append(x)
        x = _ffn_fwd(x, row(nw[i, 0]), *ffn(i, 0))
        xin.append(x)
        if i % 2 == 0:
            x, sv = _ssd_layer_fwd(x, row(nw[i, 1]), ssd_prm[j], p["ssd_w_out"][j], consts)
        else:
            sv = _norm_mm(x, row(nw[i, 1]), p["sc_w_in"][j])
            x = _sc_fwd(x, sv, sc_cw[j], p["sc_w_out"][j])
        saved.append(sv)
        xin.append(x)
        x = _ffn_fwd(x, row(nw[i, 2]), *ffn(i, 1))
    loss, dx, dfw = _loss_head(x, row(p["final_norm_w"]), target)

    g_nw = [[None] * 3 for _ in range(N_LAYERS)]
    g_ffn = {}
    g_ssd = [None, None]
    g_sc = [None, None]

    def ffn_bwd(i, k, slot, dy):
        wg, wu, wd = ffn(i, k)
        dxn, dnw, h, a, dg, du = _ffn_bwd(xin[3 * i + slot], dy, row(nw[i, slot]), wg, wu, wd)
        g_nw[i][slot] = dnw[0]
        for n, lhs, rhs, scale in (("ffn_w_gate", h, dg, 1.0), ("ffn_w_up", h, du, 1.0), ("ffn_w_down", a, dy, 0.5)):
            g_ffn[n] = _matmul_tn(lhs, rhs, scale=scale, name="wgrad_" + n, slab=(i, k), stack=(N_LAYERS, 2),
                                  buf=g_ffn.get(n))
        return dxn

    for i in reversed(range(N_LAYERS)):
        j = i // 2
        dx = ffn_bwd(i, 1, 2, dx)
        xm = xin[3 * i + 1]
        if i % 2 == 0:
            dx, gs = _ssd_layer_bwd(xm, dx, row(nw[i, 1]), ssd_prm[j], p["ssd_w_out"][j], consts, saved[i])
            g_nw[i][1] = gs[0][0]
            g_ssd[j] = gs[1:]
        else:
            bcu = saved[i]
            dbcu, pin, dcw = _sc_bwd(dx, bcu, sc_cw[j], p["sc_w_out"][j])
            dwo = _matmul_tn(pin, dx, name="wgrad_sc_out")
            dx, dnw, h = _inproj_bwd(xm, dx, row(nw[i, 1]), p["sc_w_in"][j], [dbcu])
            g_nw[i][1] = dnw[0]
            g_sc[j] = (_matmul_tn(h, dbcu, name="wgrad_sc_in"), dcw[:SC_CONV_W], dwo)
        dx = ffn_bwd(i, 0, 0, dx)

    g = {"norm_w": jnp.stack([jnp.stack(r) for r in g_nw]), "final_norm_w": dfw[0], **g_ffn}
    for k, n in enumerate(("ssd_w_in", "ssd_conv_w", "ssd_conv_b", "ssd_dt_bias", "ssd_a_log", "ssd_d", "ssd_norm_w",
                           "ssd_w_out")):
        g[n] = jnp.stack([g_ssd[0][k], g_ssd[1][k]])
    for k, n in enumerate(("sc_w_in", "sc_conv_w", "sc_w_out")):
        g[n] = jnp.stack([g_sc[0][k], g_sc[1][k]])
    return loss, dx, g


def kernel(x, norm_w, ffn_w_gate, ffn_w_up, ffn_w_down, ssd_w_in, ssd_conv_w, ssd_conv_b, ssd_dt_bias, ssd_a_log, ssd_d, ssd_norm_w, ssd_w_out, sc_w_in, sc_conv_w, sc_w_out, final_norm_w, loss_target, m_norm_w, m_ffn_w_gate, m_ffn_w_up, m_ffn_w_down, m_ssd_w_in, m_ssd_conv_w, m_ssd_conv_b, m_ssd_dt_bias, m_ssd_a_log, m_ssd_d, m_ssd_norm_w, m_ssd_w_out, m_sc_w_in, m_sc_conv_w, m_sc_w_out, m_final_norm_w, v_norm_w, v_ffn_w_gate, v_ffn_w_up, v_ffn_w_down, v_ssd_w_in, v_ssd_conv_w, v_ssd_conv_b, v_ssd_dt_bias, v_ssd_a_log, v_ssd_d, v_ssd_norm_w, v_ssd_w_out, v_sc_w_in, v_sc_conv_w, v_sc_w_out, v_final_norm_w):
    w = dict(zip(WEIGHTS, (norm_w, ffn_w_gate, ffn_w_up, ffn_w_down, ssd_w_in, ssd_conv_w, ssd_conv_b, ssd_dt_bias,
                           ssd_a_log, ssd_d, ssd_norm_w, ssd_w_out, sc_w_in, sc_conv_w, sc_w_out, final_norm_w)))
    m = dict(zip(WEIGHTS, (m_norm_w, m_ffn_w_gate, m_ffn_w_up, m_ffn_w_down, m_ssd_w_in, m_ssd_conv_w, m_ssd_conv_b,
                           m_ssd_dt_bias, m_ssd_a_log, m_ssd_d, m_ssd_norm_w, m_ssd_w_out, m_sc_w_in, m_sc_conv_w,
                           m_sc_w_out, m_final_norm_w)))
    v = dict(zip(WEIGHTS, (v_norm_w, v_ffn_w_gate, v_ffn_w_up, v_ffn_w_down, v_ssd_w_in, v_ssd_conv_w, v_ssd_conv_b,
                           v_ssd_dt_bias, v_ssd_a_log, v_ssd_d, v_ssd_norm_w, v_ssd_w_out, v_sc_w_in, v_sc_conv_w,
                           v_sc_w_out, v_final_norm_w)))
    core = lax.axis_index("c").astype(jnp.int32).reshape(1)
    big_names = [n for n, _ in BIG]
    small_names = [n for n, _ in SMALL_SHARDED] + list(REPLICATED)
    halved = lambda a, lead=(): a.reshape(lead + (2, -1, a.shape[-1]))

    gathered = _all_gather_shards([halved(w[n].astype(BF16)) for n in big_names]
                                  + [halved(_pack([w[n] for n, _ in SMALL_SHARDED], 2 * SUBLANES))])
    p = {n: w[n] for n in REPLICATED}
    for (n, ax), sh in zip(BIG, gathered):
        p[n] = _from_shards(sh.reshape((N_SHARDS,) + w[n].shape), ax)
    for (n, ax), sh in zip(SMALL_SHARDED, _unpack(gathered[-1], [w[n].shape for n, _ in SMALL_SHARDED], lead=(N_SHARDS,))):
        p[n] = _from_shards(sh, ax)

    t, d = x.shape[-2:]
    loss, dx, g = _forward_backward(x.reshape(t, d), loss_target.reshape(t, d), p)

    small_part = _pack([_to_shards(g[n], ax) for n, ax in SMALL_SHARDED]
                       + [jnp.broadcast_to(g[n][None], (N_SHARDS,) + g[n].shape) for n in REPLICATED],
                       4 * SUBLANES, lead=(N_SHARDS,))
    parts = [halved(_to_shards(g[n], ax), lead=(N_SHARDS,)) for n, ax in BIG] + [halved(small_part, lead=(N_SHARDS,))]
    chip_sums = [_add_halves(a, r, core) for a, r in zip(parts, _swap_halves(parts))]
    reduced = _join_halves([_sum_leading(s, name="sum_chips") for s in _scatter_to_chips(chip_sums)])

    grad = {n: r.reshape(w[n].shape) for n, r in zip(big_names, reduced)}
    g_small = reduced[-1].reshape(-1, FLAT_COLS)
    grad.update(zip(small_names, _unpack(g_small, [w[n].shape for n in small_names])))

    delta, new_m, new_v = {}, {}, {}
    for n in big_names:
        shp = w[n].shape
        as2d = lambda a: a.reshape(-1, shp[-1])
        out = _adamw(as2d(w[n]), as2d(grad[n]), as2d(m[n]), as2d(v[n]), name="adamw_" + n)
        delta[n], new_m[n], new_v[n] = (o.reshape(shp) for o in out)
    packed = [_pack([s[n] for n in small_names], 4 * SUBLANES) for s in (w, m, v)]
    out = _adamw(packed[0], g_small, packed[1], packed[2], name="adamw_small")
    shapes = [w[n].shape for n in small_names]
    for dst, o in zip((delta, new_m, new_v), out):
        dst.update(zip(small_names, _unpack(o, shapes)))

    loss = lax.psum(loss[0, 0], ("x", "y", "c"))
    return (loss, dx.reshape(x.shape), *[grad[n] for n in WEIGHTS], *[delta[n] for n in WEIGHTS],
            *[new_m[n] for n in WEIGHTS], *[new_v[n] for n in WEIGHTS])
```

```python
import functools

import jax
import jax.numpy as jnp
from jax import lax
from jax.experimental import pallas as pl
from jax.experimental.pallas import tpu as pltpu

F32 = jnp.float32
BF16 = jnp.bfloat16
MESH = pl.DeviceIdType.MESH

RMS_EPS = 1e-5
D_MODEL = 1024
D_FF = 2816
N_LAYERS = 4
SSD_D_INNER = 2048
SSD_N_HEADS = 32
SSD_N_GROUPS = 4
SSD_D_STATE = 128
SSD_CHUNK = 128
SSD_CONV_W = 4
SSD_CONV_DIM = 3072
SSD_IN_DIM = 5152
SC_CONV_W = 3
LANES = 128
SUBLANES = 8
N_XS_BLK = SSD_D_INNER // LANES
SSD_IN_PAD = SSD_D_INNER + SSD_CONV_DIM + SSD_N_GROUPS * LANES
VMEM_LIMIT = 56 * 2**20
TOKEN_TILE = 512
FF_CHUNK = 256
N_SHARDS = 4

ADAM_LR = 0.001
ADAM_B1 = 0.9
ADAM_B2 = 0.999
ADAM_EPS = 1e-08
ADAM_WD = 0.01
ADAM_STEP = 10


_HBM = pl.BlockSpec(memory_space=pl.ANY)


def _params(*sem):
    return pltpu.CompilerParams(dimension_semantics=sem if sem else None, vmem_limit_bytes=VMEM_LIMIT)


def _dot(a, b):
    return jnp.dot(a, b, preferred_element_type=F32)


def _dot_nt(a, b):
    return lax.dot_general(a, b, (((1,), (1,)), ((), ())), preferred_element_type=F32)


def _dot_tn(a, b):
    return lax.dot_general(a, b, (((0,), (0,)), ((), ())), preferred_element_type=F32)


def _resident(shape):
    n = len(shape)
    return pl.BlockSpec(shape, lambda *_: (0,) * n, pipeline_mode=pl.Buffered(1))


def _split3(v):
    hi = v.astype(BF16)
    r1 = v - hi.astype(F32)
    mid = r1.astype(BF16)
    lo = (r1 - mid.astype(F32)).astype(BF16)
    return hi, mid, lo


def _sel_left(sel, v3):
    return _dot(sel, v3[0]) + _dot(sel, v3[1]) + _dot(sel, v3[2])


def _sigmoid(v):
    return 1.0 / (1.0 + jnp.exp(-v))


def _rms_fwd(x, w):
    inv = lax.rsqrt(jnp.mean(x * x, axis=-1, keepdims=True) + RMS_EPS)
    xhat = x * inv
    return xhat * w, xhat, inv


def _rms_bwd(dh, xhat, inv, w):
    dxhat = dh * w
    dx = inv * (dxhat - xhat * jnp.mean(dxhat * xhat, axis=-1, keepdims=True))
    return dx, jnp.sum(dh * xhat, axis=0, keepdims=True)


def _ffn_fwd(x, nw, wg, wu, wd):
    t, d = x.shape
    f = wg.shape[1]
    tm = min(TOKEN_TILE, t)

    def body(x_ref, nw_ref, wg_ref, wu_ref, wd_ref, o_ref):
        xv = x_ref[...]
        h = _rms_fwd(xv, nw_ref[...])[0].astype(BF16)
        acc = jnp.zeros((tm, d), F32)
        for j in range(f // FF_CHUNK):
            sl = slice(j * FF_CHUNK, (j + 1) * FF_CHUNK)
            g = _dot(h, wg_ref[:, sl])
            u = _dot(h, wu_ref[:, sl])
            a = (g * _sigmoid(g) * u).astype(BF16)
            acc = acc + _dot(a, wd_ref[sl, :])
        o_ref[...] = xv + 0.5 * acc

    return pl.pallas_call(
        body, name="ffn_fwd", grid=(t // tm,),
        in_specs=[pl.BlockSpec((tm, d), lambda i: (i, 0)), _resident((1, d)), _resident((d, f)),
                  _resident((d, f)), _resident((f, d))],
        out_specs=pl.BlockSpec((tm, d), lambda i: (i, 0)),
        out_shape=jax.ShapeDtypeStruct((t, d), F32),
        compiler_params=_params("parallel"),
    )(x, nw, wg, wu, wd)


def _ffn_bwd(x, dy, nw, wg, wu, wd):
    t, d = x.shape
    f = wg.shape[1]
    tm = min(TOKEN_TILE, t)

    def body(x_ref, dy_ref, nw_ref, wg_ref, wu_ref, wd_ref, dx_ref, dnw_ref, h_ref, a_ref, dg_ref, du_ref):
        xv = x_ref[...]
        dyv = dy_ref[...]
        nwv = nw_ref[...]
        hf, xhat, inv = _rms_fwd(xv, nwv)
        h = hf.astype(BF16)
        h_ref[...] = h
        dob = (0.5 * dyv).astype(BF16)
        dh = jnp.zeros((tm, d), F32)
        for j in range(f // FF_CHUNK):
            sl = slice(j * FF_CHUNK, (j + 1) * FF_CHUNK)
            g = _dot(h, wg_ref[:, sl])
            u = _dot(h, wu_ref[:, sl])
            sig = _sigmoid(g)
            s = g * sig
            a_ref[:, sl] = (s * u).astype(BF16)
            da = _dot_nt(dob, wd_ref[sl, :])
            dgb = (da * u * (sig * (1.0 + g * (1.0 - sig)))).astype(BF16)
            dub = (da * s).astype(BF16)
            dg_ref[:, sl] = dgb
            du_ref[:, sl] = dub
            dh = dh + _dot_nt(dgb, wg_ref[:, sl]) + _dot_nt(dub, wu_ref[:, sl])
        dx, dw = _rms_bwd(dh, xhat, inv, nwv)
        dx_ref[...] = dyv + dx

        @pl.when(pl.program_id(0) == 0)
        def _():
            dnw_ref[...] = jnp.zeros_like(dnw_ref)

        dnw_ref[...] += dw

    tok = lambda n: pl.BlockSpec((tm, n), lambda i: (i, 0))
    return pl.pallas_call(
        body, name="ffn_bwd", grid=(t // tm,),
        in_specs=[tok(d), tok(d), _resident((1, d)), _resident((d, f)), _resident((d, f)), _resident((f, d))],
        out_specs=[tok(d), pl.BlockSpec((1, d), lambda i: (0, 0)), tok(d), tok(f), tok(f), tok(f)],
        out_shape=[jax.ShapeDtypeStruct((t, d), F32), jax.ShapeDtypeStruct((1, d), F32),
                   jax.ShapeDtypeStruct((t, d), BF16), jax.ShapeDtypeStruct((t, f), BF16),
                   jax.ShapeDtypeStruct((t, f), BF16), jax.ShapeDtypeStruct((t, f), BF16)],
        compiler_params=_params("arbitrary"),
    )(x, dy, nw, wg, wu, wd)


def _pick_bn(m, n, unit):
    best = unit
    for k in range(1, n // unit + 1):
        bn = k * unit
        if n % bn == 0 and m * bn * 4 <= 8 * 2**20:
            best = bn
    return best


def _matmul_tn(a, b, scale=1.0, name="wgrad", slab=None, stack=None, buf=None):
    t, m = a.shape
    n = b.shape[1]
    bt = min(1024, t)
    bn = _pick_bn(m, n, LANES)
    nt = t // bt
    lead = tuple(slab) if slab is not None else ()

    def body(a_ref, b_ref, *rest):
        o_ref = rest[-1]

        @pl.when(pl.program_id(1) == 0)
        def _():
            o_ref[...] = jnp.zeros_like(o_ref)

        o_ref[...] += _dot_tn(a_ref[...].astype(BF16), b_ref[...].astype(BF16))
        if scale != 1.0:
            @pl.when(pl.program_id(1) == nt - 1)
            def _():
                o_ref[...] *= scale

    in_specs = [pl.BlockSpec((bt, m), lambda j, k: (k, 0)), pl.BlockSpec((bt, bn), lambda j, k: (k, j))]
    args = [a, b]
    if buf is not None:
        in_specs.append(_HBM)
        args.append(buf)
    return pl.pallas_call(
        body, name=name, grid=(n // bn, nt),
        in_specs=in_specs,
        out_specs=pl.BlockSpec((None,) * len(lead) + (m, bn), lambda j, k: lead + (0, j)),
        out_shape=jax.ShapeDtypeStruct(tuple(stack or ()) + (m, n), F32),
        input_output_aliases={2: 0} if buf is not None else {},
        compiler_params=_params("parallel", "arbitrary"),
    )(*args)


def _matmul_tn_blocked(a, b, name="wgrad_blk"):
    t, m = a.shape
    nb = b.shape[0]
    bt = min(1024, t)
    nbt = _pick_bn(m, nb * LANES, LANES) // LANES
    while nb % nbt:
        nbt -= 1

    def body(a_ref, b_ref, o_ref):
        @pl.when(pl.program_id(1) == 0)
        def _():
            o_ref[...] = jnp.zeros_like(o_ref)

        bv = jnp.concatenate([b_ref[i] for i in range(nbt)], axis=1) if nbt > 1 else b_ref[0]
        o_ref[...] += _dot_tn(a_ref[...], bv)

    return pl.pallas_call(
        body, name=name, grid=(nb // nbt, t // bt),
        in_specs=[pl.BlockSpec((bt, m), lambda j, k: (k, 0)), pl.BlockSpec((nbt, bt, LANES), lambda j, k: (j, k, 0))],
        out_specs=pl.BlockSpec((m, nbt * LANES), lambda j, k: (0, j)),
        out_shape=jax.ShapeDtypeStruct((m, nb * LANES), F32),
        compiler_params=_params("parallel", "arbitrary"),
    )(a, b)


def _norm_mm(x, nw, w):
    t, d = x.shape
    n = w.shape[1]
    tm = min(TOKEN_TILE, t)
    cn = 1024 if n % 1024 == 0 else n

    def body(x_ref, nw_ref, w_ref, o_ref):
        h = _rms_fwd(x_ref[...], nw_ref[...])[0].astype(BF16)
        for j in range(n // cn):
            sl = slice(j * cn, (j + 1) * cn)
            o_ref[:, sl] = _dot(h, w_ref[:, sl])

    return pl.pallas_call(
        body, name="norm_mm", grid=(t // tm,),
        in_specs=[pl.BlockSpec((tm, d), lambda i: (i, 0)), _resident((1, d)), _resident((d, n))],
        out_specs=pl.BlockSpec((tm, n), lambda i: (i, 0)),
        out_shape=jax.ShapeDtypeStruct((t, n), F32),
        compiler_params=_params("parallel"),
    )(x, nw, w)


def _ssd_inproj(x, nw, w):
    t, d = x.shape
    tm = min(TOKEN_TILE, t)
    nz, nx, ng = SSD_D_INNER // LANES, SSD_CONV_DIM // LANES, SSD_N_GROUPS
    cn = 1024

    def body(x_ref, nw_ref, w_ref, z_ref, xr_ref, dt_ref):
        h = _rms_fwd(x_ref[...], nw_ref[...])[0].astype(BF16)
        for j in range(-(-SSD_IN_PAD // cn)):
            lo, hi = j * cn, min((j + 1) * cn, SSD_IN_PAD)
            r = _dot(h, w_ref[:, lo:hi])
            for i in range((hi - lo) // LANES):
                blk = j * (cn // LANES) + i
                v = r[:, i * LANES:(i + 1) * LANES]
                if blk < nz:
                    z_ref[blk] = v
                elif blk < nz + nx:
                    xr_ref[blk - nz] = v
                else:
                    dt_ref[blk - nz - nx] = v

    out = lambda n: pl.BlockSpec((n, tm, LANES), lambda i: (0, i, 0))
    return pl.pallas_call(
        body, name="ssd_inproj", grid=(t // tm,),
        in_specs=[pl.BlockSpec((tm, d), lambda i: (i, 0)), _resident((1, d)), _resident((d, SSD_IN_PAD))],
        out_specs=[out(nz), out(nx), out(ng)],
        out_shape=[jax.ShapeDtypeStruct((n, t, LANES), F32) for n in (nz, nx, ng)],
        compiler_params=_params("parallel"),
    )(x, nw, w)


def _inproj_bwd(x, dy, nw, w, pieces):
    t, d = x.shape
    n = w.shape[1]
    tm = min(TOKEN_TILE, t)
    npc = len(pieces)

    def body(*refs):
        x_ref, dy_ref, nw_ref, w_ref = refs[:4]
        p_refs = refs[4:4 + npc]
        dx_ref, dnw_ref, h_ref = refs[4 + npc:]
        nwv = nw_ref[...]
        hf, xhat, inv = _rms_fwd(x_ref[...], nwv)
        h_ref[...] = hf.astype(BF16)
        parts = []
        for p in p_refs:
            if len(p.shape) == 3:
                parts += [p[i] for i in range(p.shape[0])]
            else:
                parts.append(p[...])
        dz = jnp.concatenate(parts, axis=1) if len(parts) > 1 else parts[0]
        dh = _dot_nt(dz, w_ref[...])
        dx, dw = _rms_bwd(dh, xhat, inv, nwv)
        dx_ref[...] = dy_ref[...] + dx

        @pl.when(pl.program_id(0) == 0)
        def _():
            dnw_ref[...] = jnp.zeros_like(dnw_ref)

        dnw_ref[...] += dw

    tok = lambda m: pl.BlockSpec((tm, m), lambda i: (i, 0))
    p_specs = [pl.BlockSpec((p.shape[0], tm, LANES), lambda i: (0, i, 0)) if p.ndim == 3 else tok(p.shape[1])
               for p in pieces]
    return pl.pallas_call(
        body, name="inproj_bwd", grid=(t // tm,),
        in_specs=[tok(d), tok(d), _resident((1, d)), _resident((d, n))] + p_specs,
        out_specs=[tok(d), pl.BlockSpec((1, d), lambda i: (0, 0)), tok(d)],
        out_shape=[jax.ShapeDtypeStruct((t, d), F32), jax.ShapeDtypeStruct((1, d), F32),
                   jax.ShapeDtypeStruct((t, d), BF16)],
        compiler_params=_params("arbitrary"),
    )(x, dy, nw, w, *pieces)


def _shift_down(v, j, prev8):
    if j == 0:
        return v
    r = pltpu.roll(v, j, 0)
    p = pltpu.roll(prev8, j, 0)
    rows = lax.broadcasted_iota(jnp.int32, prev8.shape, 0)
    first = jnp.where(rows < j, p, r[0:SUBLANES])
    return jnp.concatenate([first, r[SUBLANES:]], axis=0)


def _shift_up(v, j, next8):
    if j == 0:
        return v
    n = v.shape[0]
    r = pltpu.roll(v, n - j, 0)
    p = pltpu.roll(next8, SUBLANES - j, 0)
    rows = lax.broadcasted_iota(jnp.int32, next8.shape, 0)
    last = jnp.where(rows >= SUBLANES - j, p, r[n - SUBLANES:])
    return jnp.concatenate([r[:n - SUBLANES], last], axis=0)


def _sc_fwd(x, bcu, cw, wo):
    t, d = x.shape
    tm = min(TOKEN_TILE, t)
    hb = tm // SUBLANES

    def body(x_ref, bcu_ref, prev_ref, cw_ref, wo_ref, o_ref):
        bg, cg, u = bcu_ref[:, 0:d], bcu_ref[:, d:2 * d], bcu_ref[:, 2 * d:3 * d]
        q = cg * u
        qp = jnp.where(pl.program_id(0) == 0, 0.0, prev_ref[:, d:2 * d] * prev_ref[:, 2 * d:3 * d])
        cwv = cw_ref[...]
        v = cwv[2:3] * q + cwv[1:2] * _shift_down(q, 1, qp) + cwv[0:1] * _shift_down(q, 2, qp)
        o_ref[...] = x_ref[...] + _dot((bg * v).astype(BF16), wo_ref[...])

    return pl.pallas_call(
        body, name="sc_fwd", grid=(t // tm,),
        in_specs=[pl.BlockSpec((tm, d), lambda i: (i, 0)), pl.BlockSpec((tm, 3 * d), lambda i: (i, 0)),
                  pl.BlockSpec((SUBLANES, 3 * d), lambda i: (jnp.maximum(i * hb - 1, 0), 0)),
                  _resident((SUBLANES, d)), _resident((d, d))],
        out_specs=pl.BlockSpec((tm, d), lambda i: (i, 0)),
        out_shape=jax.ShapeDtypeStruct((t, d), F32),
        compiler_params=_params("parallel"),
    )(x, bcu, bcu, cw, wo)


def _sc_bwd(dy, bcu, cw, wo):
    t, d = dy.shape
    tm = min(TOKEN_TILE, t)
    hb = tm // SUBLANES
    nt = t // tm

    def body(dy_ref, dyn_ref, bcu_ref, prev_ref, next_ref, cw_ref, wo_ref, dbcu_ref, p_ref, dcw_ref):
        i = pl.program_id(0)
        bg, cg, u = bcu_ref[:, 0:d], bcu_ref[:, d:2 * d], bcu_ref[:, 2 * d:3 * d]
        q = cg * u
        qp = jnp.where(i == 0, 0.0, prev_ref[:, d:2 * d] * prev_ref[:, 2 * d:3 * d])
        cwv = cw_ref[...]
        q1 = _shift_down(q, 1, qp)
        q2 = _shift_down(q, 2, qp)
        v = cwv[2:3] * q + cwv[1:2] * q1 + cwv[0:1] * q2
        p_ref[...] = (bg * v).astype(BF16)
        wov = wo_ref[...]
        dp = _dot_nt(dy_ref[...].astype(BF16), wov)
        dpn = _dot_nt(dyn_ref[...].astype(BF16), wov)
        dv = dp * bg
        dvn = jnp.where(i == nt - 1, 0.0, dpn * next_ref[:, 0:d])
        dq = cwv[2:3] * dv + cwv[1:2] * _shift_up(dv, 1, dvn) + cwv[0:1] * _shift_up(dv, 2, dvn)
        dbcu_ref[:, 0:d] = (dp * v).astype(BF16)
        dbcu_ref[:, d:2 * d] = (dq * u).astype(BF16)
        dbcu_ref[:, 2 * d:3 * d] = (dq * cg).astype(BF16)

        @pl.when(i == 0)
        def _():
            dcw_ref[...] = jnp.zeros_like(dcw_ref)

        dcw_ref[0:1, :] += jnp.sum(dv * q2, axis=0, keepdims=True)
        dcw_ref[1:2, :] += jnp.sum(dv * q1, axis=0, keepdims=True)
        dcw_ref[2:3, :] += jnp.sum(dv * q, axis=0, keepdims=True)

    last8 = t // SUBLANES - 1
    return pl.pallas_call(
        body, name="sc_bwd", grid=(nt,),
        in_specs=[pl.BlockSpec((tm, d), lambda i: (i, 0)),
                  pl.BlockSpec((SUBLANES, d), lambda i: (jnp.minimum((i + 1) * hb, last8), 0)),
                  pl.BlockSpec((tm, 3 * d), lambda i: (i, 0)),
                  pl.BlockSpec((SUBLANES, 3 * d), lambda i: (jnp.maximum(i * hb - 1, 0), 0)),
                  pl.BlockSpec((SUBLANES, 3 * d), lambda i: (jnp.minimum((i + 1) * hb, last8), 0)),
                  _resident((SUBLANES, d)), _resident((d, d))],
        out_specs=[pl.BlockSpec((tm, 3 * d), lambda i: (i, 0)), pl.BlockSpec((tm, d), lambda i: (i, 0)),
                   pl.BlockSpec((SUBLANES, d), lambda i: (0, 0))],
        out_shape=[jax.ShapeDtypeStruct((t, 3 * d), BF16), jax.ShapeDtypeStruct((t, d), BF16),
                   jax.ShapeDtypeStruct((SUBLANES, d), F32)],
        compiler_params=_params("arbitrary"),
    )(dy, dy, bcu, bcu, bcu, cw, wo)


NEG_BIG = -1e30


HEADS_PER_GROUP = SSD_N_HEADS // SSD_N_GROUPS
PAIRS_PER_GROUP = HEADS_PER_GROUP // 2


def _ssd_consts():
    r = lax.broadcasted_iota(jnp.int32, (LANES, LANES), 0)
    c = lax.broadcasted_iota(jnp.int32, (LANES, LANES), 1)
    return (c <= r).astype(BF16), (c >= r).astype(BF16)


def _ssd_decay(dtr, dtb, alog, tril):
    shape = (SSD_CHUNK, LANES)
    lanes = lax.broadcasted_iota(jnp.int32, shape, 1)
    rows = lax.broadcasted_iota(jnp.int32, shape, 0)
    pre = dtr + dtb
    valid = lanes < HEADS_PER_GROUP
    dt = jnp.where(valid, jnp.maximum(pre, 0.0) + jnp.log(1.0 + jnp.exp(-jnp.abs(pre))), 0.0)
    a = -jnp.exp(alog)
    acs = _sel_left(tril, _split3(dt * a))
    return dt, a, acs, pre, valid, rows, lanes


def _lane_col(v, j):
    return jnp.broadcast_to(v[:, j:j + 1], v.shape)


def _ssd_pair_terms(k, dt, cols, low_half, xs):
    dtp = jnp.where(low_half, _lane_col(dt, 2 * k), _lane_col(dt, 2 * k + 1))
    acsp = jnp.where(low_half, cols[2 * k], cols[2 * k + 1])
    lastp = acsp[SSD_CHUNK - 1:SSD_CHUNK, :]
    eap = jnp.exp(acsp)
    decp = jnp.exp(lastp - acsp)
    etp = jnp.exp(lastp)
    xdt = xs * dtp
    return dtp, eap, decp, etp, xdt


def _ssd_conv_taps(cwb, xr, prev8):
    sh = [_shift_down(xr, j, prev8) for j in range(SSD_CONV_W)]
    xc = cwb[4:5]
    for j in range(SSD_CONV_W):
        xc = xc + cwb[3 - j:4 - j] * sh[j]
    return xc, sh


def _ssd_specs(nc, rev):
    ch = (lambda i: nc - 1 - i) if rev else (lambda i: i)
    L = SSD_CHUNK
    xs = pl.BlockSpec((4, L, LANES), lambda g, i: (g, ch(i), 0))
    bb = pl.BlockSpec((1, L, LANES), lambda g, i: (N_XS_BLK + g, ch(i), 0))
    cc = pl.BlockSpec((1, L, LANES), lambda g, i: (N_XS_BLK + SSD_N_GROUPS + g, ch(i), 0))
    dt = pl.BlockSpec((1, L, LANES), lambda g, i: (g, ch(i), 0))
    cw_xs = pl.BlockSpec((4, SUBLANES, LANES), lambda g, i: (g, 0, 0))
    cw_b = pl.BlockSpec((1, SUBLANES, LANES), lambda g, i: (N_XS_BLK + g, 0, 0))
    cw_c = pl.BlockSpec((1, SUBLANES, LANES), lambda g, i: (N_XS_BLK + SSD_N_GROUPS + g, 0, 0))
    st = pl.BlockSpec((1, 4, LANES, LANES), lambda g, i: (ch(i), g, 0, 0))
    grp4 = pl.BlockSpec((4, L, LANES), lambda g, i: (g, ch(i), 0))
    return xs, bb, cc, dt, cw_xs, cw_b, cw_c, st, grp4


def _ssd_scan_fwd(xr, dtr, cwb, dtb, alog, dskip, consts):
    t = xr.shape[1]
    L = SSD_CHUNK
    nc = t // L
    tril, _ = consts
    xs_s, b_s, c_s, dt_s, cwx_s, cwb_s, cwc_s, st_s, grp4 = _ssd_specs(nc, False)
    grp_row = pl.BlockSpec((1, 1, LANES), lambda g, i: (g, 0, 0))

    def body(xs_ref, b_ref, c_ref, dtr_ref, cwx_ref, cwbb_ref, cwc_ref, dtb_ref, alog_ref, dsk_ref,
             tril_ref, y_ref, sp_ref, state, tail):
        @pl.when(pl.program_id(1) == 0)
        def _():
            state[...] = jnp.zeros_like(state)
            tail[...] = jnp.zeros_like(tail)

        xa = []
        for b in range(6):
            xrb = xs_ref[b] if b < 4 else (b_ref[0] if b == 4 else c_ref[0])
            cw = cwx_ref[b] if b < 4 else (cwbb_ref[0] if b == 4 else cwc_ref[0])
            xc, _ = _ssd_conv_taps(cw, xrb, tail[b])
            tail[b] = xrb[L - SUBLANES:]
            xa.append(xc * _sigmoid(xc))

        dt, a, acs, _, _, rows, lanes = _ssd_decay(dtr_ref[0], dtb_ref[0], alog_ref[0], tril_ref[...])
        acst = acs.T
        bb = xa[4].astype(BF16)
        cb_ = xa[5].astype(BF16)
        cbm = _dot_nt(cb_, bb)
        causal = rows >= lanes
        low_half = lanes < LANES // 2
        cols = [_lane_col(acs, j) for j in range(HEADS_PER_GROUP)]

        for k in range(PAIRS_PER_GROUP):
            xs = xa[k]
            dtp, eap, decp, etp, xdt = _ssd_pair_terms(k, dt, cols, low_half, xs)
            ms = []
            for j in (2 * k, 2 * k + 1):
                diff = cols[j] - jnp.broadcast_to(acst[j:j + 1, :], (L, L))
                ms.append((cbm * jnp.exp(jnp.where(causal, diff, NEG_BIG))).astype(BF16))
            xcat = jnp.concatenate([jnp.where(low_half, xdt, 0.0).astype(BF16),
                                    jnp.where(low_half, 0.0, xdt).astype(BF16)], axis=0)
            yd = _dot(jnp.concatenate(ms, axis=1), xcat)
            sp = state[k]
            yo = eap * _dot(cb_, sp.astype(BF16))
            y_ref[k] = yd + yo + dsk_ref[k][0:1] * xs
            sp_ref[0, k] = sp
            state[k] = etp * sp + _dot_tn(bb, (decp * xdt).astype(BF16))

    return pl.pallas_call(
        body, name="ssd_scan_fwd", grid=(SSD_N_GROUPS, nc),
        in_specs=[xs_s, b_s, c_s, dt_s, cwx_s, cwb_s, cwc_s, grp_row, grp_row, cwx_s, _resident(tril.shape)],
        out_specs=[grp4, st_s],
        out_shape=[jax.ShapeDtypeStruct((N_XS_BLK, t, LANES), F32),
                   jax.ShapeDtypeStruct((nc, N_XS_BLK, LANES, LANES), F32)],
        scratch_shapes=[pltpu.VMEM((4, LANES, LANES), F32), pltpu.VMEM((6, SUBLANES, LANES), F32)],
        compiler_params=_params("arbitrary", "arbitrary"),
    )(xr, xr, xr, dtr, cwb, cwb, cwb, dtb, alog, dskip, tril)


def _ssd_scan_bwd(xr, dtr, dy, sprev, cwb, dtb, alog, dskip, consts):
    t = xr.shape[1]
    L = SSD_CHUNK
    nc = t // L
    hb = L // SUBLANES
    tril, triu = consts
    xs_s, b_s, c_s, dt_s, cwx_s, cwb_s, cwc_s, st_s, grp4 = _ssd_specs(nc, True)
    grp_row = pl.BlockSpec((1, 1, LANES), lambda g, i: (g, 0, 0))
    prev = lambda off: pl.BlockSpec(
        (4 if off is None else 1, SUBLANES, LANES),
        (lambda g, i: (g, jnp.maximum((nc - 1 - i) * hb - 1, 0), 0)) if off is None else
        (lambda g, i: (off + g, jnp.maximum((nc - 1 - i) * hb - 1, 0), 0)))
    grp1 = pl.BlockSpec((1, L, LANES), lambda g, i: (g, nc - 1 - i, 0))
    acc4 = pl.BlockSpec((4, SUBLANES, LANES), lambda g, i: (g, 0, 0))
    acc1 = pl.BlockSpec((1, SUBLANES, LANES), lambda g, i: (g, 0, 0))

    def body(xs_ref, b_ref, c_ref, pxs_ref, pb_ref, pc_ref, dtr_ref, dy_ref, sp_ref,
             cwx_ref, cwbb_ref, cwc_ref, dtb_ref, alog_ref, dsk_ref, tril_ref, triu_ref,
             dxs_ref, db_ref, dc_ref, ddtr_ref, dcwx_ref, dcwb_ref, dcwc_ref, dd_ref, dsm_ref,
             dstate, head):
        step = pl.program_id(1)
        first_chunk = step == nc - 1

        @pl.when(step == 0)
        def _():
            dstate[...] = jnp.zeros_like(dstate)
            head[...] = jnp.zeros_like(head)
            for r in (dcwx_ref, dcwb_ref, dcwc_ref, dd_ref, dsm_ref):
                r[...] = jnp.zeros_like(r)

        def blk(b):
            xrb = xs_ref[b] if b < 4 else (b_ref[0] if b == 4 else c_ref[0])
            cw = cwx_ref[b] if b < 4 else (cwbb_ref[0] if b == 4 else cwc_ref[0])
            p8 = pxs_ref[b] if b < 4 else (pb_ref[0] if b == 4 else pc_ref[0])
            return xrb, cw, jnp.where(first_chunk, 0.0, p8)

        xa, dsil = [], []
        for b in range(6):
            xrb, cw, p8 = blk(b)
            xc, _ = _ssd_conv_taps(cw, xrb, p8)
            sig = _sigmoid(xc)
            xa.append(xc * sig)
            dsil.append(sig * (1.0 + xc * (1.0 - sig)))

        dt, a, acs, pre, valid, rows, lanes = _ssd_decay(dtr_ref[0], dtb_ref[0], alog_ref[0], tril_ref[...])
        acst = acs.T
        bb = xa[4].astype(BF16)
        cb_ = xa[5].astype(BF16)
        cbm = _dot_nt(cb_, bb)
        cbmt = _dot_nt(bb, cb_)
        causal = rows >= lanes
        anti = rows <= lanes
        low_half = lanes < LANES // 2
        last_row = rows == L - 1
        cols = [_lane_col(acs, j) for j in range(HEADS_PER_GROUP)]
        zeros = jnp.zeros((L, LANES), F32)
        dcb, dcbt, dbg, dcg, dacs, dacst, ddt = zeros, zeros, zeros, zeros, zeros, zeros, zeros
        dxa = []

        for k in range(PAIRS_PER_GROUP):
            xs = xa[k]
            dtp, eap, decp, etp, xdt = _ssd_pair_terms(k, dt, cols, low_half, xs)
            xdtb = xdt.astype(BF16)
            w = decp * xdt
            wb = w.astype(BF16)
            dyv = dy_ref[k]
            sp = sp_ref[0, k]
            spb = sp.astype(BF16)
            dsn = dstate[k]
            dsnb = dsn.astype(BF16)
            yoff = eap * _dot(cb_, spb)
            dgb = (eap * dyv).astype(BF16)
            dcg = dcg + _dot_nt(dgb, spb)
            dstate[k] = _dot_tn(cb_, dgb) + etp * dsn
            last_lane = etp * jnp.sum(dsn * sp, axis=0, keepdims=True)
            dbg = dbg + _dot_nt(wb, dsnb)
            dw = _dot(bb, dsnb)
            t2 = dw * w
            dxdt = decp * dw
            last_lane = last_lane + jnp.sum(t2, axis=0, keepdims=True)
            lane_acc = dyv * yoff - t2 + jnp.where(last_row, last_lane, 0.0)
            for j in (2 * k, 2 * k + 1):
                diff = cols[j] - jnp.broadcast_to(acst[j:j + 1, :], (L, L))
                lm = jnp.exp(jnp.where(causal, diff, NEG_BIG))
                lmt = jnp.exp(jnp.where(anti, -diff, NEG_BIG))
                dye = jnp.where(low_half == (j % 2 == 0), dyv, 0.0).astype(BF16)
                dm = _dot_nt(dye, xdtb)
                dmt = _dot_nt(xdtb, dye)
                mt = cbmt * lmt
                seg = dmt * mt - dm * (cbm * lm)
                dacst = dacst + jnp.where(rows == j, jnp.sum(seg, axis=0, keepdims=True), 0.0)
                dcb = dcb + dm * lm
                dcbt = dcbt + dmt * lmt
                dxdt = dxdt + _dot(mt.astype(BF16), dye)
            ddt_lane = dxdt * xs
            for j, keep in ((2 * k, low_half), (2 * k + 1, jnp.logical_not(low_half))):
                dacs = dacs + jnp.where(lanes == j, jnp.sum(jnp.where(keep, lane_acc, 0.0), axis=1, keepdims=True), 0.0)
                ddt = ddt + jnp.where(lanes == j, jnp.sum(jnp.where(keep, ddt_lane, 0.0), axis=1, keepdims=True), 0.0)
            dxa.append(dsk_ref[k][0:1] * dyv + dxdt * dtp)
            dd_ref[k, 0:1, :] += jnp.sum(dyv * xs, axis=0, keepdims=True)

        dxa.append(dbg + _dot(dcbt.astype(BF16), cb_))
        dxa.append(dcg + _dot(dcb.astype(BF16), bb))
        dac = _sel_left(triu_ref[...], _split3(dacs + dacst.T))
        ddtr = jnp.where(valid, (ddt + dac * a) * _sigmoid(pre), 0.0)
        ddtr_ref[0] = ddtr.astype(BF16)
        dsm_ref[0, 0:1, :] += jnp.sum(ddtr, axis=0, keepdims=True)
        dsm_ref[0, 1:2, :] += jnp.sum(dac * dt, axis=0, keepdims=True) * a

        for b in range(6):
            xrb, cw, p8 = blk(b)
            sh = [_shift_down(xrb, j, p8) for j in range(SSD_CONV_W)]
            dxc = dxa[b] * dsil[b]
            acc = dcwx_ref.at[b] if b < 4 else (dcwb_ref.at[0] if b == 4 else dcwc_ref.at[0])
            acc[4:5, :] += jnp.sum(dxc, axis=0, keepdims=True)
            dxr = jnp.zeros_like(dxc)
            for j in range(SSD_CONV_W):
                acc[3 - j:4 - j, :] += jnp.sum(dxc * sh[j], axis=0, keepdims=True)
                dxr = dxr + cw[3 - j:4 - j] * _shift_up(dxc, j, head[b])
            head[b] = dxc[0:SUBLANES]
            out = dxs_ref.at[b] if b < 4 else (db_ref.at[0] if b == 4 else dc_ref.at[0])
            out[...] = dxr.astype(BF16)

    return pl.pallas_call(
        body, name="ssd_scan_bwd", grid=(SSD_N_GROUPS, nc),
        in_specs=[xs_s, b_s, c_s, prev(None), prev(N_XS_BLK), prev(N_XS_BLK + SSD_N_GROUPS), dt_s, grp4, st_s,
                  cwx_s, cwb_s, cwc_s, grp_row, grp_row, cwx_s, _resident(tril.shape), _resident(triu.shape)],
        out_specs=[grp4, grp1, grp1, grp1, acc4, acc1, acc1, acc4, acc1],
        out_shape=[jax.ShapeDtypeStruct((N_XS_BLK, t, LANES), BF16),
                   jax.ShapeDtypeStruct((SSD_N_GROUPS, t, LANES), BF16),
                   jax.ShapeDtypeStruct((SSD_N_GROUPS, t, LANES), BF16),
                   jax.ShapeDtypeStruct((SSD_N_GROUPS, t, LANES), BF16),
                   jax.ShapeDtypeStruct((N_XS_BLK, SUBLANES, LANES), F32),
                   jax.ShapeDtypeStruct((SSD_N_GROUPS, SUBLANES, LANES), F32),
                   jax.ShapeDtypeStruct((SSD_N_GROUPS, SUBLANES, LANES), F32),
                   jax.ShapeDtypeStruct((N_XS_BLK, SUBLANES, LANES), F32),
                   jax.ShapeDtypeStruct((SSD_N_GROUPS, SUBLANES, LANES), F32)],
        scratch_shapes=[pltpu.VMEM((4, LANES, LANES), F32), pltpu.VMEM((6, SUBLANES, LANES), F32)],
        compiler_params=_params("arbitrary", "arbitrary"),
    )(xr, xr, xr, xr, xr, xr, dtr, dy, sprev, cwb, cwb, cwb, dtb, alog, dskip, tril, triu)


def _ssd_gate_fwd(x, y, z, gnw, wo):
    t, d = x.shape
    tm = min(TOKEN_TILE, t)
    nb = N_XS_BLK
    per = nb // SSD_N_GROUPS

    def body(x_ref, y_ref, z_ref, gnw_ref, wo_ref, o_ref, gn_ref):
        gs = []
        for j in range(nb):
            zv = z_ref[j]
            gs.append(y_ref[j] * (zv * _sigmoid(zv)))
        for q in range(SSD_N_GROUPS):
            ss = sum(jnp.sum(gs[j] * gs[j], axis=1, keepdims=True) for j in range(q * per, (q + 1) * per))
            inv = lax.rsqrt(ss / (per * LANES) + RMS_EPS)
            for j in range(q * per, (q + 1) * per):
                gn_ref[:, j * LANES:(j + 1) * LANES] = ((gs[j] * inv) * gnw_ref[j]).astype(BF16)
        o_ref[...] = x_ref[...] + _dot(gn_ref[...], wo_ref[...])

    blk = pl.BlockSpec((nb, tm, LANES), lambda i: (0, i, 0))
    return pl.pallas_call(
        body, name="ssd_gate_fwd", grid=(t // tm,),
        in_specs=[pl.BlockSpec((tm, d), lambda i: (i, 0)), blk, blk, _resident((nb, 1, LANES)),
                  _resident((SSD_D_INNER, d))],
        out_specs=[pl.BlockSpec((tm, d), lambda i: (i, 0)), pl.BlockSpec((tm, SSD_D_INNER), lambda i: (i, 0))],
        out_shape=[jax.ShapeDtypeStruct((t, d), F32), jax.ShapeDtypeStruct((t, SSD_D_INNER), BF16)],
        compiler_params=_params("parallel"),
    )(x, y, z, gnw, wo)


def _ssd_gate_bwd(dy, y, z, gnw, wo):
    t, d = dy.shape
    tm = min(TOKEN_TILE, t)
    nb = N_XS_BLK
    per = nb // SSD_N_GROUPS

    def body(dy_ref, y_ref, z_ref, gnw_ref, wo_ref, dys_ref, dz_ref, dgnw_ref):
        @pl.when(pl.program_id(0) == 0)
        def _():
            dgnw_ref[...] = jnp.zeros_like(dgnw_ref)

        dgn = _dot_nt(dy_ref[...].astype(BF16), wo_ref[...])
        for q in range(SSD_N_GROUPS):
            js = range(q * per, (q + 1) * per)
            gs, sil, dsil = {}, {}, {}
            for j in js:
                zv = z_ref[j]
                sig = _sigmoid(zv)
                sil[j] = zv * sig
                dsil[j] = sig * (1.0 + zv * (1.0 - sig))
                gs[j] = y_ref[j] * sil[j]
            ss = sum(jnp.sum(gs[j] * gs[j], axis=1, keepdims=True) for j in js)
            inv = lax.rsqrt(ss / (per * LANES) + RMS_EPS)
            ghat = {j: gs[j] * inv for j in js}
            dgh = {}
            for j in js:
                dj = dgn[:, j * LANES:(j + 1) * LANES]
                dgnw_ref[j] += jnp.sum(dj * ghat[j], axis=0, keepdims=True)
                dgh[j] = dj * gnw_ref[j]
            mean = sum(jnp.sum(dgh[j] * ghat[j], axis=1, keepdims=True) for j in js) / (per * LANES)
            for j in js:
                dg = inv * (dgh[j] - ghat[j] * mean)
                dys_ref[j] = dg * sil[j]
                dz_ref[j] = (dg * y_ref[j] * dsil[j]).astype(BF16)

    blk = pl.BlockSpec((nb, tm, LANES), lambda i: (0, i, 0))
    return pl.pallas_call(
        body, name="ssd_gate_bwd", grid=(t // tm,),
        in_specs=[pl.BlockSpec((tm, d), lambda i: (i, 0)), blk, blk, _resident((nb, 1, LANES)),
                  _resident((SSD_D_INNER, d))],
        out_specs=[blk, blk, pl.BlockSpec((nb, 1, LANES), lambda i: (0, 0, 0))],
        out_shape=[jax.ShapeDtypeStruct((nb, t, LANES), F32), jax.ShapeDtypeStruct((nb, t, LANES), BF16),
                   jax.ShapeDtypeStruct((nb, 1, LANES), F32)],
        compiler_params=_params("arbitrary"),
    )(dy, y, z, gnw, wo)


def _lane_blocks(v):
    r, n = v.shape[0], v.shape[1] // LANES
    return v.reshape(r, n, LANES).transpose(1, 0, 2)


def _ssd_prep(w_in, conv_w, conv_b, dt_bias, a_log, d_skip, norm_w):
    n_main = SSD_D_INNER + SSD_CONV_DIM
    w_dt = w_in[:, n_main:].reshape(-1, SSD_N_GROUPS, HEADS_PER_GROUP)
    w_dt = jnp.pad(w_dt, ((0, 0), (0, 0), (0, LANES - HEADS_PER_GROUP))).reshape(-1, SSD_N_GROUPS * LANES)
    w_in_pad = jnp.concatenate([w_in[:, :n_main], w_dt], axis=1)
    taps = jnp.concatenate([conv_w, conv_b[None], jnp.zeros((SUBLANES - SSD_CONV_W - 1, SSD_CONV_DIM), F32)], axis=0)
    cwb = _lane_blocks(taps)
    row = lambda v: jnp.pad(v.reshape(SSD_N_GROUPS, 1, HEADS_PER_GROUP), ((0, 0), (0, 0), (0, LANES - HEADS_PER_GROUP)))
    dskip = jnp.broadcast_to(jnp.repeat(d_skip, SSD_D_INNER // SSD_N_HEADS).reshape(N_XS_BLK, 1, LANES),
                             (N_XS_BLK, SUBLANES, LANES))
    gnw = norm_w.reshape(N_XS_BLK, 1, LANES)
    return w_in_pad, cwb, row(dt_bias), row(a_log), dskip, gnw


def _ssd_layer_fwd(x, nw, prm, wo, consts):
    w_in_pad, cwb, dtb, alog, dskip, gnw = prm
    z, xr, dtr = _ssd_inproj(x, nw, w_in_pad)
    y, sprev = _ssd_scan_fwd(xr, dtr, cwb, dtb, alog, dskip, consts)
    out, gn = _ssd_gate_fwd(x, y, z, gnw, wo)
    return out, (z, xr, dtr, y, sprev, gn)


def _ssd_layer_bwd(x, dy, nw, prm, wo, consts, saved):
    w_in_pad, cwb, dtb, alog, dskip, gnw = prm
    z, xr, dtr, y, sprev, gn = saved
    dys, dz, dgnw = _ssd_gate_bwd(dy, y, z, gnw, wo)
    dwo = _matmul_tn(gn, dy, name="wgrad_ssd_out")
    dxs, db, dc, ddtr, dcwx, dcwb, dcwc, dd, dsm = _ssd_scan_bwd(xr, dtr, dys, sprev, cwb, dtb, alog, dskip, consts)
    pieces = [dz, dxs, db, dc, ddtr]
    dx, dnw, h = _inproj_bwd(x, dy, nw, w_in_pad, pieces)
    dws = [_matmul_tn_blocked(h, p, name=f"wgrad_ssd_in{i}") for i, p in enumerate(pieces)]
    dw_dt = dws[4].reshape(-1, SSD_N_GROUPS, LANES)[:, :, :HEADS_PER_GROUP].reshape(-1, SSD_N_HEADS)
    dw_in = jnp.concatenate(dws[:4] + [dw_dt], axis=1)
    dtaps = jnp.concatenate([dcwx, dcwb, dcwc], axis=0).transpose(1, 0, 2).reshape(SUBLANES, SSD_CONV_DIM)
    by_head = lambda r: dsm[:, r, :HEADS_PER_GROUP].reshape(SSD_N_HEADS)
    d_d = jnp.sum(dd[:, 0, :].reshape(SSD_N_HEADS, SSD_D_INNER // SSD_N_HEADS), axis=1)
    return dx, (dnw, dw_in, dtaps[:SSD_CONV_W], dtaps[SSD_CONV_W], by_head(0), by_head(1),
                d_d, dgnw.reshape(SSD_D_INNER), dwo)


def _loss_head(x, fw, target):
    t, d = x.shape
    tm = min(TOKEN_TILE, t)

    def body(x_ref, fw_ref, tgt_ref, loss_ref, dx_ref, dfw_ref):
        fwv = fw_ref[...]
        y, xhat, inv = _rms_fwd(x_ref[...], fwv)
        err = y - tgt_ref[...]
        tot = jnp.sum(jnp.sum(err * err, axis=1, keepdims=True), axis=0, keepdims=True)
        dx, dw = _rms_bwd(err * (1.0 / d), xhat, inv, fwv)
        dx_ref[...] = dx

        @pl.when(pl.program_id(0) == 0)
        def _():
            loss_ref[...] = jnp.zeros_like(loss_ref)
            dfw_ref[...] = jnp.zeros_like(dfw_ref)

        loss_ref[...] += jnp.broadcast_to(tot * (0.5 / d), loss_ref.shape)
        dfw_ref[...] += dw

    tok = pl.BlockSpec((tm, d), lambda i: (i, 0))
    return pl.pallas_call(
        body, name="loss_head", grid=(t // tm,),
        in_specs=[tok, _resident((1, d)), tok],
        out_specs=[pl.BlockSpec((1, LANES), lambda i: (0, 0)), tok, pl.BlockSpec((1, d), lambda i: (0, 0))],
        out_shape=[jax.ShapeDtypeStruct((1, LANES), F32), jax.ShapeDtypeStruct((t, d), F32),
                   jax.ShapeDtypeStruct((1, d), F32)],
        compiler_params=_params("arbitrary"),
    )(x, fw, target)


def _row_tile(rows, cap):
    best = SUBLANES
    for r in range(SUBLANES, min(rows, cap) + 1, SUBLANES):
        if rows % r == 0:
            best = r
    return best


def _adamw(w, g, m, v, name):
    rows, cols = w.shape
    br = _row_tile(rows, 256)
    c1 = 1.0 - ADAM_B1 ** ADAM_STEP
    c2 = 1.0 - ADAM_B2 ** ADAM_STEP

    def body(w_ref, g_ref, m_ref, v_ref, d_ref, nm_ref, nv_ref):
        gv = g_ref[...]
        nm = ADAM_B1 * m_ref[...] + (1.0 - ADAM_B1) * gv
        nv = ADAM_B2 * v_ref[...] + (1.0 - ADAM_B2) * (gv * gv)
        nm_ref[...] = nm
        nv_ref[...] = nv
        d_ref[...] = -ADAM_LR * ((nm / c1) / (jnp.sqrt(nv / c2) + ADAM_EPS) + ADAM_WD * w_ref[...])

    blk = pl.BlockSpec((br, cols), lambda i: (i, 0))
    shp = jax.ShapeDtypeStruct((rows, cols), F32)
    return pl.pallas_call(
        body, name=name, grid=(rows // br,), in_specs=[blk] * 4, out_specs=[blk] * 3, out_shape=[shp] * 3,
        compiler_params=_params("parallel"),
    )(w, g, m, v)


def _place():
    x, y, c = lax.axis_index("x"), lax.axis_index("y"), lax.axis_index("c")
    return x, y, c, [(1 - x, y), (x, 1 - y), (1 - x, 1 - y)]


def _remote(src, dst, send_sems, recv_sems, k, to):
    return pltpu.make_async_remote_copy(src_ref=src, dst_ref=dst, send_sem=send_sems.at[k], recv_sem=recv_sems.at[k],
                                        device_id=to, device_id_type=MESH)


def _all_gather_shards(arrs):
    n = len(arrs)

    def body(*refs):
        srcs, dsts = refs[:n], refs[n:2 * n]
        send_sems, recv_sems = refs[2 * n:]
        x, y, c, chips = _place()
        me = 2 * x + y
        sibling = (x, y, 1 - c)
        sent = []
        for oi, (src, dst) in enumerate(zip(srcs, dsts)):
            for j, chip in enumerate(chips):
                sent.append(_remote(src.at[c], dst.at[me, c], send_sems, recv_sems, 6 * oi + j, (*chip, c)))
                sent[-1].start()
        for oi, dst in enumerate(dsts):
            for j, chip in enumerate(chips):
                landed = dst.at[2 * chip[0] + chip[1], c]
                _remote(landed, landed, send_sems, recv_sems, 6 * oi + j, (*chip, c)).wait_recv()
                sent.append(_remote(landed, landed, send_sems, recv_sems, 6 * oi + 3 + j, sibling))
                sent[-1].start()
        for oi, dst in enumerate(dsts):
            for j, chip in enumerate(chips):
                landed = dst.at[2 * chip[0] + chip[1], 1 - c]
                _remote(landed, landed, send_sems, recv_sems, 6 * oi + 3 + j, sibling).wait_recv()
        for cp in sent:
            cp.wait_send()

    return pl.pallas_call(
        body, name="all_gather_shards",
        in_specs=[_HBM] * n, out_specs=[_HBM] * n,
        out_shape=[jax.ShapeDtypeStruct((N_SHARDS,) + a.shape, a.dtype) for a in arrs],
        scratch_shapes=[pltpu.SemaphoreType.DMA((6 * n,)), pltpu.SemaphoreType.DMA((6 * n,))],
    )(*arrs)


def _swap_halves(arrs):
    n = len(arrs)

    def body(*refs):
        srcs, dsts = refs[:n], refs[n:2 * n]
        send_sems, recv_sems = refs[2 * n:]
        x, y, c, _ = _place()
        cps = [_remote(src.at[:, 1 - c], dst, send_sems, recv_sems, oi, (x, y, 1 - c))
               for oi, (src, dst) in enumerate(zip(srcs, dsts))]
        for cp in cps:
            cp.start()
        for cp in cps:
            cp.wait()

    return pl.pallas_call(
        body, name="swap_halves", in_specs=[_HBM] * n, out_specs=[_HBM] * n,
        out_shape=[jax.ShapeDtypeStruct((a.shape[0],) + a.shape[2:], a.dtype) for a in arrs],
        scratch_shapes=[pltpu.SemaphoreType.DMA((n,)), pltpu.SemaphoreType.DMA((n,))],
    )(*arrs)


def _scatter_to_chips(arrs):
    n = len(arrs)

    def body(*refs):
        srcs, dsts = refs[:n], refs[n:2 * n]
        send_sems, recv_sems = refs[2 * n:]
        x, y, c, chips = _place()
        sent = []
        for oi, (src, dst) in enumerate(zip(srcs, dsts)):
            for j, chip in enumerate(chips):
                sent.append(_remote(src.at[2 * chip[0] + chip[1]], dst.at[j], send_sems, recv_sems, 3 * oi + j,
                                    (*chip, c)))
                sent[-1].start()
        for oi, dst in enumerate(dsts):
            for j, chip in enumerate(chips):
                _remote(dst.at[j], dst.at[j], send_sems, recv_sems, 3 * oi + j, (*chip, c)).wait_recv()
        for cp in sent:
            cp.wait_send()

    return pl.pallas_call(
        body, name="scatter_to_chips", in_specs=[_HBM] * n, out_specs=[_HBM] * n,
        out_shape=[jax.ShapeDtypeStruct((3,) + a.shape[1:], a.dtype) for a in arrs],
        scratch_shapes=[pltpu.SemaphoreType.DMA((3 * n,)), pltpu.SemaphoreType.DMA((3 * n,))],
    )(*arrs)


def _join_halves(arrs):
    n = len(arrs)

    def body(*refs):
        bufs = refs[n:2 * n]
        send_sems, recv_sems = refs[2 * n:]
        x, y, c, _ = _place()
        sibling = (x, y, 1 - c)
        sent = [_remote(buf.at[c], buf.at[c], send_sems, recv_sems, oi, sibling) for oi, buf in enumerate(bufs)]
        for cp in sent:
            cp.start()
        for oi, buf in enumerate(bufs):
            _remote(buf.at[1 - c], buf.at[1 - c], send_sems, recv_sems, oi, sibling).wait_recv()
        for cp in sent:
            cp.wait_send()

    return pl.pallas_call(
        body, name="join_halves", in_specs=[_HBM] * n, out_specs=[_HBM] * n,
        out_shape=[jax.ShapeDtypeStruct(a.shape, a.dtype) for a in arrs],
        input_output_aliases={i: i for i in range(n)},
        scratch_shapes=[pltpu.SemaphoreType.DMA((n,)), pltpu.SemaphoreType.DMA((n,))],
    )(*arrs)


def _add_halves(full, recv, place):
    n, _, rows, cols = full.shape
    br = _row_tile(rows, 512)

    def body(p_ref, a_ref, b_ref, o_ref):
        o_ref[...] = (a_ref[...] + b_ref[...]).astype(BF16)

    grid_spec = pltpu.PrefetchScalarGridSpec(
        num_scalar_prefetch=1, grid=(n, rows // br),
        in_specs=[pl.BlockSpec((None, None, br, cols), lambda s, i, p_ref: (s, p_ref[1], i, 0)),
                  pl.BlockSpec((None, br, cols), lambda s, i, p_ref: (s, i, 0))],
        out_specs=pl.BlockSpec((None, br, cols), lambda s, i, p_ref: (s, i, 0)))
    return pl.pallas_call(
        body, name="add_halves", grid_spec=grid_spec, out_shape=jax.ShapeDtypeStruct((n, rows, cols), BF16),
        compiler_params=_params("parallel", "parallel"),
    )(place, full, recv)


def _sum_chips(mine, others, place):
    _, rows, cols = mine.shape
    br = _row_tile(rows, 512)
    slot_of_flip = {2: 0, 1: 1, 3: 2}

    def body(p_ref, m_ref, o_ref, out_ref):
        me = p_ref[0]
        own = m_ref[...].astype(F32)
        got = [o_ref[j].astype(F32) for j in range(3)]
        acc = None
        for s in range(N_SHARDS):
            flip = jnp.bitwise_xor(me, s)
            term = own
            for f, j in slot_of_flip.items():
                term = jnp.where(flip == f, got[j], term)
            acc = term if acc is None else acc + term
        out_ref[...] = acc

    grid_spec = pltpu.PrefetchScalarGridSpec(
        num_scalar_prefetch=1, grid=(rows // br,),
        in_specs=[pl.BlockSpec((None, br, cols), lambda i, p_ref: (p_ref[0], i, 0)),
                  pl.BlockSpec((3, br, cols), lambda i, p_ref: (0, i, 0))],
        out_specs=pl.BlockSpec((None, br, cols), lambda i, p_ref: (p_ref[1], i, 0)))
    return pl.pallas_call(
        body, name="sum_chips", grid_spec=grid_spec, out_shape=jax.ShapeDtypeStruct((2, rows, cols), F32),
        compiler_params=_params("parallel"),
    )(place, mine, others)


WEIGHTS = ("norm_w", "ffn_w_gate", "ffn_w_up", "ffn_w_down", "ssd_w_in", "ssd_conv_w", "ssd_conv_b", "ssd_dt_bias",
           "ssd_a_log", "ssd_d", "ssd_norm_w", "ssd_w_out", "sc_w_in", "sc_conv_w", "sc_w_out", "final_norm_w")
BIG = (("ffn_w_gate", 3), ("ffn_w_up", 3), ("ffn_w_down", 2), ("ssd_w_in", 2), ("ssd_w_out", 1), ("sc_w_in", 2),
       ("sc_w_out", 1))
SMALL_SHARDED = (("norm_w", 2), ("ssd_conv_w", 2), ("sc_conv_w", 2))
REPLICATED = ("ssd_conv_b", "ssd_dt_bias", "ssd_a_log", "ssd_d", "ssd_norm_w", "final_norm_w")
FLAT_COLS = 1024


def _pack(arrays, row_multiple, lead=()):
    flat = jnp.concatenate([a.reshape(lead + (-1,)) for a in arrays], axis=len(lead))
    unit = row_multiple * FLAT_COLS
    n = flat.shape[-1]
    pad = (-n) % unit
    if pad:
        flat = jnp.pad(flat, [(0, 0)] * len(lead) + [(0, pad)])
    return flat.reshape(lead + (-1, FLAT_COLS))


def _unpack(flat, shapes, lead=()):
    flat = flat.reshape(lead + (-1,))
    out, off = [], 0
    for shp in shapes:
        n = 1
        for s in shp:
            n *= s
        out.append(flat[..., off:off + n].reshape(lead + tuple(shp)))
        off += n
    return out


def _to_shards(full, axis):
    shp = full.shape
    r = full.reshape(shp[:axis] + (N_SHARDS, shp[axis] // N_SHARDS) + shp[axis + 1:])
    return jnp.moveaxis(r, axis, 0)


def _from_shards(sh, axis):
    r = jnp.moveaxis(sh, 0, axis)
    shp = r.shape
    return r.reshape(shp[:axis] + (shp[axis] * shp[axis + 1],) + shp[axis + 2:])


def _forward_backward(x, target, p):
    consts = _ssd_consts()
    nw = p["norm_w"]
    row = lambda v: v[None]
    ssd_prm = [_ssd_prep(p["ssd_w_in"][j], p["ssd_conv_w"][j], p["ssd_conv_b"][j], p["ssd_dt_bias"][j],
                         p["ssd_a_log"][j], p["ssd_d"][j], p["ssd_norm_w"][j]) for j in range(2)]
    sc_cw = [jnp.pad(p["sc_conv_w"][j], ((0, SUBLANES - SC_CONV_W), (0, 0))) for j in range(2)]
    ffn = lambda i, k: (p["ffn_w_gate"][i, k], p["ffn_w_up"][i, k], p["ffn_w_down"][i, k])

    xin, saved = [], []
    for i in range(N_LAYERS):
        j = i // 2
        xin.append(x)
        x = _ffn_fwd(x, row(nw[i, 0]), *ffn(i, 0))
        xin.append(x)
        if i % 2 == 0:
            x, sv = _ssd_layer_fwd(x, row(nw[i, 1]), ssd_prm[j], p["ssd_w_out"][j], consts)
        else:
            sv = _norm_mm(x, row(nw[i, 1]), p["sc_w_in"][j])
            x = _sc_fwd(x, sv, sc_cw[j], p["sc_w_out"][j])
        saved.append(sv)
        xin.append(x)
        x = _ffn_fwd(x, row(nw[i, 2]), *ffn(i, 1))
    loss, dx, dfw = _loss_head(x, row(p["final_norm_w"]), target)

    g_nw = [[None] * 3 for _ in range(N_LAYERS)]
    g_ffn = {}
    g_ssd = [None, None]
    g_sc = [None, None]

    def ffn_bwd(i, k, slot, dy):
        wg, wu, wd = ffn(i, k)
        dxn, dnw, h, a, dg, du = _ffn_bwd(xin[3 * i + slot], dy, row(nw[i, slot]), wg, wu, wd)
        g_nw[i][slot] = dnw[0]
        for n, lhs, rhs, scale in (("ffn_w_gate", h, dg, 1.0), ("ffn_w_up", h, du, 1.0), ("ffn_w_down", a, dy, 0.5)):
            g_ffn[n] = _matmul_tn(lhs, rhs, scale=scale, name="wgrad_" + n, slab=(i, k), stack=(N_LAYERS, 2),
                                  buf=g_ffn.get(n))
        return dxn

    for i in reversed(range(N_LAYERS)):
        j = i // 2
        dx = ffn_bwd(i, 1, 2, dx)
        xm = xin[3 * i + 1]
        if i % 2 == 0:
            dx, gs = _ssd_layer_bwd(xm, dx, row(nw[i, 1]), ssd_prm[j], p["ssd_w_out"][j], consts, saved[i])
            g_nw[i][1] = gs[0][0]
            g_ssd[j] = gs[1:]
        else:
            bcu = saved[i]
            dbcu, pin, dcw = _sc_bwd(dx, bcu, sc_cw[j], p["sc_w_out"][j])
            dwo = _matmul_tn(pin, dx, name="wgrad_sc_out")
            dx, dnw, h = _inproj_bwd(xm, dx, row(nw[i, 1]), p["sc_w_in"][j], [dbcu])
            g_nw[i][1] = dnw[0]
            g_sc[j] = (_matmul_tn(h, dbcu, name="wgrad_sc_in"), dcw[:SC_CONV_W], dwo)
        dx = ffn_bwd(i, 0, 0, dx)

    g = {"norm_w": jnp.stack([jnp.stack(r) for r in g_nw]), "final_norm_w": dfw[0], **g_ffn}
    for k, n in enumerate(("ssd_w_in", "ssd_conv_w", "ssd_conv_b", "ssd_dt_bias", "ssd_a_log", "ssd_d", "ssd_norm_w",
                           "ssd_w_out")):
        g[n] = jnp.stack([g_ssd[0][k], g_ssd[1][k]])
    for k, n in enumerate(("sc_w_in", "sc_conv_w", "sc_w_out")):
        g[n] = jnp.stack([g_sc[0][k], g_sc[1][k]])
    return loss, dx, g


def kernel(x, norm_w, ffn_w_gate, ffn_w_up, ffn_w_down, ssd_w_in, ssd_conv_w, ssd_conv_b, ssd_dt_bias, ssd_a_log, ssd_d, ssd_norm_w, ssd_w_out, sc_w_in, sc_conv_w, sc_w_out, final_norm_w, loss_target, m_norm_w, m_ffn_w_gate, m_ffn_w_up, m_ffn_w_down, m_ssd_w_in, m_ssd_conv_w, m_ssd_conv_b, m_ssd_dt_bias, m_ssd_a_log, m_ssd_d, m_ssd_norm_w, m_ssd_w_out, m_sc_w_in, m_sc_conv_w, m_sc_w_out, m_final_norm_w, v_norm_w, v_ffn_w_gate, v_ffn_w_up, v_ffn_w_down, v_ssd_w_in, v_ssd_conv_w, v_ssd_conv_b, v_ssd_dt_bias, v_ssd_a_log, v_ssd_d, v_ssd_norm_w, v_ssd_w_out, v_sc_w_in, v_sc_conv_w, v_sc_w_out, v_final_norm_w):
    w = dict(zip(WEIGHTS, (norm_w, ffn_w_gate, ffn_w_up, ffn_w_down, ssd_w_in, ssd_conv_w, ssd_conv_b, ssd_dt_bias,
                           ssd_a_log, ssd_d, ssd_norm_w, ssd_w_out, sc_w_in, sc_conv_w, sc_w_out, final_norm_w)))
    m = dict(zip(WEIGHTS, (m_norm_w, m_ffn_w_gate, m_ffn_w_up, m_ffn_w_down, m_ssd_w_in, m_ssd_conv_w, m_ssd_conv_b,
                           m_ssd_dt_bias, m_ssd_a_log, m_ssd_d, m_ssd_norm_w, m_ssd_w_out, m_sc_w_in, m_sc_conv_w,
                           m_sc_w_out, m_final_norm_w)))
    v = dict(zip(WEIGHTS, (v_norm_w, v_ffn_w_gate, v_ffn_w_up, v_ffn_w_down, v_ssd_w_in, v_ssd_conv_w, v_ssd_conv_b,
                           v_ssd_dt_bias, v_ssd_a_log, v_ssd_d, v_ssd_norm_w, v_ssd_w_out, v_sc_w_in, v_sc_conv_w,
                           v_sc_w_out, v_final_norm_w)))
    chip = 2 * lax.axis_index("x") + lax.axis_index("y")
    place = jnp.stack([chip, lax.axis_index("c")]).astype(jnp.int32)
    big_names = [n for n, _ in BIG]
    small_names = [n for n, _ in SMALL_SHARDED] + list(REPLICATED)
    halved = lambda a, lead=(): a.reshape(lead + (2, -1, a.shape[-1]))

    shards = [halved(w[n].astype(BF16)) for n in big_names] + [halved(_pack([w[n] for n, _ in SMALL_SHARDED], 2 * SUBLANES))]
    gathered = [lax.dynamic_update_index_in_dim(g_, s_, chip, 0) for g_, s_ in zip(_all_gather_shards(shards), shards)]
    p = {n: w[n] for n in REPLICATED}
    for (n, ax), sh in zip(BIG, gathered):
        p[n] = _from_shards(sh.reshape((N_SHARDS,) + w[n].shape), ax)
    for (n, ax), sh in zip(SMALL_SHARDED, _unpack(gathered[-1], [w[n].shape for n, _ in SMALL_SHARDED], lead=(N_SHARDS,))):
        p[n] = _from_shards(sh, ax)

    t, d = x.shape[-2:]
    loss, dx, g = _forward_backward(x.reshape(t, d), loss_target.reshape(t, d), p)

    small_part = _pack([_to_shards(g[n], ax) for n, ax in SMALL_SHARDED]
                       + [jnp.broadcast_to(g[n][None], (N_SHARDS,) + g[n].shape) for n in REPLICATED],
                       4 * SUBLANES, lead=(N_SHARDS,))
    parts = [halved(_to_shards(g[n], ax), lead=(N_SHARDS,)) for n, ax in BIG] + [halved(small_part, lead=(N_SHARDS,))]
    chip_sums = [_add_halves(a, r, place) for a, r in zip(parts, _swap_halves(parts))]
    reduced = _join_halves([_sum_chips(mine, others, place)
                            for mine, others in zip(chip_sums, _scatter_to_chips(chip_sums))])

    grad = {n: r.reshape(w[n].shape) for n, r in zip(big_names, reduced)}
    g_small = reduced[-1].reshape(-1, FLAT_COLS)
    grad.update(zip(small_names, _unpack(g_small, [w[n].shape for n in small_names])))

    delta, new_m, new_v = {}, {}, {}
    for n in big_names:
        shp = w[n].shape
        as2d = lambda a: a.reshape(-1, shp[-1])
        out = _adamw(as2d(w[n]), as2d(grad[n]), as2d(m[n]), as2d(v[n]), name="adamw_" + n)
        delta[n], new_m[n], new_v[n] = (o.reshape(shp) for o in out)
    packed = [_pack([s[n] for n in small_names], 4 * SUBLANES) for s in (w, m, v)]
    out = _adamw(packed[0], g_small, packed[1], packed[2], name="adamw_small")
    shapes = [w[n].shape for n in small_names]
    for dst, o in zip((delta, new_m, new_v), out):
        dst.update(zip(small_names, _unpack(o, shapes)))

    loss = lax.psum(loss[0, 0], ("x", "y", "c"))
    return (loss, dx.reshape(x.shape), *[grad[n] for n in WEIGHTS], *[delta[n] for n in WEIGHTS],
            *[new_m[n] for n in WEIGHTS], *[new_v[n] for n in WEIGHTS])
```

```python
import functools

import jax
import jax.numpy as jnp
from jax import lax
from jax.experimental import pallas as pl
from jax.experimental.pallas import tpu as pltpu

F32 = jnp.float32
BF16 = jnp.bfloat16
MESH = pl.DeviceIdType.MESH

RMS_EPS = 1e-5
D_MODEL = 1024
D_FF = 2816
N_LAYERS = 4
SSD_D_INNER = 2048
SSD_N_HEADS = 32
SSD_N_GROUPS = 4
SSD_D_STATE = 128
SSD_CHUNK = 128
SSD_CONV_W = 4
SSD_CONV_DIM = 3072
SSD_IN_DIM = 5152
SC_CONV_W = 3
LANES = 128
SUBLANES = 8
N_XS_BLK = SSD_D_INNER // LANES
SSD_IN_PAD = SSD_D_INNER + SSD_CONV_DIM + SSD_N_GROUPS * LANES
VMEM_LIMIT = 56 * 2**20
TOKEN_TILE = 512
WGRAD_TOKENS = 2048
FF_CHUNK = 256
N_SHARDS = 4

ADAM_LR = 0.001
ADAM_B1 = 0.9
ADAM_B2 = 0.999
ADAM_EPS = 1e-08
ADAM_WD = 0.01
ADAM_STEP = 10


_HBM = pl.BlockSpec(memory_space=pl.ANY)


def _params(*sem):
    return pltpu.CompilerParams(dimension_semantics=sem if sem else None, vmem_limit_bytes=VMEM_LIMIT)


def _dot(a, b):
    return jnp.dot(a, b, preferred_element_type=F32)


def _dot_nt(a, b):
    return lax.dot_general(a, b, (((1,), (1,)), ((), ())), preferred_element_type=F32)


def _dot_tn(a, b):
    return lax.dot_general(a, b, (((0,), (0,)), ((), ())), preferred_element_type=F32)


def _resident(shape):
    n = len(shape)
    return pl.BlockSpec(shape, lambda *_: (0,) * n, pipeline_mode=pl.Buffered(1))


def _split3(v):
    hi = v.astype(BF16)
    r1 = v - hi.astype(F32)
    mid = r1.astype(BF16)
    lo = (r1 - mid.astype(F32)).astype(BF16)
    return hi, mid, lo


def _sel_left(sel, v3):
    return _dot(sel, v3[0]) + _dot(sel, v3[1]) + _dot(sel, v3[2])


def _sigmoid(v):
    return 1.0 / (1.0 + jnp.exp(-v))


def _rms_fwd(x, w):
    inv = lax.rsqrt(jnp.mean(x * x, axis=-1, keepdims=True) + RMS_EPS)
    xhat = x * inv
    return xhat * w, xhat, inv


def _rms_bwd(dh, xhat, inv, w):
    dxhat = dh * w
    dx = inv * (dxhat - xhat * jnp.mean(dxhat * xhat, axis=-1, keepdims=True))
    return dx, jnp.sum(dh * xhat, axis=0, keepdims=True)


def _ffn_fwd(x, nw, wg, wu, wd):
    t, d = x.shape
    f = wg.shape[1]
    tm = min(TOKEN_TILE, t)

    def body(x_ref, nw_ref, wg_ref, wu_ref, wd_ref, o_ref, g_ref, u_ref):
        xv = x_ref[...]
        h = _rms_fwd(xv, nw_ref[...])[0].astype(BF16)
        acc = jnp.zeros((tm, d), F32)
        for j in range(f // FF_CHUNK):
            sl = slice(j * FF_CHUNK, (j + 1) * FF_CHUNK)
            g = _dot(h, wg_ref[:, sl])
            u = _dot(h, wu_ref[:, sl])
            g_ref[:, sl] = g.astype(BF16)
            u_ref[:, sl] = u.astype(BF16)
            a = (g * _sigmoid(g) * u).astype(BF16)
            acc = acc + _dot(a, wd_ref[sl, :])
        o_ref[...] = xv + 0.5 * acc

    tok = lambda n: pl.BlockSpec((tm, n), lambda i: (i, 0))
    return pl.pallas_call(
        body, name="ffn_fwd", grid=(t // tm,),
        in_specs=[tok(d), _resident((1, d)), _resident((d, f)), _resident((d, f)), _resident((f, d))],
        out_specs=[tok(d), tok(f), tok(f)],
        out_shape=[jax.ShapeDtypeStruct((t, d), F32), jax.ShapeDtypeStruct((t, f), BF16),
                   jax.ShapeDtypeStruct((t, f), BF16)],
        compiler_params=_params("parallel"),
    )(x, nw, wg, wu, wd)


def _ffn_bwd_act(dy, gs, us, wd):
    t, d = dy.shape
    f = wd.shape[0]
    tm = min(TOKEN_TILE, t)

    def body(dy_ref, g_ref, u_ref, wd_ref, a_ref, dg_ref, du_ref):
        dob = (0.5 * dy_ref[...]).astype(BF16)
        for j in range(f // FF_CHUNK):
            sl = slice(j * FF_CHUNK, (j + 1) * FF_CHUNK)
            g = g_ref[:, sl].astype(F32)
            u = u_ref[:, sl].astype(F32)
            sig = _sigmoid(g)
            s = g * sig
            a_ref[:, sl] = (s * u).astype(BF16)
            da = _dot_nt(dob, wd_ref[sl, :])
            dg_ref[:, sl] = (da * u * (sig * (1.0 + g * (1.0 - sig)))).astype(BF16)
            du_ref[:, sl] = (da * s).astype(BF16)

    tok = lambda n: pl.BlockSpec((tm, n), lambda i: (i, 0))
    return pl.pallas_call(
        body, name="ffn_bwd_act", grid=(t // tm,),
        in_specs=[tok(d), tok(f), tok(f), _resident((f, d))],
        out_specs=[tok(f), tok(f), tok(f)],
        out_shape=[jax.ShapeDtypeStruct((t, f), BF16)] * 3,
        compiler_params=_params("parallel"),
    )(dy, gs, us, wd)


def _pick_bn(m, n, unit):
    best = unit
    for k in range(1, n // unit + 1):
        bn = k * unit
        if n % bn == 0 and m * bn * 4 <= 8 * 2**20:
            best = bn
    return best


def _matmul_tn(a, b, scale=1.0, name="wgrad", slab=None, stack=None, buf=None):
    t, m = a.shape
    n = b.shape[1]
    bt = min(WGRAD_TOKENS, t)
    bn = _pick_bn(m, n, LANES)
    nt = t // bt
    lead = tuple(slab) if slab is not None else ()

    def body(a_ref, b_ref, *rest):
        o_ref = rest[-1]

        @pl.when(pl.program_id(1) == 0)
        def _():
            o_ref[...] = jnp.zeros_like(o_ref)

        o_ref[...] += _dot_tn(a_ref[...].astype(BF16), b_ref[...].astype(BF16))
        if scale != 1.0:
            @pl.when(pl.program_id(1) == nt - 1)
            def _():
                o_ref[...] *= scale

    in_specs = [pl.BlockSpec((bt, m), lambda j, k: (k, 0)), pl.BlockSpec((bt, bn), lambda j, k: (k, j))]
    args = [a, b]
    if buf is not None:
        in_specs.append(_HBM)
        args.append(buf)
    return pl.pallas_call(
        body, name=name, grid=(n // bn, nt),
        in_specs=in_specs,
        out_specs=pl.BlockSpec((None,) * len(lead) + (m, bn), lambda j, k: lead + (0, j)),
        out_shape=jax.ShapeDtypeStruct(tuple(stack or ()) + (m, n), F32),
        input_output_aliases={2: 0} if buf is not None else {},
        compiler_params=_params("parallel", "arbitrary"),
    )(*args)


def _matmul_tn_blocked(a, b, name="wgrad_blk"):
    t, m = a.shape
    nb = b.shape[0]
    bt = min(1024, t)
    nbt = _pick_bn(m, nb * LANES, LANES) // LANES
    while nb % nbt:
        nbt -= 1

    def body(a_ref, b_ref, o_ref):
        @pl.when(pl.program_id(1) == 0)
        def _():
            o_ref[...] = jnp.zeros_like(o_ref)

        bv = jnp.concatenate([b_ref[i] for i in range(nbt)], axis=1) if nbt > 1 else b_ref[0]
        o_ref[...] += _dot_tn(a_ref[...], bv)

    return pl.pallas_call(
        body, name=name, grid=(nb // nbt, t // bt),
        in_specs=[pl.BlockSpec((bt, m), lambda j, k: (k, 0)), pl.BlockSpec((nbt, bt, LANES), lambda j, k: (j, k, 0))],
        out_specs=pl.BlockSpec((m, nbt * LANES), lambda j, k: (0, j)),
        out_shape=jax.ShapeDtypeStruct((m, nb * LANES), F32),
        compiler_params=_params("parallel", "arbitrary"),
    )(a, b)


def _norm_mm(x, nw, w):
    t, d = x.shape
    n = w.shape[1]
    tm = min(TOKEN_TILE, t)
    cn = 1024 if n % 1024 == 0 else n

    def body(x_ref, nw_ref, w_ref, o_ref):
        h = _rms_fwd(x_ref[...], nw_ref[...])[0].astype(BF16)
        for j in range(n // cn):
            sl = slice(j * cn, (j + 1) * cn)
            o_ref[:, sl] = _dot(h, w_ref[:, sl])

    return pl.pallas_call(
        body, name="norm_mm", grid=(t // tm,),
        in_specs=[pl.BlockSpec((tm, d), lambda i: (i, 0)), _resident((1, d)), _resident((d, n))],
        out_specs=pl.BlockSpec((tm, n), lambda i: (i, 0)),
        out_shape=jax.ShapeDtypeStruct((t, n), F32),
        compiler_params=_params("parallel"),
    )(x, nw, w)


def _ssd_inproj(x, nw, w):
    t, d = x.shape
    tm = min(TOKEN_TILE, t)
    nz, nx, ng = SSD_D_INNER // LANES, SSD_CONV_DIM // LANES, SSD_N_GROUPS
    cn = 1024

    def body(x_ref, nw_ref, w_ref, z_ref, xr_ref, dt_ref):
        h = _rms_fwd(x_ref[...], nw_ref[...])[0].astype(BF16)
        for j in range(-(-SSD_IN_PAD // cn)):
            lo, hi = j * cn, min((j + 1) * cn, SSD_IN_PAD)
            r = _dot(h, w_ref[:, lo:hi])
            for i in range((hi - lo) // LANES):
                blk = j * (cn // LANES) + i
                v = r[:, i * LANES:(i + 1) * LANES]
                if blk < nz:
                    z_ref[blk] = v
                elif blk < nz + nx:
                    xr_ref[blk - nz] = v
                else:
                    dt_ref[blk - nz - nx] = v

    out = lambda n: pl.BlockSpec((n, tm, LANES), lambda i: (0, i, 0))
    return pl.pallas_call(
        body, name="ssd_inproj", grid=(t // tm,),
        in_specs=[pl.BlockSpec((tm, d), lambda i: (i, 0)), _resident((1, d)), _resident((d, SSD_IN_PAD))],
        out_specs=[out(nz), out(nx), out(ng)],
        out_shape=[jax.ShapeDtypeStruct((n, t, LANES), F32) for n in (nz, nx, ng)],
        compiler_params=_params("parallel"),
    )(x, nw, w)


def _inproj_bwd(x, dy, nw, ws, pieces):
    t, d = x.shape
    tm = min(TOKEN_TILE, t)
    nws = len(ws)
    flat = [p for group in pieces for p in group]

    def body(*refs):
        x_ref, dy_ref, nw_ref = refs[:3]
        w_refs = refs[3:3 + nws]
        p_refs = list(refs[3 + nws:3 + nws + len(flat)])
        dx_ref, dnw_ref, h_ref = refs[3 + nws + len(flat):]
        nwv = nw_ref[...]
        hf, xhat, inv = _rms_fwd(x_ref[...], nwv)
        h_ref[...] = hf.astype(BF16)
        dh = None
        for w_ref, group in zip(w_refs, pieces):
            parts = []
            for _ in group:
                p = p_refs.pop(0)
                parts += [p[i] for i in range(p.shape[0])] if len(p.shape) == 3 else [p[...]]
            dz = jnp.concatenate(parts, axis=1) if len(parts) > 1 else parts[0]
            part = _dot_nt(dz, w_ref[...])
            dh = part if dh is None else dh + part
        dx, dw = _rms_bwd(dh, xhat, inv, nwv)
        dx_ref[...] = dy_ref[...] + dx

        @pl.when(pl.program_id(0) == 0)
        def _():
            dnw_ref[...] = jnp.zeros_like(dnw_ref)

        dnw_ref[...] += dw

    tok = lambda m: pl.BlockSpec((tm, m), lambda i: (i, 0))
    p_specs = [pl.BlockSpec((p.shape[0], tm, LANES), lambda i: (0, i, 0)) if p.ndim == 3 else tok(p.shape[1])
               for p in flat]
    return pl.pallas_call(
        body, name="inproj_bwd", grid=(t // tm,),
        in_specs=[tok(d), tok(d), _resident((1, d))] + [_resident(w.shape) for w in ws] + p_specs,
        out_specs=[tok(d), pl.BlockSpec((1, d), lambda i: (0, 0)), tok(d)],
        out_shape=[jax.ShapeDtypeStruct((t, d), F32), jax.ShapeDtypeStruct((1, d), F32),
                   jax.ShapeDtypeStruct((t, d), BF16)],
        compiler_params=_params("arbitrary"),
    )(x, dy, nw, *ws, *flat)


def _shift_down(v, j, prev8):
    if j == 0:
        return v
    r = pltpu.roll(v, j, 0)
    p = pltpu.roll(prev8, j, 0)
    rows = lax.broadcasted_iota(jnp.int32, prev8.shape, 0)
    first = jnp.where(rows < j, p, r[0:SUBLANES])
    return jnp.concatenate([first, r[SUBLANES:]], axis=0)


def _shift_up(v, j, next8):
    if j == 0:
        return v
    n = v.shape[0]
    r = pltpu.roll(v, n - j, 0)
    p = pltpu.roll(next8, SUBLANES - j, 0)
    rows = lax.broadcasted_iota(jnp.int32, next8.shape, 0)
    last = jnp.where(rows >= SUBLANES - j, p, r[n - SUBLANES:])
    return jnp.concatenate([r[:n - SUBLANES], last], axis=0)


def _sc_fwd(x, bcu, cw, wo):
    t, d = x.shape
    tm = min(TOKEN_TILE, t)
    hb = tm // SUBLANES

    def body(x_ref, bcu_ref, prev_ref, cw_ref, wo_ref, o_ref):
        bg, cg, u = bcu_ref[:, 0:d], bcu_ref[:, d:2 * d], bcu_ref[:, 2 * d:3 * d]
        q = cg * u
        qp = jnp.where(pl.program_id(0) == 0, 0.0, prev_ref[:, d:2 * d] * prev_ref[:, 2 * d:3 * d])
        cwv = cw_ref[...]
        v = cwv[2:3] * q + cwv[1:2] * _shift_down(q, 1, qp) + cwv[0:1] * _shift_down(q, 2, qp)
        o_ref[...] = x_ref[...] + _dot((bg * v).astype(BF16), wo_ref[...])

    return pl.pallas_call(
        body, name="sc_fwd", grid=(t // tm,),
        in_specs=[pl.BlockSpec((tm, d), lambda i: (i, 0)), pl.BlockSpec((tm, 3 * d), lambda i: (i, 0)),
                  pl.BlockSpec((SUBLANES, 3 * d), lambda i: (jnp.maximum(i * hb - 1, 0), 0)),
                  _resident((SUBLANES, d)), _resident((d, d))],
        out_specs=pl.BlockSpec((tm, d), lambda i: (i, 0)),
        out_shape=jax.ShapeDtypeStruct((t, d), F32),
        compiler_params=_params("parallel"),
    )(x, bcu, bcu, cw, wo)


def _sc_bwd(dy, bcu, cw, wo):
    t, d = dy.shape
    tm = min(TOKEN_TILE, t)
    hb = tm // SUBLANES
    nt = t // tm

    def body(dy_ref, dyn_ref, bcu_ref, prev_ref, next_ref, cw_ref, wo_ref, dbcu_ref, p_ref, dcw_ref):
        i = pl.program_id(0)
        bg, cg, u = bcu_ref[:, 0:d], bcu_ref[:, d:2 * d], bcu_ref[:, 2 * d:3 * d]
        q = cg * u
        qp = jnp.where(i == 0, 0.0, prev_ref[:, d:2 * d] * prev_ref[:, 2 * d:3 * d])
        cwv = cw_ref[...]
        q1 = _shift_down(q, 1, qp)
        q2 = _shift_down(q, 2, qp)
        v = cwv[2:3] * q + cwv[1:2] * q1 + cwv[0:1] * q2
        p_ref[...] = (bg * v).astype(BF16)
        wov = wo_ref[...]
        dp = _dot_nt(dy_ref[...].astype(BF16), wov)
        dpn = _dot_nt(dyn_ref[...].astype(BF16), wov)
        dv = dp * bg
        dvn = jnp.where(i == nt - 1, 0.0, dpn * next_ref[:, 0:d])
        dq = cwv[2:3] * dv + cwv[1:2] * _shift_up(dv, 1, dvn) + cwv[0:1] * _shift_up(dv, 2, dvn)
        dbcu_ref[:, 0:d] = (dp * v).astype(BF16)
        dbcu_ref[:, d:2 * d] = (dq * u).astype(BF16)
        dbcu_ref[:, 2 * d:3 * d] = (dq * cg).astype(BF16)

        @pl.when(i == 0)
        def _():
            dcw_ref[...] = jnp.zeros_like(dcw_ref)

        dcw_ref[0:1, :] += jnp.sum(dv * q2, axis=0, keepdims=True)
        dcw_ref[1:2, :] += jnp.sum(dv * q1, axis=0, keepdims=True)
        dcw_ref[2:3, :] += jnp.sum(dv * q, axis=0, keepdims=True)

    last8 = t // SUBLANES - 1
    return pl.pallas_call(
        body, name="sc_bwd", grid=(nt,),
        in_specs=[pl.BlockSpec((tm, d), lambda i: (i, 0)),
                  pl.BlockSpec((SUBLANES, d), lambda i: (jnp.minimum((i + 1) * hb, last8), 0)),
                  pl.BlockSpec((tm, 3 * d), lambda i: (i, 0)),
                  pl.BlockSpec((SUBLANES, 3 * d), lambda i: (jnp.maximum(i * hb - 1, 0), 0)),
                  pl.BlockSpec((SUBLANES, 3 * d), lambda i: (jnp.minimum((i + 1) * hb, last8), 0)),
                  _resident((SUBLANES, d)), _resident((d, d))],
        out_specs=[pl.BlockSpec((tm, 3 * d), lambda i: (i, 0)), pl.BlockSpec((tm, d), lambda i: (i, 0)),
                   pl.BlockSpec((SUBLANES, d), lambda i: (0, 0))],
        out_shape=[jax.ShapeDtypeStruct((t, 3 * d), BF16), jax.ShapeDtypeStruct((t, d), BF16),
                   jax.ShapeDtypeStruct((SUBLANES, d), F32)],
        compiler_params=_params("arbitrary"),
    )(dy, dy, bcu, bcu, bcu, cw, wo)


NEG_BIG = -1e30


HEADS_PER_GROUP = SSD_N_HEADS // SSD_N_GROUPS
PAIRS_PER_GROUP = HEADS_PER_GROUP // 2


def _ssd_consts():
    r = lax.broadcasted_iota(jnp.int32, (LANES, LANES), 0)
    c = lax.broadcasted_iota(jnp.int32, (LANES, LANES), 1)
    return (c <= r).astype(BF16), (c >= r).astype(BF16)


def _ssd_decay(dtr, dtb, alog, tril):
    shape = (SSD_CHUNK, LANES)
    lanes = lax.broadcasted_iota(jnp.int32, shape, 1)
    rows = lax.broadcasted_iota(jnp.int32, shape, 0)
    pre = dtr + dtb
    valid = lanes < HEADS_PER_GROUP
    dt = jnp.where(valid, jnp.maximum(pre, 0.0) + jnp.log(1.0 + jnp.exp(-jnp.abs(pre))), 0.0)
    a = -jnp.exp(alog)
    acs = _sel_left(tril, _split3(dt * a))
    return dt, a, acs, pre, valid, rows, lanes


def _lane_col(v, j):
    return jnp.broadcast_to(v[:, j:j + 1], v.shape)


def _ssd_pair_terms(k, dt, cols, low_half, xs):
    dtp = jnp.where(low_half, _lane_col(dt, 2 * k), _lane_col(dt, 2 * k + 1))
    acsp = jnp.where(low_half, cols[2 * k], cols[2 * k + 1])
    lastp = acsp[SSD_CHUNK - 1:SSD_CHUNK, :]
    eap = jnp.exp(acsp)
    decp = jnp.exp(lastp - acsp)
    etp = jnp.exp(lastp)
    xdt = xs * dtp
    return dtp, eap, decp, etp, xdt


def _ssd_conv_taps(cwb, xr, prev8):
    sh = [_shift_down(xr, j, prev8) for j in range(SSD_CONV_W)]
    xc = cwb[4:5]
    for j in range(SSD_CONV_W):
        xc = xc + cwb[3 - j:4 - j] * sh[j]
    return xc, sh


def _ssd_specs(nc, rev):
    ch = (lambda i: nc - 1 - i) if rev else (lambda i: i)
    L = SSD_CHUNK
    xs = pl.BlockSpec((4, L, LANES), lambda g, i: (g, ch(i), 0))
    bb = pl.BlockSpec((1, L, LANES), lambda g, i: (N_XS_BLK + g, ch(i), 0))
    cc = pl.BlockSpec((1, L, LANES), lambda g, i: (N_XS_BLK + SSD_N_GROUPS + g, ch(i), 0))
    dt = pl.BlockSpec((1, L, LANES), lambda g, i: (g, ch(i), 0))
    cw_xs = pl.BlockSpec((4, SUBLANES, LANES), lambda g, i: (g, 0, 0))
    cw_b = pl.BlockSpec((1, SUBLANES, LANES), lambda g, i: (N_XS_BLK + g, 0, 0))
    cw_c = pl.BlockSpec((1, SUBLANES, LANES), lambda g, i: (N_XS_BLK + SSD_N_GROUPS + g, 0, 0))
    st = pl.BlockSpec((1, 4, LANES, LANES), lambda g, i: (ch(i), g, 0, 0))
    grp4 = pl.BlockSpec((4, L, LANES), lambda g, i: (g, ch(i), 0))
    return xs, bb, cc, dt, cw_xs, cw_b, cw_c, st, grp4


def _ssd_scan_fwd(xr, dtr, cwb, dtb, alog, dskip, consts):
    t = xr.shape[1]
    L = SSD_CHUNK
    nc = t // L
    tril, _ = consts
    xs_s, b_s, c_s, dt_s, cwx_s, cwb_s, cwc_s, st_s, grp4 = _ssd_specs(nc, False)
    grp_row = pl.BlockSpec((1, 1, LANES), lambda g, i: (g, 0, 0))

    def body(xs_ref, b_ref, c_ref, dtr_ref, cwx_ref, cwbb_ref, cwc_ref, dtb_ref, alog_ref, dsk_ref,
             tril_ref, y_ref, sp_ref, state, tail):
        @pl.when(pl.program_id(1) == 0)
        def _():
            state[...] = jnp.zeros_like(state)
            tail[...] = jnp.zeros_like(tail)

        xa = []
        for b in range(6):
            xrb = xs_ref[b] if b < 4 else (b_ref[0] if b == 4 else c_ref[0])
            cw = cwx_ref[b] if b < 4 else (cwbb_ref[0] if b == 4 else cwc_ref[0])
            xc, _ = _ssd_conv_taps(cw, xrb, tail[b])
            tail[b] = xrb[L - SUBLANES:]
            xa.append(xc * _sigmoid(xc))

        dt, a, acs, _, _, rows, lanes = _ssd_decay(dtr_ref[0], dtb_ref[0], alog_ref[0], tril_ref[...])
        acst = acs.T
        bb = xa[4].astype(BF16)
        cb_ = xa[5].astype(BF16)
        cbm = _dot_nt(cb_, bb)
        causal = rows >= lanes
        low_half = lanes < LANES // 2
        cols = [_lane_col(acs, j) for j in range(HEADS_PER_GROUP)]

        for k in range(PAIRS_PER_GROUP):
            xs = xa[k]
            dtp, eap, decp, etp, xdt = _ssd_pair_terms(k, dt, cols, low_half, xs)
            ms = []
            for j in (2 * k, 2 * k + 1):
                diff = cols[j] - jnp.broadcast_to(acst[j:j + 1, :], (L, L))
                ms.append((cbm * jnp.exp(jnp.where(causal, diff, NEG_BIG))).astype(BF16))
            xcat = jnp.concatenate([jnp.where(low_half, xdt, 0.0).astype(BF16),
                                    jnp.where(low_half, 0.0, xdt).astype(BF16)], axis=0)
            yd = _dot(jnp.concatenate(ms, axis=1), xcat)
            sp = state[k]
            yo = eap * _dot(cb_, sp.astype(BF16))
            y_ref[k] = yd + yo + dsk_ref[k][0:1] * xs
            sp_ref[0, k] = sp
            state[k] = etp * sp + _dot_tn(bb, (decp * xdt).astype(BF16))

    return pl.pallas_call(
        body, name="ssd_scan_fwd", grid=(SSD_N_GROUPS, nc),
        in_specs=[xs_s, b_s, c_s, dt_s, cwx_s, cwb_s, cwc_s, grp_row, grp_row, cwx_s, _resident(tril.shape)],
        out_specs=[grp4, st_s],
        out_shape=[jax.ShapeDtypeStruct((N_XS_BLK, t, LANES), F32),
                   jax.ShapeDtypeStruct((nc, N_XS_BLK, LANES, LANES), F32)],
        scratch_shapes=[pltpu.VMEM((4, LANES, LANES), F32), pltpu.VMEM((6, SUBLANES, LANES), F32)],
        compiler_params=_params("arbitrary", "arbitrary"),
    )(xr, xr, xr, dtr, cwb, cwb, cwb, dtb, alog, dskip, tril)


def _ssd_scan_bwd(xr, dtr, dy, sprev, cwb, dtb, alog, dskip, consts):
    t = xr.shape[1]
    L = SSD_CHUNK
    nc = t // L
    hb = L // SUBLANES
    tril, triu = consts
    xs_s, b_s, c_s, dt_s, cwx_s, cwb_s, cwc_s, st_s, grp4 = _ssd_specs(nc, True)
    grp_row = pl.BlockSpec((1, 1, LANES), lambda g, i: (g, 0, 0))
    prev = lambda off: pl.BlockSpec(
        (4 if off is None else 1, SUBLANES, LANES),
        (lambda g, i: (g, jnp.maximum((nc - 1 - i) * hb - 1, 0), 0)) if off is None else
        (lambda g, i: (off + g, jnp.maximum((nc - 1 - i) * hb - 1, 0), 0)))
    grp1 = pl.BlockSpec((1, L, LANES), lambda g, i: (g, nc - 1 - i, 0))
    acc4 = pl.BlockSpec((4, SUBLANES, LANES), lambda g, i: (g, 0, 0))
    acc1 = pl.BlockSpec((1, SUBLANES, LANES), lambda g, i: (g, 0, 0))

    def body(xs_ref, b_ref, c_ref, pxs_ref, pb_ref, pc_ref, dtr_ref, dy_ref, sp_ref,
             cwx_ref, cwbb_ref, cwc_ref, dtb_ref, alog_ref, dsk_ref, tril_ref, triu_ref,
             dxs_ref, db_ref, dc_ref, ddtr_ref, dcwx_ref, dcwb_ref, dcwc_ref, dd_ref, dsm_ref,
             dstate, head):
        step = pl.program_id(1)
        first_chunk = step == nc - 1

        @pl.when(step == 0)
        def _():
            dstate[...] = jnp.zeros_like(dstate)
            head[...] = jnp.zeros_like(head)
            for r in (dcwx_ref, dcwb_ref, dcwc_ref, dd_ref, dsm_ref):
                r[...] = jnp.zeros_like(r)

        def blk(b):
            xrb = xs_ref[b] if b < 4 else (b_ref[0] if b == 4 else c_ref[0])
            cw = cwx_ref[b] if b < 4 else (cwbb_ref[0] if b == 4 else cwc_ref[0])
            p8 = pxs_ref[b] if b < 4 else (pb_ref[0] if b == 4 else pc_ref[0])
            return xrb, cw, jnp.where(first_chunk, 0.0, p8)

        xa, dsil = [], []
        for b in range(6):
            xrb, cw, p8 = blk(b)
            xc, _ = _ssd_conv_taps(cw, xrb, p8)
            sig = _sigmoid(xc)
            xa.append(xc * sig)
            dsil.append(sig * (1.0 + xc * (1.0 - sig)))

        dt, a, acs, pre, valid, rows, lanes = _ssd_decay(dtr_ref[0], dtb_ref[0], alog_ref[0], tril_ref[...])
        acst = acs.T
        bb = xa[4].astype(BF16)
        cb_ = xa[5].astype(BF16)
        cbm = _dot_nt(cb_, bb)
        cbmt = _dot_nt(bb, cb_)
        causal = rows >= lanes
        anti = rows <= lanes
        low_half = lanes < LANES // 2
        last_row = rows == L - 1
        cols = [_lane_col(acs, j) for j in range(HEADS_PER_GROUP)]
        zeros = jnp.zeros((L, LANES), F32)
        dcb, dcbt, dbg, dcg, dacs, dacst, ddt = zeros, zeros, zeros, zeros, zeros, zeros, zeros
        dxa = []

        for k in range(PAIRS_PER_GROUP):
            xs = xa[k]
            dtp, eap, decp, etp, xdt = _ssd_pair_terms(k, dt, cols, low_half, xs)
            xdtb = xdt.astype(BF16)
            w = decp * xdt
            wb = w.astype(BF16)
            dyv = dy_ref[k]
            sp = sp_ref[0, k]
            spb = sp.astype(BF16)
            dsn = dstate[k]
            dsnb = dsn.astype(BF16)
            yoff = eap * _dot(cb_, spb)
            dgb = (eap * dyv).astype(BF16)
            dcg = dcg + _dot_nt(dgb, spb)
            dstate[k] = _dot_tn(cb_, dgb) + etp * dsn
            last_lane = etp * jnp.sum(dsn * sp, axis=0, keepdims=True)
            dbg = dbg + _dot_nt(wb, dsnb)
            dw = _dot(bb, dsnb)
            t2 = dw * w
            dxdt = decp * dw
            last_lane = last_lane + jnp.sum(t2, axis=0, keepdims=True)
            lane_acc = dyv * yoff - t2 + jnp.where(last_row, last_lane, 0.0)
            for j in (2 * k, 2 * k + 1):
                diff = cols[j] - jnp.broadcast_to(acst[j:j + 1, :], (L, L))
                lm = jnp.exp(jnp.where(causal, diff, NEG_BIG))
                lmt = jnp.exp(jnp.where(anti, -diff, NEG_BIG))
                dye = jnp.where(low_half == (j % 2 == 0), dyv, 0.0).astype(BF16)
                dm = _dot_nt(dye, xdtb)
                dmt = _dot_nt(xdtb, dye)
                mt = cbmt * lmt
                seg = dmt * mt - dm * (cbm * lm)
                dacst = dacst + jnp.where(rows == j, jnp.sum(seg, axis=0, keepdims=True), 0.0)
                dcb = dcb + dm * lm
                dcbt = dcbt + dmt * lmt
                dxdt = dxdt + _dot(mt.astype(BF16), dye)
            ddt_lane = dxdt * xs
            for j, keep in ((2 * k, low_half), (2 * k + 1, jnp.logical_not(low_half))):
                dacs = dacs + jnp.where(lanes == j, jnp.sum(jnp.where(keep, lane_acc, 0.0), axis=1, keepdims=True), 0.0)
                ddt = ddt + jnp.where(lanes == j, jnp.sum(jnp.where(keep, ddt_lane, 0.0), axis=1, keepdims=True), 0.0)
            dxa.append(dsk_ref[k][0:1] * dyv + dxdt * dtp)
            dd_ref[k, 0:1, :] += jnp.sum(dyv * xs, axis=0, keepdims=True)

        dxa.append(dbg + _dot(dcbt.astype(BF16), cb_))
        dxa.append(dcg + _dot(dcb.astype(BF16), bb))
        dac = _sel_left(triu_ref[...], _split3(dacs + dacst.T))
        ddtr = jnp.where(valid, (ddt + dac * a) * _sigmoid(pre), 0.0)
        ddtr_ref[0] = ddtr.astype(BF16)
        dsm_ref[0, 0:1, :] += jnp.sum(ddtr, axis=0, keepdims=True)
        dsm_ref[0, 1:2, :] += jnp.sum(dac * dt, axis=0, keepdims=True) * a

        for b in range(6):
            xrb, cw, p8 = blk(b)
            sh = [_shift_down(xrb, j, p8) for j in range(SSD_CONV_W)]
            dxc = dxa[b] * dsil[b]
            acc = dcwx_ref.at[b] if b < 4 else (dcwb_ref.at[0] if b == 4 else dcwc_ref.at[0])
            acc[4:5, :] += jnp.sum(dxc, axis=0, keepdims=True)
            dxr = jnp.zeros_like(dxc)
            for j in range(SSD_CONV_W):
                acc[3 - j:4 - j, :] += jnp.sum(dxc * sh[j], axis=0, keepdims=True)
                dxr = dxr + cw[3 - j:4 - j] * _shift_up(dxc, j, head[b])
            head[b] = dxc[0:SUBLANES]
            out = dxs_ref.at[b] if b < 4 else (db_ref.at[0] if b == 4 else dc_ref.at[0])
            out[...] = dxr.astype(BF16)

    return pl.pallas_call(
        body, name="ssd_scan_bwd", grid=(SSD_N_GROUPS, nc),
        in_specs=[xs_s, b_s, c_s, prev(None), prev(N_XS_BLK), prev(N_XS_BLK + SSD_N_GROUPS), dt_s, grp4, st_s,
                  cwx_s, cwb_s, cwc_s, grp_row, grp_row, cwx_s, _resident(tril.shape), _resident(triu.shape)],
        out_specs=[grp4, grp1, grp1, grp1, acc4, acc1, acc1, acc4, acc1],
        out_shape=[jax.ShapeDtypeStruct((N_XS_BLK, t, LANES), BF16),
                   jax.ShapeDtypeStruct((SSD_N_GROUPS, t, LANES), BF16),
                   jax.ShapeDtypeStruct((SSD_N_GROUPS, t, LANES), BF16),
                   jax.ShapeDtypeStruct((SSD_N_GROUPS, t, LANES), BF16),
                   jax.ShapeDtypeStruct((N_XS_BLK, SUBLANES, LANES), F32),
                   jax.ShapeDtypeStruct((SSD_N_GROUPS, SUBLANES, LANES), F32),
                   jax.ShapeDtypeStruct((SSD_N_GROUPS, SUBLANES, LANES), F32),
                   jax.ShapeDtypeStruct((N_XS_BLK, SUBLANES, LANES), F32),
                   jax.ShapeDtypeStruct((SSD_N_GROUPS, SUBLANES, LANES), F32)],
        scratch_shapes=[pltpu.VMEM((4, LANES, LANES), F32), pltpu.VMEM((6, SUBLANES, LANES), F32)],
        compiler_params=_params("arbitrary", "arbitrary"),
    )(xr, xr, xr, xr, xr, xr, dtr, dy, sprev, cwb, cwb, cwb, dtb, alog, dskip, tril, triu)


def _ssd_gate_fwd(x, y, z, gnw, wo):
    t, d = x.shape
    tm = min(TOKEN_TILE, t)
    nb = N_XS_BLK
    per = nb // SSD_N_GROUPS

    def body(x_ref, y_ref, z_ref, gnw_ref, wo_ref, o_ref, gn_ref):
        gs = []
        for j in range(nb):
            zv = z_ref[j]
            gs.append(y_ref[j] * (zv * _sigmoid(zv)))
        for q in range(SSD_N_GROUPS):
            ss = sum(jnp.sum(gs[j] * gs[j], axis=1, keepdims=True) for j in range(q * per, (q + 1) * per))
            inv = lax.rsqrt(ss / (per * LANES) + RMS_EPS)
            for j in range(q * per, (q + 1) * per):
                gn_ref[:, j * LANES:(j + 1) * LANES] = ((gs[j] * inv) * gnw_ref[j]).astype(BF16)
        o_ref[...] = x_ref[...] + _dot(gn_ref[...], wo_ref[...])

    blk = pl.BlockSpec((nb, tm, LANES), lambda i: (0, i, 0))
    return pl.pallas_call(
        body, name="ssd_gate_fwd", grid=(t // tm,),
        in_specs=[pl.BlockSpec((tm, d), lambda i: (i, 0)), blk, blk, _resident((nb, 1, LANES)),
                  _resident((SSD_D_INNER, d))],
        out_specs=[pl.BlockSpec((tm, d), lambda i: (i, 0)), pl.BlockSpec((tm, SSD_D_INNER), lambda i: (i, 0))],
        out_shape=[jax.ShapeDtypeStruct((t, d), F32), jax.ShapeDtypeStruct((t, SSD_D_INNER), BF16)],
        compiler_params=_params("parallel"),
    )(x, y, z, gnw, wo)


def _ssd_gate_bwd(dy, y, z, gnw, wo):
    t, d = dy.shape
    tm = min(TOKEN_TILE, t)
    nb = N_XS_BLK
    per = nb // SSD_N_GROUPS

    def body(dy_ref, y_ref, z_ref, gnw_ref, wo_ref, dys_ref, dz_ref, dgnw_ref):
        @pl.when(pl.program_id(0) == 0)
        def _():
            dgnw_ref[...] = jnp.zeros_like(dgnw_ref)

        dgn = _dot_nt(dy_ref[...].astype(BF16), wo_ref[...])
        for q in range(SSD_N_GROUPS):
            js = range(q * per, (q + 1) * per)
            gs, sil, dsil = {}, {}, {}
            for j in js:
                zv = z_ref[j]
                sig = _sigmoid(zv)
                sil[j] = zv * sig
                dsil[j] = sig * (1.0 + zv * (1.0 - sig))
                gs[j] = y_ref[j] * sil[j]
            ss = sum(jnp.sum(gs[j] * gs[j], axis=1, keepdims=True) for j in js)
            inv = lax.rsqrt(ss / (per * LANES) + RMS_EPS)
            ghat = {j: gs[j] * inv for j in js}
            dgh = {}
            for j in js:
                dj = dgn[:, j * LANES:(j + 1) * LANES]
                dgnw_ref[j] += jnp.sum(dj * ghat[j], axis=0, keepdims=True)
                dgh[j] = dj * gnw_ref[j]
            mean = sum(jnp.sum(dgh[j] * ghat[j], axis=1, keepdims=True) for j in js) / (per * LANES)
            for j in js:
                dg = inv * (dgh[j] - ghat[j] * mean)
                dys_ref[j] = dg * sil[j]
                dz_ref[j] = (dg * y_ref[j] * dsil[j]).astype(BF16)

    blk = pl.BlockSpec((nb, tm, LANES), lambda i: (0, i, 0))
    return pl.pallas_call(
        body, name="ssd_gate_bwd", grid=(t // tm,),
        in_specs=[pl.BlockSpec((tm, d), lambda i: (i, 0)), blk, blk, _resident((nb, 1, LANES)),
                  _resident((SSD_D_INNER, d))],
        out_specs=[blk, blk, pl.BlockSpec((nb, 1, LANES), lambda i: (0, 0, 0))],
        out_shape=[jax.ShapeDtypeStruct((nb, t, LANES), F32), jax.ShapeDtypeStruct((nb, t, LANES), BF16),
                   jax.ShapeDtypeStruct((nb, 1, LANES), F32)],
        compiler_params=_params("arbitrary"),
    )(dy, y, z, gnw, wo)


def _lane_blocks(v):
    r, n = v.shape[0], v.shape[1] // LANES
    return v.reshape(r, n, LANES).transpose(1, 0, 2)


def _ssd_prep(w_in, conv_w, conv_b, dt_bias, a_log, d_skip, norm_w):
    n_main = SSD_D_INNER + SSD_CONV_DIM
    w_dt = w_in[:, n_main:].reshape(-1, SSD_N_GROUPS, HEADS_PER_GROUP)
    w_dt = jnp.pad(w_dt, ((0, 0), (0, 0), (0, LANES - HEADS_PER_GROUP))).reshape(-1, SSD_N_GROUPS * LANES)
    w_in_pad = jnp.concatenate([w_in[:, :n_main], w_dt], axis=1)
    taps = jnp.concatenate([conv_w, conv_b[None], jnp.zeros((SUBLANES - SSD_CONV_W - 1, SSD_CONV_DIM), F32)], axis=0)
    cwb = _lane_blocks(taps)
    row = lambda v: jnp.pad(v.reshape(SSD_N_GROUPS, 1, HEADS_PER_GROUP), ((0, 0), (0, 0), (0, LANES - HEADS_PER_GROUP)))
    dskip = jnp.broadcast_to(jnp.repeat(d_skip, SSD_D_INNER // SSD_N_HEADS).reshape(N_XS_BLK, 1, LANES),
                             (N_XS_BLK, SUBLANES, LANES))
    gnw = norm_w.reshape(N_XS_BLK, 1, LANES)
    return w_in_pad, cwb, row(dt_bias), row(a_log), dskip, gnw


def _ssd_layer_fwd(x, nw, prm, wo, consts):
    w_in_pad, cwb, dtb, alog, dskip, gnw = prm
    z, xr, dtr = _ssd_inproj(x, nw, w_in_pad)
    y, sprev = _ssd_scan_fwd(xr, dtr, cwb, dtb, alog, dskip, consts)
    out, gn = _ssd_gate_fwd(x, y, z, gnw, wo)
    return out, (z, xr, dtr, y, sprev, gn)


def _ssd_layer_bwd(x, dy, nw, prm, wo, consts, saved):
    w_in_pad, cwb, dtb, alog, dskip, gnw = prm
    z, xr, dtr, y, sprev, gn = saved
    dys, dz, dgnw = _ssd_gate_bwd(dy, y, z, gnw, wo)
    dwo = _matmul_tn(gn, dy, name="wgrad_ssd_out")
    dxs, db, dc, ddtr, dcwx, dcwb, dcwc, dd, dsm = _ssd_scan_bwd(xr, dtr, dys, sprev, cwb, dtb, alog, dskip, consts)
    pieces = [dz, dxs, db, dc, ddtr]
    dx, dnw, h = _inproj_bwd(x, dy, nw, [w_in_pad], [pieces])
    dws = [_matmul_tn_blocked(h, p, name=f"wgrad_ssd_in{i}") for i, p in enumerate(pieces)]
    dw_dt = dws[4].reshape(-1, SSD_N_GROUPS, LANES)[:, :, :HEADS_PER_GROUP].reshape(-1, SSD_N_HEADS)
    dw_in = jnp.concatenate(dws[:4] + [dw_dt], axis=1)
    dtaps = jnp.concatenate([dcwx, dcwb, dcwc], axis=0).transpose(1, 0, 2).reshape(SUBLANES, SSD_CONV_DIM)
    by_head = lambda r: dsm[:, r, :HEADS_PER_GROUP].reshape(SSD_N_HEADS)
    d_d = jnp.sum(dd[:, 0, :].reshape(SSD_N_HEADS, SSD_D_INNER // SSD_N_HEADS), axis=1)
    return dx, (dnw, dw_in, dtaps[:SSD_CONV_W], dtaps[SSD_CONV_W], by_head(0), by_head(1),
                d_d, dgnw.reshape(SSD_D_INNER), dwo)


def _loss_head(x, fw, target):
    t, d = x.shape
    tm = min(TOKEN_TILE, t)

    def body(x_ref, fw_ref, tgt_ref, loss_ref, dx_ref, dfw_ref):
        fwv = fw_ref[...]
        y, xhat, inv = _rms_fwd(x_ref[...], fwv)
        err = y - tgt_ref[...]
        tot = jnp.sum(jnp.sum(err * err, axis=1, keepdims=True), axis=0, keepdims=True)
        dx, dw = _rms_bwd(err * (1.0 / d), xhat, inv, fwv)
        dx_ref[...] = dx

        @pl.when(pl.program_id(0) == 0)
        def _():
            loss_ref[...] = jnp.zeros_like(loss_ref)
            dfw_ref[...] = jnp.zeros_like(dfw_ref)

        loss_ref[...] += jnp.broadcast_to(tot * (0.5 / d), loss_ref.shape)
        dfw_ref[...] += dw

    tok = pl.BlockSpec((tm, d), lambda i: (i, 0))
    return pl.pallas_call(
        body, name="loss_head", grid=(t // tm,),
        in_specs=[tok, _resident((1, d)), tok],
        out_specs=[pl.BlockSpec((1, LANES), lambda i: (0, 0)), tok, pl.BlockSpec((1, d), lambda i: (0, 0))],
        out_shape=[jax.ShapeDtypeStruct((1, LANES), F32), jax.ShapeDtypeStruct((t, d), F32),
                   jax.ShapeDtypeStruct((1, d), F32)],
        compiler_params=_params("arbitrary"),
    )(x, fw, target)


def _row_tile(rows, cap):
    best = SUBLANES
    for r in range(SUBLANES, min(rows, cap) + 1, SUBLANES):
        if rows % r == 0:
            best = r
    return best


def _adamw(w, g, m, v, name):
    rows, cols = w.shape
    br = _row_tile(rows, 256)
    c1 = 1.0 - ADAM_B1 ** ADAM_STEP
    c2 = 1.0 - ADAM_B2 ** ADAM_STEP

    def body(w_ref, g_ref, m_ref, v_ref, d_ref, nm_ref, nv_ref):
        gv = g_ref[...]
        nm = ADAM_B1 * m_ref[...] + (1.0 - ADAM_B1) * gv
        nv = ADAM_B2 * v_ref[...] + (1.0 - ADAM_B2) * (gv * gv)
        nm_ref[...] = nm
        nv_ref[...] = nv
        d_ref[...] = -ADAM_LR * ((nm / c1) / (jnp.sqrt(nv / c2) + ADAM_EPS) + ADAM_WD * w_ref[...])

    blk = pl.BlockSpec((br, cols), lambda i: (i, 0))
    shp = jax.ShapeDtypeStruct((rows, cols), F32)
    return pl.pallas_call(
        body, name=name, grid=(rows // br,), in_specs=[blk] * 4, out_specs=[blk] * 3, out_shape=[shp] * 3,
        compiler_params=_params("parallel"),
    )(w, g, m, v)


def _place():
    x, y, c = lax.axis_index("x"), lax.axis_index("y"), lax.axis_index("c")
    return x, y, c, [(1 - x, y), (x, 1 - y), (1 - x, 1 - y)]


def _remote(src, dst, send_sems, recv_sems, k, to):
    return pltpu.make_async_remote_copy(src_ref=src, dst_ref=dst, send_sem=send_sems.at[k], recv_sem=recv_sems.at[k],
                                        device_id=to, device_id_type=MESH)


def _all_gather_shards(arrs):
    n = len(arrs)

    def body(*refs):
        srcs, dsts = refs[:n], refs[n:2 * n]
        send_sems, recv_sems = refs[2 * n:]
        x, y, c, chips = _place()
        me = 2 * x + y
        sibling = (x, y, 1 - c)
        sent = []
        for oi, (src, dst) in enumerate(zip(srcs, dsts)):
            for j, chip in enumerate(chips):
                sent.append(_remote(src.at[c], dst.at[me, c], send_sems, recv_sems, 6 * oi + j, (*chip, c)))
                sent[-1].start()
        for oi, dst in enumerate(dsts):
            for j, chip in enumerate(chips):
                landed = dst.at[2 * chip[0] + chip[1], c]
                _remote(landed, landed, send_sems, recv_sems, 6 * oi + j, (*chip, c)).wait_recv()
                sent.append(_remote(landed, landed, send_sems, recv_sems, 6 * oi + 3 + j, sibling))
                sent[-1].start()
        for oi, dst in enumerate(dsts):
            for j, chip in enumerate(chips):
                landed = dst.at[2 * chip[0] + chip[1], 1 - c]
                _remote(landed, landed, send_sems, recv_sems, 6 * oi + 3 + j, sibling).wait_recv()
        for cp in sent:
            cp.wait_send()

    return pl.pallas_call(
        body, name="all_gather_shards",
        in_specs=[_HBM] * n, out_specs=[_HBM] * n,
        out_shape=[jax.ShapeDtypeStruct((N_SHARDS,) + a.shape, a.dtype) for a in arrs],
        scratch_shapes=[pltpu.SemaphoreType.DMA((6 * n,)), pltpu.SemaphoreType.DMA((6 * n,))],
    )(*arrs)


def _swap_halves(arrs):
    n = len(arrs)

    def body(*refs):
        srcs, dsts = refs[:n], refs[n:2 * n]
        send_sems, recv_sems = refs[2 * n:]
        x, y, c, _ = _place()
        cps = [_remote(src.at[:, 1 - c], dst, send_sems, recv_sems, oi, (x, y, 1 - c))
               for oi, (src, dst) in enumerate(zip(srcs, dsts))]
        for cp in cps:
            cp.start()
        for cp in cps:
            cp.wait()

    return pl.pallas_call(
        body, name="swap_halves", in_specs=[_HBM] * n, out_specs=[_HBM] * n,
        out_shape=[jax.ShapeDtypeStruct((a.shape[0],) + a.shape[2:], a.dtype) for a in arrs],
        scratch_shapes=[pltpu.SemaphoreType.DMA((n,)), pltpu.SemaphoreType.DMA((n,))],
    )(*arrs)


def _scatter_to_chips(arrs):
    n = len(arrs)

    def body(*refs):
        srcs, dsts = refs[:n], refs[n:2 * n]
        send_sems, recv_sems = refs[2 * n:]
        x, y, c, chips = _place()
        sent = []
        for oi, (src, dst) in enumerate(zip(srcs, dsts)):
            for j, chip in enumerate(chips):
                sent.append(_remote(src.at[2 * chip[0] + chip[1]], dst.at[j], send_sems, recv_sems, 3 * oi + j,
                                    (*chip, c)))
                sent[-1].start()
        for oi, dst in enumerate(dsts):
            for j, chip in enumerate(chips):
                _remote(dst.at[j], dst.at[j], send_sems, recv_sems, 3 * oi + j, (*chip, c)).wait_recv()
        for cp in sent:
            cp.wait_send()

    return pl.pallas_call(
        body, name="scatter_to_chips", in_specs=[_HBM] * n, out_specs=[_HBM] * n,
        out_shape=[jax.ShapeDtypeStruct((3,) + a.shape[1:], a.dtype) for a in arrs],
        scratch_shapes=[pltpu.SemaphoreType.DMA((3 * n,)), pltpu.SemaphoreType.DMA((3 * n,))],
    )(*arrs)


def _join_halves(arrs):
    n = len(arrs)

    def body(*refs):
        bufs = refs[n:2 * n]
        send_sems, recv_sems = refs[2 * n:]
        x, y, c, _ = _place()
        sibling = (x, y, 1 - c)
        sent = [_remote(buf.at[c], buf.at[c], send_sems, recv_sems, oi, sibling) for oi, buf in enumerate(bufs)]
        for cp in sent:
            cp.start()
        for oi, buf in enumerate(bufs):
            _remote(buf.at[1 - c], buf.at[1 - c], send_sems, recv_sems, oi, sibling).wait_recv()
        for cp in sent:
            cp.wait_send()

    return pl.pallas_call(
        body, name="join_halves", in_specs=[_HBM] * n, out_specs=[_HBM] * n,
        out_shape=[jax.ShapeDtypeStruct(a.shape, a.dtype) for a in arrs],
        input_output_aliases={i: i for i in range(n)},
        scratch_shapes=[pltpu.SemaphoreType.DMA((n,)), pltpu.SemaphoreType.DMA((n,))],
    )(*arrs)


def _add_halves(full, recv, place):
    n, _, rows, cols = full.shape
    br = _row_tile(rows, 512)

    def body(p_ref, a_ref, b_ref, o_ref):
        o_ref[...] = (a_ref[...] + b_ref[...]).astype(BF16)

    grid_spec = pltpu.PrefetchScalarGridSpec(
        num_scalar_prefetch=1, grid=(n, rows // br),
        in_specs=[pl.BlockSpec((None, None, br, cols), lambda s, i, p_ref: (s, p_ref[1], i, 0)),
                  pl.BlockSpec((None, br, cols), lambda s, i, p_ref: (s, i, 0))],
        out_specs=pl.BlockSpec((None, br, cols), lambda s, i, p_ref: (s, i, 0)))
    return pl.pallas_call(
        body, name="add_halves", grid_spec=grid_spec, out_shape=jax.ShapeDtypeStruct((n, rows, cols), BF16),
        compiler_params=_params("parallel", "parallel"),
    )(place, full, recv)


def _sum_chips(mine, others, place):
    _, rows, cols = mine.shape
    br = _row_tile(rows, 512)
    slot_of_flip = {2: 0, 1: 1, 3: 2}

    def body(p_ref, m_ref, o_ref, out_ref):
        me = p_ref[0]
        own = m_ref[...].astype(F32)
        got = [o_ref[j].astype(F32) for j in range(3)]
        acc = None
        for s in range(N_SHARDS):
            flip = jnp.bitwise_xor(me, s)
            term = own
            for f, j in slot_of_flip.items():
                term = jnp.where(flip == f, got[j], term)
            acc = term if acc is None else acc + term
        out_ref[...] = acc

    grid_spec = pltpu.PrefetchScalarGridSpec(
        num_scalar_prefetch=1, grid=(rows // br,),
        in_specs=[pl.BlockSpec((None, br, cols), lambda i, p_ref: (p_ref[0], i, 0)),
                  pl.BlockSpec((3, br, cols), lambda i, p_ref: (0, i, 0))],
        out_specs=pl.BlockSpec((None, br, cols), lambda i, p_ref: (p_ref[1], i, 0)))
    return pl.pallas_call(
        body, name="sum_chips", grid_spec=grid_spec, out_shape=jax.ShapeDtypeStruct((2, rows, cols), F32),
        compiler_params=_params("parallel"),
    )(place, mine, others)


WEIGHTS = ("norm_w", "ffn_w_gate", "ffn_w_up", "ffn_w_down", "ssd_w_in", "ssd_conv_w", "ssd_conv_b", "ssd_dt_bias",
           "ssd_a_log", "ssd_d", "ssd_norm_w", "ssd_w_out", "sc_w_in", "sc_conv_w", "sc_w_out", "final_norm_w")
BIG = (("ffn_w_gate", 3), ("ffn_w_up", 3), ("ffn_w_down", 2), ("ssd_w_in", 2), ("ssd_w_out", 1), ("sc_w_in", 2),
       ("sc_w_out", 1))
SMALL_SHARDED = (("norm_w", 2), ("ssd_conv_w", 2), ("sc_conv_w", 2))
REPLICATED = ("ssd_conv_b", "ssd_dt_bias", "ssd_a_log", "ssd_d", "ssd_norm_w", "final_norm_w")
FLAT_COLS = 1024


def _pack(arrays, row_multiple, lead=()):
    flat = jnp.concatenate([a.reshape(lead + (-1,)) for a in arrays], axis=len(lead))
    unit = row_multiple * FLAT_COLS
    n = flat.shape[-1]
    pad = (-n) % unit
    if pad:
        flat = jnp.pad(flat, [(0, 0)] * len(lead) + [(0, pad)])
    return flat.reshape(lead + (-1, FLAT_COLS))


def _unpack(flat, shapes, lead=()):
    flat = flat.reshape(lead + (-1,))
    out, off = [], 0
    for shp in shapes:
        n = 1
        for s in shp:
            n *= s
        out.append(flat[..., off:off + n].reshape(lead + tuple(shp)))
        off += n
    return out


def _to_shards(full, axis):
    shp = full.shape
    r = full.reshape(shp[:axis] + (N_SHARDS, shp[axis] // N_SHARDS) + shp[axis + 1:])
    return jnp.moveaxis(r, axis, 0)


def _from_shards(sh, axis):
    r = jnp.moveaxis(sh, 0, axis)
    shp = r.shape
    return r.reshape(shp[:axis] + (shp[axis] * shp[axis + 1],) + shp[axis + 2:])


def _forward_backward(x, target, p):
    consts = _ssd_consts()
    nw = p["norm_w"]
    row = lambda v: v[None]
    ssd_prm = [_ssd_prep(p["ssd_w_in"][j], p["ssd_conv_w"][j], p["ssd_conv_b"][j], p["ssd_dt_bias"][j],
                         p["ssd_a_log"][j], p["ssd_d"][j], p["ssd_norm_w"][j]) for j in range(2)]
    sc_cw = [jnp.pad(p["sc_conv_w"][j], ((0, SUBLANES - SC_CONV_W), (0, 0))) for j in range(2)]
    ffn = lambda i, k: (p["ffn_w_gate"][i, k], p["ffn_w_up"][i, k], p["ffn_w_down"][i, k])

    xin, saved, pre = [], [], {}
    for i in range(N_LAYERS):
        j = i // 2
        xin.append(x)
        x, *pre[i, 0] = _ffn_fwd(x, row(nw[i, 0]), *ffn(i, 0))
        xin.append(x)
        if i % 2 == 0:
            x, sv = _ssd_layer_fwd(x, row(nw[i, 1]), ssd_prm[j], p["ssd_w_out"][j], consts)
        else:
            sv = _norm_mm(x, row(nw[i, 1]), p["sc_w_in"][j])
            x = _sc_fwd(x, sv, sc_cw[j], p["sc_w_out"][j])
        saved.append(sv)
        xin.append(x)
        x, *pre[i, 1] = _ffn_fwd(x, row(nw[i, 2]), *ffn(i, 1))
    loss, dx, dfw = _loss_head(x, row(p["final_norm_w"]), target)

    g_nw = [[None] * 3 for _ in range(N_LAYERS)]
    g_ffn = {}
    g_ssd = [None, None]
    g_sc = [None, None]

    def ffn_bwd(i, k, slot, dy):
        wg, wu, wd = ffn(i, k)
        a, dg, du = _ffn_bwd_act(dy, *pre[i, k], wd)
        dxn, dnw, h = _inproj_bwd(xin[3 * i + slot], dy, row(nw[i, slot]), [wg, wu], [[dg], [du]])
        g_nw[i][slot] = dnw[0]
        for n, lhs, rhs, scale in (("ffn_w_gate", h, dg, 1.0), ("ffn_w_up", h, du, 1.0), ("ffn_w_down", a, dy, 0.5)):
            g_ffn[n] = _matmul_tn(lhs, rhs, scale=scale, name="wgrad_" + n, slab=(i, k), stack=(N_LAYERS, 2),
                                  buf=g_ffn.get(n))
        return dxn

    for i in reversed(range(N_LAYERS)):
        j = i // 2
        dx = ffn_bwd(i, 1, 2, dx)
        xm = xin[3 * i + 1]
        if i % 2 == 0:
            dx, gs = _ssd_layer_bwd(xm, dx, row(nw[i, 1]), ssd_prm[j], p["ssd_w_out"][j], consts, saved[i])
            g_nw[i][1] = gs[0][0]
            g_ssd[j] = gs[1:]
        else:
            bcu = saved[i]
            dbcu, pin, dcw = _sc_bwd(dx, bcu, sc_cw[j], p["sc_w_out"][j])
            dwo = _matmul_tn(pin, dx, name="wgrad_sc_out")
            dx, dnw, h = _inproj_bwd(xm, dx, row(nw[i, 1]), [p["sc_w_in"][j]], [[dbcu]])
            g_nw[i][1] = dnw[0]
            g_sc[j] = (_matmul_tn(h, dbcu, name="wgrad_sc_in"), dcw[:SC_CONV_W], dwo)
        dx = ffn_bwd(i, 0, 0, dx)

    g = {"norm_w": jnp.stack([jnp.stack(r) for r in g_nw]), "final_norm_w": dfw[0], **g_ffn}
    for k, n in enumerate(("ssd_w_in", "ssd_conv_w", "ssd_conv_b", "ssd_dt_bias", "ssd_a_log", "ssd_d", "ssd_norm_w",
                           "ssd_w_out")):
        g[n] = jnp.stack([g_ssd[0][k], g_ssd[1][k]])
    for k, n in enumerate(("sc_w_in", "sc_conv_w", "sc_w_out")):
        g[n] = jnp.stack([g_sc[0][k], g_sc[1][k]])
    return loss, dx, g


def kernel(x, norm_w, ffn_w_gate, ffn_w_up, ffn_w_down, ssd_w_in, ssd_conv_w, ssd_conv_b, ssd_dt_bias, ssd_a_log, ssd_d, ssd_norm_w, ssd_w_out, sc_w_in, sc_conv_w, sc_w_out, final_norm_w, loss_target, m_norm_w, m_ffn_w_gate, m_ffn_w_up, m_ffn_w_down, m_ssd_w_in, m_ssd_conv_w, m_ssd_conv_b, m_ssd_dt_bias, m_ssd_a_log, m_ssd_d, m_ssd_norm_w, m_ssd_w_out, m_sc_w_in, m_sc_conv_w, m_sc_w_out, m_final_norm_w, v_norm_w, v_ffn_w_gate, v_ffn_w_up, v_ffn_w_down, v_ssd_w_in, v_ssd_conv_w, v_ssd_conv_b, v_ssd_dt_bias, v_ssd_a_log, v_ssd_d, v_ssd_norm_w, v_ssd_w_out, v_sc_w_in, v_sc_conv_w, v_sc_w_out, v_final_norm_w):
    w = dict(zip(WEIGHTS, (norm_w, ffn_w_gate, ffn_w_up, ffn_w_down, ssd_w_in, ssd_conv_w, ssd_conv_b, ssd_dt_bias,
                           ssd_a_log, ssd_d, ssd_norm_w, ssd_w_out, sc_w_in, sc_conv_w, sc_w_out, final_norm_w)))
    m = dict(zip(WEIGHTS, (m_norm_w, m_ffn_w_gate, m_ffn_w_up, m_ffn_w_down, m_ssd_w_in, m_ssd_conv_w, m_ssd_conv_b,
                           m_ssd_dt_bias, m_ssd_a_log, m_ssd_d, m_ssd_norm_w, m_ssd_w_out, m_sc_w_in, m_sc_conv_w,
                           m_sc_w_out, m_final_norm_w)))
    v = dict(zip(WEIGHTS, (v_norm_w, v_ffn_w_gate, v_ffn_w_up, v_ffn_w_down, v_ssd_w_in, v_ssd_conv_w, v_ssd_conv_b,
                           v_ssd_dt_bias, v_ssd_a_log, v_ssd_d, v_ssd_norm_w, v_ssd_w_out, v_sc_w_in, v_sc_conv_w,
                           v_sc_w_out, v_final_norm_w)))
    chip = 2 * lax.axis_index("x") + lax.axis_index("y")
    place = jnp.stack([chip, lax.axis_index("c")]).astype(jnp.int32)
    big_names = [n for n, _ in BIG]
    small_names = [n for n, _ in SMALL_SHARDED] + list(REPLICATED)
    halved = lambda a, lead=(): a.reshape(lead + (2, -1, a.shape[-1]))

    shards = [halved(w[n].astype(BF16)) for n in big_names] + [halved(_pack([w[n] for n, _ in SMALL_SHARDED], 2 * SUBLANES))]
    gathered = [lax.dynamic_update_index_in_dim(g_, s_, chip, 0) for g_, s_ in zip(_all_gather_shards(shards), shards)]
    p = {n: w[n] for n in REPLICATED}
    for (n, ax), sh in zip(BIG, gathered):
        p[n] = _from_shards(sh.reshape((N_SHARDS,) + w[n].shape), ax)
    for (n, ax), sh in zip(SMALL_SHARDED, _unpack(gathered[-1], [w[n].shape for n, _ in SMALL_SHARDED], lead=(N_SHARDS,))):
        p[n] = _from_shards(sh, ax)

    t, d = x.shape[-2:]
    loss, dx, g = _forward_backward(x.reshape(t, d), loss_target.reshape(t, d), p)

    small_part = _pack([_to_shards(g[n], ax) for n, ax in SMALL_SHARDED]
                       + [jnp.broadcast_to(g[n][None], (N_SHARDS,) + g[n].shape) for n in REPLICATED],
                       4 * SUBLANES, lead=(N_SHARDS,))
    parts = [halved(_to_shards(g[n], ax), lead=(N_SHARDS,)) for n, ax in BIG] + [halved(small_part, lead=(N_SHARDS,))]
    chip_sums = [_add_halves(a, r, place) for a, r in zip(parts, _swap_halves(parts))]
    reduced = _join_halves([_sum_chips(mine, others, place)
                            for mine, others in zip(chip_sums, _scatter_to_chips(chip_sums))])

    grad = {n: r.reshape(w[n].shape) for n, r in zip(big_names, reduced)}
    g_small = reduced[-1].reshape(-1, FLAT_COLS)
    grad.update(zip(small_names, _unpack(g_small, [w[n].shape for n in small_names])))

    delta, new_m, new_v = {}, {}, {}
    for n in big_names:
        shp = w[n].shape
        as2d = lambda a: a.reshape(-1, shp[-1])
        out = _adamw(as2d(w[n]), as2d(grad[n]), as2d(m[n]), as2d(v[n]), name="adamw_" + n)
        delta[n], new_m[n], new_v[n] = (o.reshape(shp) for o in out)
    packed = [_pack([s[n] for n in small_names], 4 * SUBLANES) for s in (w, m, v)]
    out = _adamw(packed[0], g_small, packed[1], packed[2], name="adamw_small")
    shapes = [w[n].shape for n in small_names]
    for dst, o in zip((delta, new_m, new_v), out):
        dst.update(zip(small_names, _unpack(o, shapes)))

    loss = lax.psum(loss[0, 0], ("x", "y", "c"))
    return (loss, dx.reshape(x.shape), *[grad[n] for n in WEIGHTS], *[delta[n] for n in WEIGHTS],
            *[new_m[n] for n in WEIGHTS], *[new_v[n] for n in WEIGHTS])
```

```python
import functools

import jax
import jax.numpy as jnp
from jax import lax
from jax.experimental import pallas as pl
from jax.experimental.pallas import tpu as pltpu

F32 = jnp.float32
BF16 = jnp.bfloat16
MESH = pl.DeviceIdType.MESH

RMS_EPS = 1e-5
D_MODEL = 1024
D_FF = 2816
N_LAYERS = 4
SSD_D_INNER = 2048
SSD_N_HEADS = 32
SSD_N_GROUPS = 4
SSD_D_STATE = 128
SSD_CHUNK = 128
SSD_CONV_W = 4
SSD_CONV_DIM = 3072
SSD_IN_DIM = 5152
SC_CONV_W = 3
LANES = 128
SUBLANES = 8
N_XS_BLK = SSD_D_INNER // LANES
SSD_IN_PAD = SSD_D_INNER + SSD_CONV_DIM + SSD_N_GROUPS * LANES
VMEM_LIMIT = 56 * 2**20
TOKEN_TILE = 512
WGRAD_TOKENS = 2048
FF_CHUNK = 256
N_SHARDS = 4

ADAM_LR = 0.001
ADAM_B1 = 0.9
ADAM_B2 = 0.999
ADAM_EPS = 1e-08
ADAM_WD = 0.01
ADAM_STEP = 10


_HBM = pl.BlockSpec(memory_space=pl.ANY)


def _params(*sem):
    return pltpu.CompilerParams(dimension_semantics=sem if sem else None, vmem_limit_bytes=VMEM_LIMIT)


def _dot(a, b):
    return jnp.dot(a, b, preferred_element_type=F32)


def _dot_nt(a, b):
    return lax.dot_general(a, b, (((1,), (1,)), ((), ())), preferred_element_type=F32)


def _dot_tn(a, b):
    return lax.dot_general(a, b, (((0,), (0,)), ((), ())), preferred_element_type=F32)


def _resident(shape):
    n = len(shape)
    return pl.BlockSpec(shape, lambda *_: (0,) * n, pipeline_mode=pl.Buffered(1))


def _split3(v):
    hi = v.astype(BF16)
    r1 = v - hi.astype(F32)
    mid = r1.astype(BF16)
    lo = (r1 - mid.astype(F32)).astype(BF16)
    return hi, mid, lo


def _sel_left(sel, v3):
    return _dot(sel, v3[0]) + _dot(sel, v3[1]) + _dot(sel, v3[2])


def _sigmoid(v):
    return 1.0 / (1.0 + jnp.exp(-v))


def _rms_fwd(x, w):
    inv = lax.rsqrt(jnp.mean(x * x, axis=-1, keepdims=True) + RMS_EPS)
    xhat = x * inv
    return xhat * w, xhat, inv


def _rms_bwd(dh, xhat, inv, w):
    dxhat = dh * w
    dx = inv * (dxhat - xhat * jnp.mean(dxhat * xhat, axis=-1, keepdims=True))
    return dx, jnp.sum(dh * xhat, axis=0, keepdims=True)


def _ffn_fwd(x, nw, wg, wu, wd, carry=()):
    t, d = x.shape
    f = wg.shape[1]
    tm = min(TOKEN_TILE, t)
    nsteps = t // tm
    ncar = len(carry)

    def body(x_ref, nw_ref, wg_ref, wu_ref, wd_ref, *rest):
        srcs = rest[:ncar]
        o_ref, a_ref, s_ref, p_ref = rest[ncar:ncar + 4]
        dsts = rest[ncar + 4:2 * ncar + 4]
        if ncar:
            send_sems, recv_sems = rest[2 * ncar + 4:]
            x_, y_, c_, chips = _place()
            me = 2 * x_ + y_

            @pl.when(pl.program_id(0) == 0)
            def _():
                for oi, (src, dst) in enumerate(zip(srcs, dsts)):
                    for j, chip in enumerate(chips):
                        _remote(src, dst.at[me], send_sems, recv_sems, 3 * oi + j, (*chip, c_)).start()

        xv = x_ref[...]
        h = _rms_fwd(xv, nw_ref[...])[0].astype(BF16)
        acc = jnp.zeros((tm, d), F32)
        for j in range(f // FF_CHUNK):
            sl = slice(j * FF_CHUNK, (j + 1) * FF_CHUNK)
            g = _dot(h, wg_ref[:, sl])
            u = _dot(h, wu_ref[:, sl])
            sig = _sigmoid(g)
            s = g * sig
            a = (s * u).astype(BF16)
            a_ref[:, sl] = a
            s_ref[:, sl] = s.astype(BF16)
            p_ref[:, sl] = (u * (sig + s * (1.0 - sig))).astype(BF16)
            acc = acc + _dot(a, wd_ref[sl, :])
        o_ref[...] = xv + 0.5 * acc

        if ncar:
            @pl.when(pl.program_id(0) == nsteps - 1)
            def _():
                for oi, (src, dst) in enumerate(zip(srcs, dsts)):
                    for j, chip in enumerate(chips):
                        landed = dst.at[2 * chip[0] + chip[1]]
                        _remote(landed, landed, send_sems, recv_sems, 3 * oi + j, (*chip, c_)).wait_recv()
                for oi, (src, dst) in enumerate(zip(srcs, dsts)):
                    for j, chip in enumerate(chips):
                        _remote(src, dst.at[me], send_sems, recv_sems, 3 * oi + j, (*chip, c_)).wait_send()

    tok = lambda n: pl.BlockSpec((tm, n), lambda i: (i, 0))
    sems = [pltpu.SemaphoreType.DMA((3 * ncar,)), pltpu.SemaphoreType.DMA((3 * ncar,))] if ncar else []
    return pl.pallas_call(
        body, name="ffn_fwd_carry" if ncar else "ffn_fwd", grid=(nsteps,),
        in_specs=[tok(d), _resident((1, d)), _resident((d, f)), _resident((d, f)), _resident((f, d))] + [_HBM] * ncar,
        out_specs=[tok(d), tok(f), tok(f), tok(f)] + [_HBM] * ncar,
        out_shape=[jax.ShapeDtypeStruct((t, d), F32)] + [jax.ShapeDtypeStruct((t, f), BF16)] * 3
        + [jax.ShapeDtypeStruct((N_SHARDS,) + c.shape, c.dtype) for c in carry],
        scratch_shapes=sems,
        compiler_params=_params("arbitrary" if ncar else "parallel"),
    )(x, nw, wg, wu, wd, *carry)


def _ffn_bwd_act(dy, s, p, wd):
    t, d = dy.shape
    f = wd.shape[0]
    tm = min(TOKEN_TILE, t)

    def body(dy_ref, s_ref, p_ref, wd_ref, dg_ref, du_ref):
        dob = (0.5 * dy_ref[...]).astype(BF16)
        for j in range(f // FF_CHUNK):
            sl = slice(j * FF_CHUNK, (j + 1) * FF_CHUNK)
            da = _dot_nt(dob, wd_ref[sl, :])
            dg_ref[:, sl] = (da * p_ref[:, sl].astype(F32)).astype(BF16)
            du_ref[:, sl] = (da * s_ref[:, sl].astype(F32)).astype(BF16)

    tok = lambda n: pl.BlockSpec((tm, n), lambda i: (i, 0))
    return pl.pallas_call(
        body, name="ffn_bwd_act", grid=(t // tm,),
        in_specs=[tok(d), tok(f), tok(f), _resident((f, d))],
        out_specs=[tok(f), tok(f)],
        out_shape=[jax.ShapeDtypeStruct((t, f), BF16)] * 2,
        compiler_params=_params("parallel"),
    )(dy, s, p, wd)


def _pick_bn(m, n, unit):
    best = unit
    for k in range(1, n // unit + 1):
        bn = k * unit
        if n % bn == 0 and m * bn * 4 <= 8 * 2**20:
            best = bn
    return best


def _matmul_tn(a, b, scale=1.0, name="wgrad", slab=None, stack=None, buf=None):
    t, m = a.shape
    n = b.shape[1]
    bt = min(WGRAD_TOKENS, t)
    bn = _pick_bn(m, n, LANES)
    nt = t // bt
    lead = tuple(slab) if slab is not None else ()

    def body(a_ref, b_ref, *rest):
        o_ref = rest[-1]

        @pl.when(pl.program_id(1) == 0)
        def _():
            o_ref[...] = jnp.zeros_like(o_ref)

        o_ref[...] += _dot_tn(a_ref[...].astype(BF16), b_ref[...].astype(BF16))
        if scale != 1.0:
            @pl.when(pl.program_id(1) == nt - 1)
            def _():
                o_ref[...] *= scale

    in_specs = [pl.BlockSpec((bt, m), lambda j, k: (k, 0)), pl.BlockSpec((bt, bn), lambda j, k: (k, j))]
    args = [a, b]
    if buf is not None:
        in_specs.append(_HBM)
        args.append(buf)
    return pl.pallas_call(
        body, name=name, grid=(n // bn, nt),
        in_specs=in_specs,
        out_specs=pl.BlockSpec((None,) * len(lead) + (m, bn), lambda j, k: lead + (0, j)),
        out_shape=jax.ShapeDtypeStruct(tuple(stack or ()) + (m, n), F32),
        input_output_aliases={2: 0} if buf is not None else {},
        compiler_params=_params("parallel", "arbitrary"),
    )(*args)


def _matmul_tn_blocked(a, b, name="wgrad_blk"):
    t, m = a.shape
    nb = b.shape[0]
    bt = min(1024, t)
    nbt = _pick_bn(m, nb * LANES, LANES) // LANES
    while nb % nbt:
        nbt -= 1

    def body(a_ref, b_ref, o_ref):
        @pl.when(pl.program_id(1) == 0)
        def _():
            o_ref[...] = jnp.zeros_like(o_ref)

        bv = jnp.concatenate([b_ref[i] for i in range(nbt)], axis=1) if nbt > 1 else b_ref[0]
        o_ref[...] += _dot_tn(a_ref[...], bv)

    return pl.pallas_call(
        body, name=name, grid=(nb // nbt, t // bt),
        in_specs=[pl.BlockSpec((bt, m), lambda j, k: (k, 0)), pl.BlockSpec((nbt, bt, LANES), lambda j, k: (j, k, 0))],
        out_specs=pl.BlockSpec((m, nbt * LANES), lambda j, k: (0, j)),
        out_shape=jax.ShapeDtypeStruct((m, nb * LANES), F32),
        compiler_params=_params("parallel", "arbitrary"),
    )(a, b)


def _norm_mm(x, nw, w):
    t, d = x.shape
    n = w.shape[1]
    tm = min(TOKEN_TILE, t)
    cn = 1024 if n % 1024 == 0 else n

    def body(x_ref, nw_ref, w_ref, o_ref):
        h = _rms_fwd(x_ref[...], nw_ref[...])[0].astype(BF16)
        for j in range(n // cn):
            sl = slice(j * cn, (j + 1) * cn)
            o_ref[:, sl] = _dot(h, w_ref[:, sl])

    return pl.pallas_call(
        body, name="norm_mm", grid=(t // tm,),
        in_specs=[pl.BlockSpec((tm, d), lambda i: (i, 0)), _resident((1, d)), _resident((d, n))],
        out_specs=pl.BlockSpec((tm, n), lambda i: (i, 0)),
        out_shape=jax.ShapeDtypeStruct((t, n), F32),
        compiler_params=_params("parallel"),
    )(x, nw, w)


def _ssd_inproj(x, nw, w):
    t, d = x.shape
    tm = min(TOKEN_TILE, t)
    nz, nx, ng = SSD_D_INNER // LANES, SSD_CONV_DIM // LANES, SSD_N_GROUPS
    cn = 1024

    def body(x_ref, nw_ref, w_ref, z_ref, xr_ref, dt_ref):
        h = _rms_fwd(x_ref[...], nw_ref[...])[0].astype(BF16)
        for j in range(-(-SSD_IN_PAD // cn)):
            lo, hi = j * cn, min((j + 1) * cn, SSD_IN_PAD)
            r = _dot(h, w_ref[:, lo:hi])
            for i in range((hi - lo) // LANES):
                blk = j * (cn // LANES) + i
                v = r[:, i * LANES:(i + 1) * LANES]
                if blk < nz:
                    z_ref[blk] = v
                elif blk < nz + nx:
                    xr_ref[blk - nz] = v
                else:
                    dt_ref[blk - nz - nx] = v

    out = lambda n: pl.BlockSpec((n, tm, LANES), lambda i: (0, i, 0))
    return pl.pallas_call(
        body, name="ssd_inproj", grid=(t // tm,),
        in_specs=[pl.BlockSpec((tm, d), lambda i: (i, 0)), _resident((1, d)), _resident((d, SSD_IN_PAD))],
        out_specs=[out(nz), out(nx), out(ng)],
        out_shape=[jax.ShapeDtypeStruct((n, t, LANES), F32) for n in (nz, nx, ng)],
        compiler_params=_params("parallel"),
    )(x, nw, w)


def _inproj_bwd(x, dy, nw, ws, pieces):
    t, d = x.shape
    tm = min(TOKEN_TILE, t)
    nws = len(ws)
    flat = [p for group in pieces for p in group]

    def body(*refs):
        x_ref, dy_ref, nw_ref = refs[:3]
        w_refs = refs[3:3 + nws]
        p_refs = list(refs[3 + nws:3 + nws + len(flat)])
        dx_ref, dnw_ref, h_ref = refs[3 + nws + len(flat):]
        nwv = nw_ref[...]
        hf, xhat, inv = _rms_fwd(x_ref[...], nwv)
        h_ref[...] = hf.astype(BF16)
        dh = None
        for w_ref, group in zip(w_refs, pieces):
            parts = []
            for _ in group:
                p = p_refs.pop(0)
                parts += [p[i] for i in range(p.shape[0])] if len(p.shape) == 3 else [p[...]]
            dz = jnp.concatenate(parts, axis=1) if len(parts) > 1 else parts[0]
            part = _dot_nt(dz, w_ref[...])
            dh = part if dh is None else dh + part
        dx, dw = _rms_bwd(dh, xhat, inv, nwv)
        dx_ref[...] = dy_ref[...] + dx

        @pl.when(pl.program_id(0) == 0)
        def _():
            dnw_ref[...] = jnp.zeros_like(dnw_ref)

        dnw_ref[...] += dw

    tok = lambda m: pl.BlockSpec((tm, m), lambda i: (i, 0))
    p_specs = [pl.BlockSpec((p.shape[0], tm, LANES), lambda i: (0, i, 0)) if p.ndim == 3 else tok(p.shape[1])
               for p in flat]
    return pl.pallas_call(
        body, name="inproj_bwd", grid=(t // tm,),
        in_specs=[tok(d), tok(d), _resident((1, d))] + [_resident(w.shape) for w in ws] + p_specs,
        out_specs=[tok(d), pl.BlockSpec((1, d), lambda i: (0, 0)), tok(d)],
        out_shape=[jax.ShapeDtypeStruct((t, d), F32), jax.ShapeDtypeStruct((1, d), F32),
                   jax.ShapeDtypeStruct((t, d), BF16)],
        compiler_params=_params("arbitrary"),
    )(x, dy, nw, *ws, *flat)


def _shift_down(v, j, prev8):
    if j == 0:
        return v
    r = pltpu.roll(v, j, 0)
    p = pltpu.roll(prev8, j, 0)
    rows = lax.broadcasted_iota(jnp.int32, prev8.shape, 0)
    first = jnp.where(rows < j, p, r[0:SUBLANES])
    return jnp.concatenate([first, r[SUBLANES:]], axis=0)


def _shift_up(v, j, next8):
    if j == 0:
        return v
    n = v.shape[0]
    r = pltpu.roll(v, n - j, 0)
    p = pltpu.roll(next8, SUBLANES - j, 0)
    rows = lax.broadcasted_iota(jnp.int32, next8.shape, 0)
    last = jnp.where(rows >= SUBLANES - j, p, r[n - SUBLANES:])
    return jnp.concatenate([r[:n - SUBLANES], last], axis=0)


def _sc_fwd(x, bcu, cw, wo):
    t, d = x.shape
    tm = min(TOKEN_TILE, t)
    hb = tm // SUBLANES

    def body(x_ref, bcu_ref, prev_ref, cw_ref, wo_ref, o_ref):
        bg, cg, u = bcu_ref[:, 0:d], bcu_ref[:, d:2 * d], bcu_ref[:, 2 * d:3 * d]
        q = cg * u
        qp = jnp.where(pl.program_id(0) == 0, 0.0, prev_ref[:, d:2 * d] * prev_ref[:, 2 * d:3 * d])
        cwv = cw_ref[...]
        v = cwv[2:3] * q + cwv[1:2] * _shift_down(q, 1, qp) + cwv[0:1] * _shift_down(q, 2, qp)
        o_ref[...] = x_ref[...] + _dot((bg * v).astype(BF16), wo_ref[...])

    return pl.pallas_call(
        body, name="sc_fwd", grid=(t // tm,),
        in_specs=[pl.BlockSpec((tm, d), lambda i: (i, 0)), pl.BlockSpec((tm, 3 * d), lambda i: (i, 0)),
                  pl.BlockSpec((SUBLANES, 3 * d), lambda i: (jnp.maximum(i * hb - 1, 0), 0)),
                  _resident((SUBLANES, d)), _resident((d, d))],
        out_specs=pl.BlockSpec((tm, d), lambda i: (i, 0)),
        out_shape=jax.ShapeDtypeStruct((t, d), F32),
        compiler_params=_params("parallel"),
    )(x, bcu, bcu, cw, wo)


def _sc_bwd(dy, bcu, cw, wo):
    t, d = dy.shape
    tm = min(TOKEN_TILE, t)
    hb = tm // SUBLANES
    nt = t // tm

    def body(dy_ref, dyn_ref, bcu_ref, prev_ref, next_ref, cw_ref, wo_ref, dbcu_ref, p_ref, dcw_ref):
        i = pl.program_id(0)
        bg, cg, u = bcu_ref[:, 0:d], bcu_ref[:, d:2 * d], bcu_ref[:, 2 * d:3 * d]
        q = cg * u
        qp = jnp.where(i == 0, 0.0, prev_ref[:, d:2 * d] * prev_ref[:, 2 * d:3 * d])
        cwv = cw_ref[...]
        q1 = _shift_down(q, 1, qp)
        q2 = _shift_down(q, 2, qp)
        v = cwv[2:3] * q + cwv[1:2] * q1 + cwv[0:1] * q2
        p_ref[...] = (bg * v).astype(BF16)
        wov = wo_ref[...]
        dp = _dot_nt(dy_ref[...].astype(BF16), wov)
        dpn = _dot_nt(dyn_ref[...].astype(BF16), wov)
        dv = dp * bg
        dvn = jnp.where(i == nt - 1, 0.0, dpn * next_ref[:, 0:d])
        dq = cwv[2:3] * dv + cwv[1:2] * _shift_up(dv, 1, dvn) + cwv[0:1] * _shift_up(dv, 2, dvn)
        dbcu_ref[:, 0:d] = (dp * v).astype(BF16)
        dbcu_ref[:, d:2 * d] = (dq * u).astype(BF16)
        dbcu_ref[:, 2 * d:3 * d] = (dq * cg).astype(BF16)

        @pl.when(i == 0)
        def _():
            dcw_ref[...] = jnp.zeros_like(dcw_ref)

        dcw_ref[0:1, :] += jnp.sum(dv * q2, axis=0, keepdims=True)
        dcw_ref[1:2, :] += jnp.sum(dv * q1, axis=0, keepdims=True)
        dcw_ref[2:3, :] += jnp.sum(dv * q, axis=0, keepdims=True)

    last8 = t // SUBLANES - 1
    return pl.pallas_call(
        body, name="sc_bwd", grid=(nt,),
        in_specs=[pl.BlockSpec((tm, d), lambda i: (i, 0)),
                  pl.BlockSpec((SUBLANES, d), lambda i: (jnp.minimum((i + 1) * hb, last8), 0)),
                  pl.BlockSpec((tm, 3 * d), lambda i: (i, 0)),
                  pl.BlockSpec((SUBLANES, 3 * d), lambda i: (jnp.maximum(i * hb - 1, 0), 0)),
                  pl.BlockSpec((SUBLANES, 3 * d), lambda i: (jnp.minimum((i + 1) * hb, last8), 0)),
                  _resident((SUBLANES, d)), _resident((d, d))],
        out_specs=[pl.BlockSpec((tm, 3 * d), lambda i: (i, 0)), pl.BlockSpec((tm, d), lambda i: (i, 0)),
                   pl.BlockSpec((SUBLANES, d), lambda i: (0, 0))],
        out_shape=[jax.ShapeDtypeStruct((t, 3 * d), BF16), jax.ShapeDtypeStruct((t, d), BF16),
                   jax.ShapeDtypeStruct((SUBLANES, d), F32)],
        compiler_params=_params("arbitrary"),
    )(dy, dy, bcu, bcu, bcu, cw, wo)


NEG_BIG = -1e30


HEADS_PER_GROUP = SSD_N_HEADS // SSD_N_GROUPS
PAIRS_PER_GROUP = HEADS_PER_GROUP // 2


def _ssd_consts():
    r = lax.broadcasted_iota(jnp.int32, (LANES, LANES), 0)
    c = lax.broadcasted_iota(jnp.int32, (LANES, LANES), 1)
    return (c <= r).astype(BF16), (c >= r).astype(BF16)


def _ssd_decay(dtr, dtb, alog, tril):
    shape = (SSD_CHUNK, LANES)
    lanes = lax.broadcasted_iota(jnp.int32, shape, 1)
    rows = lax.broadcasted_iota(jnp.int32, shape, 0)
    pre = dtr + dtb
    valid = lanes < HEADS_PER_GROUP
    dt = jnp.where(valid, jnp.maximum(pre, 0.0) + jnp.log(1.0 + jnp.exp(-jnp.abs(pre))), 0.0)
    a = -jnp.exp(alog)
    acs = _sel_left(tril, _split3(dt * a))
    return dt, a, acs, pre, valid, rows, lanes


def _lane_col(v, j):
    return jnp.broadcast_to(v[:, j:j + 1], v.shape)


def _ssd_pair_terms(k, dt, cols, low_half, xs):
    dtp = jnp.where(low_half, _lane_col(dt, 2 * k), _lane_col(dt, 2 * k + 1))
    acsp = jnp.where(low_half, cols[2 * k], cols[2 * k + 1])
    lastp = acsp[SSD_CHUNK - 1:SSD_CHUNK, :]
    eap = jnp.exp(acsp)
    decp = jnp.exp(lastp - acsp)
    etp = jnp.exp(lastp)
    xdt = xs * dtp
    return dtp, eap, decp, etp, xdt


def _ssd_conv_taps(cwb, xr, prev8):
    sh = [_shift_down(xr, j, prev8) for j in range(SSD_CONV_W)]
    xc = cwb[4:5]
    for j in range(SSD_CONV_W):
        xc = xc + cwb[3 - j:4 - j] * sh[j]
    return xc, sh


def _ssd_specs(nc, rev):
    ch = (lambda i: nc - 1 - i) if rev else (lambda i: i)
    L = SSD_CHUNK
    xs = pl.BlockSpec((4, L, LANES), lambda g, i: (g, ch(i), 0))
    bb = pl.BlockSpec((1, L, LANES), lambda g, i: (N_XS_BLK + g, ch(i), 0))
    cc = pl.BlockSpec((1, L, LANES), lambda g, i: (N_XS_BLK + SSD_N_GROUPS + g, ch(i), 0))
    dt = pl.BlockSpec((1, L, LANES), lambda g, i: (g, ch(i), 0))
    cw_xs = pl.BlockSpec((4, SUBLANES, LANES), lambda g, i: (g, 0, 0))
    cw_b = pl.BlockSpec((1, SUBLANES, LANES), lambda g, i: (N_XS_BLK + g, 0, 0))
    cw_c = pl.BlockSpec((1, SUBLANES, LANES), lambda g, i: (N_XS_BLK + SSD_N_GROUPS + g, 0, 0))
    st = pl.BlockSpec((1, 4, LANES, LANES), lambda g, i: (ch(i), g, 0, 0))
    grp4 = pl.BlockSpec((4, L, LANES), lambda g, i: (g, ch(i), 0))
    return xs, bb, cc, dt, cw_xs, cw_b, cw_c, st, grp4


def _ssd_scan_fwd(xr, dtr, cwb, dtb, alog, dskip, consts):
    t = xr.shape[1]
    L = SSD_CHUNK
    nc = t // L
    tril, _ = consts
    xs_s, b_s, c_s, dt_s, cwx_s, cwb_s, cwc_s, st_s, grp4 = _ssd_specs(nc, False)
    grp_row = pl.BlockSpec((1, 1, LANES), lambda g, i: (g, 0, 0))

    def body(xs_ref, b_ref, c_ref, dtr_ref, cwx_ref, cwbb_ref, cwc_ref, dtb_ref, alog_ref, dsk_ref,
             tril_ref, y_ref, sp_ref, state, tail):
        @pl.when(pl.program_id(1) == 0)
        def _():
            state[...] = jnp.zeros_like(state)
            tail[...] = jnp.zeros_like(tail)

        xa = []
        for b in range(6):
            xrb = xs_ref[b] if b < 4 else (b_ref[0] if b == 4 else c_ref[0])
            cw = cwx_ref[b] if b < 4 else (cwbb_ref[0] if b == 4 else cwc_ref[0])
            xc, _ = _ssd_conv_taps(cw, xrb, tail[b])
            tail[b] = xrb[L - SUBLANES:]
            xa.append(xc * _sigmoid(xc))

        dt, a, acs, _, _, rows, lanes = _ssd_decay(dtr_ref[0], dtb_ref[0], alog_ref[0], tril_ref[...])
        acst = acs.T
        bb = xa[4].astype(BF16)
        cb_ = xa[5].astype(BF16)
        cbm = _dot_nt(cb_, bb)
        causal = rows >= lanes
        low_half = lanes < LANES // 2
        cols = [_lane_col(acs, j) for j in range(HEADS_PER_GROUP)]

        for k in range(PAIRS_PER_GROUP):
            xs = xa[k]
            dtp, eap, decp, etp, xdt = _ssd_pair_terms(k, dt, cols, low_half, xs)
            ms = []
            for j in (2 * k, 2 * k + 1):
                diff = cols[j] - jnp.broadcast_to(acst[j:j + 1, :], (L, L))
                ms.append((cbm * jnp.exp(jnp.where(causal, diff, NEG_BIG))).astype(BF16))
            xcat = jnp.concatenate([jnp.where(low_half, xdt, 0.0).astype(BF16),
                                    jnp.where(low_half, 0.0, xdt).astype(BF16)], axis=0)
            yd = _dot(jnp.concatenate(ms, axis=1), xcat)
            sp = state[k]
            yo = eap * _dot(cb_, sp.astype(BF16))
            y_ref[k] = yd + yo + dsk_ref[k][0:1] * xs
            sp_ref[0, k] = sp
            state[k] = etp * sp + _dot_tn(bb, (decp * xdt).astype(BF16))

    return pl.pallas_call(
        body, name="ssd_scan_fwd", grid=(SSD_N_GROUPS, nc),
        in_specs=[xs_s, b_s, c_s, dt_s, cwx_s, cwb_s, cwc_s, grp_row, grp_row, cwx_s, _resident(tril.shape)],
        out_specs=[grp4, st_s],
        out_shape=[jax.ShapeDtypeStruct((N_XS_BLK, t, LANES), F32),
                   jax.ShapeDtypeStruct((nc, N_XS_BLK, LANES, LANES), F32)],
        scratch_shapes=[pltpu.VMEM((4, LANES, LANES), F32), pltpu.VMEM((6, SUBLANES, LANES), F32)],
        compiler_params=_params("arbitrary", "arbitrary"),
    )(xr, xr, xr, dtr, cwb, cwb, cwb, dtb, alog, dskip, tril)


def _ssd_scan_bwd(xr, dtr, dy, sprev, cwb, dtb, alog, dskip, consts):
    t = xr.shape[1]
    L = SSD_CHUNK
    nc = t // L
    hb = L // SUBLANES
    tril, triu = consts
    xs_s, b_s, c_s, dt_s, cwx_s, cwb_s, cwc_s, st_s, grp4 = _ssd_specs(nc, True)
    grp_row = pl.BlockSpec((1, 1, LANES), lambda g, i: (g, 0, 0))
    prev = lambda off: pl.BlockSpec(
        (4 if off is None else 1, SUBLANES, LANES),
        (lambda g, i: (g, jnp.maximum((nc - 1 - i) * hb - 1, 0), 0)) if off is None else
        (lambda g, i: (off + g, jnp.maximum((nc - 1 - i) * hb - 1, 0), 0)))
    grp1 = pl.BlockSpec((1, L, LANES), lambda g, i: (g, nc - 1 - i, 0))
    acc4 = pl.BlockSpec((4, SUBLANES, LANES), lambda g, i: (g, 0, 0))
    acc1 = pl.BlockSpec((1, SUBLANES, LANES), lambda g, i: (g, 0, 0))

    def body(xs_ref, b_ref, c_ref, pxs_ref, pb_ref, pc_ref, dtr_ref, dy_ref, sp_ref,
             cwx_ref, cwbb_ref, cwc_ref, dtb_ref, alog_ref, dsk_ref, tril_ref, triu_ref,
             dxs_ref, db_ref, dc_ref, ddtr_ref, dcwx_ref, dcwb_ref, dcwc_ref, dd_ref, dsm_ref,
             dstate, head):
        step = pl.program_id(1)
        first_chunk = step == nc - 1

        @pl.when(step == 0)
        def _():
            dstate[...] = jnp.zeros_like(dstate)
            head[...] = jnp.zeros_like(head)
            for r in (dcwx_ref, dcwb_ref, dcwc_ref, dd_ref, dsm_ref):
                r[...] = jnp.zeros_like(r)

        def blk(b):
            xrb = xs_ref[b] if b < 4 else (b_ref[0] if b == 4 else c_ref[0])
            cw = cwx_ref[b] if b < 4 else (cwbb_ref[0] if b == 4 else cwc_ref[0])
            p8 = pxs_ref[b] if b < 4 else (pb_ref[0] if b == 4 else pc_ref[0])
            return xrb, cw, jnp.where(first_chunk, 0.0, p8)

        xa, dsil = [], []
        for b in range(6):
            xrb, cw, p8 = blk(b)
            xc, _ = _ssd_conv_taps(cw, xrb, p8)
            sig = _sigmoid(xc)
            xa.append(xc * sig)
            dsil.append(sig * (1.0 + xc * (1.0 - sig)))

        dt, a, acs, pre, valid, rows, lanes = _ssd_decay(dtr_ref[0], dtb_ref[0], alog_ref[0], tril_ref[...])
        acst = acs.T
        bb = xa[4].astype(BF16)
        cb_ = xa[5].astype(BF16)
        cbm = _dot_nt(cb_, bb)
        cbmt = _dot_nt(bb, cb_)
        causal = rows >= lanes
        anti = rows <= lanes
        low_half = lanes < LANES // 2
        last_row = rows == L - 1
        cols = [_lane_col(acs, j) for j in range(HEADS_PER_GROUP)]
        zeros = jnp.zeros((L, LANES), F32)
        dcb, dcbt, dbg, dcg, dacs, dacst, ddt = zeros, zeros, zeros, zeros, zeros, zeros, zeros
        dxa = []

        for k in range(PAIRS_PER_GROUP):
            xs = xa[k]
            dtp, eap, decp, etp, xdt = _ssd_pair_terms(k, dt, cols, low_half, xs)
            xdtb = xdt.astype(BF16)
            w = decp * xdt
            wb = w.astype(BF16)
            dyv = dy_ref[k]
            sp = sp_ref[0, k]
            spb = sp.astype(BF16)
            dsn = dstate[k]
            dsnb = dsn.astype(BF16)
            yoff = eap * _dot(cb_, spb)
            dgb = (eap * dyv).astype(BF16)
            dcg = dcg + _dot_nt(dgb, spb)
            dstate[k] = _dot_tn(cb_, dgb) + etp * dsn
            last_lane = etp * jnp.sum(dsn * sp, axis=0, keepdims=True)
            dbg = dbg + _dot_nt(wb, dsnb)
            dw = _dot(bb, dsnb)
            t2 = dw * w
            dxdt = decp * dw
            last_lane = last_lane + jnp.sum(t2, axis=0, keepdims=True)
            lane_acc = dyv * yoff - t2 + jnp.where(last_row, last_lane, 0.0)
            for j in (2 * k, 2 * k + 1):
                diff = cols[j] - jnp.broadcast_to(acst[j:j + 1, :], (L, L))
                lm = jnp.exp(jnp.where(causal, diff, NEG_BIG))
                lmt = jnp.exp(jnp.where(anti, -diff, NEG_BIG))
                dye = jnp.where(low_half == (j % 2 == 0), dyv, 0.0).astype(BF16)
                dm = _dot_nt(dye, xdtb)
                dmt = _dot_nt(xdtb, dye)
                mt = cbmt * lmt
                seg = dmt * mt - dm * (cbm * lm)
                dacst = dacst + jnp.where(rows == j, jnp.sum(seg, axis=0, keepdims=True), 0.0)
                dcb = dcb + dm * lm
                dcbt = dcbt + dmt * lmt
                dxdt = dxdt + _dot(mt.astype(BF16), dye)
            ddt_lane = dxdt * xs
            for j, keep in ((2 * k, low_half), (2 * k + 1, jnp.logical_not(low_half))):
                dacs = dacs + jnp.where(lanes == j, jnp.sum(jnp.where(keep, lane_acc, 0.0), axis=1, keepdims=True), 0.0)
                ddt = ddt + jnp.where(lanes == j, jnp.sum(jnp.where(keep, ddt_lane, 0.0), axis=1, keepdims=True), 0.0)
            dxa.append(dsk_ref[k][0:1] * dyv + dxdt * dtp)
            dd_ref[k, 0:1, :] += jnp.sum(dyv * xs, axis=0, keepdims=True)

        dxa.append(dbg + _dot(dcbt.astype(BF16), cb_))
        dxa.append(dcg + _dot(dcb.astype(BF16), bb))
        dac = _sel_left(triu_ref[...], _split3(dacs + dacst.T))
        ddtr = jnp.where(valid, (ddt + dac * a) * _sigmoid(pre), 0.0)
        ddtr_ref[0] = ddtr.astype(BF16)
        dsm_ref[0, 0:1, :] += jnp.sum(ddtr, axis=0, keepdims=True)
        dsm_ref[0, 1:2, :] += jnp.sum(dac * dt, axis=0, keepdims=True) * a

        for b in range(6):
            xrb, cw, p8 = blk(b)
            sh = [_shift_down(xrb, j, p8) for j in range(SSD_CONV_W)]
            dxc = dxa[b] * dsil[b]
            acc = dcwx_ref.at[b] if b < 4 else (dcwb_ref.at[0] if b == 4 else dcwc_ref.at[0])
            acc[4:5, :] += jnp.sum(dxc, axis=0, keepdims=True)
            dxr = jnp.zeros_like(dxc)
            for j in range(SSD_CONV_W):
                acc[3 - j:4 - j, :] += jnp.sum(dxc * sh[j], axis=0, keepdims=True)
                dxr = dxr + cw[3 - j:4 - j] * _shift_up(dxc, j, head[b])
            head[b] = dxc[0:SUBLANES]
            out = dxs_ref.at[b] if b < 4 else (db_ref.at[0] if b == 4 else dc_ref.at[0])
            out[...] = dxr.astype(BF16)

    return pl.pallas_call(
        body, name="ssd_scan_bwd", grid=(SSD_N_GROUPS, nc),
        in_specs=[xs_s, b_s, c_s, prev(None), prev(N_XS_BLK), prev(N_XS_BLK + SSD_N_GROUPS), dt_s, grp4, st_s,
                  cwx_s, cwb_s, cwc_s, grp_row, grp_row, cwx_s, _resident(tril.shape), _resident(triu.shape)],
        out_specs=[grp4, grp1, grp1, grp1, acc4, acc1, acc1, acc4, acc1],
        out_shape=[jax.ShapeDtypeStruct((N_XS_BLK, t, LANES), BF16),
                   jax.ShapeDtypeStruct((SSD_N_GROUPS, t, LANES), BF16),
                   jax.ShapeDtypeStruct((SSD_N_GROUPS, t, LANES), BF16),
                   jax.ShapeDtypeStruct((SSD_N_GROUPS, t, LANES), BF16),
                   jax.ShapeDtypeStruct((N_XS_BLK, SUBLANES, LANES), F32),
                   jax.ShapeDtypeStruct((SSD_N_GROUPS, SUBLANES, LANES), F32),
                   jax.ShapeDtypeStruct((SSD_N_GROUPS, SUBLANES, LANES), F32),
                   jax.ShapeDtypeStruct((N_XS_BLK, SUBLANES, LANES), F32),
                   jax.ShapeDtypeStruct((SSD_N_GROUPS, SUBLANES, LANES), F32)],
        scratch_shapes=[pltpu.VMEM((4, LANES, LANES), F32), pltpu.VMEM((6, SUBLANES, LANES), F32)],
        compiler_params=_params("arbitrary", "arbitrary"),
    )(xr, xr, xr, xr, xr, xr, dtr, dy, sprev, cwb, cwb, cwb, dtb, alog, dskip, tril, triu)


def _ssd_gate_fwd(x, y, z, gnw, wo):
    t, d = x.shape
    tm = min(TOKEN_TILE, t)
    nb = N_XS_BLK
    per = nb // SSD_N_GROUPS

    def body(x_ref, y_ref, z_ref, gnw_ref, wo_ref, o_ref, gn_ref):
        gs = []
        for j in range(nb):
            zv = z_ref[j]
            gs.append(y_ref[j] * (zv * _sigmoid(zv)))
        for q in range(SSD_N_GROUPS):
            ss = sum(jnp.sum(gs[j] * gs[j], axis=1, keepdims=True) for j in range(q * per, (q + 1) * per))
            inv = lax.rsqrt(ss / (per * LANES) + RMS_EPS)
            for j in range(q * per, (q + 1) * per):
                gn_ref[:, j * LANES:(j + 1) * LANES] = ((gs[j] * inv) * gnw_ref[j]).astype(BF16)
        o_ref[...] = x_ref[...] + _dot(gn_ref[...], wo_ref[...])

    blk = pl.BlockSpec((nb, tm, LANES), lambda i: (0, i, 0))
    return pl.pallas_call(
        body, name="ssd_gate_fwd", grid=(t // tm,),
        in_specs=[pl.BlockSpec((tm, d), lambda i: (i, 0)), blk, blk, _resident((nb, 1, LANES)),
                  _resident((SSD_D_INNER, d))],
        out_specs=[pl.BlockSpec((tm, d), lambda i: (i, 0)), pl.BlockSpec((tm, SSD_D_INNER), lambda i: (i, 0))],
        out_shape=[jax.ShapeDtypeStruct((t, d), F32), jax.ShapeDtypeStruct((t, SSD_D_INNER), BF16)],
        compiler_params=_params("parallel"),
    )(x, y, z, gnw, wo)


def _ssd_gate_bwd(dy, y, z, gnw, wo):
    t, d = dy.shape
    tm = min(TOKEN_TILE, t)
    nb = N_XS_BLK
    per = nb // SSD_N_GROUPS

    def body(dy_ref, y_ref, z_ref, gnw_ref, wo_ref, dys_ref, dz_ref, dgnw_ref):
        @pl.when(pl.program_id(0) == 0)
        def _():
            dgnw_ref[...] = jnp.zeros_like(dgnw_ref)

        dgn = _dot_nt(dy_ref[...].astype(BF16), wo_ref[...])
        for q in range(SSD_N_GROUPS):
            js = range(q * per, (q + 1) * per)
            gs, sil, dsil = {}, {}, {}
            for j in js:
                zv = z_ref[j]
                sig = _sigmoid(zv)
                sil[j] = zv * sig
                dsil[j] = sig * (1.0 + zv * (1.0 - sig))
                gs[j] = y_ref[j] * sil[j]
            ss = sum(jnp.sum(gs[j] * gs[j], axis=1, keepdims=True) for j in js)
            inv = lax.rsqrt(ss / (per * LANES) + RMS_EPS)
            ghat = {j: gs[j] * inv for j in js}
            dgh = {}
            for j in js:
                dj = dgn[:, j * LANES:(j + 1) * LANES]
                dgnw_ref[j] += jnp.sum(dj * ghat[j], axis=0, keepdims=True)
                dgh[j] = dj * gnw_ref[j]
            mean = sum(jnp.sum(dgh[j] * ghat[j], axis=1, keepdims=True) for j in js) / (per * LANES)
            for j in js:
                dg = inv * (dgh[j] - ghat[j] * mean)
                dys_ref[j] = dg * sil[j]
                dz_ref[j] = (dg * y_ref[j] * dsil[j]).astype(BF16)

    blk = pl.BlockSpec((nb, tm, LANES), lambda i: (0, i, 0))
    return pl.pallas_call(
        body, name="ssd_gate_bwd", grid=(t // tm,),
        in_specs=[pl.BlockSpec((tm, d), lambda i: (i, 0)), blk, blk, _resident((nb, 1, LANES)),
                  _resident((SSD_D_INNER, d))],
        out_specs=[blk, blk, pl.BlockSpec((nb, 1, LANES), lambda i: (0, 0, 0))],
        out_shape=[jax.ShapeDtypeStruct((nb, t, LANES), F32), jax.ShapeDtypeStruct((nb, t, LANES), BF16),
                   jax.ShapeDtypeStruct((nb, 1, LANES), F32)],
        compiler_params=_params("arbitrary"),
    )(dy, y, z, gnw, wo)


def _lane_blocks(v):
    r, n = v.shape[0], v.shape[1] // LANES
    return v.reshape(r, n, LANES).transpose(1, 0, 2)


def _ssd_prep(w_in, conv_w, conv_b, dt_bias, a_log, d_skip, norm_w):
    n_main = SSD_D_INNER + SSD_CONV_DIM
    w_dt = w_in[:, n_main:].reshape(-1, SSD_N_GROUPS, HEADS_PER_GROUP)
    w_dt = jnp.pad(w_dt, ((0, 0), (0, 0), (0, LANES - HEADS_PER_GROUP))).reshape(-1, SSD_N_GROUPS * LANES)
    w_in_pad = jnp.concatenate([w_in[:, :n_main], w_dt], axis=1)
    taps = jnp.concatenate([conv_w, conv_b[None], jnp.zeros((SUBLANES - SSD_CONV_W - 1, SSD_CONV_DIM), F32)], axis=0)
    cwb = _lane_blocks(taps)
    row = lambda v: jnp.pad(v.reshape(SSD_N_GROUPS, 1, HEADS_PER_GROUP), ((0, 0), (0, 0), (0, LANES - HEADS_PER_GROUP)))
    dskip = jnp.broadcast_to(jnp.repeat(d_skip, SSD_D_INNER // SSD_N_HEADS).reshape(N_XS_BLK, 1, LANES),
                             (N_XS_BLK, SUBLANES, LANES))
    gnw = norm_w.reshape(N_XS_BLK, 1, LANES)
    return w_in_pad, cwb, row(dt_bias), row(a_log), dskip, gnw


def _ssd_layer_fwd(x, nw, prm, wo, consts):
    w_in_pad, cwb, dtb, alog, dskip, gnw = prm
    z, xr, dtr = _ssd_inproj(x, nw, w_in_pad)
    y, sprev = _ssd_scan_fwd(xr, dtr, cwb, dtb, alog, dskip, consts)
    out, gn = _ssd_gate_fwd(x, y, z, gnw, wo)
    return out, (z, xr, dtr, y, sprev, gn)


def _ssd_layer_bwd(x, dy, nw, prm, wo, consts, saved):
    w_in_pad, cwb, dtb, alog, dskip, gnw = prm
    z, xr, dtr, y, sprev, gn = saved
    dys, dz, dgnw = _ssd_gate_bwd(dy, y, z, gnw, wo)
    dwo = _matmul_tn(gn, dy, name="wgrad_ssd_out")
    dxs, db, dc, ddtr, dcwx, dcwb, dcwc, dd, dsm = _ssd_scan_bwd(xr, dtr, dys, sprev, cwb, dtb, alog, dskip, consts)
    pieces = [dz, dxs, db, dc, ddtr]
    dx, dnw, h = _inproj_bwd(x, dy, nw, [w_in_pad], [pieces])
    dws = [_matmul_tn_blocked(h, p, name=f"wgrad_ssd_in{i}") for i, p in enumerate(pieces)]
    dw_dt = dws[4].reshape(-1, SSD_N_GROUPS, LANES)[:, :, :HEADS_PER_GROUP].reshape(-1, SSD_N_HEADS)
    dw_in = jnp.concatenate(dws[:4] + [dw_dt], axis=1)
    dtaps = jnp.concatenate([dcwx, dcwb, dcwc], axis=0).transpose(1, 0, 2).reshape(SUBLANES, SSD_CONV_DIM)
    by_head = lambda r: dsm[:, r, :HEADS_PER_GROUP].reshape(SSD_N_HEADS)
    d_d = jnp.sum(dd[:, 0, :].reshape(SSD_N_HEADS, SSD_D_INNER // SSD_N_HEADS), axis=1)
    return dx, (dnw, dw_in, dtaps[:SSD_CONV_W], dtaps[SSD_CONV_W], by_head(0), by_head(1),
                d_d, dgnw.reshape(SSD_D_INNER), dwo)


def _loss_head(x, fw, target):
    t, d = x.shape
    tm = min(TOKEN_TILE, t)

    def body(x_ref, fw_ref, tgt_ref, loss_ref, dx_ref, dfw_ref):
        fwv = fw_ref[...]
        y, xhat, inv = _rms_fwd(x_ref[...], fwv)
        err = y - tgt_ref[...]
        tot = jnp.sum(jnp.sum(err * err, axis=1, keepdims=True), axis=0, keepdims=True)
        dx, dw = _rms_bwd(err * (1.0 / d), xhat, inv, fwv)
        dx_ref[...] = dx

        @pl.when(pl.program_id(0) == 0)
        def _():
            loss_ref[...] = jnp.zeros_like(loss_ref)
            dfw_ref[...] = jnp.zeros_like(dfw_ref)

        loss_ref[...] += jnp.broadcast_to(tot * (0.5 / d), loss_ref.shape)
        dfw_ref[...] += dw

    tok = pl.BlockSpec((tm, d), lambda i: (i, 0))
    return pl.pallas_call(
        body, name="loss_head", grid=(t // tm,),
        in_specs=[tok, _resident((1, d)), tok],
        out_specs=[pl.BlockSpec((1, LANES), lambda i: (0, 0)), tok, pl.BlockSpec((1, d), lambda i: (0, 0))],
        out_shape=[jax.ShapeDtypeStruct((1, LANES), F32), jax.ShapeDtypeStruct((t, d), F32),
                   jax.ShapeDtypeStruct((1, d), F32)],
        compiler_params=_params("arbitrary"),
    )(x, fw, target)


def _row_tile(rows, cap):
    best = SUBLANES
    for r in range(SUBLANES, min(rows, cap) + 1, SUBLANES):
        if rows % r == 0:
            best = r
    return best


def _adamw(w, g, m, v, name):
    rows, cols = w.shape
    br = _row_tile(rows, 256)
    c1 = 1.0 - ADAM_B1 ** ADAM_STEP
    c2 = 1.0 - ADAM_B2 ** ADAM_STEP

    def body(w_ref, g_ref, m_ref, v_ref, d_ref, nm_ref, nv_ref):
        gv = g_ref[...]
        nm = ADAM_B1 * m_ref[...] + (1.0 - ADAM_B1) * gv
        nv = ADAM_B2 * v_ref[...] + (1.0 - ADAM_B2) * (gv * gv)
        nm_ref[...] = nm
        nv_ref[...] = nv
        d_ref[...] = -ADAM_LR * ((nm / c1) / (jnp.sqrt(nv / c2) + ADAM_EPS) + ADAM_WD * w_ref[...])

    blk = pl.BlockSpec((br, cols), lambda i: (i, 0))
    shp = jax.ShapeDtypeStruct((rows, cols), F32)
    return pl.pallas_call(
        body, name=name, grid=(rows // br,), in_specs=[blk] * 4, out_specs=[blk] * 3, out_shape=[shp] * 3,
        compiler_params=_params("parallel"),
    )(w, g, m, v)


def _place():
    x, y, c = lax.axis_index("x"), lax.axis_index("y"), lax.axis_index("c")
    return x, y, c, [(1 - x, y), (x, 1 - y), (1 - x, 1 - y)]


def _remote(src, dst, send_sems, recv_sems, k, to):
    return pltpu.make_async_remote_copy(src_ref=src, dst_ref=dst, send_sem=send_sems.at[k], recv_sem=recv_sems.at[k],
                                        device_id=to, device_id_type=MESH)


def _all_gather_shards(arrs):
    n = len(arrs)

    def body(*refs):
        srcs, dsts = refs[:n], refs[n:2 * n]
        send_sems, recv_sems = refs[2 * n:]
        x, y, c, chips = _place()
        me = 2 * x + y
        sibling = (x, y, 1 - c)
        sent = []
        for oi, (src, dst) in enumerate(zip(srcs, dsts)):
            for j, chip in enumerate(chips):
                sent.append(_remote(src.at[c], dst.at[me, c], send_sems, recv_sems, 6 * oi + j, (*chip, c)))
                sent[-1].start()
        for oi, dst in enumerate(dsts):
            for j, chip in enumerate(chips):
                landed = dst.at[2 * chip[0] + chip[1], c]
                _remote(landed, landed, send_sems, recv_sems, 6 * oi + j, (*chip, c)).wait_recv()
                sent.append(_remote(landed, landed, send_sems, recv_sems, 6 * oi + 3 + j, sibling))
                sent[-1].start()
        for oi, dst in enumerate(dsts):
            for j, chip in enumerate(chips):
                landed = dst.at[2 * chip[0] + chip[1], 1 - c]
                _remote(landed, landed, send_sems, recv_sems, 6 * oi + 3 + j, sibling).wait_recv()
        for cp in sent:
            cp.wait_send()

    return pl.pallas_call(
        body, name="all_gather_shards",
        in_specs=[_HBM] * n, out_specs=[_HBM] * n,
        out_shape=[jax.ShapeDtypeStruct((N_SHARDS,) + a.shape, a.dtype) for a in arrs],
        scratch_shapes=[pltpu.SemaphoreType.DMA((6 * n,)), pltpu.SemaphoreType.DMA((6 * n,))],
    )(*arrs)


def _swap_halves(arrs):
    n = len(arrs)

    def body(*refs):
        srcs, dsts = refs[:n], refs[n:2 * n]
        send_sems, recv_sems = refs[2 * n:]
        x, y, c, _ = _place()
        cps = [_remote(src.at[:, 1 - c], dst, send_sems, recv_sems, oi, (x, y, 1 - c))
               for oi, (src, dst) in enumerate(zip(srcs, dsts))]
        for cp in cps:
            cp.start()
        for cp in cps:
            cp.wait()

    return pl.pallas_call(
        body, name="swap_halves", in_specs=[_HBM] * n, out_specs=[_HBM] * n,
        out_shape=[jax.ShapeDtypeStruct((a.shape[0],) + a.shape[2:], a.dtype) for a in arrs],
        scratch_shapes=[pltpu.SemaphoreType.DMA((n,)), pltpu.SemaphoreType.DMA((n,))],
    )(*arrs)


def _scatter_to_chips(arrs):
    n = len(arrs)

    def body(*refs):
        srcs, dsts = refs[:n], refs[n:2 * n]
        send_sems, recv_sems = refs[2 * n:]
        x, y, c, chips = _place()
        sent = []
        for oi, (src, dst) in enumerate(zip(srcs, dsts)):
            for j, chip in enumerate(chips):
                sent.append(_remote(src.at[2 * chip[0] + chip[1]], dst.at[j], send_sems, recv_sems, 3 * oi + j,
                                    (*chip, c)))
                sent[-1].start()
        for oi, dst in enumerate(dsts):
            for j, chip in enumerate(chips):
                _remote(dst.at[j], dst.at[j], send_sems, recv_sems, 3 * oi + j, (*chip, c)).wait_recv()
        for cp in sent:
            cp.wait_send()

    return pl.pallas_call(
        body, name="scatter_to_chips", in_specs=[_HBM] * n, out_specs=[_HBM] * n,
        out_shape=[jax.ShapeDtypeStruct((3,) + a.shape[1:], a.dtype) for a in arrs],
        scratch_shapes=[pltpu.SemaphoreType.DMA((3 * n,)), pltpu.SemaphoreType.DMA((3 * n,))],
    )(*arrs)


def _join_halves(arrs):
    n = len(arrs)

    def body(*refs):
        bufs = refs[n:2 * n]
        send_sems, recv_sems = refs[2 * n:]
        x, y, c, _ = _place()
        sibling = (x, y, 1 - c)
        sent = [_remote(buf.at[c], buf.at[c], send_sems, recv_sems, oi, sibling) for oi, buf in enumerate(bufs)]
        for cp in sent:
            cp.start()
        for oi, buf in enumerate(bufs):
            _remote(buf.at[1 - c], buf.at[1 - c], send_sems, recv_sems, oi, sibling).wait_recv()
        for cp in sent:
            cp.wait_send()

    return pl.pallas_call(
        body, name="join_halves", in_specs=[_HBM] * n, out_specs=[_HBM] * n,
        out_shape=[jax.ShapeDtypeStruct(a.shape, a.dtype) for a in arrs],
        input_output_aliases={i: i for i in range(n)},
        scratch_shapes=[pltpu.SemaphoreType.DMA((n,)), pltpu.SemaphoreType.DMA((n,))],
    )(*arrs)


def _add_halves(full, recv, place):
    n, _, rows, cols = full.shape
    br = _row_tile(rows, 512)

    def body(p_ref, a_ref, b_ref, o_ref):
        o_ref[...] = (a_ref[...] + b_ref[...]).astype(BF16)

    grid_spec = pltpu.PrefetchScalarGridSpec(
        num_scalar_prefetch=1, grid=(n, rows // br),
        in_specs=[pl.BlockSpec((None, None, br, cols), lambda s, i, p_ref: (s, p_ref[1], i, 0)),
                  pl.BlockSpec((None, br, cols), lambda s, i, p_ref: (s, i, 0))],
        out_specs=pl.BlockSpec((None, br, cols), lambda s, i, p_ref: (s, i, 0)))
    return pl.pallas_call(
        body, name="add_halves", grid_spec=grid_spec, out_shape=jax.ShapeDtypeStruct((n, rows, cols), BF16),
        compiler_params=_params("parallel", "parallel"),
    )(place, full, recv)


def _sum_chips(mine, others, place):
    _, rows, cols = mine.shape
    br = _row_tile(rows, 512)
    slot_of_flip = {2: 0, 1: 1, 3: 2}

    def body(p_ref, m_ref, o_ref, out_ref):
        me = p_ref[0]
        own = m_ref[...].astype(F32)
        got = [o_ref[j].astype(F32) for j in range(3)]
        acc = None
        for s in range(N_SHARDS):
            flip = jnp.bitwise_xor(me, s)
            term = own
            for f, j in slot_of_flip.items():
                term = jnp.where(flip == f, got[j], term)
            acc = term if acc is None else acc + term
        out_ref[...] = acc

    grid_spec = pltpu.PrefetchScalarGridSpec(
        num_scalar_prefetch=1, grid=(rows // br,),
        in_specs=[pl.BlockSpec((None, br, cols), lambda i, p_ref: (p_ref[0], i, 0)),
                  pl.BlockSpec((3, br, cols), lambda i, p_ref: (0, i, 0))],
        out_specs=pl.BlockSpec((None, br, cols), lambda i, p_ref: (p_ref[1], i, 0)))
    return pl.pallas_call(
        body, name="sum_chips", grid_spec=grid_spec, out_shape=jax.ShapeDtypeStruct((2, rows, cols), F32),
        compiler_params=_params("parallel"),
    )(place, mine, others)


WEIGHTS = ("norm_w", "ffn_w_gate", "ffn_w_up", "ffn_w_down", "ssd_w_in", "ssd_conv_w", "ssd_conv_b", "ssd_dt_bias",
           "ssd_a_log", "ssd_d", "ssd_norm_w", "ssd_w_out", "sc_w_in", "sc_conv_w", "sc_w_out", "final_norm_w")
BIG = (("ffn_w_gate", 3), ("ffn_w_up", 3), ("ffn_w_down", 2), ("ssd_w_in", 2), ("ssd_w_out", 1), ("sc_w_in", 2),
       ("sc_w_out", 1))
SMALL_SHARDED = (("norm_w", 2), ("ssd_conv_w", 2), ("sc_conv_w", 2))
REPLICATED = ("ssd_conv_b", "ssd_dt_bias", "ssd_a_log", "ssd_d", "ssd_norm_w", "final_norm_w")
FLAT_COLS = 1024


def _pack(arrays, row_multiple, lead=()):
    flat = jnp.concatenate([a.reshape(lead + (-1,)) for a in arrays], axis=len(lead))
    unit = row_multiple * FLAT_COLS
    n = flat.shape[-1]
    pad = (-n) % unit
    if pad:
        flat = jnp.pad(flat, [(0, 0)] * len(lead) + [(0, pad)])
    return flat.reshape(lead + (-1, FLAT_COLS))


def _unpack(flat, shapes, lead=()):
    flat = flat.reshape(lead + (-1,))
    out, off = [], 0
    for shp in shapes:
        n = 1
        for s in shp:
            n *= s
        out.append(flat[..., off:off + n].reshape(lead + tuple(shp)))
        off += n
    return out


def _to_shards(full, axis):
    shp = full.shape
    r = full.reshape(shp[:axis] + (N_SHARDS, shp[axis] // N_SHARDS) + shp[axis + 1:])
    return jnp.moveaxis(r, axis, 0)


def _from_shards(sh, axis):
    r = jnp.moveaxis(sh, 0, axis)
    shp = r.shape
    return r.reshape(shp[:axis] + (shp[axis] * shp[axis + 1],) + shp[axis + 2:])


def _layer_shards(wl, i):
    j = i // 2
    ffn = lambda k: [(("ffn_w_gate", k), wl["ffn_w_gate"][i, k], 1), (("ffn_w_up", k), wl["ffn_w_up"][i, k], 1),
                     (("ffn_w_down", k), wl["ffn_w_down"][i, k], 0)]
    mix = "ssd" if i % 2 == 0 else "sc"
    return ffn(0), [((mix + "_w_in",), wl[mix + "_w_in"][j], 1), ((mix + "_w_out",), wl[mix + "_w_out"][j], 0)] + ffn(1)


def _assemble(group, received, chip):
    return {key: _from_shards(lax.dynamic_update_index_in_dim(r, own, chip, 0), axis)
            for (key, own, axis), r in zip(group, received)}


def _forward_backward(x, target, p, wl, layer0, chip):
    consts = _ssd_consts()
    nw = p["norm_w"]
    row = lambda v: v[None]
    full = {0: layer0}
    ffn = lambda i, k: (full[i]["ffn_w_gate", k], full[i]["ffn_w_up", k], full[i]["ffn_w_down", k])
    ssd_prm, sc_cw = {}, {}

    xin, saved, pre = [], [], {}
    for i in range(N_LAYERS):
        j = i // 2
        first, second = _layer_shards(wl, i + 1) if i + 1 < N_LAYERS else ([], [])
        xin.append(x)
        x, *rest = _ffn_fwd(x, row(nw[i, 0]), *ffn(i, 0), carry=[s[1] for s in first])
        pre[i, 0] = rest[:3]
        if first:
            full[i + 1] = _assemble(first, rest[3:], chip)
        xin.append(x)
        if i % 2 == 0:
            ssd_prm[j] = _ssd_prep(full[i]["ssd_w_in",], p["ssd_conv_w"][j], p["ssd_conv_b"][j], p["ssd_dt_bias"][j],
                                   p["ssd_a_log"][j], p["ssd_d"][j], p["ssd_norm_w"][j])
            x, sv = _ssd_layer_fwd(x, row(nw[i, 1]), ssd_prm[j], full[i]["ssd_w_out",], consts)
        else:
            sc_cw[j] = jnp.pad(p["sc_conv_w"][j], ((0, SUBLANES - SC_CONV_W), (0, 0)))
            sv = _norm_mm(x, row(nw[i, 1]), full[i]["sc_w_in",])
            x = _sc_fwd(x, sv, sc_cw[j], full[i]["sc_w_out",])
        saved.append(sv)
        xin.append(x)
        x, *rest = _ffn_fwd(x, row(nw[i, 2]), *ffn(i, 1), carry=[s[1] for s in second])
        pre[i, 1] = rest[:3]
        if second:
            full[i + 1].update(_assemble(second, rest[3:], chip))
    loss, dx, dfw = _loss_head(x, row(p["final_norm_w"]), target)

    g_nw = [[None] * 3 for _ in range(N_LAYERS)]
    g_ffn = {}
    g_ssd = [None, None]
    g_sc = [None, None]

    def ffn_bwd(i, k, slot, dy):
        wg, wu, wd = ffn(i, k)
        a, s, p_ = pre[i, k]
        dg, du = _ffn_bwd_act(dy, s, p_, wd)
        dxn, dnw, h = _inproj_bwd(xin[3 * i + slot], dy, row(nw[i, slot]), [wg, wu], [[dg], [du]])
        g_nw[i][slot] = dnw[0]
        for n, lhs, rhs, scale in (("ffn_w_gate", h, dg, 1.0), ("ffn_w_up", h, du, 1.0), ("ffn_w_down", a, dy, 0.5)):
            g_ffn[n] = _matmul_tn(lhs, rhs, scale=scale, name="wgrad_" + n, slab=(i, k), stack=(N_LAYERS, 2),
                                  buf=g_ffn.get(n))
        return dxn

    for i in reversed(range(N_LAYERS)):
        j = i // 2
        dx = ffn_bwd(i, 1, 2, dx)
        xm = xin[3 * i + 1]
        if i % 2 == 0:
            dx, gs = _ssd_layer_bwd(xm, dx, row(nw[i, 1]), ssd_prm[j], full[i]["ssd_w_out",], consts, saved[i])
            g_nw[i][1] = gs[0][0]
            g_ssd[j] = gs[1:]
        else:
            bcu = saved[i]
            dbcu, pin, dcw = _sc_bwd(dx, bcu, sc_cw[j], full[i]["sc_w_out",])
            dwo = _matmul_tn(pin, dx, name="wgrad_sc_out")
            dx, dnw, h = _inproj_bwd(xm, dx, row(nw[i, 1]), [full[i]["sc_w_in",]], [[dbcu]])
            g_nw[i][1] = dnw[0]
            g_sc[j] = (_matmul_tn(h, dbcu, name="wgrad_sc_in"), dcw[:SC_CONV_W], dwo)
        dx = ffn_bwd(i, 0, 0, dx)

    g = {"norm_w": jnp.stack([jnp.stack(r) for r in g_nw]), "final_norm_w": dfw[0], **g_ffn}
    for k, n in enumerate(("ssd_w_in", "ssd_conv_w", "ssd_conv_b", "ssd_dt_bias", "ssd_a_log", "ssd_d", "ssd_norm_w",
                           "ssd_w_out")):
        g[n] = jnp.stack([g_ssd[0][k], g_ssd[1][k]])
    for k, n in enumerate(("sc_w_in", "sc_conv_w", "sc_w_out")):
        g[n] = jnp.stack([g_sc[0][k], g_sc[1][k]])
    return loss, dx, g


def kernel(x, norm_w, ffn_w_gate, ffn_w_up, ffn_w_down, ssd_w_in, ssd_conv_w, ssd_conv_b, ssd_dt_bias, ssd_a_log, ssd_d, ssd_norm_w, ssd_w_out, sc_w_in, sc_conv_w, sc_w_out, final_norm_w, loss_target, m_norm_w, m_ffn_w_gate, m_ffn_w_up, m_ffn_w_down, m_ssd_w_in, m_ssd_conv_w, m_ssd_conv_b, m_ssd_dt_bias, m_ssd_a_log, m_ssd_d, m_ssd_norm_w, m_ssd_w_out, m_sc_w_in, m_sc_conv_w, m_sc_w_out, m_final_norm_w, v_norm_w, v_ffn_w_gate, v_ffn_w_up, v_ffn_w_down, v_ssd_w_in, v_ssd_conv_w, v_ssd_conv_b, v_ssd_dt_bias, v_ssd_a_log, v_ssd_d, v_ssd_norm_w, v_ssd_w_out, v_sc_w_in, v_sc_conv_w, v_sc_w_out, v_final_norm_w):
    w = dict(zip(WEIGHTS, (norm_w, ffn_w_gate, ffn_w_up, ffn_w_down, ssd_w_in, ssd_conv_w, ssd_conv_b, ssd_dt_bias,
                           ssd_a_log, ssd_d, ssd_norm_w, ssd_w_out, sc_w_in, sc_conv_w, sc_w_out, final_norm_w)))
    m = dict(zip(WEIGHTS, (m_norm_w, m_ffn_w_gate, m_ffn_w_up, m_ffn_w_down, m_ssd_w_in, m_ssd_conv_w, m_ssd_conv_b,
                           m_ssd_dt_bias, m_ssd_a_log, m_ssd_d, m_ssd_norm_w, m_ssd_w_out, m_sc_w_in, m_sc_conv_w,
                           m_sc_w_out, m_final_norm_w)))
    v = dict(zip(WEIGHTS, (v_norm_w, v_ffn_w_gate, v_ffn_w_up, v_ffn_w_down, v_ssd_w_in, v_ssd_conv_w, v_ssd_conv_b,
                           v_ssd_dt_bias, v_ssd_a_log, v_ssd_d, v_ssd_norm_w, v_ssd_w_out, v_sc_w_in, v_sc_conv_w,
                           v_sc_w_out, v_final_norm_w)))
    chip = 2 * lax.axis_index("x") + lax.axis_index("y")
    place = jnp.stack([chip, lax.axis_index("c")]).astype(jnp.int32)
    big_names = [n for n, _ in BIG]
    small_names = [n for n, _ in SMALL_SHARDED] + list(REPLICATED)
    halved = lambda a, lead=(): a.reshape(lead + (2, -1, a.shape[-1]))

    wl = {n: w[n].astype(BF16) for n in big_names}
    first, second = _layer_shards(wl, 0)
    small = halved(_pack([w[n] for n, _ in SMALL_SHARDED], 2 * SUBLANES))
    received = _all_gather_shards([halved(s[1]) for s in first + second] + [small])
    layer0 = _assemble(first + second, [r.reshape((N_SHARDS,) + s[1].shape) for r, s in zip(received, first + second)],
                       chip)
    p = {n: w[n] for n in REPLICATED}
    small_full = lax.dynamic_update_index_in_dim(received[-1], small, chip, 0)
    for (n, ax), sh in zip(SMALL_SHARDED, _unpack(small_full, [w[n].shape for n, _ in SMALL_SHARDED], lead=(N_SHARDS,))):
        p[n] = _from_shards(sh, ax)

    t, d = x.shape[-2:]
    loss, dx, g = _forward_backward(x.reshape(t, d), loss_target.reshape(t, d), p, wl, layer0, chip)

    small_part = _pack([_to_shards(g[n], ax) for n, ax in SMALL_SHARDED]
                       + [jnp.broadcast_to(g[n][None], (N_SHARDS,) + g[n].shape) for n in REPLICATED],
                       4 * SUBLANES, lead=(N_SHARDS,))
    parts = [halved(_to_shards(g[n], ax), lead=(N_SHARDS,)) for n, ax in BIG] + [halved(small_part, lead=(N_SHARDS,))]
    chip_sums = [_add_halves(a, r, place) for a, r in zip(parts, _swap_halves(parts))]
    reduced = _join_halves([_sum_chips(mine, others, place)
                            for mine, others in zip(chip_sums, _scatter_to_chips(chip_sums))])

    grad = {n: r.reshape(w[n].shape) for n, r in zip(big_names, reduced)}
    g_small = reduced[-1].reshape(-1, FLAT_COLS)
    grad.update(zip(small_names, _unpack(g_small, [w[n].shape for n in small_names])))

    delta, new_m, new_v = {}, {}, {}
    for n in big_names:
        shp = w[n].shape
        as2d = lambda a: a.reshape(-1, shp[-1])
        out = _adamw(as2d(w[n]), as2d(grad[n]), as2d(m[n]), as2d(v[n]), name="adamw_" + n)
        delta[n], new_m[n], new_v[n] = (o.reshape(shp) for o in out)
    packed = [_pack([s[n] for n in small_names], 4 * SUBLANES) for s in (w, m, v)]
    out = _adamw(packed[0], g_small, packed[1], packed[2], name="adamw_small")
    shapes = [w[n].shape for n in small_names]
    for dst, o in zip((delta, new_m, new_v), out):
        dst.update(zip(small_names, _unpack(o, shapes)))

    loss = lax.psum(loss[0, 0], ("x", "y", "c"))
    return (loss, dx.reshape(x.shape), *[grad[n] for n in WEIGHTS], *[delta[n] for n in WEIGHTS],
            *[new_m[n] for n in WEIGHTS], *[new_v[n] for n in WEIGHTS])
```

```python
import functools

import jax
import jax.numpy as jnp
from jax import lax
from jax.experimental import pallas as pl
from jax.experimental.pallas import tpu as pltpu

F32 = jnp.float32
BF16 = jnp.bfloat16
MESH = pl.DeviceIdType.MESH

RMS_EPS = 1e-5
D_MODEL = 1024
D_FF = 2816
N_LAYERS = 4
SSD_D_INNER = 2048
SSD_N_HEADS = 32
SSD_N_GROUPS = 4
SSD_D_STATE = 128
SSD_CHUNK = 128
SSD_CONV_W = 4
SSD_CONV_DIM = 3072
SSD_IN_DIM = 5152
SC_CONV_W = 3
LANES = 128
SUBLANES = 8
N_XS_BLK = SSD_D_INNER // LANES
SSD_IN_PAD = SSD_D_INNER + SSD_CONV_DIM + SSD_N_GROUPS * LANES
VMEM_LIMIT = 56 * 2**20
TOKEN_TILE = 512
WGRAD_TOKENS = 2048
FF_CHUNK = 256
N_SHARDS = 4

ADAM_LR = 0.001
ADAM_B1 = 0.9
ADAM_B2 = 0.999
ADAM_EPS = 1e-08
ADAM_WD = 0.01
ADAM_STEP = 10


_HBM = pl.BlockSpec(memory_space=pl.ANY)


def _params(*sem):
    return pltpu.CompilerParams(dimension_semantics=sem if sem else None, vmem_limit_bytes=VMEM_LIMIT)


def _dot(a, b):
    return jnp.dot(a, b, preferred_element_type=F32)


def _dot_nt(a, b):
    return lax.dot_general(a, b, (((1,), (1,)), ((), ())), preferred_element_type=F32)


def _dot_tn(a, b):
    return lax.dot_general(a, b, (((0,), (0,)), ((), ())), preferred_element_type=F32)


def _resident(shape):
    n = len(shape)
    return pl.BlockSpec(shape, lambda *_: (0,) * n, pipeline_mode=pl.Buffered(1))


def _split3(v):
    hi = v.astype(BF16)
    r1 = v - hi.astype(F32)
    mid = r1.astype(BF16)
    lo = (r1 - mid.astype(F32)).astype(BF16)
    return hi, mid, lo


def _sel_left(sel, v3):
    return _dot(sel, v3[0]) + _dot(sel, v3[1]) + _dot(sel, v3[2])


def _sigmoid(v):
    return 1.0 / (1.0 + jnp.exp(-v))


def _rms_fwd(x, w):
    inv = lax.rsqrt(jnp.mean(x * x, axis=-1, keepdims=True) + RMS_EPS)
    xhat = x * inv
    return xhat * w, xhat, inv


def _rms_bwd(dh, xhat, inv, w):
    dxhat = dh * w
    dx = inv * (dxhat - xhat * jnp.mean(dxhat * xhat, axis=-1, keepdims=True))
    return dx, jnp.sum(dh * xhat, axis=0, keepdims=True)


def _ffn_fwd(x, nw, wg, wu, wd, carry=()):
    t, d = x.shape
    f = wg.shape[1]
    tm = min(TOKEN_TILE, t)
    nsteps = t // tm
    ncar = len(carry)

    def body(x_ref, nw_ref, wg_ref, wu_ref, wd_ref, *rest):
        srcs = rest[:ncar]
        o_ref, a_ref, s_ref, p_ref = rest[ncar:ncar + 4]
        dsts = rest[ncar + 4:2 * ncar + 4]
        if ncar:
            send_sems, recv_sems = rest[2 * ncar + 4:]
            x_, y_, c_, chips = _place()
            me = 2 * x_ + y_

            @pl.when(pl.program_id(0) == 0)
            def _():
                for oi, (src, dst) in enumerate(zip(srcs, dsts)):
                    for j, chip in enumerate(chips):
                        _remote(src, dst.at[me], send_sems, recv_sems, 3 * oi + j, (*chip, c_)).start()

        xv = x_ref[...]
        h = _rms_fwd(xv, nw_ref[...])[0].astype(BF16)
        for j in range(f // FF_CHUNK):
            sl = slice(j * FF_CHUNK, (j + 1) * FF_CHUNK)
            g = _dot(h, wg_ref[:, sl])
            u = _dot(h, wu_ref[:, sl])
            sig = _sigmoid(g)
            s = g * sig
            a_ref[:, sl] = (s * u).astype(BF16)
            s_ref[:, sl] = s.astype(BF16)
            p_ref[:, sl] = (u * (sig + s * (1.0 - sig))).astype(BF16)
        o_ref[...] = xv + 0.5 * _dot(a_ref[...], wd_ref[...])

        if ncar:
            @pl.when(pl.program_id(0) == nsteps - 1)
            def _():
                for oi, (src, dst) in enumerate(zip(srcs, dsts)):
                    for j, chip in enumerate(chips):
                        landed = dst.at[2 * chip[0] + chip[1]]
                        _remote(landed, landed, send_sems, recv_sems, 3 * oi + j, (*chip, c_)).wait_recv()
                for oi, (src, dst) in enumerate(zip(srcs, dsts)):
                    for j, chip in enumerate(chips):
                        _remote(src, dst.at[me], send_sems, recv_sems, 3 * oi + j, (*chip, c_)).wait_send()

    tok = lambda n: pl.BlockSpec((tm, n), lambda i: (i, 0))
    sems = [pltpu.SemaphoreType.DMA((3 * ncar,)), pltpu.SemaphoreType.DMA((3 * ncar,))] if ncar else []
    return pl.pallas_call(
        body, name="ffn_fwd_carry" if ncar else "ffn_fwd", grid=(nsteps,),
        in_specs=[tok(d), _resident((1, d)), _resident((d, f)), _resident((d, f)), _resident((f, d))] + [_HBM] * ncar,
        out_specs=[tok(d), tok(f), tok(f), tok(f)] + [_HBM] * ncar,
        out_shape=[jax.ShapeDtypeStruct((t, d), F32)] + [jax.ShapeDtypeStruct((t, f), BF16)] * 3
        + [jax.ShapeDtypeStruct((N_SHARDS,) + c.shape, c.dtype) for c in carry],
        scratch_shapes=sems,
        compiler_params=_params("arbitrary" if ncar else "parallel"),
    )(x, nw, wg, wu, wd, *carry)


def _ffn_bwd_act(dy, s, p, wd):
    t, d = dy.shape
    f = wd.shape[0]
    tm = min(TOKEN_TILE, t)

    def body(dy_ref, s_ref, p_ref, wd_ref, dg_ref, du_ref):
        dob = (0.5 * dy_ref[...]).astype(BF16)
        for j in range(f // FF_CHUNK):
            sl = slice(j * FF_CHUNK, (j + 1) * FF_CHUNK)
            da = _dot_nt(dob, wd_ref[sl, :])
            dg_ref[:, sl] = (da * p_ref[:, sl].astype(F32)).astype(BF16)
            du_ref[:, sl] = (da * s_ref[:, sl].astype(F32)).astype(BF16)

    tok = lambda n: pl.BlockSpec((tm, n), lambda i: (i, 0))
    return pl.pallas_call(
        body, name="ffn_bwd_act", grid=(t // tm,),
        in_specs=[tok(d), tok(f), tok(f), _resident((f, d))],
        out_specs=[tok(f), tok(f)],
        out_shape=[jax.ShapeDtypeStruct((t, f), BF16)] * 2,
        compiler_params=_params("parallel"),
    )(dy, s, p, wd)


def _pick_bn(m, n, unit):
    best = unit
    for k in range(1, n // unit + 1):
        bn = k * unit
        if n % bn == 0 and m * bn * 4 <= 8 * 2**20:
            best = bn
    return best


def _matmul_tn(a, b, scale=1.0, name="wgrad", slab=None, stack=None, buf=None):
    t, m = a.shape
    n = b.shape[1]
    bt = min(WGRAD_TOKENS, t)
    bn = _pick_bn(m, n, LANES)
    nt = t // bt
    lead = tuple(slab) if slab is not None else ()

    def body(a_ref, b_ref, *rest):
        o_ref = rest[-1]

        @pl.when(pl.program_id(1) == 0)
        def _():
            o_ref[...] = jnp.zeros_like(o_ref)

        o_ref[...] += _dot_tn(a_ref[...].astype(BF16), b_ref[...].astype(BF16))
        if scale != 1.0:
            @pl.when(pl.program_id(1) == nt - 1)
            def _():
                o_ref[...] *= scale

    in_specs = [pl.BlockSpec((bt, m), lambda j, k: (k, 0)), pl.BlockSpec((bt, bn), lambda j, k: (k, j))]
    args = [a, b]
    if buf is not None:
        in_specs.append(_HBM)
        args.append(buf)
    return pl.pallas_call(
        body, name=name, grid=(n // bn, nt),
        in_specs=in_specs,
        out_specs=pl.BlockSpec((None,) * len(lead) + (m, bn), lambda j, k: lead + (0, j)),
        out_shape=jax.ShapeDtypeStruct(tuple(stack or ()) + (m, n), F32),
        input_output_aliases={2: 0} if buf is not None else {},
        compiler_params=_params("parallel", "arbitrary"),
    )(*args)


def _matmul_tn_blocked(a, b, name="wgrad_blk"):
    t, m = a.shape
    nb = b.shape[0]
    bt = min(1024, t)
    nbt = _pick_bn(m, nb * LANES, LANES) // LANES
    while nb % nbt:
        nbt -= 1

    def body(a_ref, b_ref, o_ref):
        @pl.when(pl.program_id(1) == 0)
        def _():
            o_ref[...] = jnp.zeros_like(o_ref)

        bv = jnp.concatenate([b_ref[i] for i in range(nbt)], axis=1) if nbt > 1 else b_ref[0]
        o_ref[...] += _dot_tn(a_ref[...], bv)

    return pl.pallas_call(
        body, name=name, grid=(nb // nbt, t // bt),
        in_specs=[pl.BlockSpec((bt, m), lambda j, k: (k, 0)), pl.BlockSpec((nbt, bt, LANES), lambda j, k: (j, k, 0))],
        out_specs=pl.BlockSpec((m, nbt * LANES), lambda j, k: (0, j)),
        out_shape=jax.ShapeDtypeStruct((m, nb * LANES), F32),
        compiler_params=_params("parallel", "arbitrary"),
    )(a, b)


def _norm_mm(x, nw, w):
    t, d = x.shape
    n = w.shape[1]
    tm = min(TOKEN_TILE, t)
    cn = 1024 if n % 1024 == 0 else n

    def body(x_ref, nw_ref, w_ref, o_ref):
        h = _rms_fwd(x_ref[...], nw_ref[...])[0].astype(BF16)
        for j in range(n // cn):
            sl = slice(j * cn, (j + 1) * cn)
            o_ref[:, sl] = _dot(h, w_ref[:, sl])

    return pl.pallas_call(
        body, name="norm_mm", grid=(t // tm,),
        in_specs=[pl.BlockSpec((tm, d), lambda i: (i, 0)), _resident((1, d)), _resident((d, n))],
        out_specs=pl.BlockSpec((tm, n), lambda i: (i, 0)),
        out_shape=jax.ShapeDtypeStruct((t, n), F32),
        compiler_params=_params("parallel"),
    )(x, nw, w)


def _ssd_inproj(x, nw, w):
    t, d = x.shape
    tm = min(TOKEN_TILE, t)
    nz, nx, ng = SSD_D_INNER // LANES, SSD_CONV_DIM // LANES, SSD_N_GROUPS
    cn = 1024

    def body(x_ref, nw_ref, w_ref, z_ref, xr_ref, dt_ref):
        h = _rms_fwd(x_ref[...], nw_ref[...])[0].astype(BF16)
        for j in range(-(-SSD_IN_PAD // cn)):
            lo, hi = j * cn, min((j + 1) * cn, SSD_IN_PAD)
            r = _dot(h, w_ref[:, lo:hi])
            for i in range((hi - lo) // LANES):
                blk = j * (cn // LANES) + i
                v = r[:, i * LANES:(i + 1) * LANES]
                if blk < nz:
                    z_ref[blk] = v
                elif blk < nz + nx:
                    xr_ref[blk - nz] = v
                else:
                    dt_ref[blk - nz - nx] = v

    out = lambda n: pl.BlockSpec((n, tm, LANES), lambda i: (0, i, 0))
    return pl.pallas_call(
        body, name="ssd_inproj", grid=(t // tm,),
        in_specs=[pl.BlockSpec((tm, d), lambda i: (i, 0)), _resident((1, d)), _resident((d, SSD_IN_PAD))],
        out_specs=[out(nz), out(nx), out(ng)],
        out_shape=[jax.ShapeDtypeStruct((n, t, LANES), F32) for n in (nz, nx, ng)],
        compiler_params=_params("parallel"),
    )(x, nw, w)


def _inproj_bwd(x, dy, nw, ws, pieces, carry=()):
    t, d = x.shape
    tm = min(TOKEN_TILE, t)
    nsteps = t // tm
    nws = len(ws)
    ncar = len(carry)
    flat = [p for group in pieces for p in group]

    def body(*refs):
        x_ref, dy_ref, nw_ref = refs[:3]
        w_refs = refs[3:3 + nws]
        p_refs = list(refs[3 + nws:3 + nws + len(flat)])
        rest = refs[3 + nws + len(flat):]
        srcs, (dx_ref, dnw_ref, h_ref) = rest[:ncar], rest[ncar:ncar + 3]
        dsts, sems = rest[ncar + 3:2 * ncar + 3], rest[2 * ncar + 3:]
        if ncar:
            @pl.when(pl.program_id(0) == 0)
            def _():
                _scatter_start(srcs, dsts, *sems)

        nwv = nw_ref[...]
        hf, xhat, inv = _rms_fwd(x_ref[...], nwv)
        h_ref[...] = hf.astype(BF16)
        dh = None
        for w_ref, group in zip(w_refs, pieces):
            parts = []
            for _ in group:
                p = p_refs.pop(0)
                parts += [p[i] for i in range(p.shape[0])] if len(p.shape) == 3 else [p[...]]
            dz = jnp.concatenate(parts, axis=1) if len(parts) > 1 else parts[0]
            part = _dot_nt(dz, w_ref[...])
            dh = part if dh is None else dh + part
        dx, dw = _rms_bwd(dh, xhat, inv, nwv)
        dx_ref[...] = dy_ref[...] + dx

        @pl.when(pl.program_id(0) == 0)
        def _():
            dnw_ref[...] = jnp.zeros_like(dnw_ref)

        dnw_ref[...] += dw

        if ncar:
            @pl.when(pl.program_id(0) == nsteps - 1)
            def _():
                _scatter_wait(srcs, dsts, *sems)

    tok = lambda m: pl.BlockSpec((tm, m), lambda i: (i, 0))
    p_specs = [pl.BlockSpec((p.shape[0], tm, LANES), lambda i: (0, i, 0)) if p.ndim == 3 else tok(p.shape[1])
               for p in flat]
    sems = [pltpu.SemaphoreType.DMA((3 * ncar,)), pltpu.SemaphoreType.DMA((3 * ncar,))] if ncar else []
    return pl.pallas_call(
        body, name="inproj_bwd_carry" if ncar else "inproj_bwd", grid=(nsteps,),
        in_specs=[tok(d), tok(d), _resident((1, d))] + [_resident(w.shape) for w in ws] + p_specs + [_HBM] * ncar,
        out_specs=[tok(d), pl.BlockSpec((1, d), lambda i: (0, 0)), tok(d)] + [_HBM] * ncar,
        out_shape=[jax.ShapeDtypeStruct((t, d), F32), jax.ShapeDtypeStruct((1, d), F32),
                   jax.ShapeDtypeStruct((t, d), BF16)]
        + [jax.ShapeDtypeStruct((3,) + c.shape[1:], c.dtype) for c in carry],
        scratch_shapes=sems,
        compiler_params=_params("arbitrary"),
    )(x, dy, nw, *ws, *flat, *carry)


def _shift_down(v, j, prev8):
    if j == 0:
        return v
    r = pltpu.roll(v, j, 0)
    p = pltpu.roll(prev8, j, 0)
    rows = lax.broadcasted_iota(jnp.int32, prev8.shape, 0)
    first = jnp.where(rows < j, p, r[0:SUBLANES])
    return jnp.concatenate([first, r[SUBLANES:]], axis=0)


def _shift_up(v, j, next8):
    if j == 0:
        return v
    n = v.shape[0]
    r = pltpu.roll(v, n - j, 0)
    p = pltpu.roll(next8, SUBLANES - j, 0)
    rows = lax.broadcasted_iota(jnp.int32, next8.shape, 0)
    last = jnp.where(rows >= SUBLANES - j, p, r[n - SUBLANES:])
    return jnp.concatenate([r[:n - SUBLANES], last], axis=0)


def _sc_fwd(x, bcu, cw, wo):
    t, d = x.shape
    tm = min(TOKEN_TILE, t)
    hb = tm // SUBLANES

    def body(x_ref, bcu_ref, prev_ref, cw_ref, wo_ref, o_ref):
        bg, cg, u = bcu_ref[:, 0:d], bcu_ref[:, d:2 * d], bcu_ref[:, 2 * d:3 * d]
        q = cg * u
        qp = jnp.where(pl.program_id(0) == 0, 0.0, prev_ref[:, d:2 * d] * prev_ref[:, 2 * d:3 * d])
        cwv = cw_ref[...]
        v = cwv[2:3] * q + cwv[1:2] * _shift_down(q, 1, qp) + cwv[0:1] * _shift_down(q, 2, qp)
        o_ref[...] = x_ref[...] + _dot((bg * v).astype(BF16), wo_ref[...])

    return pl.pallas_call(
        body, name="sc_fwd", grid=(t // tm,),
        in_specs=[pl.BlockSpec((tm, d), lambda i: (i, 0)), pl.BlockSpec((tm, 3 * d), lambda i: (i, 0)),
                  pl.BlockSpec((SUBLANES, 3 * d), lambda i: (jnp.maximum(i * hb - 1, 0), 0)),
                  _resident((SUBLANES, d)), _resident((d, d))],
        out_specs=pl.BlockSpec((tm, d), lambda i: (i, 0)),
        out_shape=jax.ShapeDtypeStruct((t, d), F32),
        compiler_params=_params("parallel"),
    )(x, bcu, bcu, cw, wo)


def _sc_bwd(dy, bcu, cw, wo):
    t, d = dy.shape
    tm = min(TOKEN_TILE, t)
    hb = tm // SUBLANES
    nt = t // tm

    def body(dy_ref, dyn_ref, bcu_ref, prev_ref, next_ref, cw_ref, wo_ref, dbcu_ref, p_ref, dcw_ref):
        i = pl.program_id(0)
        bg, cg, u = bcu_ref[:, 0:d], bcu_ref[:, d:2 * d], bcu_ref[:, 2 * d:3 * d]
        q = cg * u
        qp = jnp.where(i == 0, 0.0, prev_ref[:, d:2 * d] * prev_ref[:, 2 * d:3 * d])
        cwv = cw_ref[...]
        q1 = _shift_down(q, 1, qp)
        q2 = _shift_down(q, 2, qp)
        v = cwv[2:3] * q + cwv[1:2] * q1 + cwv[0:1] * q2
        p_ref[...] = (bg * v).astype(BF16)
        wov = wo_ref[...]
        dp = _dot_nt(dy_ref[...].astype(BF16), wov)
        dpn = _dot_nt(dyn_ref[...].astype(BF16), wov)
        dv = dp * bg
        dvn = jnp.where(i == nt - 1, 0.0, dpn * next_ref[:, 0:d])
        dq = cwv[2:3] * dv + cwv[1:2] * _shift_up(dv, 1, dvn) + cwv[0:1] * _shift_up(dv, 2, dvn)
        dbcu_ref[:, 0:d] = (dp * v).astype(BF16)
        dbcu_ref[:, d:2 * d] = (dq * u).astype(BF16)
        dbcu_ref[:, 2 * d:3 * d] = (dq * cg).astype(BF16)

        @pl.when(i == 0)
        def _():
            dcw_ref[...] = jnp.zeros_like(dcw_ref)

        dcw_ref[0:1, :] += jnp.sum(dv * q2, axis=0, keepdims=True)
        dcw_ref[1:2, :] += jnp.sum(dv * q1, axis=0, keepdims=True)
        dcw_ref[2:3, :] += jnp.sum(dv * q, axis=0, keepdims=True)

    last8 = t // SUBLANES - 1
    return pl.pallas_call(
        body, name="sc_bwd", grid=(nt,),
        in_specs=[pl.BlockSpec((tm, d), lambda i: (i, 0)),
                  pl.BlockSpec((SUBLANES, d), lambda i: (jnp.minimum((i + 1) * hb, last8), 0)),
                  pl.BlockSpec((tm, 3 * d), lambda i: (i, 0)),
                  pl.BlockSpec((SUBLANES, 3 * d), lambda i: (jnp.maximum(i * hb - 1, 0), 0)),
                  pl.BlockSpec((SUBLANES, 3 * d), lambda i: (jnp.minimum((i + 1) * hb, last8), 0)),
                  _resident((SUBLANES, d)), _resident((d, d))],
        out_specs=[pl.BlockSpec((tm, 3 * d), lambda i: (i, 0)), pl.BlockSpec((tm, d), lambda i: (i, 0)),
                   pl.BlockSpec((SUBLANES, d), lambda i: (0, 0))],
        out_shape=[jax.ShapeDtypeStruct((t, 3 * d), BF16), jax.ShapeDtypeStruct((t, d), BF16),
                   jax.ShapeDtypeStruct((SUBLANES, d), F32)],
        compiler_params=_params("arbitrary"),
    )(dy, dy, bcu, bcu, bcu, cw, wo)


NEG_BIG = -1e30


HEADS_PER_GROUP = SSD_N_HEADS // SSD_N_GROUPS
PAIRS_PER_GROUP = HEADS_PER_GROUP // 2


def _ssd_consts():
    r = lax.broadcasted_iota(jnp.int32, (LANES, LANES), 0)
    c = lax.broadcasted_iota(jnp.int32, (LANES, LANES), 1)
    return (c <= r).astype(BF16), (c >= r).astype(BF16)


def _ssd_decay(dtr, dtb, alog, tril):
    shape = (SSD_CHUNK, LANES)
    lanes = lax.broadcasted_iota(jnp.int32, shape, 1)
    rows = lax.broadcasted_iota(jnp.int32, shape, 0)
    pre = dtr + dtb
    valid = lanes < HEADS_PER_GROUP
    dt = jnp.where(valid, jnp.maximum(pre, 0.0) + jnp.log(1.0 + jnp.exp(-jnp.abs(pre))), 0.0)
    a = -jnp.exp(alog)
    acs = _sel_left(tril, _split3(dt * a))
    return dt, a, acs, pre, valid, rows, lanes


def _lane_col(v, j):
    return jnp.broadcast_to(v[:, j:j + 1], v.shape)


def _ssd_pair_terms(k, dt, cols, low_half, xs):
    dtp = jnp.where(low_half, _lane_col(dt, 2 * k), _lane_col(dt, 2 * k + 1))
    acsp = jnp.where(low_half, cols[2 * k], cols[2 * k + 1])
    lastp = acsp[SSD_CHUNK - 1:SSD_CHUNK, :]
    eap = jnp.exp(acsp)
    decp = jnp.exp(lastp - acsp)
    etp = jnp.exp(lastp)
    xdt = xs * dtp
    return dtp, eap, decp, etp, xdt


def _ssd_conv_taps(cwb, xr, prev8):
    sh = [_shift_down(xr, j, prev8) for j in range(SSD_CONV_W)]
    xc = cwb[4:5]
    for j in range(SSD_CONV_W):
        xc = xc + cwb[3 - j:4 - j] * sh[j]
    return xc, sh


def _ssd_specs(nc, rev):
    ch = (lambda i: nc - 1 - i) if rev else (lambda i: i)
    L = SSD_CHUNK
    xs = pl.BlockSpec((4, L, LANES), lambda g, i: (g, ch(i), 0))
    bb = pl.BlockSpec((1, L, LANES), lambda g, i: (N_XS_BLK + g, ch(i), 0))
    cc = pl.BlockSpec((1, L, LANES), lambda g, i: (N_XS_BLK + SSD_N_GROUPS + g, ch(i), 0))
    dt = pl.BlockSpec((1, L, LANES), lambda g, i: (g, ch(i), 0))
    cw_xs = pl.BlockSpec((4, SUBLANES, LANES), lambda g, i: (g, 0, 0))
    cw_b = pl.BlockSpec((1, SUBLANES, LANES), lambda g, i: (N_XS_BLK + g, 0, 0))
    cw_c = pl.BlockSpec((1, SUBLANES, LANES), lambda g, i: (N_XS_BLK + SSD_N_GROUPS + g, 0, 0))
    st = pl.BlockSpec((1, 4, LANES, LANES), lambda g, i: (ch(i), g, 0, 0))
    grp4 = pl.BlockSpec((4, L, LANES), lambda g, i: (g, ch(i), 0))
    return xs, bb, cc, dt, cw_xs, cw_b, cw_c, st, grp4


def _ssd_scan_fwd(xr, dtr, cwb, dtb, alog, dskip, consts):
    t = xr.shape[1]
    L = SSD_CHUNK
    nc = t // L
    tril, _ = consts
    xs_s, b_s, c_s, dt_s, cwx_s, cwb_s, cwc_s, st_s, grp4 = _ssd_specs(nc, False)
    grp_row = pl.BlockSpec((1, 1, LANES), lambda g, i: (g, 0, 0))

    def body(xs_ref, b_ref, c_ref, dtr_ref, cwx_ref, cwbb_ref, cwc_ref, dtb_ref, alog_ref, dsk_ref,
             tril_ref, y_ref, sp_ref, state, tail):
        @pl.when(pl.program_id(1) == 0)
        def _():
            state[...] = jnp.zeros_like(state)
            tail[...] = jnp.zeros_like(tail)

        xa = []
        for b in range(6):
            xrb = xs_ref[b] if b < 4 else (b_ref[0] if b == 4 else c_ref[0])
            cw = cwx_ref[b] if b < 4 else (cwbb_ref[0] if b == 4 else cwc_ref[0])
            xc, _ = _ssd_conv_taps(cw, xrb, tail[b])
            tail[b] = xrb[L - SUBLANES:]
            xa.append(xc * _sigmoid(xc))

        dt, a, acs, _, _, rows, lanes = _ssd_decay(dtr_ref[0], dtb_ref[0], alog_ref[0], tril_ref[...])
        acst = acs.T
        bb = xa[4].astype(BF16)
        cb_ = xa[5].astype(BF16)
        cbm = _dot_nt(cb_, bb)
        causal = rows >= lanes
        low_half = lanes < LANES // 2
        cols = [_lane_col(acs, j) for j in range(HEADS_PER_GROUP)]

        for k in range(PAIRS_PER_GROUP):
            xs = xa[k]
            dtp, eap, decp, etp, xdt = _ssd_pair_terms(k, dt, cols, low_half, xs)
            ms = []
            for j in (2 * k, 2 * k + 1):
                diff = cols[j] - jnp.broadcast_to(acst[j:j + 1, :], (L, L))
                ms.append((cbm * jnp.exp(jnp.where(causal, diff, NEG_BIG))).astype(BF16))
            xcat = jnp.concatenate([jnp.where(low_half, xdt, 0.0).astype(BF16),
                                    jnp.where(low_half, 0.0, xdt).astype(BF16)], axis=0)
            yd = _dot(jnp.concatenate(ms, axis=1), xcat)
            sp = state[k]
            yo = eap * _dot(cb_, sp.astype(BF16))
            y_ref[k] = yd + yo + dsk_ref[k][0:1] * xs
            sp_ref[0, k] = sp
            state[k] = etp * sp + _dot_tn(bb, (decp * xdt).astype(BF16))

    return pl.pallas_call(
        body, name="ssd_scan_fwd", grid=(SSD_N_GROUPS, nc),
        in_specs=[xs_s, b_s, c_s, dt_s, cwx_s, cwb_s, cwc_s, grp_row, grp_row, cwx_s, _resident(tril.shape)],
        out_specs=[grp4, st_s],
        out_shape=[jax.ShapeDtypeStruct((N_XS_BLK, t, LANES), F32),
                   jax.ShapeDtypeStruct((nc, N_XS_BLK, LANES, LANES), F32)],
        scratch_shapes=[pltpu.VMEM((4, LANES, LANES), F32), pltpu.VMEM((6, SUBLANES, LANES), F32)],
        compiler_params=_params("arbitrary", "arbitrary"),
    )(xr, xr, xr, dtr, cwb, cwb, cwb, dtb, alog, dskip, tril)


def _ssd_scan_bwd(xr, dtr, dy, sprev, cwb, dtb, alog, dskip, consts):
    t = xr.shape[1]
    L = SSD_CHUNK
    nc = t // L
    hb = L // SUBLANES
    tril, triu = consts
    xs_s, b_s, c_s, dt_s, cwx_s, cwb_s, cwc_s, st_s, grp4 = _ssd_specs(nc, True)
    grp_row = pl.BlockSpec((1, 1, LANES), lambda g, i: (g, 0, 0))
    prev = lambda off: pl.BlockSpec(
        (4 if off is None else 1, SUBLANES, LANES),
        (lambda g, i: (g, jnp.maximum((nc - 1 - i) * hb - 1, 0), 0)) if off is None else
        (lambda g, i: (off + g, jnp.maximum((nc - 1 - i) * hb - 1, 0), 0)))
    grp1 = pl.BlockSpec((1, L, LANES), lambda g, i: (g, nc - 1 - i, 0))
    acc4 = pl.BlockSpec((4, SUBLANES, LANES), lambda g, i: (g, 0, 0))
    acc1 = pl.BlockSpec((1, SUBLANES, LANES), lambda g, i: (g, 0, 0))

    def body(xs_ref, b_ref, c_ref, pxs_ref, pb_ref, pc_ref, dtr_ref, dy_ref, sp_ref,
             cwx_ref, cwbb_ref, cwc_ref, dtb_ref, alog_ref, dsk_ref, tril_ref, triu_ref,
             dxs_ref, db_ref, dc_ref, ddtr_ref, dcwx_ref, dcwb_ref, dcwc_ref, dd_ref, dsm_ref,
             dstate, head):
        step = pl.program_id(1)
        first_chunk = step == nc - 1

        @pl.when(step == 0)
        def _():
            dstate[...] = jnp.zeros_like(dstate)
            head[...] = jnp.zeros_like(head)
            for r in (dcwx_ref, dcwb_ref, dcwc_ref, dd_ref, dsm_ref):
                r[...] = jnp.zeros_like(r)

        def blk(b):
            xrb = xs_ref[b] if b < 4 else (b_ref[0] if b == 4 else c_ref[0])
            cw = cwx_ref[b] if b < 4 else (cwbb_ref[0] if b == 4 else cwc_ref[0])
            p8 = pxs_ref[b] if b < 4 else (pb_ref[0] if b == 4 else pc_ref[0])
            return xrb, cw, jnp.where(first_chunk, 0.0, p8)

        xa, dsil = [], []
        for b in range(6):
            xrb, cw, p8 = blk(b)
            xc, _ = _ssd_conv_taps(cw, xrb, p8)
            sig = _sigmoid(xc)
            xa.append(xc * sig)
            dsil.append(sig * (1.0 + xc * (1.0 - sig)))

        dt, a, acs, pre, valid, rows, lanes = _ssd_decay(dtr_ref[0], dtb_ref[0], alog_ref[0], tril_ref[...])
        acst = acs.T
        bb = xa[4].astype(BF16)
        cb_ = xa[5].astype(BF16)
        cbm = _dot_nt(cb_, bb)
        cbmt = _dot_nt(bb, cb_)
        causal = rows >= lanes
        anti = rows <= lanes
        low_half = lanes < LANES // 2
        last_row = rows == L - 1
        cols = [_lane_col(acs, j) for j in range(HEADS_PER_GROUP)]
        zeros = jnp.zeros((L, LANES), F32)
        dcb, dcbt, dbg, dcg, dacs, dacst, ddt = zeros, zeros, zeros, zeros, zeros, zeros, zeros
        dxa = []

        for k in range(PAIRS_PER_GROUP):
            xs = xa[k]
            dtp, eap, decp, etp, xdt = _ssd_pair_terms(k, dt, cols, low_half, xs)
            xdtb = xdt.astype(BF16)
            w = decp * xdt
            wb = w.astype(BF16)
            dyv = dy_ref[k]
            sp = sp_ref[0, k]
            spb = sp.astype(BF16)
            dsn = dstate[k]
            dsnb = dsn.astype(BF16)
            yoff = eap * _dot(cb_, spb)
            dgb = (eap * dyv).astype(BF16)
            dcg = dcg + _dot_nt(dgb, spb)
            dstate[k] = _dot_tn(cb_, dgb) + etp * dsn
            last_lane = etp * jnp.sum(dsn * sp, axis=0, keepdims=True)
            dbg = dbg + _dot_nt(wb, dsnb)
            dw = _dot(bb, dsnb)
            t2 = dw * w
            dxdt = decp * dw
            last_lane = last_lane + jnp.sum(t2, axis=0, keepdims=True)
            lane_acc = dyv * yoff - t2 + jnp.where(last_row, last_lane, 0.0)
            for j in (2 * k, 2 * k + 1):
                diff = cols[j] - jnp.broadcast_to(acst[j:j + 1, :], (L, L))
                lm = jnp.exp(jnp.where(causal, diff, NEG_BIG))
                lmt = jnp.exp(jnp.where(anti, -diff, NEG_BIG))
                dye = jnp.where(low_half == (j % 2 == 0), dyv, 0.0).astype(BF16)
                dm = _dot_nt(dye, xdtb)
                dmt = _dot_nt(xdtb, dye)
                mt = cbmt * lmt
                seg = dmt * mt - dm * (cbm * lm)
                dacst = dacst + jnp.where(rows == j, jnp.sum(seg, axis=0, keepdims=True), 0.0)
                dcb = dcb + dm * lm
                dcbt = dcbt + dmt * lmt
                dxdt = dxdt + _dot(mt.astype(BF16), dye)
            ddt_lane = dxdt * xs
            for j, keep in ((2 * k, low_half), (2 * k + 1, jnp.logical_not(low_half))):
                dacs = dacs + jnp.where(lanes == j, jnp.sum(jnp.where(keep, lane_acc, 0.0), axis=1, keepdims=True), 0.0)
                ddt = ddt + jnp.where(lanes == j, jnp.sum(jnp.where(keep, ddt_lane, 0.0), axis=1, keepdims=True), 0.0)
            dxa.append(dsk_ref[k][0:1] * dyv + dxdt * dtp)
            dd_ref[k, 0:1, :] += jnp.sum(dyv * xs, axis=0, keepdims=True)

        dxa.append(dbg + _dot(dcbt.astype(BF16), cb_))
        dxa.append(dcg + _dot(dcb.astype(BF16), bb))
        dac = _sel_left(triu_ref[...], _split3(dacs + dacst.T))
        ddtr = jnp.where(valid, (ddt + dac * a) * _sigmoid(pre), 0.0)
        ddtr_ref[0] = ddtr.astype(BF16)
        dsm_ref[0, 0:1, :] += jnp.sum(ddtr, axis=0, keepdims=True)
        dsm_ref[0, 1:2, :] += jnp.sum(dac * dt, axis=0, keepdims=True) * a

        for b in range(6):
            xrb, cw, p8 = blk(b)
            sh = [_shift_down(xrb, j, p8) for j in range(SSD_CONV_W)]
            dxc = dxa[b] * dsil[b]
            acc = dcwx_ref.at[b] if b < 4 else (dcwb_ref.at[0] if b == 4 else dcwc_ref.at[0])
            acc[4:5, :] += jnp.sum(dxc, axis=0, keepdims=True)
            dxr = jnp.zeros_like(dxc)
            for j in range(SSD_CONV_W):
                acc[3 - j:4 - j, :] += jnp.sum(dxc * sh[j], axis=0, keepdims=True)
                dxr = dxr + cw[3 - j:4 - j] * _shift_up(dxc, j, head[b])
            head[b] = dxc[0:SUBLANES]
            out = dxs_ref.at[b] if b < 4 else (db_ref.at[0] if b == 4 else dc_ref.at[0])
            out[...] = dxr.astype(BF16)

    return pl.pallas_call(
        body, name="ssd_scan_bwd", grid=(SSD_N_GROUPS, nc),
        in_specs=[xs_s, b_s, c_s, prev(None), prev(N_XS_BLK), prev(N_XS_BLK + SSD_N_GROUPS), dt_s, grp4, st_s,
                  cwx_s, cwb_s, cwc_s, grp_row, grp_row, cwx_s, _resident(tril.shape), _resident(triu.shape)],
        out_specs=[grp4, grp1, grp1, grp1, acc4, acc1, acc1, acc4, acc1],
        out_shape=[jax.ShapeDtypeStruct((N_XS_BLK, t, LANES), BF16),
                   jax.ShapeDtypeStruct((SSD_N_GROUPS, t, LANES), BF16),
                   jax.ShapeDtypeStruct((SSD_N_GROUPS, t, LANES), BF16),
                   jax.ShapeDtypeStruct((SSD_N_GROUPS, t, LANES), BF16),
                   jax.ShapeDtypeStruct((N_XS_BLK, SUBLANES, LANES), F32),
                   jax.ShapeDtypeStruct((SSD_N_GROUPS, SUBLANES, LANES), F32),
                   jax.ShapeDtypeStruct((SSD_N_GROUPS, SUBLANES, LANES), F32),
                   jax.ShapeDtypeStruct((N_XS_BLK, SUBLANES, LANES), F32),
                   jax.ShapeDtypeStruct((SSD_N_GROUPS, SUBLANES, LANES), F32)],
        scratch_shapes=[pltpu.VMEM((4, LANES, LANES), F32), pltpu.VMEM((6, SUBLANES, LANES), F32)],
        compiler_params=_params("arbitrary", "arbitrary"),
    )(xr, xr, xr, xr, xr, xr, dtr, dy, sprev, cwb, cwb, cwb, dtb, alog, dskip, tril, triu)


def _ssd_gate_fwd(x, y, z, gnw, wo):
    t, d = x.shape
    tm = min(TOKEN_TILE, t)
    nb = N_XS_BLK
    per = nb // SSD_N_GROUPS

    def body(x_ref, y_ref, z_ref, gnw_ref, wo_ref, o_ref, gn_ref):
        gs = []
        for j in range(nb):
            zv = z_ref[j]
            gs.append(y_ref[j] * (zv * _sigmoid(zv)))
        for q in range(SSD_N_GROUPS):
            ss = sum(jnp.sum(gs[j] * gs[j], axis=1, keepdims=True) for j in range(q * per, (q + 1) * per))
            inv = lax.rsqrt(ss / (per * LANES) + RMS_EPS)
            for j in range(q * per, (q + 1) * per):
                gn_ref[:, j * LANES:(j + 1) * LANES] = ((gs[j] * inv) * gnw_ref[j]).astype(BF16)
        o_ref[...] = x_ref[...] + _dot(gn_ref[...], wo_ref[...])

    blk = pl.BlockSpec((nb, tm, LANES), lambda i: (0, i, 0))
    return pl.pallas_call(
        body, name="ssd_gate_fwd", grid=(t // tm,),
        in_specs=[pl.BlockSpec((tm, d), lambda i: (i, 0)), blk, blk, _resident((nb, 1, LANES)),
                  _resident((SSD_D_INNER, d))],
        out_specs=[pl.BlockSpec((tm, d), lambda i: (i, 0)), pl.BlockSpec((tm, SSD_D_INNER), lambda i: (i, 0))],
        out_shape=[jax.ShapeDtypeStruct((t, d), F32), jax.ShapeDtypeStruct((t, SSD_D_INNER), BF16)],
        compiler_params=_params("parallel"),
    )(x, y, z, gnw, wo)


def _ssd_gate_bwd(dy, y, z, gnw, wo):
    t, d = dy.shape
    tm = min(TOKEN_TILE, t)
    nb = N_XS_BLK
    per = nb // SSD_N_GROUPS

    def body(dy_ref, y_ref, z_ref, gnw_ref, wo_ref, dys_ref, dz_ref, dgnw_ref):
        @pl.when(pl.program_id(0) == 0)
        def _():
            dgnw_ref[...] = jnp.zeros_like(dgnw_ref)

        dgn = _dot_nt(dy_ref[...].astype(BF16), wo_ref[...])
        for q in range(SSD_N_GROUPS):
            js = range(q * per, (q + 1) * per)
            gs, sil, dsil = {}, {}, {}
            for j in js:
                zv = z_ref[j]
                sig = _sigmoid(zv)
                sil[j] = zv * sig
                dsil[j] = sig * (1.0 + zv * (1.0 - sig))
                gs[j] = y_ref[j] * sil[j]
            ss = sum(jnp.sum(gs[j] * gs[j], axis=1, keepdims=True) for j in js)
            inv = lax.rsqrt(ss / (per * LANES) + RMS_EPS)
            ghat = {j: gs[j] * inv for j in js}
            dgh = {}
            for j in js:
                dj = dgn[:, j * LANES:(j + 1) * LANES]
                dgnw_ref[j] += jnp.sum(dj * ghat[j], axis=0, keepdims=True)
                dgh[j] = dj * gnw_ref[j]
            mean = sum(jnp.sum(dgh[j] * ghat[j], axis=1, keepdims=True) for j in js) / (per * LANES)
            for j in js:
                dg = inv * (dgh[j] - ghat[j] * mean)
                dys_ref[j] = dg * sil[j]
                dz_ref[j] = (dg * y_ref[j] * dsil[j]).astype(BF16)

    blk = pl.BlockSpec((nb, tm, LANES), lambda i: (0, i, 0))
    return pl.pallas_call(
        body, name="ssd_gate_bwd", grid=(t // tm,),
        in_specs=[pl.BlockSpec((tm, d), lambda i: (i, 0)), blk, blk, _resident((nb, 1, LANES)),
                  _resident((SSD_D_INNER, d))],
        out_specs=[blk, blk, pl.BlockSpec((nb, 1, LANES), lambda i: (0, 0, 0))],
        out_shape=[jax.ShapeDtypeStruct((nb, t, LANES), F32), jax.ShapeDtypeStruct((nb, t, LANES), BF16),
                   jax.ShapeDtypeStruct((nb, 1, LANES), F32)],
        compiler_params=_params("arbitrary"),
    )(dy, y, z, gnw, wo)


def _lane_blocks(v):
    r, n = v.shape[0], v.shape[1] // LANES
    return v.reshape(r, n, LANES).transpose(1, 0, 2)


def _ssd_prep(w_in, conv_w, conv_b, dt_bias, a_log, d_skip, norm_w):
    n_main = SSD_D_INNER + SSD_CONV_DIM
    w_dt = w_in[:, n_main:].reshape(-1, SSD_N_GROUPS, HEADS_PER_GROUP)
    w_dt = jnp.pad(w_dt, ((0, 0), (0, 0), (0, LANES - HEADS_PER_GROUP))).reshape(-1, SSD_N_GROUPS * LANES)
    w_in_pad = jnp.concatenate([w_in[:, :n_main], w_dt], axis=1)
    taps = jnp.concatenate([conv_w, conv_b[None], jnp.zeros((SUBLANES - SSD_CONV_W - 1, SSD_CONV_DIM), F32)], axis=0)
    cwb = _lane_blocks(taps)
    row = lambda v: jnp.pad(v.reshape(SSD_N_GROUPS, 1, HEADS_PER_GROUP), ((0, 0), (0, 0), (0, LANES - HEADS_PER_GROUP)))
    dskip = jnp.broadcast_to(jnp.repeat(d_skip, SSD_D_INNER // SSD_N_HEADS).reshape(N_XS_BLK, 1, LANES),
                             (N_XS_BLK, SUBLANES, LANES))
    gnw = norm_w.reshape(N_XS_BLK, 1, LANES)
    return w_in_pad, cwb, row(dt_bias), row(a_log), dskip, gnw


def _ssd_layer_fwd(x, nw, prm, wo, consts):
    w_in_pad, cwb, dtb, alog, dskip, gnw = prm
    z, xr, dtr = _ssd_inproj(x, nw, w_in_pad)
    y, sprev = _ssd_scan_fwd(xr, dtr, cwb, dtb, alog, dskip, consts)
    out, gn = _ssd_gate_fwd(x, y, z, gnw, wo)
    return out, (z, xr, dtr, y, sprev, gn)


def _ssd_layer_bwd(x, dy, nw, prm, wo, consts, saved):
    w_in_pad, cwb, dtb, alog, dskip, gnw = prm
    z, xr, dtr, y, sprev, gn = saved
    dys, dz, dgnw = _ssd_gate_bwd(dy, y, z, gnw, wo)
    dwo = _matmul_tn(gn, dy, name="wgrad_ssd_out")
    dxs, db, dc, ddtr, dcwx, dcwb, dcwc, dd, dsm = _ssd_scan_bwd(xr, dtr, dys, sprev, cwb, dtb, alog, dskip, consts)
    pieces = [dz, dxs, db, dc, ddtr]
    dx, dnw, h = _inproj_bwd(x, dy, nw, [w_in_pad], [pieces])
    dws = [_matmul_tn_blocked(h, p, name=f"wgrad_ssd_in{i}") for i, p in enumerate(pieces)]
    dw_dt = dws[4].reshape(-1, SSD_N_GROUPS, LANES)[:, :, :HEADS_PER_GROUP].reshape(-1, SSD_N_HEADS)
    dw_in = jnp.concatenate(dws[:4] + [dw_dt], axis=1)
    dtaps = jnp.concatenate([dcwx, dcwb, dcwc], axis=0).transpose(1, 0, 2).reshape(SUBLANES, SSD_CONV_DIM)
    by_head = lambda r: dsm[:, r, :HEADS_PER_GROUP].reshape(SSD_N_HEADS)
    d_d = jnp.sum(dd[:, 0, :].reshape(SSD_N_HEADS, SSD_D_INNER // SSD_N_HEADS), axis=1)
    return dx, (dnw, dw_in, dtaps[:SSD_CONV_W], dtaps[SSD_CONV_W], by_head(0), by_head(1),
                d_d, dgnw.reshape(SSD_D_INNER), dwo)


def _loss_head(x, fw, target):
    t, d = x.shape
    tm = min(TOKEN_TILE, t)

    def body(x_ref, fw_ref, tgt_ref, loss_ref, dx_ref, dfw_ref):
        fwv = fw_ref[...]
        y, xhat, inv = _rms_fwd(x_ref[...], fwv)
        err = y - tgt_ref[...]
        tot = jnp.sum(jnp.sum(err * err, axis=1, keepdims=True), axis=0, keepdims=True)
        dx, dw = _rms_bwd(err * (1.0 / d), xhat, inv, fwv)
        dx_ref[...] = dx

        @pl.when(pl.program_id(0) == 0)
        def _():
            loss_ref[...] = jnp.zeros_like(loss_ref)
            dfw_ref[...] = jnp.zeros_like(dfw_ref)

        loss_ref[...] += jnp.broadcast_to(tot * (0.5 / d), loss_ref.shape)
        dfw_ref[...] += dw

    tok = pl.BlockSpec((tm, d), lambda i: (i, 0))
    return pl.pallas_call(
        body, name="loss_head", grid=(t // tm,),
        in_specs=[tok, _resident((1, d)), tok],
        out_specs=[pl.BlockSpec((1, LANES), lambda i: (0, 0)), tok, pl.BlockSpec((1, d), lambda i: (0, 0))],
        out_shape=[jax.ShapeDtypeStruct((1, LANES), F32), jax.ShapeDtypeStruct((t, d), F32),
                   jax.ShapeDtypeStruct((1, d), F32)],
        compiler_params=_params("arbitrary"),
    )(x, fw, target)


def _row_tile(rows, cap):
    best = SUBLANES
    for r in range(SUBLANES, min(rows, cap) + 1, SUBLANES):
        if rows % r == 0:
            best = r
    return best


def _adamw(w, g, m, v, name):
    rows, cols = w.shape
    br = _row_tile(rows, 256)
    c1 = 1.0 - ADAM_B1 ** ADAM_STEP
    c2 = 1.0 - ADAM_B2 ** ADAM_STEP

    def body(w_ref, g_ref, m_ref, v_ref, d_ref, nm_ref, nv_ref):
        gv = g_ref[...]
        nm = ADAM_B1 * m_ref[...] + (1.0 - ADAM_B1) * gv
        nv = ADAM_B2 * v_ref[...] + (1.0 - ADAM_B2) * (gv * gv)
        nm_ref[...] = nm
        nv_ref[...] = nv
        d_ref[...] = -ADAM_LR * ((nm / c1) / (jnp.sqrt(nv / c2) + ADAM_EPS) + ADAM_WD * w_ref[...])

    blk = pl.BlockSpec((br, cols), lambda i: (i, 0))
    shp = jax.ShapeDtypeStruct((rows, cols), F32)
    return pl.pallas_call(
        body, name=name, grid=(rows // br,), in_specs=[blk] * 4, out_specs=[blk] * 3, out_shape=[shp] * 3,
        compiler_params=_params("parallel"),
    )(w, g, m, v)


def _place():
    x, y, c = lax.axis_index("x"), lax.axis_index("y"), lax.axis_index("c")
    return x, y, c, [(1 - x, y), (x, 1 - y), (1 - x, 1 - y)]


def _remote(src, dst, send_sems, recv_sems, k, to):
    return pltpu.make_async_remote_copy(src_ref=src, dst_ref=dst, send_sem=send_sems.at[k], recv_sem=recv_sems.at[k],
                                        device_id=to, device_id_type=MESH)


def _scatter_copies(srcs, dsts, send_sems, recv_sems):
    x, y, c, chips = _place()
    sends, arrivals = [], []
    for oi, (src, dst) in enumerate(zip(srcs, dsts)):
        for j, chip in enumerate(chips):
            sends.append(_remote(src.at[2 * chip[0] + chip[1]], dst.at[j], send_sems, recv_sems, 3 * oi + j, (*chip, c)))
            arrivals.append(_remote(dst.at[j], dst.at[j], send_sems, recv_sems, 3 * oi + j, (*chip, c)))
    return sends, arrivals


def _scatter_start(srcs, dsts, send_sems, recv_sems):
    for cp in _scatter_copies(srcs, dsts, send_sems, recv_sems)[0]:
        cp.start()


def _scatter_wait(srcs, dsts, send_sems, recv_sems):
    sends, arrivals = _scatter_copies(srcs, dsts, send_sems, recv_sems)
    for cp in arrivals:
        cp.wait_recv()
    for cp in sends:
        cp.wait_send()


def _all_gather_shards(arrs):
    n = len(arrs)

    def body(*refs):
        srcs, dsts = refs[:n], refs[n:2 * n]
        send_sems, recv_sems = refs[2 * n:]
        x, y, c, chips = _place()
        me = 2 * x + y
        sibling = (x, y, 1 - c)
        sent = []
        for oi, (src, dst) in enumerate(zip(srcs, dsts)):
            for j, chip in enumerate(chips):
                sent.append(_remote(src.at[c], dst.at[me, c], send_sems, recv_sems, 6 * oi + j, (*chip, c)))
                sent[-1].start()
        for oi, dst in enumerate(dsts):
            for j, chip in enumerate(chips):
                landed = dst.at[2 * chip[0] + chip[1], c]
                _remote(landed, landed, send_sems, recv_sems, 6 * oi + j, (*chip, c)).wait_recv()
                sent.append(_remote(landed, landed, send_sems, recv_sems, 6 * oi + 3 + j, sibling))
                sent[-1].start()
        for oi, dst in enumerate(dsts):
            for j, chip in enumerate(chips):
                landed = dst.at[2 * chip[0] + chip[1], 1 - c]
                _remote(landed, landed, send_sems, recv_sems, 6 * oi + 3 + j, sibling).wait_recv()
        for cp in sent:
            cp.wait_send()

    return pl.pallas_call(
        body, name="all_gather_shards",
        in_specs=[_HBM] * n, out_specs=[_HBM] * n,
        out_shape=[jax.ShapeDtypeStruct((N_SHARDS,) + a.shape, a.dtype) for a in arrs],
        scratch_shapes=[pltpu.SemaphoreType.DMA((6 * n,)), pltpu.SemaphoreType.DMA((6 * n,))],
    )(*arrs)


def _swap_halves(arrs):
    n = len(arrs)

    def body(*refs):
        srcs, dsts = refs[:n], refs[n:2 * n]
        send_sems, recv_sems = refs[2 * n:]
        x, y, c, _ = _place()
        cps = [_remote(src.at[:, 1 - c], dst, send_sems, recv_sems, oi, (x, y, 1 - c))
               for oi, (src, dst) in enumerate(zip(srcs, dsts))]
        for cp in cps:
            cp.start()
        for cp in cps:
            cp.wait()

    return pl.pallas_call(
        body, name="swap_halves", in_specs=[_HBM] * n, out_specs=[_HBM] * n,
        out_shape=[jax.ShapeDtypeStruct((a.shape[0],) + a.shape[2:], a.dtype) for a in arrs],
        scratch_shapes=[pltpu.SemaphoreType.DMA((n,)), pltpu.SemaphoreType.DMA((n,))],
    )(*arrs)


def _scatter_to_chips(arrs):
    n = len(arrs)

    def body(*refs):
        srcs, dsts = refs[:n], refs[n:2 * n]
        _scatter_start(srcs, dsts, *refs[2 * n:])
        _scatter_wait(srcs, dsts, *refs[2 * n:])

    return pl.pallas_call(
        body, name="scatter_to_chips", in_specs=[_HBM] * n, out_specs=[_HBM] * n,
        out_shape=[jax.ShapeDtypeStruct((3,) + a.shape[1:], a.dtype) for a in arrs],
        scratch_shapes=[pltpu.SemaphoreType.DMA((3 * n,)), pltpu.SemaphoreType.DMA((3 * n,))],
    )(*arrs)


def _join_halves(arrs):
    n = len(arrs)

    def body(*refs):
        bufs = refs[n:2 * n]
        send_sems, recv_sems = refs[2 * n:]
        x, y, c, _ = _place()
        sibling = (x, y, 1 - c)
        sent = [_remote(buf.at[c], buf.at[c], send_sems, recv_sems, oi, sibling) for oi, buf in enumerate(bufs)]
        for cp in sent:
            cp.start()
        for oi, buf in enumerate(bufs):
            _remote(buf.at[1 - c], buf.at[1 - c], send_sems, recv_sems, oi, sibling).wait_recv()
        for cp in sent:
            cp.wait_send()

    return pl.pallas_call(
        body, name="join_halves", in_specs=[_HBM] * n, out_specs=[_HBM] * n,
        out_shape=[jax.ShapeDtypeStruct(a.shape, a.dtype) for a in arrs],
        input_output_aliases={i: i for i in range(n)},
        scratch_shapes=[pltpu.SemaphoreType.DMA((n,)), pltpu.SemaphoreType.DMA((n,))],
    )(*arrs)


def _add_halves(full, recv, place):
    n, _, rows, cols = full.shape
    br = _row_tile(rows, 512)

    def body(p_ref, a_ref, b_ref, o_ref):
        o_ref[...] = (a_ref[...] + b_ref[...]).astype(BF16)

    grid_spec = pltpu.PrefetchScalarGridSpec(
        num_scalar_prefetch=1, grid=(n, rows // br),
        in_specs=[pl.BlockSpec((None, None, br, cols), lambda s, i, p_ref: (s, p_ref[1], i, 0)),
                  pl.BlockSpec((None, br, cols), lambda s, i, p_ref: (s, i, 0))],
        out_specs=pl.BlockSpec((None, br, cols), lambda s, i, p_ref: (s, i, 0)))
    return pl.pallas_call(
        body, name="add_halves", grid_spec=grid_spec, out_shape=jax.ShapeDtypeStruct((n, rows, cols), BF16),
        compiler_params=_params("parallel", "parallel"),
    )(place, full, recv)


def _sum_chips(mine, others, place):
    _, rows, cols = mine.shape
    br = _row_tile(rows, 512)
    slot_of_flip = {2: 0, 1: 1, 3: 2}

    def body(p_ref, m_ref, o_ref, out_ref):
        me = p_ref[0]
        own = m_ref[...].astype(F32)
        got = [o_ref[j].astype(F32) for j in range(3)]
        acc = None
        for s in range(N_SHARDS):
            flip = jnp.bitwise_xor(me, s)
            term = own
            for f, j in slot_of_flip.items():
                term = jnp.where(flip == f, got[j], term)
            acc = term if acc is None else acc + term
        out_ref[...] = acc

    grid_spec = pltpu.PrefetchScalarGridSpec(
        num_scalar_prefetch=1, grid=(rows // br,),
        in_specs=[pl.BlockSpec((None, br, cols), lambda i, p_ref: (p_ref[0], i, 0)),
                  pl.BlockSpec((3, br, cols), lambda i, p_ref: (0, i, 0))],
        out_specs=pl.BlockSpec((None, br, cols), lambda i, p_ref: (p_ref[1], i, 0)))
    return pl.pallas_call(
        body, name="sum_chips", grid_spec=grid_spec, out_shape=jax.ShapeDtypeStruct((2, rows, cols), F32),
        compiler_params=_params("parallel"),
    )(place, mine, others)


WEIGHTS = ("norm_w", "ffn_w_gate", "ffn_w_up", "ffn_w_down", "ssd_w_in", "ssd_conv_w", "ssd_conv_b", "ssd_dt_bias",
           "ssd_a_log", "ssd_d", "ssd_norm_w", "ssd_w_out", "sc_w_in", "sc_conv_w", "sc_w_out", "final_norm_w")
BIG = (("ffn_w_gate", 3), ("ffn_w_up", 3), ("ffn_w_down", 2), ("ssd_w_in", 2), ("ssd_w_out", 1), ("sc_w_in", 2),
       ("sc_w_out", 1))
SMALL_SHARDED = (("norm_w", 2), ("ssd_conv_w", 2), ("sc_conv_w", 2))
REPLICATED = ("ssd_conv_b", "ssd_dt_bias", "ssd_a_log", "ssd_d", "ssd_norm_w", "final_norm_w")
FLAT_COLS = 1024


def _pack(arrays, row_multiple, lead=()):
    flat = jnp.concatenate([a.reshape(lead + (-1,)) for a in arrays], axis=len(lead))
    unit = row_multiple * FLAT_COLS
    n = flat.shape[-1]
    pad = (-n) % unit
    if pad:
        flat = jnp.pad(flat, [(0, 0)] * len(lead) + [(0, pad)])
    return flat.reshape(lead + (-1, FLAT_COLS))


def _unpack(flat, shapes, lead=()):
    flat = flat.reshape(lead + (-1,))
    out, off = [], 0
    for shp in shapes:
        n = 1
        for s in shp:
            n *= s
        out.append(flat[..., off:off + n].reshape(lead + tuple(shp)))
        off += n
    return out


def _to_shards(full, axis):
    shp = full.shape
    r = full.reshape(shp[:axis] + (N_SHARDS, shp[axis] // N_SHARDS) + shp[axis + 1:])
    return jnp.moveaxis(r, axis, 0)


def _from_shards(sh, axis):
    r = jnp.moveaxis(sh, 0, axis)
    shp = r.shape
    return r.reshape(shp[:axis] + (shp[axis] * shp[axis + 1],) + shp[axis + 2:])


def _layer_shards(wl, i):
    j = i // 2
    ffn = lambda k: [(("ffn_w_gate", k), wl["ffn_w_gate"][i, k], 1), (("ffn_w_up", k), wl["ffn_w_up"][i, k], 1),
                     (("ffn_w_down", k), wl["ffn_w_down"][i, k], 0)]
    mix = "ssd" if i % 2 == 0 else "sc"
    return ffn(0), [((mix + "_w_in",), wl[mix + "_w_in"][j], 1), ((mix + "_w_out",), wl[mix + "_w_out"][j], 0)] + ffn(1)


def _assemble(group, received, chip):
    return {key: _from_shards(lax.dynamic_update_index_in_dim(r, own, chip, 0), axis)
            for (key, own, axis), r in zip(group, received)}


def _forward_backward(x, target, p, wl, layer0, place):
    chip = place[0]
    consts = _ssd_consts()
    nw = p["norm_w"]
    row = lambda v: v[None]
    full = {0: layer0}
    ffn = lambda i, k: (full[i]["ffn_w_gate", k], full[i]["ffn_w_up", k], full[i]["ffn_w_down", k])
    ssd_prm, sc_cw = {}, {}

    xin, saved, pre = [], [], {}
    for i in range(N_LAYERS):
        j = i // 2
        first, second = _layer_shards(wl, i + 1) if i + 1 < N_LAYERS else ([], [])
        xin.append(x)
        x, *rest = _ffn_fwd(x, row(nw[i, 0]), *ffn(i, 0), carry=[s[1] for s in first])
        pre[i, 0] = rest[:3]
        if first:
            full[i + 1] = _assemble(first, rest[3:], chip)
        xin.append(x)
        if i % 2 == 0:
            ssd_prm[j] = _ssd_prep(full[i]["ssd_w_in",], p["ssd_conv_w"][j], p["ssd_conv_b"][j], p["ssd_dt_bias"][j],
                                   p["ssd_a_log"][j], p["ssd_d"][j], p["ssd_norm_w"][j])
            x, sv = _ssd_layer_fwd(x, row(nw[i, 1]), ssd_prm[j], full[i]["ssd_w_out",], consts)
        else:
            sc_cw[j] = jnp.pad(p["sc_conv_w"][j], ((0, SUBLANES - SC_CONV_W), (0, 0)))
            sv = _norm_mm(x, row(nw[i, 1]), full[i]["sc_w_in",])
            x = _sc_fwd(x, sv, sc_cw[j], full[i]["sc_w_out",])
        saved.append(sv)
        xin.append(x)
        x, *rest = _ffn_fwd(x, row(nw[i, 2]), *ffn(i, 1), carry=[s[1] for s in second])
        pre[i, 1] = rest[:3]
        if second:
            full[i + 1].update(_assemble(second, rest[3:], chip))
    loss, dx, dfw = _loss_head(x, row(p["final_norm_w"]), target)

    g_nw = [[None] * 3 for _ in range(N_LAYERS)]
    g_ffn = {}
    g_ssd = [None, None]
    g_sc = [None, None]
    halved = lambda a: a.reshape((N_SHARDS, 2, -1, a.shape[-1]))

    def ffn_bwd(i, k, slot, dy, carry):
        wg, wu, wd = ffn(i, k)
        a, s, p_ = pre[i, k]
        dg, du = _ffn_bwd_act(dy, s, p_, wd)
        dxn, dnw, h, *arrived = _inproj_bwd(xin[3 * i + slot], dy, row(nw[i, slot]), [wg, wu], [[dg], [du]], carry=carry)
        g_nw[i][slot] = dnw[0]
        for n, lhs, rhs, scale in (("ffn_w_gate", h, dg, 1.0), ("ffn_w_up", h, du, 1.0), ("ffn_w_down", a, dy, 0.5)):
            g_ffn[n, i] = _matmul_tn(lhs, rhs, scale=scale, name="wgrad_" + n, slab=(k,), stack=(2,),
                                     buf=g_ffn.get((n, i)))
        return dxn, arrived

    def chip_sums(parts):
        return [_add_halves(a, r, place) for a, r in zip(parts, _swap_halves(parts))]

    def finish(sums, arrived):
        return _join_halves([_sum_chips(mine, others, place) for mine, others in zip(sums, arrived)])

    reduced = {}
    waiting = None
    for i in reversed(range(N_LAYERS)):
        j = i // 2
        carry = waiting[1] if waiting else []
        dx, arrived = ffn_bwd(i, 1, 2, dx, carry[:2])
        xm = xin[3 * i + 1]
        if i % 2 == 0:
            dx, gs = _ssd_layer_bwd(xm, dx, row(nw[i, 1]), ssd_prm[j], full[i]["ssd_w_out",], consts, saved[i])
            g_nw[i][1] = gs[0][0]
            g_ssd[j] = gs[1:]
            mixer = [(("ssd_w_in", j), _to_shards(gs[1], 1)), (("ssd_w_out", j), _to_shards(gs[-1], 0))]
        else:
            bcu = saved[i]
            dbcu, pin, dcw = _sc_bwd(dx, bcu, sc_cw[j], full[i]["sc_w_out",])
            dwo = _matmul_tn(pin, dx, name="wgrad_sc_out")
            dx, dnw, h = _inproj_bwd(xm, dx, row(nw[i, 1]), [full[i]["sc_w_in",]], [[dbcu]])
            g_nw[i][1] = dnw[0]
            dwi = _matmul_tn(h, dbcu, name="wgrad_sc_in")
            g_sc[j] = dcw[:SC_CONV_W]
            mixer = [(("sc_w_in", j), _to_shards(dwi, 1)), (("sc_w_out", j), _to_shards(dwo, 0))]
        dx, more = ffn_bwd(i, 0, 0, dx, carry[2:])
        if waiting:
            reduced.update(zip(waiting[0], finish(waiting[1], arrived + more)))
        mine = [(("ffn_w_gate", i), _to_shards(g_ffn["ffn_w_gate", i], 2)), (("ffn_w_up", i), _to_shards(g_ffn["ffn_w_up", i], 2)),
                (("ffn_w_down", i), _to_shards(g_ffn["ffn_w_down", i], 1))] + mixer
        waiting = ([key for key, _ in mine], chip_sums([halved(v) for _, v in mine]))

    g = {"norm_w": jnp.stack([jnp.stack(r) for r in g_nw]), "final_norm_w": dfw[0]}
    for k, n in enumerate(("ssd_conv_w", "ssd_conv_b", "ssd_dt_bias", "ssd_a_log", "ssd_d", "ssd_norm_w")):
        g[n] = jnp.stack([g_ssd[0][k + 1], g_ssd[1][k + 1]])
    g["sc_conv_w"] = jnp.stack(g_sc)
    small_part = _pack([_to_shards(g[n], ax) for n, ax in SMALL_SHARDED]
                       + [jnp.broadcast_to(g[n][None], (N_SHARDS,) + g[n].shape) for n in REPLICATED],
                       4 * SUBLANES, lead=(N_SHARDS,))
    last = waiting[1] + chip_sums([halved(small_part)])
    out = finish(last, _scatter_to_chips(last))
    reduced.update(zip(waiting[0], out[:-1]))
    return loss, dx, reduced, out[-1]


def kernel(x, norm_w, ffn_w_gate, ffn_w_up, ffn_w_down, ssd_w_in, ssd_conv_w, ssd_conv_b, ssd_dt_bias, ssd_a_log, ssd_d, ssd_norm_w, ssd_w_out, sc_w_in, sc_conv_w, sc_w_out, final_norm_w, loss_target, m_norm_w, m_ffn_w_gate, m_ffn_w_up, m_ffn_w_down, m_ssd_w_in, m_ssd_conv_w, m_ssd_conv_b, m_ssd_dt_bias, m_ssd_a_log, m_ssd_d, m_ssd_norm_w, m_ssd_w_out, m_sc_w_in, m_sc_conv_w, m_sc_w_out, m_final_norm_w, v_norm_w, v_ffn_w_gate, v_ffn_w_up, v_ffn_w_down, v_ssd_w_in, v_ssd_conv_w, v_ssd_conv_b, v_ssd_dt_bias, v_ssd_a_log, v_ssd_d, v_ssd_norm_w, v_ssd_w_out, v_sc_w_in, v_sc_conv_w, v_sc_w_out, v_final_norm_w):
    w = dict(zip(WEIGHTS, (norm_w, ffn_w_gate, ffn_w_up, ffn_w_down, ssd_w_in, ssd_conv_w, ssd_conv_b, ssd_dt_bias,
                           ssd_a_log, ssd_d, ssd_norm_w, ssd_w_out, sc_w_in, sc_conv_w, sc_w_out, final_norm_w)))
    m = dict(zip(WEIGHTS, (m_norm_w, m_ffn_w_gate, m_ffn_w_up, m_ffn_w_down, m_ssd_w_in, m_ssd_conv_w, m_ssd_conv_b,
                           m_ssd_dt_bias, m_ssd_a_log, m_ssd_d, m_ssd_norm_w, m_ssd_w_out, m_sc_w_in, m_sc_conv_w,
                           m_sc_w_out, m_final_norm_w)))
    v = dict(zip(WEIGHTS, (v_norm_w, v_ffn_w_gate, v_ffn_w_up, v_ffn_w_down, v_ssd_w_in, v_ssd_conv_w, v_ssd_conv_b,
                           v_ssd_dt_bias, v_ssd_a_log, v_ssd_d, v_ssd_norm_w, v_ssd_w_out, v_sc_w_in, v_sc_conv_w,
                           v_sc_w_out, v_final_norm_w)))
    chip = 2 * lax.axis_index("x") + lax.axis_index("y")
    place = jnp.stack([chip, lax.axis_index("c")]).astype(jnp.int32)
    big_names = [n for n, _ in BIG]
    small_names = [n for n, _ in SMALL_SHARDED] + list(REPLICATED)
    halved = lambda a, lead=(): a.reshape(lead + (2, -1, a.shape[-1]))

    wl = {n: w[n].astype(BF16) for n in big_names}
    first, second = _layer_shards(wl, 0)
    small = halved(_pack([w[n] for n, _ in SMALL_SHARDED], 2 * SUBLANES))
    received = _all_gather_shards([halved(s[1]) for s in first + second] + [small])
    layer0 = _assemble(first + second, [r.reshape((N_SHARDS,) + s[1].shape) for r, s in zip(received, first + second)],
                       chip)
    p = {n: w[n] for n in REPLICATED}
    small_full = lax.dynamic_update_index_in_dim(received[-1], small, chip, 0)
    for (n, ax), sh in zip(SMALL_SHARDED, _unpack(small_full, [w[n].shape for n, _ in SMALL_SHARDED], lead=(N_SHARDS,))):
        p[n] = _from_shards(sh, ax)

    t, d = x.shape[-2:]
    loss, dx, reduced, g_small = _forward_backward(x.reshape(t, d), loss_target.reshape(t, d), p, wl, layer0, place)

    grad = {}
    for n in big_names:
        per_layer = w[n].shape[0]
        grad[n] = jnp.stack([reduced[n, i].reshape(w[n].shape[1:]) for i in range(per_layer)])
    g_small = g_small.reshape(-1, FLAT_COLS)
    grad.update(zip(small_names, _unpack(g_small, [w[n].shape for n in small_names])))

    delta, new_m, new_v = {}, {}, {}
    for n in big_names:
        shp = w[n].shape
        as2d = lambda a: a.reshape(-1, shp[-1])
        out = _adamw(as2d(w[n]), as2d(grad[n]), as2d(m[n]), as2d(v[n]), name="adamw_" + n)
        delta[n], new_m[n], new_v[n] = (o.reshape(shp) for o in out)
    packed = [_pack([s[n] for n in small_names], 4 * SUBLANES) for s in (w, m, v)]
    out = _adamw(packed[0], g_small, packed[1], packed[2], name="adamw_small")
    shapes = [w[n].shape for n in small_names]
    for dst, o in zip((delta, new_m, new_v), out):
        dst.update(zip(small_names, _unpack(o, shapes)))

    loss = lax.psum(loss[0, 0], ("x", "y", "c"))
    return (loss, dx.reshape(x.shape), *[grad[n] for n in WEIGHTS], *[delta[n] for n in WEIGHTS],
            *[new_m[n] for n in WEIGHTS], *[new_v[n] for n in WEIGHTS])
```

```python
import functools

import jax
import jax.numpy as jnp
from jax import lax
from jax.experimental import pallas as pl
from jax.experimental.pallas import tpu as pltpu

F32 = jnp.float32
BF16 = jnp.bfloat16
MESH = pl.DeviceIdType.MESH

RMS_EPS = 1e-5
D_MODEL = 1024
D_FF = 2816
N_LAYERS = 4
SSD_D_INNER = 2048
SSD_N_HEADS = 32
SSD_N_GROUPS = 4
SSD_D_STATE = 128
SSD_CHUNK = 128
SSD_CONV_W = 4
SSD_CONV_DIM = 3072
SSD_IN_DIM = 5152
SC_CONV_W = 3
LANES = 128
SUBLANES = 8
N_XS_BLK = SSD_D_INNER // LANES
SSD_IN_PAD = SSD_D_INNER + SSD_CONV_DIM + SSD_N_GROUPS * LANES
VMEM_LIMIT = 56 * 2**20
TOKEN_TILE = 512
WGRAD_TOKENS = 2048
FF_CHUNK = 256
N_SHARDS = 4

ADAM_LR = 0.001
ADAM_B1 = 0.9
ADAM_B2 = 0.999
ADAM_EPS = 1e-08
ADAM_WD = 0.01
ADAM_STEP = 10


_HBM = pl.BlockSpec(memory_space=pl.ANY)


def _params(*sem):
    return pltpu.CompilerParams(dimension_semantics=sem if sem else None, vmem_limit_bytes=VMEM_LIMIT)


def _dot(a, b):
    return jnp.dot(a, b, preferred_element_type=F32)


def _dot_nt(a, b):
    return lax.dot_general(a, b, (((1,), (1,)), ((), ())), preferred_element_type=F32)


def _dot_tn(a, b):
    return lax.dot_general(a, b, (((0,), (0,)), ((), ())), preferred_element_type=F32)


def _resident(shape):
    n = len(shape)
    return pl.BlockSpec(shape, lambda *_: (0,) * n, pipeline_mode=pl.Buffered(1))


def _split3(v):
    hi = v.astype(BF16)
    r1 = v - hi.astype(F32)
    mid = r1.astype(BF16)
    lo = (r1 - mid.astype(F32)).astype(BF16)
    return hi, mid, lo


def _sel_left(sel, v3):
    return _dot(sel, v3[0]) + _dot(sel, v3[1]) + _dot(sel, v3[2])


def _sigmoid(v):
    return 1.0 / (1.0 + jnp.exp(-v))


def _rms_fwd(x, w):
    inv = lax.rsqrt(jnp.mean(x * x, axis=-1, keepdims=True) + RMS_EPS)
    xhat = x * inv
    return xhat * w, xhat, inv


def _rms_bwd(dh, xhat, inv, w):
    dxhat = dh * w
    dx = inv * (dxhat - xhat * jnp.mean(dxhat * xhat, axis=-1, keepdims=True))
    return dx, jnp.sum(dh * xhat, axis=0, keepdims=True)


def _ffn_fwd(x, nw, wg, wu, wd, carry=()):
    t, d = x.shape
    f = wg.shape[1]
    tm = min(TOKEN_TILE, t)
    nsteps = t // tm
    ncar = len(carry)

    def body(x_ref, nw_ref, wg_ref, wu_ref, wd_ref, *rest):
        srcs = rest[:ncar]
        o_ref, a_ref, s_ref, p_ref = rest[ncar:ncar + 4]
        dsts = rest[ncar + 4:2 * ncar + 4]
        if ncar:
            send_sems, recv_sems = rest[2 * ncar + 4:]
            x_, y_, c_, chips = _place()
            me = 2 * x_ + y_

            @pl.when(pl.program_id(0) == 0)
            def _():
                for oi, (src, dst) in enumerate(zip(srcs, dsts)):
                    for j, chip in enumerate(chips):
                        _remote(src, dst.at[me], send_sems, recv_sems, 3 * oi + j, (*chip, c_)).start()

        xv = x_ref[...]
        h = _rms_fwd(xv, nw_ref[...])[0].astype(BF16)
        for j in range(f // FF_CHUNK):
            sl = slice(j * FF_CHUNK, (j + 1) * FF_CHUNK)
            g = _dot(h, wg_ref[:, sl])
            u = _dot(h, wu_ref[:, sl])
            sig = _sigmoid(g)
            s = g * sig
            a_ref[:, sl] = (s * u).astype(BF16)
            s_ref[:, sl] = s.astype(BF16)
            p_ref[:, sl] = (u * (sig + s * (1.0 - sig))).astype(BF16)
        o_ref[...] = xv + 0.5 * _dot(a_ref[...], wd_ref[...])

        if ncar:
            @pl.when(pl.program_id(0) == nsteps - 1)
            def _():
                for oi, (src, dst) in enumerate(zip(srcs, dsts)):
                    for j, chip in enumerate(chips):
                        landed = dst.at[2 * chip[0] + chip[1]]
                        _remote(landed, landed, send_sems, recv_sems, 3 * oi + j, (*chip, c_)).wait_recv()
                for oi, (src, dst) in enumerate(zip(srcs, dsts)):
                    for j, chip in enumerate(chips):
                        _remote(src, dst.at[me], send_sems, recv_sems, 3 * oi + j, (*chip, c_)).wait_send()

    tok = lambda n: pl.BlockSpec((tm, n), lambda i: (i, 0))
    sems = [pltpu.SemaphoreType.DMA((3 * ncar,)), pltpu.SemaphoreType.DMA((3 * ncar,))] if ncar else []
    return pl.pallas_call(
        body, name="ffn_fwd_carry" if ncar else "ffn_fwd", grid=(nsteps,),
        in_specs=[tok(d), _resident((1, d)), _resident((d, f)), _resident((d, f)), _resident((f, d))] + [_HBM] * ncar,
        out_specs=[tok(d), tok(f), tok(f), tok(f)] + [_HBM] * ncar,
        out_shape=[jax.ShapeDtypeStruct((t, d), F32)] + [jax.ShapeDtypeStruct((t, f), BF16)] * 3
        + [jax.ShapeDtypeStruct((N_SHARDS,) + c.shape, c.dtype) for c in carry],
        scratch_shapes=sems,
        compiler_params=_params("arbitrary" if ncar else "parallel"),
    )(x, nw, wg, wu, wd, *carry)


def _ffn_bwd_act(dy, s, p, wd, carry=()):
    t, d = dy.shape
    f = wd.shape[0]
    tm = min(TOKEN_TILE, t)
    nsteps = t // tm
    ncar = len(carry)

    def body(dy_ref, s_ref, p_ref, wd_ref, *rest):
        srcs, (dg_ref, du_ref) = rest[:ncar], rest[ncar:ncar + 2]
        dsts, sems = rest[ncar + 2:2 * ncar + 2], rest[2 * ncar + 2:]
        if ncar:
            @pl.when(pl.program_id(0) == 0)
            def _():
                for cp in _swap_copies(srcs, dsts, *sems):
                    cp.start()

        dob = (0.5 * dy_ref[...]).astype(BF16)
        for j in range(f // FF_CHUNK):
            sl = slice(j * FF_CHUNK, (j + 1) * FF_CHUNK)
            da = _dot_nt(dob, wd_ref[sl, :])
            dg_ref[:, sl] = (da * p_ref[:, sl].astype(F32)).astype(BF16)
            du_ref[:, sl] = (da * s_ref[:, sl].astype(F32)).astype(BF16)

        if ncar:
            @pl.when(pl.program_id(0) == nsteps - 1)
            def _():
                for cp in _swap_copies(srcs, dsts, *sems):
                    cp.wait()

    tok = lambda n: pl.BlockSpec((tm, n), lambda i: (i, 0))
    sems = [pltpu.SemaphoreType.DMA((ncar,)), pltpu.SemaphoreType.DMA((ncar,))] if ncar else []
    return pl.pallas_call(
        body, name="ffn_bwd_act_carry" if ncar else "ffn_bwd_act", grid=(nsteps,),
        in_specs=[tok(d), tok(f), tok(f), _resident((f, d))] + [_HBM] * ncar,
        out_specs=[tok(f), tok(f)] + [_HBM] * ncar,
        out_shape=[jax.ShapeDtypeStruct((t, f), BF16)] * 2
        + [jax.ShapeDtypeStruct((c.shape[0],) + c.shape[2:], c.dtype) for c in carry],
        scratch_shapes=sems,
        compiler_params=_params("arbitrary" if ncar else "parallel"),
    )(dy, s, p, wd, *carry)


def _pick_bn(m, n, unit):
    best = unit
    for k in range(1, n // unit + 1):
        bn = k * unit
        if n % bn == 0 and m * bn * 4 <= 8 * 2**20:
            best = bn
    return best


def _matmul_tn(a, b, scale=1.0, name="wgrad", slab=None, stack=None, buf=None):
    t, m = a.shape
    n = b.shape[1]
    bt = min(WGRAD_TOKENS, t)
    bn = _pick_bn(m, n, LANES)
    nt = t // bt
    lead = tuple(slab) if slab is not None else ()

    def body(a_ref, b_ref, *rest):
        o_ref = rest[-1]

        @pl.when(pl.program_id(1) == 0)
        def _():
            o_ref[...] = jnp.zeros_like(o_ref)

        o_ref[...] += _dot_tn(a_ref[...].astype(BF16), b_ref[...].astype(BF16))
        if scale != 1.0:
            @pl.when(pl.program_id(1) == nt - 1)
            def _():
                o_ref[...] *= scale

    in_specs = [pl.BlockSpec((bt, m), lambda j, k: (k, 0)), pl.BlockSpec((bt, bn), lambda j, k: (k, j))]
    args = [a, b]
    if buf is not None:
        in_specs.append(_HBM)
        args.append(buf)
    return pl.pallas_call(
        body, name=name, grid=(n // bn, nt),
        in_specs=in_specs,
        out_specs=pl.BlockSpec((None,) * len(lead) + (m, bn), lambda j, k: lead + (0, j)),
        out_shape=jax.ShapeDtypeStruct(tuple(stack or ()) + (m, n), F32),
        input_output_aliases={2: 0} if buf is not None else {},
        compiler_params=_params("parallel", "arbitrary"),
    )(*args)


def _matmul_tn_blocked(a, b, name="wgrad_blk"):
    t, m = a.shape
    nb = b.shape[0]
    bt = min(1024, t)
    nbt = _pick_bn(m, nb * LANES, LANES) // LANES
    while nb % nbt:
        nbt -= 1

    def body(a_ref, b_ref, o_ref):
        @pl.when(pl.program_id(1) == 0)
        def _():
            o_ref[...] = jnp.zeros_like(o_ref)

        bv = jnp.concatenate([b_ref[i] for i in range(nbt)], axis=1) if nbt > 1 else b_ref[0]
        o_ref[...] += _dot_tn(a_ref[...], bv)

    return pl.pallas_call(
        body, name=name, grid=(nb // nbt, t // bt),
        in_specs=[pl.BlockSpec((bt, m), lambda j, k: (k, 0)), pl.BlockSpec((nbt, bt, LANES), lambda j, k: (j, k, 0))],
        out_specs=pl.BlockSpec((m, nbt * LANES), lambda j, k: (0, j)),
        out_shape=jax.ShapeDtypeStruct((m, nb * LANES), F32),
        compiler_params=_params("parallel", "arbitrary"),
    )(a, b)


def _norm_mm(x, nw, w):
    t, d = x.shape
    n = w.shape[1]
    tm = min(TOKEN_TILE, t)
    cn = 1024 if n % 1024 == 0 else n

    def body(x_ref, nw_ref, w_ref, o_ref):
        h = _rms_fwd(x_ref[...], nw_ref[...])[0].astype(BF16)
        for j in range(n // cn):
            sl = slice(j * cn, (j + 1) * cn)
            o_ref[:, sl] = _dot(h, w_ref[:, sl])

    return pl.pallas_call(
        body, name="norm_mm", grid=(t // tm,),
        in_specs=[pl.BlockSpec((tm, d), lambda i: (i, 0)), _resident((1, d)), _resident((d, n))],
        out_specs=pl.BlockSpec((tm, n), lambda i: (i, 0)),
        out_shape=jax.ShapeDtypeStruct((t, n), F32),
        compiler_params=_params("parallel"),
    )(x, nw, w)


def _ssd_inproj(x, nw, w):
    t, d = x.shape
    tm = min(TOKEN_TILE, t)
    nz, nx, ng = SSD_D_INNER // LANES, SSD_CONV_DIM // LANES, SSD_N_GROUPS
    cn = 1024

    def body(x_ref, nw_ref, w_ref, z_ref, xr_ref, dt_ref):
        h = _rms_fwd(x_ref[...], nw_ref[...])[0].astype(BF16)
        for j in range(-(-SSD_IN_PAD // cn)):
            lo, hi = j * cn, min((j + 1) * cn, SSD_IN_PAD)
            r = _dot(h, w_ref[:, lo:hi])
            for i in range((hi - lo) // LANES):
                blk = j * (cn // LANES) + i
                v = r[:, i * LANES:(i + 1) * LANES]
                if blk < nz:
                    z_ref[blk] = v
                elif blk < nz + nx:
                    xr_ref[blk - nz] = v
                else:
                    dt_ref[blk - nz - nx] = v

    out = lambda n: pl.BlockSpec((n, tm, LANES), lambda i: (0, i, 0))
    return pl.pallas_call(
        body, name="ssd_inproj", grid=(t // tm,),
        in_specs=[pl.BlockSpec((tm, d), lambda i: (i, 0)), _resident((1, d)), _resident((d, SSD_IN_PAD))],
        out_specs=[out(nz), out(nx), out(ng)],
        out_shape=[jax.ShapeDtypeStruct((n, t, LANES), F32) for n in (nz, nx, ng)],
        compiler_params=_params("parallel"),
    )(x, nw, w)


def _inproj_bwd(x, dy, nw, ws, pieces, carry=()):
    t, d = x.shape
    tm = min(TOKEN_TILE, t)
    nsteps = t // tm
    nws = len(ws)
    ncar = len(carry)
    flat = [p for group in pieces for p in group]

    def body(*refs):
        x_ref, dy_ref, nw_ref = refs[:3]
        w_refs = refs[3:3 + nws]
        p_refs = list(refs[3 + nws:3 + nws + len(flat)])
        rest = refs[3 + nws + len(flat):]
        srcs, (dx_ref, dnw_ref, h_ref) = rest[:ncar], rest[ncar:ncar + 3]
        dsts, sems = rest[ncar + 3:2 * ncar + 3], rest[2 * ncar + 3:]
        if ncar:
            @pl.when(pl.program_id(0) == 0)
            def _():
                _scatter_start(srcs, dsts, *sems)

        nwv = nw_ref[...]
        hf, xhat, inv = _rms_fwd(x_ref[...], nwv)
        h_ref[...] = hf.astype(BF16)
        dh = None
        for w_ref, group in zip(w_refs, pieces):
            parts = []
            for _ in group:
                p = p_refs.pop(0)
                parts += [p[i] for i in range(p.shape[0])] if len(p.shape) == 3 else [p[...]]
            dz = jnp.concatenate(parts, axis=1) if len(parts) > 1 else parts[0]
            part = _dot_nt(dz, w_ref[...])
            dh = part if dh is None else dh + part
        dx, dw = _rms_bwd(dh, xhat, inv, nwv)
        dx_ref[...] = dy_ref[...] + dx

        @pl.when(pl.program_id(0) == 0)
        def _():
            dnw_ref[...] = jnp.zeros_like(dnw_ref)

        dnw_ref[...] += dw

        if ncar:
            @pl.when(pl.program_id(0) == nsteps - 1)
            def _():
                _scatter_wait(srcs, dsts, *sems)

    tok = lambda m: pl.BlockSpec((tm, m), lambda i: (i, 0))
    p_specs = [pl.BlockSpec((p.shape[0], tm, LANES), lambda i: (0, i, 0)) if p.ndim == 3 else tok(p.shape[1])
               for p in flat]
    sems = [pltpu.SemaphoreType.DMA((3 * ncar,)), pltpu.SemaphoreType.DMA((3 * ncar,))] if ncar else []
    return pl.pallas_call(
        body, name="inproj_bwd_carry" if ncar else "inproj_bwd", grid=(nsteps,),
        in_specs=[tok(d), tok(d), _resident((1, d))] + [_resident(w.shape) for w in ws] + p_specs + [_HBM] * ncar,
        out_specs=[tok(d), pl.BlockSpec((1, d), lambda i: (0, 0)), tok(d)] + [_HBM] * ncar,
        out_shape=[jax.ShapeDtypeStruct((t, d), F32), jax.ShapeDtypeStruct((1, d), F32),
                   jax.ShapeDtypeStruct((t, d), BF16)]
        + [jax.ShapeDtypeStruct((3,) + c.shape[1:], c.dtype) for c in carry],
        scratch_shapes=sems,
        compiler_params=_params("arbitrary"),
    )(x, dy, nw, *ws, *flat, *carry)


def _shift_down(v, j, prev8):
    if j == 0:
        return v
    r = pltpu.roll(v, j, 0)
    p = pltpu.roll(prev8, j, 0)
    rows = lax.broadcasted_iota(jnp.int32, prev8.shape, 0)
    first = jnp.where(rows < j, p, r[0:SUBLANES])
    return jnp.concatenate([first, r[SUBLANES:]], axis=0)


def _shift_up(v, j, next8):
    if j == 0:
        return v
    n = v.shape[0]
    r = pltpu.roll(v, n - j, 0)
    p = pltpu.roll(next8, SUBLANES - j, 0)
    rows = lax.broadcasted_iota(jnp.int32, next8.shape, 0)
    last = jnp.where(rows >= SUBLANES - j, p, r[n - SUBLANES:])
    return jnp.concatenate([r[:n - SUBLANES], last], axis=0)


def _sc_fwd(x, bcu, cw, wo):
    t, d = x.shape
    tm = min(TOKEN_TILE, t)
    hb = tm // SUBLANES

    def body(x_ref, bcu_ref, prev_ref, cw_ref, wo_ref, o_ref):
        bg, cg, u = bcu_ref[:, 0:d], bcu_ref[:, d:2 * d], bcu_ref[:, 2 * d:3 * d]
        q = cg * u
        qp = jnp.where(pl.program_id(0) == 0, 0.0, prev_ref[:, d:2 * d] * prev_ref[:, 2 * d:3 * d])
        cwv = cw_ref[...]
        v = cwv[2:3] * q + cwv[1:2] * _shift_down(q, 1, qp) + cwv[0:1] * _shift_down(q, 2, qp)
        o_ref[...] = x_ref[...] + _dot((bg * v).astype(BF16), wo_ref[...])

    return pl.pallas_call(
        body, name="sc_fwd", grid=(t // tm,),
        in_specs=[pl.BlockSpec((tm, d), lambda i: (i, 0)), pl.BlockSpec((tm, 3 * d), lambda i: (i, 0)),
                  pl.BlockSpec((SUBLANES, 3 * d), lambda i: (jnp.maximum(i * hb - 1, 0), 0)),
                  _resident((SUBLANES, d)), _resident((d, d))],
        out_specs=pl.BlockSpec((tm, d), lambda i: (i, 0)),
        out_shape=jax.ShapeDtypeStruct((t, d), F32),
        compiler_params=_params("parallel"),
    )(x, bcu, bcu, cw, wo)


def _sc_bwd(dy, bcu, cw, wo):
    t, d = dy.shape
    tm = min(TOKEN_TILE, t)
    hb = tm // SUBLANES
    nt = t // tm

    def body(dy_ref, dyn_ref, bcu_ref, prev_ref, next_ref, cw_ref, wo_ref, dbcu_ref, p_ref, dcw_ref):
        i = pl.program_id(0)
        bg, cg, u = bcu_ref[:, 0:d], bcu_ref[:, d:2 * d], bcu_ref[:, 2 * d:3 * d]
        q = cg * u
        qp = jnp.where(i == 0, 0.0, prev_ref[:, d:2 * d] * prev_ref[:, 2 * d:3 * d])
        cwv = cw_ref[...]
        q1 = _shift_down(q, 1, qp)
        q2 = _shift_down(q, 2, qp)
        v = cwv[2:3] * q + cwv[1:2] * q1 + cwv[0:1] * q2
        p_ref[...] = (bg * v).astype(BF16)
        wov = wo_ref[...]
        dp = _dot_nt(dy_ref[...].astype(BF16), wov)
        dpn = _dot_nt(dyn_ref[...].astype(BF16), wov)
        dv = dp * bg
        dvn = jnp.where(i == nt - 1, 0.0, dpn * next_ref[:, 0:d])
        dq = cwv[2:3] * dv + cwv[1:2] * _shift_up(dv, 1, dvn) + cwv[0:1] * _shift_up(dv, 2, dvn)
        dbcu_ref[:, 0:d] = (dp * v).astype(BF16)
        dbcu_ref[:, d:2 * d] = (dq * u).astype(BF16)
        dbcu_ref[:, 2 * d:3 * d] = (dq * cg).astype(BF16)

        @pl.when(i == 0)
        def _():
            dcw_ref[...] = jnp.zeros_like(dcw_ref)

        dcw_ref[0:1, :] += jnp.sum(dv * q2, axis=0, keepdims=True)
        dcw_ref[1:2, :] += jnp.sum(dv * q1, axis=0, keepdims=True)
        dcw_ref[2:3, :] += jnp.sum(dv * q, axis=0, keepdims=True)

    last8 = t // SUBLANES - 1
    return pl.pallas_call(
        body, name="sc_bwd", grid=(nt,),
        in_specs=[pl.BlockSpec((tm, d), lambda i: (i, 0)),
                  pl.BlockSpec((SUBLANES, d), lambda i: (jnp.minimum((i + 1) * hb, last8), 0)),
                  pl.BlockSpec((tm, 3 * d), lambda i: (i, 0)),
                  pl.BlockSpec((SUBLANES, 3 * d), lambda i: (jnp.maximum(i * hb - 1, 0), 0)),
                  pl.BlockSpec((SUBLANES, 3 * d), lambda i: (jnp.minimum((i + 1) * hb, last8), 0)),
                  _resident((SUBLANES, d)), _resident((d, d))],
        out_specs=[pl.BlockSpec((tm, 3 * d), lambda i: (i, 0)), pl.BlockSpec((tm, d), lambda i: (i, 0)),
                   pl.BlockSpec((SUBLANES, d), lambda i: (0, 0))],
        out_shape=[jax.ShapeDtypeStruct((t, 3 * d), BF16), jax.ShapeDtypeStruct((t, d), BF16),
                   jax.ShapeDtypeStruct((SUBLANES, d), F32)],
        compiler_params=_params("arbitrary"),
    )(dy, dy, bcu, bcu, bcu, cw, wo)


NEG_BIG = -1e30


HEADS_PER_GROUP = SSD_N_HEADS // SSD_N_GROUPS
PAIRS_PER_GROUP = HEADS_PER_GROUP // 2
GROUPS_PER_STEP = 4


def _ssd_consts():
    r = lax.broadcasted_iota(jnp.int32, (LANES, LANES), 0)
    c = lax.broadcasted_iota(jnp.int32, (LANES, LANES), 1)
    return (c <= r).astype(BF16), (c >= r).astype(BF16)


def _ssd_decay(dtr, dtb, alog, tril):
    shape = (SSD_CHUNK, LANES)
    lanes = lax.broadcasted_iota(jnp.int32, shape, 1)
    rows = lax.broadcasted_iota(jnp.int32, shape, 0)
    pre = dtr + dtb
    valid = lanes < HEADS_PER_GROUP
    dt = jnp.where(valid, jnp.maximum(pre, 0.0) + jnp.log(1.0 + jnp.exp(-jnp.abs(pre))), 0.0)
    a = -jnp.exp(alog)
    acs = _sel_left(tril, _split3(dt * a))
    return dt, a, acs, pre, valid, rows, lanes


def _lane_col(v, j):
    return jnp.broadcast_to(v[:, j:j + 1], v.shape)


def _ssd_pair_terms(k, dt, cols, low_half, xs):
    dtp = jnp.where(low_half, _lane_col(dt, 2 * k), _lane_col(dt, 2 * k + 1))
    acsp = jnp.where(low_half, cols[2 * k], cols[2 * k + 1])
    lastp = acsp[SSD_CHUNK - 1:SSD_CHUNK, :]
    eap = jnp.exp(acsp)
    decp = jnp.exp(lastp - acsp)
    etp = jnp.exp(lastp)
    xdt = xs * dtp
    return dtp, eap, decp, etp, xdt


def _ssd_conv_taps(cwb, xr, prev8):
    sh = [_shift_down(xr, j, prev8) for j in range(SSD_CONV_W)]
    xc = cwb[4:5]
    for j in range(SSD_CONV_W):
        xc = xc + cwb[3 - j:4 - j] * sh[j]
    return xc, sh


def _ssd_specs(nc, rev):
    ch = (lambda i: nc - 1 - i) if rev else (lambda i: i)
    L = SSD_CHUNK
    gps = GROUPS_PER_STEP
    b0, c0 = N_XS_BLK // gps, (N_XS_BLK + SSD_N_GROUPS) // gps
    xs = pl.BlockSpec((4 * gps, L, LANES), lambda g, i: (g, ch(i), 0))
    bb = pl.BlockSpec((gps, L, LANES), lambda g, i: (b0 + g, ch(i), 0))
    cc = pl.BlockSpec((gps, L, LANES), lambda g, i: (c0 + g, ch(i), 0))
    dt = pl.BlockSpec((gps, L, LANES), lambda g, i: (g, ch(i), 0))
    cw_xs = pl.BlockSpec((4 * gps, SUBLANES, LANES), lambda g, i: (g, 0, 0))
    cw_b = pl.BlockSpec((gps, SUBLANES, LANES), lambda g, i: (b0 + g, 0, 0))
    cw_c = pl.BlockSpec((gps, SUBLANES, LANES), lambda g, i: (c0 + g, 0, 0))
    st = pl.BlockSpec((1, 4 * gps, LANES, LANES), lambda g, i: (ch(i), g, 0, 0))
    grp4 = pl.BlockSpec((4 * gps, L, LANES), lambda g, i: (g, ch(i), 0))
    return xs, bb, cc, dt, cw_xs, cw_b, cw_c, st, grp4


def _group_views(q, refs4, refs1, refs6=()):
    return ([r.at[pl.ds(4 * q, 4)] for r in refs4], [r.at[pl.ds(q, 1)] for r in refs1],
            [r.at[pl.ds(6 * q, 6)] for r in refs6])


def _ssd_scan_fwd(xr, dtr, cwb, dtb, alog, dskip, consts):
    t = xr.shape[1]
    L = SSD_CHUNK
    nc = t // L
    tril, _ = consts
    xs_s, b_s, c_s, dt_s, cwx_s, cwb_s, cwc_s, st_s, grp4 = _ssd_specs(nc, False)
    gps = GROUPS_PER_STEP
    grp_row = pl.BlockSpec((gps, 1, LANES), lambda g, i: (g, 0, 0))

    def body(xs_all, b_all, c_all, dtr_all, cwx_all, cwbb_all, cwc_all, dtb_all, alog_all, dsk_all,
             tril_ref, y_all, sp_all, state_all, tail_all):
        @pl.when(pl.program_id(1) == 0)
        def _():
            state_all[...] = jnp.zeros_like(state_all)
            tail_all[...] = jnp.zeros_like(tail_all)

        for q in range(gps):
            fours, ones, sixes = _group_views(q, (xs_all, cwx_all, dsk_all, y_all, sp_all.at[0], state_all),
                                              (b_all, c_all, dtr_all, cwbb_all, cwc_all, dtb_all, alog_all), (tail_all,))
            group(*fours, *ones, *sixes, tril_ref)

    def group(xs_ref, cwx_ref, dsk_ref, y_ref, sp_ref, state, b_ref, c_ref, dtr_ref, cwbb_ref, cwc_ref,
              dtb_ref, alog_ref, tail, tril_ref):
        xa = []
        for b in range(6):
            xrb = xs_ref[b] if b < 4 else (b_ref[0] if b == 4 else c_ref[0])
            cw = cwx_ref[b] if b < 4 else (cwbb_ref[0] if b == 4 else cwc_ref[0])
            xc, _ = _ssd_conv_taps(cw, xrb, tail[b])
            tail[b] = xrb[L - SUBLANES:]
            xa.append(xc * _sigmoid(xc))

        dt, a, acs, _, _, rows, lanes = _ssd_decay(dtr_ref[0], dtb_ref[0], alog_ref[0], tril_ref[...])
        acst = acs.T
        bb = xa[4].astype(BF16)
        cb_ = xa[5].astype(BF16)
        cbm = _dot_nt(cb_, bb)
        causal = rows >= lanes
        low_half = lanes < LANES // 2
        cols = [_lane_col(acs, j) for j in range(HEADS_PER_GROUP)]

        for k in range(PAIRS_PER_GROUP):
            xs = xa[k]
            dtp, eap, decp, etp, xdt = _ssd_pair_terms(k, dt, cols, low_half, xs)
            ms = []
            for j in (2 * k, 2 * k + 1):
                diff = cols[j] - jnp.broadcast_to(acst[j:j + 1, :], (L, L))
                ms.append((cbm * jnp.exp(jnp.where(causal, diff, NEG_BIG))).astype(BF16))
            xcat = jnp.concatenate([jnp.where(low_half, xdt, 0.0).astype(BF16),
                                    jnp.where(low_half, 0.0, xdt).astype(BF16)], axis=0)
            yd = _dot(jnp.concatenate(ms, axis=1), xcat)
            sp = state[k]
            yo = eap * _dot(cb_, sp.astype(BF16))
            y_ref[k] = yd + yo + dsk_ref[k][0:1] * xs
            sp_ref[k] = sp
            state[k] = etp * sp + _dot_tn(bb, (decp * xdt).astype(BF16))

    return pl.pallas_call(
        body, name="ssd_scan_fwd", grid=(SSD_N_GROUPS // gps, nc),
        in_specs=[xs_s, b_s, c_s, dt_s, cwx_s, cwb_s, cwc_s, grp_row, grp_row, cwx_s, _resident(tril.shape)],
        out_specs=[grp4, st_s],
        out_shape=[jax.ShapeDtypeStruct((N_XS_BLK, t, LANES), F32),
                   jax.ShapeDtypeStruct((nc, N_XS_BLK, LANES, LANES), F32)],
        scratch_shapes=[pltpu.VMEM((4 * gps, LANES, LANES), F32), pltpu.VMEM((6 * gps, SUBLANES, LANES), F32)],
        compiler_params=_params("arbitrary", "arbitrary"),
    )(xr, xr, xr, dtr, cwb, cwb, cwb, dtb, alog, dskip, tril)


def _ssd_scan_bwd(xr, dtr, dy, sprev, cwb, dtb, alog, dskip, consts):
    t = xr.shape[1]
    L = SSD_CHUNK
    nc = t // L
    hb = L // SUBLANES
    tril, triu = consts
    xs_s, b_s, c_s, dt_s, cwx_s, cwb_s, cwc_s, st_s, grp4 = _ssd_specs(nc, True)
    gps = GROUPS_PER_STEP
    grp_row = pl.BlockSpec((gps, 1, LANES), lambda g, i: (g, 0, 0))
    prev = lambda off: pl.BlockSpec(
        (4 * gps if off is None else gps, SUBLANES, LANES),
        (lambda g, i: (g, jnp.maximum((nc - 1 - i) * hb - 1, 0), 0)) if off is None else
        (lambda g, i: (off // gps + g, jnp.maximum((nc - 1 - i) * hb - 1, 0), 0)))
    grp1 = pl.BlockSpec((gps, L, LANES), lambda g, i: (g, nc - 1 - i, 0))
    acc4 = pl.BlockSpec((4 * gps, SUBLANES, LANES), lambda g, i: (g, 0, 0))
    acc1 = pl.BlockSpec((gps, SUBLANES, LANES), lambda g, i: (g, 0, 0))

    def body(xs_all, b_all, c_all, pxs_all, pb_all, pc_all, dtr_all, dy_all, sp_all,
             cwx_all, cwbb_all, cwc_all, dtb_all, alog_all, dsk_all, tril_ref, triu_ref,
             dxs_all, db_all, dc_all, ddtr_all, dcwx_all, dcwb_all, dcwc_all, dd_all, dsm_all,
             dstate_all, head_all):
        @pl.when(pl.program_id(1) == 0)
        def _():
            for r in (dstate_all, head_all, dcwx_all, dcwb_all, dcwc_all, dd_all, dsm_all):
                r[...] = jnp.zeros_like(r)

        for q in range(gps):
            fours, ones, sixes = _group_views(
                q, (xs_all, pxs_all, dy_all, sp_all.at[0], cwx_all, dsk_all, dxs_all, dcwx_all, dd_all, dstate_all),
                (b_all, c_all, pb_all, pc_all, dtr_all, cwbb_all, cwc_all, dtb_all, alog_all, db_all, dc_all, ddtr_all,
                 dcwb_all, dcwc_all, dsm_all), (head_all,))
            group(*fours, *ones, *sixes, tril_ref, triu_ref)

    def group(xs_ref, pxs_ref, dy_ref, sp_ref, cwx_ref, dsk_ref, dxs_ref, dcwx_ref, dd_ref, dstate,
              b_ref, c_ref, pb_ref, pc_ref, dtr_ref, cwbb_ref, cwc_ref, dtb_ref, alog_ref, db_ref, dc_ref, ddtr_ref,
              dcwb_ref, dcwc_ref, dsm_ref, head, tril_ref, triu_ref):
        first_chunk = pl.program_id(1) == nc - 1

        def blk(b):
            xrb = xs_ref[b] if b < 4 else (b_ref[0] if b == 4 else c_ref[0])
            cw = cwx_ref[b] if b < 4 else (cwbb_ref[0] if b == 4 else cwc_ref[0])
            p8 = pxs_ref[b] if b < 4 else (pb_ref[0] if b == 4 else pc_ref[0])
            return xrb, cw, jnp.where(first_chunk, 0.0, p8)

        xa, dsil = [], []
        for b in range(6):
            xrb, cw, p8 = blk(b)
            xc, _ = _ssd_conv_taps(cw, xrb, p8)
            sig = _sigmoid(xc)
            xa.append(xc * sig)
            dsil.append(sig * (1.0 + xc * (1.0 - sig)))

        dt, a, acs, pre, valid, rows, lanes = _ssd_decay(dtr_ref[0], dtb_ref[0], alog_ref[0], tril_ref[...])
        acst = acs.T
        bb = xa[4].astype(BF16)
        cb_ = xa[5].astype(BF16)
        cbm = _dot_nt(cb_, bb)
        cbmt = _dot_nt(bb, cb_)
        causal = rows >= lanes
        anti = rows <= lanes
        low_half = lanes < LANES // 2
        last_row = rows == L - 1
        cols = [_lane_col(acs, j) for j in range(HEADS_PER_GROUP)]
        zeros = jnp.zeros((L, LANES), F32)
        dcb, dcbt, dbg, dcg, dacs, dacst, ddt = zeros, zeros, zeros, zeros, zeros, zeros, zeros
        dxa = []

        for k in range(PAIRS_PER_GROUP):
            xs = xa[k]
            dtp, eap, decp, etp, xdt = _ssd_pair_terms(k, dt, cols, low_half, xs)
            xdtb = xdt.astype(BF16)
            w = decp * xdt
            wb = w.astype(BF16)
            dyv = dy_ref[k]
            sp = sp_ref[k]
            spb = sp.astype(BF16)
            dsn = dstate[k]
            dsnb = dsn.astype(BF16)
            yoff = eap * _dot(cb_, spb)
            dgb = (eap * dyv).astype(BF16)
            dcg = dcg + _dot_nt(dgb, spb)
            dstate[k] = _dot_tn(cb_, dgb) + etp * dsn
            last_lane = etp * jnp.sum(dsn * sp, axis=0, keepdims=True)
            dbg = dbg + _dot_nt(wb, dsnb)
            dw = _dot(bb, dsnb)
            t2 = dw * w
            dxdt = decp * dw
            last_lane = last_lane + jnp.sum(t2, axis=0, keepdims=True)
            lane_acc = dyv * yoff - t2 + jnp.where(last_row, last_lane, 0.0)
            for j in (2 * k, 2 * k + 1):
                diff = cols[j] - jnp.broadcast_to(acst[j:j + 1, :], (L, L))
                lm = jnp.exp(jnp.where(causal, diff, NEG_BIG))
                lmt = jnp.exp(jnp.where(anti, -diff, NEG_BIG))
                dye = jnp.where(low_half == (j % 2 == 0), dyv, 0.0).astype(BF16)
                dm = _dot_nt(dye, xdtb)
                dmt = _dot_nt(xdtb, dye)
                mt = cbmt * lmt
                seg = dmt * mt - dm * (cbm * lm)
                dacst = dacst + jnp.where(rows == j, jnp.sum(seg, axis=0, keepdims=True), 0.0)
                dcb = dcb + dm * lm
                dcbt = dcbt + dmt * lmt
                dxdt = dxdt + _dot(mt.astype(BF16), dye)
            ddt_lane = dxdt * xs
            for j, keep in ((2 * k, low_half), (2 * k + 1, jnp.logical_not(low_half))):
                dacs = dacs + jnp.where(lanes == j, jnp.sum(jnp.where(keep, lane_acc, 0.0), axis=1, keepdims=True), 0.0)
                ddt = ddt + jnp.where(lanes == j, jnp.sum(jnp.where(keep, ddt_lane, 0.0), axis=1, keepdims=True), 0.0)
            dxa.append(dsk_ref[k][0:1] * dyv + dxdt * dtp)
            dd_ref[k, 0:1, :] += jnp.sum(dyv * xs, axis=0, keepdims=True)

        dxa.append(dbg + _dot(dcbt.astype(BF16), cb_))
        dxa.append(dcg + _dot(dcb.astype(BF16), bb))
        dac = _sel_left(triu_ref[...], _split3(dacs + dacst.T))
        ddtr = jnp.where(valid, (ddt + dac * a) * _sigmoid(pre), 0.0)
        ddtr_ref[0] = ddtr.astype(BF16)
        dsm_ref[0, 0:1, :] += jnp.sum(ddtr, axis=0, keepdims=True)
        dsm_ref[0, 1:2, :] += jnp.sum(dac * dt, axis=0, keepdims=True) * a

        for b in range(6):
            xrb, cw, p8 = blk(b)
            sh = [_shift_down(xrb, j, p8) for j in range(SSD_CONV_W)]
            dxc = dxa[b] * dsil[b]
            acc = dcwx_ref.at[b] if b < 4 else (dcwb_ref.at[0] if b == 4 else dcwc_ref.at[0])
            acc[4:5, :] += jnp.sum(dxc, axis=0, keepdims=True)
            dxr = jnp.zeros_like(dxc)
            for j in range(SSD_CONV_W):
                acc[3 - j:4 - j, :] += jnp.sum(dxc * sh[j], axis=0, keepdims=True)
                dxr = dxr + cw[3 - j:4 - j] * _shift_up(dxc, j, head[b])
            head[b] = dxc[0:SUBLANES]
            out = dxs_ref.at[b] if b < 4 else (db_ref.at[0] if b == 4 else dc_ref.at[0])
            out[...] = dxr.astype(BF16)

    return pl.pallas_call(
        body, name="ssd_scan_bwd", grid=(SSD_N_GROUPS // gps, nc),
        in_specs=[xs_s, b_s, c_s, prev(None), prev(N_XS_BLK), prev(N_XS_BLK + SSD_N_GROUPS), dt_s, grp4, st_s,
                  cwx_s, cwb_s, cwc_s, grp_row, grp_row, cwx_s, _resident(tril.shape), _resident(triu.shape)],
        out_specs=[grp4, grp1, grp1, grp1, acc4, acc1, acc1, acc4, acc1],
        out_shape=[jax.ShapeDtypeStruct((N_XS_BLK, t, LANES), BF16),
                   jax.ShapeDtypeStruct((SSD_N_GROUPS, t, LANES), BF16),
                   jax.ShapeDtypeStruct((SSD_N_GROUPS, t, LANES), BF16),
                   jax.ShapeDtypeStruct((SSD_N_GROUPS, t, LANES), BF16),
                   jax.ShapeDtypeStruct((N_XS_BLK, SUBLANES, LANES), F32),
                   jax.ShapeDtypeStruct((SSD_N_GROUPS, SUBLANES, LANES), F32),
                   jax.ShapeDtypeStruct((SSD_N_GROUPS, SUBLANES, LANES), F32),
                   jax.ShapeDtypeStruct((N_XS_BLK, SUBLANES, LANES), F32),
                   jax.ShapeDtypeStruct((SSD_N_GROUPS, SUBLANES, LANES), F32)],
        scratch_shapes=[pltpu.VMEM((4 * gps, LANES, LANES), F32), pltpu.VMEM((6 * gps, SUBLANES, LANES), F32)],
        compiler_params=_params("arbitrary", "arbitrary"),
    )(xr, xr, xr, xr, xr, xr, dtr, dy, sprev, cwb, cwb, cwb, dtb, alog, dskip, tril, triu)


def _ssd_gate_fwd(x, y, z, gnw, wo):
    t, d = x.shape
    tm = min(TOKEN_TILE, t)
    nb = N_XS_BLK
    per = nb // SSD_N_GROUPS

    def body(x_ref, y_ref, z_ref, gnw_ref, wo_ref, o_ref, gn_ref):
        gs = []
        for j in range(nb):
            zv = z_ref[j]
            gs.append(y_ref[j] * (zv * _sigmoid(zv)))
        for q in range(SSD_N_GROUPS):
            ss = sum(jnp.sum(gs[j] * gs[j], axis=1, keepdims=True) for j in range(q * per, (q + 1) * per))
            inv = lax.rsqrt(ss / (per * LANES) + RMS_EPS)
            for j in range(q * per, (q + 1) * per):
                gn_ref[:, j * LANES:(j + 1) * LANES] = ((gs[j] * inv) * gnw_ref[j]).astype(BF16)
        o_ref[...] = x_ref[...] + _dot(gn_ref[...], wo_ref[...])

    blk = pl.BlockSpec((nb, tm, LANES), lambda i: (0, i, 0))
    return pl.pallas_call(
        body, name="ssd_gate_fwd", grid=(t // tm,),
        in_specs=[pl.BlockSpec((tm, d), lambda i: (i, 0)), blk, blk, _resident((nb, 1, LANES)),
                  _resident((SSD_D_INNER, d))],
        out_specs=[pl.BlockSpec((tm, d), lambda i: (i, 0)), pl.BlockSpec((tm, SSD_D_INNER), lambda i: (i, 0))],
        out_shape=[jax.ShapeDtypeStruct((t, d), F32), jax.ShapeDtypeStruct((t, SSD_D_INNER), BF16)],
        compiler_params=_params("parallel"),
    )(x, y, z, gnw, wo)


def _ssd_gate_bwd(dy, y, z, gnw, wo):
    t, d = dy.shape
    tm = min(TOKEN_TILE, t)
    nb = N_XS_BLK
    per = nb // SSD_N_GROUPS

    def body(dy_ref, y_ref, z_ref, gnw_ref, wo_ref, dys_ref, dz_ref, dgnw_ref):
        @pl.when(pl.program_id(0) == 0)
        def _():
            dgnw_ref[...] = jnp.zeros_like(dgnw_ref)

        dgn = _dot_nt(dy_ref[...].astype(BF16), wo_ref[...])
        for q in range(SSD_N_GROUPS):
            js = range(q * per, (q + 1) * per)
            gs, sil, dsil = {}, {}, {}
            for j in js:
                zv = z_ref[j]
                sig = _sigmoid(zv)
                sil[j] = zv * sig
                dsil[j] = sig * (1.0 + zv * (1.0 - sig))
                gs[j] = y_ref[j] * sil[j]
            ss = sum(jnp.sum(gs[j] * gs[j], axis=1, keepdims=True) for j in js)
            inv = lax.rsqrt(ss / (per * LANES) + RMS_EPS)
            ghat = {j: gs[j] * inv for j in js}
            dgh = {}
            for j in js:
                dj = dgn[:, j * LANES:(j + 1) * LANES]
                dgnw_ref[j] += jnp.sum(dj * ghat[j], axis=0, keepdims=True)
                dgh[j] = dj * gnw_ref[j]
            mean = sum(jnp.sum(dgh[j] * ghat[j], axis=1, keepdims=True) for j in js) / (per * LANES)
            for j in js:
                dg = inv * (dgh[j] - ghat[j] * mean)
                dys_ref[j] = dg * sil[j]
                dz_ref[j] = (dg * y_ref[j] * dsil[j]).astype(BF16)

    blk = pl.BlockSpec((nb, tm, LANES), lambda i: (0, i, 0))
    return pl.pallas_call(
        body, name="ssd_gate_bwd", grid=(t // tm,),
        in_specs=[pl.BlockSpec((tm, d), lambda i: (i, 0)), blk, blk, _resident((nb, 1, LANES)),
                  _resident((SSD_D_INNER, d))],
        out_specs=[blk, blk, pl.BlockSpec((nb, 1, LANES), lambda i: (0, 0, 0))],
        out_shape=[jax.ShapeDtypeStruct((nb, t, LANES), F32), jax.ShapeDtypeStruct((nb, t, LANES), BF16),
                   jax.ShapeDtypeStruct((nb, 1, LANES), F32)],
        compiler_params=_params("arbitrary"),
    )(dy, y, z, gnw, wo)


def _lane_blocks(v):
    r, n = v.shape[0], v.shape[1] // LANES
    return v.reshape(r, n, LANES).transpose(1, 0, 2)


def _ssd_prep(w_in, conv_w, conv_b, dt_bias, a_log, d_skip, norm_w):
    n_main = SSD_D_INNER + SSD_CONV_DIM
    w_dt = w_in[:, n_main:].reshape(-1, SSD_N_GROUPS, HEADS_PER_GROUP)
    w_dt = jnp.pad(w_dt, ((0, 0), (0, 0), (0, LANES - HEADS_PER_GROUP))).reshape(-1, SSD_N_GROUPS * LANES)
    w_in_pad = jnp.concatenate([w_in[:, :n_main], w_dt], axis=1)
    taps = jnp.concatenate([conv_w, conv_b[None], jnp.zeros((SUBLANES - SSD_CONV_W - 1, SSD_CONV_DIM), F32)], axis=0)
    cwb = _lane_blocks(taps)
    row = lambda v: jnp.pad(v.reshape(SSD_N_GROUPS, 1, HEADS_PER_GROUP), ((0, 0), (0, 0), (0, LANES - HEADS_PER_GROUP)))
    dskip = jnp.broadcast_to(jnp.repeat(d_skip, SSD_D_INNER // SSD_N_HEADS).reshape(N_XS_BLK, 1, LANES),
                             (N_XS_BLK, SUBLANES, LANES))
    gnw = norm_w.reshape(N_XS_BLK, 1, LANES)
    return w_in_pad, cwb, row(dt_bias), row(a_log), dskip, gnw


def _ssd_layer_fwd(x, nw, prm, wo, consts):
    w_in_pad, cwb, dtb, alog, dskip, gnw = prm
    z, xr, dtr = _ssd_inproj(x, nw, w_in_pad)
    y, sprev = _ssd_scan_fwd(xr, dtr, cwb, dtb, alog, dskip, consts)
    out, gn = _ssd_gate_fwd(x, y, z, gnw, wo)
    return out, (z, xr, dtr, y, sprev, gn)


def _ssd_layer_bwd(x, dy, nw, prm, wo, consts, saved):
    w_in_pad, cwb, dtb, alog, dskip, gnw = prm
    z, xr, dtr, y, sprev, gn = saved
    dys, dz, dgnw = _ssd_gate_bwd(dy, y, z, gnw, wo)
    dwo = _matmul_tn(gn, dy, name="wgrad_ssd_out")
    dxs, db, dc, ddtr, dcwx, dcwb, dcwc, dd, dsm = _ssd_scan_bwd(xr, dtr, dys, sprev, cwb, dtb, alog, dskip, consts)
    pieces = [dz, dxs, db, dc, ddtr]
    dx, dnw, h = _inproj_bwd(x, dy, nw, [w_in_pad], [pieces])
    dws = [_matmul_tn_blocked(h, p, name=f"wgrad_ssd_in{i}") for i, p in enumerate(pieces)]
    dw_dt = dws[4].reshape(-1, SSD_N_GROUPS, LANES)[:, :, :HEADS_PER_GROUP].reshape(-1, SSD_N_HEADS)
    dw_in = jnp.concatenate(dws[:4] + [dw_dt], axis=1)
    dtaps = jnp.concatenate([dcwx, dcwb, dcwc], axis=0).transpose(1, 0, 2).reshape(SUBLANES, SSD_CONV_DIM)
    by_head = lambda r: dsm[:, r, :HEADS_PER_GROUP].reshape(SSD_N_HEADS)
    d_d = jnp.sum(dd[:, 0, :].reshape(SSD_N_HEADS, SSD_D_INNER // SSD_N_HEADS), axis=1)
    return dx, (dnw, dw_in, dtaps[:SSD_CONV_W], dtaps[SSD_CONV_W], by_head(0), by_head(1),
                d_d, dgnw.reshape(SSD_D_INNER), dwo)


def _loss_head(x, fw, target):
    t, d = x.shape
    tm = min(TOKEN_TILE, t)

    def body(x_ref, fw_ref, tgt_ref, loss_ref, dx_ref, dfw_ref):
        fwv = fw_ref[...]
        y, xhat, inv = _rms_fwd(x_ref[...], fwv)
        err = y - tgt_ref[...]
        tot = jnp.sum(jnp.sum(err * err, axis=1, keepdims=True), axis=0, keepdims=True)
        dx, dw = _rms_bwd(err * (1.0 / d), xhat, inv, fwv)
        dx_ref[...] = dx

        @pl.when(pl.program_id(0) == 0)
        def _():
            loss_ref[...] = jnp.zeros_like(loss_ref)
            dfw_ref[...] = jnp.zeros_like(dfw_ref)

        loss_ref[...] += jnp.broadcast_to(tot * (0.5 / d), loss_ref.shape)
        dfw_ref[...] += dw

    tok = pl.BlockSpec((tm, d), lambda i: (i, 0))
    return pl.pallas_call(
        body, name="loss_head", grid=(t // tm,),
        in_specs=[tok, _resident((1, d)), tok],
        out_specs=[pl.BlockSpec((1, LANES), lambda i: (0, 0)), tok, pl.BlockSpec((1, d), lambda i: (0, 0))],
        out_shape=[jax.ShapeDtypeStruct((1, LANES), F32), jax.ShapeDtypeStruct((t, d), F32),
                   jax.ShapeDtypeStruct((1, d), F32)],
        compiler_params=_params("arbitrary"),
    )(x, fw, target)


def _row_tile(rows, cap):
    best = SUBLANES
    for r in range(SUBLANES, min(rows, cap) + 1, SUBLANES):
        if rows % r == 0:
            best = r
    return best


def _adamw(w, g, m, v, name):
    rows, cols = w.shape
    br = _row_tile(rows, 256)
    c1 = 1.0 - ADAM_B1 ** ADAM_STEP
    c2 = 1.0 - ADAM_B2 ** ADAM_STEP

    def body(w_ref, g_ref, m_ref, v_ref, d_ref, nm_ref, nv_ref):
        gv = g_ref[...]
        nm = ADAM_B1 * m_ref[...] + (1.0 - ADAM_B1) * gv
        nv = ADAM_B2 * v_ref[...] + (1.0 - ADAM_B2) * (gv * gv)
        nm_ref[...] = nm
        nv_ref[...] = nv
        d_ref[...] = -ADAM_LR * ((nm / c1) / (jnp.sqrt(nv / c2) + ADAM_EPS) + ADAM_WD * w_ref[...])

    blk = pl.BlockSpec((br, cols), lambda i: (i, 0))
    shp = jax.ShapeDtypeStruct((rows, cols), F32)
    return pl.pallas_call(
        body, name=name, grid=(rows // br,), in_specs=[blk] * 4, out_specs=[blk] * 3, out_shape=[shp] * 3,
        compiler_params=_params("parallel"),
    )(w, g, m, v)


def _place():
    x, y, c = lax.axis_index("x"), lax.axis_index("y"), lax.axis_index("c")
    return x, y, c, [(1 - x, y), (x, 1 - y), (1 - x, 1 - y)]


def _remote(src, dst, send_sems, recv_sems, k, to):
    return pltpu.make_async_remote_copy(src_ref=src, dst_ref=dst, send_sem=send_sems.at[k], recv_sem=recv_sems.at[k],
                                        device_id=to, device_id_type=MESH)


def _scatter_copies(srcs, dsts, send_sems, recv_sems):
    x, y, c, chips = _place()
    sends, arrivals = [], []
    for oi, (src, dst) in enumerate(zip(srcs, dsts)):
        for j, chip in enumerate(chips):
            sends.append(_remote(src.at[2 * chip[0] + chip[1]], dst.at[j], send_sems, recv_sems, 3 * oi + j, (*chip, c)))
            arrivals.append(_remote(dst.at[j], dst.at[j], send_sems, recv_sems, 3 * oi + j, (*chip, c)))
    return sends, arrivals


def _swap_copies(srcs, dsts, send_sems, recv_sems):
    x, y, c, _ = _place()
    return [_remote(src.at[:, 1 - c], dst, send_sems, recv_sems, oi, (x, y, 1 - c))
            for oi, (src, dst) in enumerate(zip(srcs, dsts))]


def _scatter_start(srcs, dsts, send_sems, recv_sems):
    for cp in _scatter_copies(srcs, dsts, send_sems, recv_sems)[0]:
        cp.start()


def _scatter_wait(srcs, dsts, send_sems, recv_sems):
    sends, arrivals = _scatter_copies(srcs, dsts, send_sems, recv_sems)
    for cp in arrivals:
        cp.wait_recv()
    for cp in sends:
        cp.wait_send()


def _all_gather_shards(arrs):
    n = len(arrs)

    def body(*refs):
        srcs, dsts = refs[:n], refs[n:2 * n]
        send_sems, recv_sems = refs[2 * n:]
        x, y, c, chips = _place()
        me = 2 * x + y
        sibling = (x, y, 1 - c)
        sent = []
        for oi, (src, dst) in enumerate(zip(srcs, dsts)):
            for j, chip in enumerate(chips):
                sent.append(_remote(src.at[c], dst.at[me, c], send_sems, recv_sems, 6 * oi + j, (*chip, c)))
                sent[-1].start()
        for oi, dst in enumerate(dsts):
            for j, chip in enumerate(chips):
                landed = dst.at[2 * chip[0] + chip[1], c]
                _remote(landed, landed, send_sems, recv_sems, 6 * oi + j, (*chip, c)).wait_recv()
                sent.append(_remote(landed, landed, send_sems, recv_sems, 6 * oi + 3 + j, sibling))
                sent[-1].start()
        for oi, dst in enumerate(dsts):
            for j, chip in enumerate(chips):
                landed = dst.at[2 * chip[0] + chip[1], 1 - c]
                _remote(landed, landed, send_sems, recv_sems, 6 * oi + 3 + j, sibling).wait_recv()
        for cp in sent:
            cp.wait_send()

    return pl.pallas_call(
        body, name="all_gather_shards",
        in_specs=[_HBM] * n, out_specs=[_HBM] * n,
        out_shape=[jax.ShapeDtypeStruct((N_SHARDS,) + a.shape, a.dtype) for a in arrs],
        scratch_shapes=[pltpu.SemaphoreType.DMA((6 * n,)), pltpu.SemaphoreType.DMA((6 * n,))],
    )(*arrs)


def _swap_halves(arrs):
    n = len(arrs)

    def body(*refs):
        srcs, dsts = refs[:n], refs[n:2 * n]
        cps = _swap_copies(srcs, dsts, *refs[2 * n:])
        for cp in cps:
            cp.start()
        for cp in cps:
            cp.wait()

    return pl.pallas_call(
        body, name="swap_halves", in_specs=[_HBM] * n, out_specs=[_HBM] * n,
        out_shape=[jax.ShapeDtypeStruct((a.shape[0],) + a.shape[2:], a.dtype) for a in arrs],
        scratch_shapes=[pltpu.SemaphoreType.DMA((n,)), pltpu.SemaphoreType.DMA((n,))],
    )(*arrs)


def _scatter_to_chips(arrs):
    n = len(arrs)

    def body(*refs):
        srcs, dsts = refs[:n], refs[n:2 * n]
        _scatter_start(srcs, dsts, *refs[2 * n:])
        _scatter_wait(srcs, dsts, *refs[2 * n:])

    return pl.pallas_call(
        body, name="scatter_to_chips", in_specs=[_HBM] * n, out_specs=[_HBM] * n,
        out_shape=[jax.ShapeDtypeStruct((3,) + a.shape[1:], a.dtype) for a in arrs],
        scratch_shapes=[pltpu.SemaphoreType.DMA((3 * n,)), pltpu.SemaphoreType.DMA((3 * n,))],
    )(*arrs)


def _join_halves(arrs):
    n = len(arrs)

    def body(*refs):
        bufs = refs[n:2 * n]
        send_sems, recv_sems = refs[2 * n:]
        x, y, c, _ = _place()
        sibling = (x, y, 1 - c)
        sent = [_remote(buf.at[c], buf.at[c], send_sems, recv_sems, oi, sibling) for oi, buf in enumerate(bufs)]
        for cp in sent:
            cp.start()
        for oi, buf in enumerate(bufs):
            _remote(buf.at[1 - c], buf.at[1 - c], send_sems, recv_sems, oi, sibling).wait_recv()
        for cp in sent:
            cp.wait_send()

    return pl.pallas_call(
        body, name="join_halves", in_specs=[_HBM] * n, out_specs=[_HBM] * n,
        out_shape=[jax.ShapeDtypeStruct(a.shape, a.dtype) for a in arrs],
        input_output_aliases={i: i for i in range(n)},
        scratch_shapes=[pltpu.SemaphoreType.DMA((n,)), pltpu.SemaphoreType.DMA((n,))],
    )(*arrs)


def _add_halves(full, recv, place):
    n, _, rows, cols = full.shape
    br = _row_tile(rows, 512)

    def body(p_ref, a_ref, b_ref, o_ref):
        o_ref[...] = (a_ref[...] + b_ref[...]).astype(BF16)

    grid_spec = pltpu.PrefetchScalarGridSpec(
        num_scalar_prefetch=1, grid=(n, rows // br),
        in_specs=[pl.BlockSpec((None, None, br, cols), lambda s, i, p_ref: (s, p_ref[1], i, 0)),
                  pl.BlockSpec((None, br, cols), lambda s, i, p_ref: (s, i, 0))],
        out_specs=pl.BlockSpec((None, br, cols), lambda s, i, p_ref: (s, i, 0)))
    return pl.pallas_call(
        body, name="add_halves", grid_spec=grid_spec, out_shape=jax.ShapeDtypeStruct((n, rows, cols), BF16),
        compiler_params=_params("parallel", "parallel"),
    )(place, full, recv)


def _sum_chips(mine, others, place):
    _, rows, cols = mine.shape
    br = _row_tile(rows, 512)
    slot_of_flip = {2: 0, 1: 1, 3: 2}

    def body(p_ref, m_ref, o_ref, out_ref):
        me = p_ref[0]
        own = m_ref[...].astype(F32)
        got = [o_ref[j].astype(F32) for j in range(3)]
        acc = None
        for s in range(N_SHARDS):
            flip = jnp.bitwise_xor(me, s)
            term = own
            for f, j in slot_of_flip.items():
                term = jnp.where(flip == f, got[j], term)
            acc = term if acc is None else acc + term
        out_ref[...] = acc

    grid_spec = pltpu.PrefetchScalarGridSpec(
        num_scalar_prefetch=1, grid=(rows // br,),
        in_specs=[pl.BlockSpec((None, br, cols), lambda i, p_ref: (p_ref[0], i, 0)),
                  pl.BlockSpec((3, br, cols), lambda i, p_ref: (0, i, 0))],
        out_specs=pl.BlockSpec((None, br, cols), lambda i, p_ref: (p_ref[1], i, 0)))
    return pl.pallas_call(
        body, name="sum_chips", grid_spec=grid_spec, out_shape=jax.ShapeDtypeStruct((2, rows, cols), F32),
        compiler_params=_params("parallel"),
    )(place, mine, others)


WEIGHTS = ("norm_w", "ffn_w_gate", "ffn_w_up", "ffn_w_down", "ssd_w_in", "ssd_conv_w", "ssd_conv_b", "ssd_dt_bias",
           "ssd_a_log", "ssd_d", "ssd_norm_w", "ssd_w_out", "sc_w_in", "sc_conv_w", "sc_w_out", "final_norm_w")
BIG = (("ffn_w_gate", 3), ("ffn_w_up", 3), ("ffn_w_down", 2), ("ssd_w_in", 2), ("ssd_w_out", 1), ("sc_w_in", 2),
       ("sc_w_out", 1))
SMALL_SHARDED = (("norm_w", 2), ("ssd_conv_w", 2), ("sc_conv_w", 2))
REPLICATED = ("ssd_conv_b", "ssd_dt_bias", "ssd_a_log", "ssd_d", "ssd_norm_w", "final_norm_w")
FLAT_COLS = 1024


def _pack(arrays, row_multiple, lead=()):
    flat = jnp.concatenate([a.reshape(lead + (-1,)) for a in arrays], axis=len(lead))
    unit = row_multiple * FLAT_COLS
    n = flat.shape[-1]
    pad = (-n) % unit
    if pad:
        flat = jnp.pad(flat, [(0, 0)] * len(lead) + [(0, pad)])
    return flat.reshape(lead + (-1, FLAT_COLS))


def _unpack(flat, shapes, lead=()):
    flat = flat.reshape(lead + (-1,))
    out, off = [], 0
    for shp in shapes:
        n = 1
        for s in shp:
            n *= s
        out.append(flat[..., off:off + n].reshape(lead + tuple(shp)))
        off += n
    return out


def _to_shards(full, axis):
    shp = full.shape
    r = full.reshape(shp[:axis] + (N_SHARDS, shp[axis] // N_SHARDS) + shp[axis + 1:])
    return jnp.moveaxis(r, axis, 0)


def _from_shards(sh, axis):
    r = jnp.moveaxis(sh, 0, axis)
    shp = r.shape
    return r.reshape(shp[:axis] + (shp[axis] * shp[axis + 1],) + shp[axis + 2:])


def _layer_shards(wl, i):
    j = i // 2
    ffn = lambda k: [(("ffn_w_gate", k), wl["ffn_w_gate"][i, k], 1), (("ffn_w_up", k), wl["ffn_w_up"][i, k], 1),
                     (("ffn_w_down", k), wl["ffn_w_down"][i, k], 0)]
    mix = "ssd" if i % 2 == 0 else "sc"
    return ffn(0), [((mix + "_w_in",), wl[mix + "_w_in"][j], 1), ((mix + "_w_out",), wl[mix + "_w_out"][j], 0)] + ffn(1)


def _assemble(group, received, chip):
    return {key: _from_shards(lax.dynamic_update_index_in_dim(r, own, chip, 0), axis)
            for (key, own, axis), r in zip(group, received)}


def _forward_backward(x, target, p, wl, layer0, place):
    chip = place[0]
    consts = _ssd_consts()
    nw = p["norm_w"]
    row = lambda v: v[None]
    full = {0: layer0}
    ffn = lambda i, k: (full[i]["ffn_w_gate", k], full[i]["ffn_w_up", k], full[i]["ffn_w_down", k])
    ssd_prm, sc_cw = {}, {}

    xin, saved, pre = [], [], {}
    for i in range(N_LAYERS):
        j = i // 2
        first, second = _layer_shards(wl, i + 1) if i + 1 < N_LAYERS else ([], [])
        xin.append(x)
        x, *rest = _ffn_fwd(x, row(nw[i, 0]), *ffn(i, 0), carry=[s[1] for s in first])
        pre[i, 0] = rest[:3]
        if first:
            full[i + 1] = _assemble(first, rest[3:], chip)
        xin.append(x)
        if i % 2 == 0:
            ssd_prm[j] = _ssd_prep(full[i]["ssd_w_in",], p["ssd_conv_w"][j], p["ssd_conv_b"][j], p["ssd_dt_bias"][j],
                                   p["ssd_a_log"][j], p["ssd_d"][j], p["ssd_norm_w"][j])
            x, sv = _ssd_layer_fwd(x, row(nw[i, 1]), ssd_prm[j], full[i]["ssd_w_out",], consts)
        else:
            sc_cw[j] = jnp.pad(p["sc_conv_w"][j], ((0, SUBLANES - SC_CONV_W), (0, 0)))
            sv = _norm_mm(x, row(nw[i, 1]), full[i]["sc_w_in",])
            x = _sc_fwd(x, sv, sc_cw[j], full[i]["sc_w_out",])
        saved.append(sv)
        xin.append(x)
        x, *rest = _ffn_fwd(x, row(nw[i, 2]), *ffn(i, 1), carry=[s[1] for s in second])
        pre[i, 1] = rest[:3]
        if second:
            full[i + 1].update(_assemble(second, rest[3:], chip))
    loss, dx, dfw = _loss_head(x, row(p["final_norm_w"]), target)

    g_nw = [[None] * 3 for _ in range(N_LAYERS)]
    g_ffn = {}
    g_ssd = [None, None]
    g_sc = [None, None]
    halved = lambda a: a.reshape((N_SHARDS, 2, -1, a.shape[-1]))

    def ffn_bwd(i, k, slot, dy, parts=(), sums=()):
        wg, wu, wd = ffn(i, k)
        a, s, p_ = pre[i, k]
        dg, du, *theirs = _ffn_bwd_act(dy, s, p_, wd, carry=parts)
        if parts:
            sums = [_add_halves(mine, got, place) for mine, got in zip(parts, theirs)]
        carry = sums[:2] if parts else sums
        dxn, dnw, h, *arrived = _inproj_bwd(xin[3 * i + slot], dy, row(nw[i, slot]), [wg, wu], [[dg], [du]], carry=carry)
        g_nw[i][slot] = dnw[0]
        for n, lhs, rhs, scale in (("ffn_w_gate", h, dg, 1.0), ("ffn_w_up", h, du, 1.0), ("ffn_w_down", a, dy, 0.5)):
            g_ffn[n, i] = _matmul_tn(lhs, rhs, scale=scale, name="wgrad_" + n, slab=(k,), stack=(2,),
                                     buf=g_ffn.get((n, i)))
        return dxn, arrived, sums

    def finish(sums, arrived):
        return _join_halves([_sum_chips(mine, others, place) for mine, others in zip(sums, arrived)])

    reduced = {}
    waiting = None
    for i in reversed(range(N_LAYERS)):
        j = i // 2
        dx, arrived, sums = ffn_bwd(i, 1, 2, dx, parts=waiting[1] if waiting else ())
        xm = xin[3 * i + 1]
        if i % 2 == 0:
            dx, gs = _ssd_layer_bwd(xm, dx, row(nw[i, 1]), ssd_prm[j], full[i]["ssd_w_out",], consts, saved[i])
            g_nw[i][1] = gs[0][0]
            g_ssd[j] = gs[1:]
            mixer = [(("ssd_w_in", j), _to_shards(gs[1], 1)), (("ssd_w_out", j), _to_shards(gs[-1], 0))]
        else:
            bcu = saved[i]
            dbcu, pin, dcw = _sc_bwd(dx, bcu, sc_cw[j], full[i]["sc_w_out",])
            dwo = _matmul_tn(pin, dx, name="wgrad_sc_out")
            dx, dnw, h = _inproj_bwd(xm, dx, row(nw[i, 1]), [full[i]["sc_w_in",]], [[dbcu]])
            g_nw[i][1] = dnw[0]
            dwi = _matmul_tn(h, dbcu, name="wgrad_sc_in")
            g_sc[j] = dcw[:SC_CONV_W]
            mixer = [(("sc_w_in", j), _to_shards(dwi, 1)), (("sc_w_out", j), _to_shards(dwo, 0))]
        dx, more, _ = ffn_bwd(i, 0, 0, dx, sums=sums[2:])
        if waiting:
            reduced.update(zip(waiting[0], finish(sums, arrived + more)))
        mine = [(("ffn_w_gate", i), _to_shards(g_ffn["ffn_w_gate", i], 2)), (("ffn_w_up", i), _to_shards(g_ffn["ffn_w_up", i], 2)),
                (("ffn_w_down", i), _to_shards(g_ffn["ffn_w_down", i], 1))] + mixer
        waiting = ([key for key, _ in mine], [halved(v) for _, v in mine])

    g = {"norm_w": jnp.stack([jnp.stack(r) for r in g_nw]), "final_norm_w": dfw[0]}
    for k, n in enumerate(("ssd_conv_w", "ssd_conv_b", "ssd_dt_bias", "ssd_a_log", "ssd_d", "ssd_norm_w")):
        g[n] = jnp.stack([g_ssd[0][k + 1], g_ssd[1][k + 1]])
    g["sc_conv_w"] = jnp.stack(g_sc)
    small_part = _pack([_to_shards(g[n], ax) for n, ax in SMALL_SHARDED]
                       + [jnp.broadcast_to(g[n][None], (N_SHARDS,) + g[n].shape) for n in REPLICATED],
                       4 * SUBLANES, lead=(N_SHARDS,))
    parts = waiting[1] + [halved(small_part)]
    last = [_add_halves(mine, got, place) for mine, got in zip(parts, _swap_halves(parts))]
    out = finish(last, _scatter_to_chips(last))
    reduced.update(zip(waiting[0], out[:-1]))
    return loss, dx, reduced, out[-1]


def kernel(x, norm_w, ffn_w_gate, ffn_w_up, ffn_w_down, ssd_w_in, ssd_conv_w, ssd_conv_b, ssd_dt_bias, ssd_a_log, ssd_d, ssd_norm_w, ssd_w_out, sc_w_in, sc_conv_w, sc_w_out, final_norm_w, loss_target, m_norm_w, m_ffn_w_gate, m_ffn_w_up, m_ffn_w_down, m_ssd_w_in, m_ssd_conv_w, m_ssd_conv_b, m_ssd_dt_bias, m_ssd_a_log, m_ssd_d, m_ssd_norm_w, m_ssd_w_out, m_sc_w_in, m_sc_conv_w, m_sc_w_out, m_final_norm_w, v_norm_w, v_ffn_w_gate, v_ffn_w_up, v_ffn_w_down, v_ssd_w_in, v_ssd_conv_w, v_ssd_conv_b, v_ssd_dt_bias, v_ssd_a_log, v_ssd_d, v_ssd_norm_w, v_ssd_w_out, v_sc_w_in, v_sc_conv_w, v_sc_w_out, v_final_norm_w):
    w = dict(zip(WEIGHTS, (norm_w, ffn_w_gate, ffn_w_up, ffn_w_down, ssd_w_in, ssd_conv_w, ssd_conv_b, ssd_dt_bias,
                           ssd_a_log, ssd_d, ssd_norm_w, ssd_w_out, sc_w_in, sc_conv_w, sc_w_out, final_norm_w)))
    m = dict(zip(WEIGHTS, (m_norm_w, m_ffn_w_gate, m_ffn_w_up, m_ffn_w_down, m_ssd_w_in, m_ssd_conv_w, m_ssd_conv_b,
                           m_ssd_dt_bias, m_ssd_a_log, m_ssd_d, m_ssd_norm_w, m_ssd_w_out, m_sc_w_in, m_sc_conv_w,
                           m_sc_w_out, m_final_norm_w)))
    v = dict(zip(WEIGHTS, (v_norm_w, v_ffn_w_gate, v_ffn_w_up, v_ffn_w_down, v_ssd_w_in, v_ssd_conv_w, v_ssd_conv_b,
                           v_ssd_dt_bias, v_ssd_a_log, v_ssd_d, v_ssd_norm_w, v_ssd_w_out, v_sc_w_in, v_sc_conv_w,
                           v_sc_w_out, v_final_norm_w)))
    chip = 2 * lax.axis_index("x") + lax.axis_index("y")
    place = jnp.stack([chip, lax.axis_index("c")]).astype(jnp.int32)
    big_names = [n for n, _ in BIG]
    small_names = [n for n, _ in SMALL_SHARDED] + list(REPLICATED)
    halved = lambda a, lead=(): a.reshape(lead + (2, -1, a.shape[-1]))

    wl = {n: w[n].astype(BF16) for n in big_names}
    first, second = _layer_shards(wl, 0)
    small = halved(_pack([w[n] for n, _ in SMALL_SHARDED], 2 * SUBLANES))
    received = _all_gather_shards([halved(s[1]) for s in first + second] + [small])
    layer0 = _assemble(first + second, [r.reshape((N_SHARDS,) + s[1].shape) for r, s in zip(received, first + second)],
                       chip)
    p = {n: w[n] for n in REPLICATED}
    small_full = lax.dynamic_update_index_in_dim(received[-1], small, chip, 0)
    for (n, ax), sh in zip(SMALL_SHARDED, _unpack(small_full, [w[n].shape for n, _ in SMALL_SHARDED], lead=(N_SHARDS,))):
        p[n] = _from_shards(sh, ax)

    t, d = x.shape[-2:]
    loss, dx, reduced, g_small = _forward_backward(x.reshape(t, d), loss_target.reshape(t, d), p, wl, layer0, place)

    grad = {}
    for n in big_names:
        per_layer = w[n].shape[0]
        grad[n] = jnp.stack([reduced[n, i].reshape(w[n].shape[1:]) for i in range(per_layer)])
    g_small = g_small.reshape(-1, FLAT_COLS)
    grad.update(zip(small_names, _unpack(g_small, [w[n].shape for n in small_names])))

    delta, new_m, new_v = {}, {}, {}
    for n in big_names:
        shp = w[n].shape
        as2d = lambda a: a.reshape(-1, shp[-1])
        out = _adamw(as2d(w[n]), as2d(grad[n]), as2d(m[n]), as2d(v[n]), name="adamw_" + n)
        delta[n], new_m[n], new_v[n] = (o.reshape(shp) for o in out)
    packed = [_pack([s[n] for n in small_names], 4 * SUBLANES) for s in (w, m, v)]
    out = _adamw(packed[0], g_small, packed[1], packed[2], name="adamw_small")
    shapes = [w[n].shape for n in small_names]
    for dst, o in zip((delta, new_m, new_v), out):
        dst.update(zip(small_names, _unpack(o, shapes)))

    loss = lax.psum(loss[0, 0], ("x", "y", "c"))
    return (loss, dx.reshape(x.shape), *[grad[n] for n in WEIGHTS], *[delta[n] for n in WEIGHTS],
            *[new_m[n] for n in WEIGHTS], *[new_v[n] for n in WEIGHTS])
```

```python
import functools

import jax
import jax.numpy as jnp
from jax import lax
from jax.experimental import pallas as pl
from jax.experimental.pallas import tpu as pltpu

F32 = jnp.float32
BF16 = jnp.bfloat16
MESH = pl.DeviceIdType.MESH

RMS_EPS = 1e-5
D_MODEL = 1024
D_FF = 2816
N_LAYERS = 4
SSD_D_INNER = 2048
SSD_N_HEADS = 32
SSD_N_GROUPS = 4
SSD_D_STATE = 128
SSD_CHUNK = 128
SSD_CONV_W = 4
SSD_CONV_DIM = 3072
SSD_IN_DIM = 5152
SC_CONV_W = 3
LANES = 128
SUBLANES = 8
N_XS_BLK = SSD_D_INNER // LANES
SSD_IN_PAD = SSD_D_INNER + SSD_CONV_DIM + SSD_N_GROUPS * LANES
VMEM_LIMIT = 56 * 2**20
TOKEN_TILE = 512
WGRAD_TOKENS = 2048
FF_CHUNK = 256
N_SHARDS = 4

ADAM_LR = 0.001
ADAM_B1 = 0.9
ADAM_B2 = 0.999
ADAM_EPS = 1e-08
ADAM_WD = 0.01
ADAM_STEP = 10


_HBM = pl.BlockSpec(memory_space=pl.ANY)


def _params(*sem):
    return pltpu.CompilerParams(dimension_semantics=sem if sem else None, vmem_limit_bytes=VMEM_LIMIT)


def _dot(a, b):
    return jnp.dot(a, b, preferred_element_type=F32)


def _dot_nt(a, b):
    return lax.dot_general(a, b, (((1,), (1,)), ((), ())), preferred_element_type=F32)


def _dot_tn(a, b):
    return lax.dot_general(a, b, (((0,), (0,)), ((), ())), preferred_element_type=F32)


def _resident(shape):
    n = len(shape)
    return pl.BlockSpec(shape, lambda *_: (0,) * n, pipeline_mode=pl.Buffered(1))


def _split3(v):
    hi = v.astype(BF16)
    r1 = v - hi.astype(F32)
    mid = r1.astype(BF16)
    lo = (r1 - mid.astype(F32)).astype(BF16)
    return hi, mid, lo


def _sel_left(sel, v3):
    return _dot(sel, v3[0]) + _dot(sel, v3[1]) + _dot(sel, v3[2])


def _sigmoid(v):
    return 1.0 / (1.0 + jnp.exp(-v))


def _rms_fwd(x, w):
    inv = lax.rsqrt(jnp.mean(x * x, axis=-1, keepdims=True) + RMS_EPS)
    xhat = x * inv
    return xhat * w, xhat, inv


def _rms_bwd(dh, xhat, inv, w):
    dxhat = dh * w
    dx = inv * (dxhat - xhat * jnp.mean(dxhat * xhat, axis=-1, keepdims=True))
    return dx, jnp.sum(dh * xhat, axis=0, keepdims=True)


def _ffn_fwd(x, nw, wg, wu, wd, carry=()):
    t, d = x.shape
    f = wg.shape[1]
    tm = min(TOKEN_TILE, t)
    nsteps = t // tm
    ncar = len(carry)

    def body(x_ref, nw_ref, wg_ref, wu_ref, wd_ref, *rest):
        srcs = rest[:ncar]
        o_ref, a_ref, s_ref, p_ref = rest[ncar:ncar + 4]
        dsts = rest[ncar + 4:2 * ncar + 4]
        if ncar:
            send_sems, recv_sems = rest[2 * ncar + 4:]
            x_, y_, c_, chips = _place()
            me = 2 * x_ + y_

            @pl.when(pl.program_id(0) == 0)
            def _():
                for oi, (src, dst) in enumerate(zip(srcs, dsts)):
                    for j, chip in enumerate(chips):
                        _remote(src, dst.at[me], send_sems, recv_sems, 3 * oi + j, (*chip, c_)).start()

        xv = x_ref[...]
        h = _rms_fwd(xv, nw_ref[...])[0].astype(BF16)
        for j in range(f // FF_CHUNK):
            sl = slice(j * FF_CHUNK, (j + 1) * FF_CHUNK)
            g = _dot(h, wg_ref[:, sl])
            u = _dot(h, wu_ref[:, sl])
            sig = _sigmoid(g)
            s = g * sig
            a_ref[:, sl] = (s * u).astype(BF16)
            s_ref[:, sl] = s.astype(BF16)
            p_ref[:, sl] = (u * (sig + s * (1.0 - sig))).astype(BF16)
        o_ref[...] = xv + 0.5 * _dot(a_ref[...], wd_ref[...])

        if ncar:
            @pl.when(pl.program_id(0) == nsteps - 1)
            def _():
                for oi, (src, dst) in enumerate(zip(srcs, dsts)):
                    for j, chip in enumerate(chips):
                        landed = dst.at[2 * chip[0] + chip[1]]
                        _remote(landed, landed, send_sems, recv_sems, 3 * oi + j, (*chip, c_)).wait_recv()
                for oi, (src, dst) in enumerate(zip(srcs, dsts)):
                    for j, chip in enumerate(chips):
                        _remote(src, dst.at[me], send_sems, recv_sems, 3 * oi + j, (*chip, c_)).wait_send()

    tok = lambda n: pl.BlockSpec((tm, n), lambda i: (i, 0))
    sems = [pltpu.SemaphoreType.DMA((3 * ncar,)), pltpu.SemaphoreType.DMA((3 * ncar,))] if ncar else []
    return pl.pallas_call(
        body, name="ffn_fwd_carry" if ncar else "ffn_fwd", grid=(nsteps,),
        in_specs=[tok(d), _resident((1, d)), _resident((d, f)), _resident((d, f)), _resident((f, d))] + [_HBM] * ncar,
        out_specs=[tok(d), tok(f), tok(f), tok(f)] + [_HBM] * ncar,
        out_shape=[jax.ShapeDtypeStruct((t, d), F32)] + [jax.ShapeDtypeStruct((t, f), BF16)] * 3
        + [jax.ShapeDtypeStruct((N_SHARDS,) + c.shape, c.dtype) for c in carry],
        scratch_shapes=sems,
        compiler_params=_params("arbitrary" if ncar else "parallel"),
    )(x, nw, wg, wu, wd, *carry)


def _ffn_bwd_act(dy, s, p, wd, carry=()):
    t, d = dy.shape
    f = wd.shape[0]
    tm = min(TOKEN_TILE, t)
    nsteps = t // tm
    ncar = len(carry)

    def body(dy_ref, s_ref, p_ref, wd_ref, *rest):
        srcs, (dg_ref, du_ref) = rest[:ncar], rest[ncar:ncar + 2]
        dsts, sems = rest[ncar + 2:2 * ncar + 2], rest[2 * ncar + 2:]
        if ncar:
            @pl.when(pl.program_id(0) == 0)
            def _():
                for cp in _swap_copies(srcs, dsts, *sems):
                    cp.start()

        dob = (0.5 * dy_ref[...]).astype(BF16)
        for j in range(f // FF_CHUNK):
            sl = slice(j * FF_CHUNK, (j + 1) * FF_CHUNK)
            da = _dot_nt(dob, wd_ref[sl, :])
            dg_ref[:, sl] = (da * p_ref[:, sl].astype(F32)).astype(BF16)
            du_ref[:, sl] = (da * s_ref[:, sl].astype(F32)).astype(BF16)

        if ncar:
            @pl.when(pl.program_id(0) == nsteps - 1)
            def _():
                for cp in _swap_copies(srcs, dsts, *sems):
                    cp.wait()

    tok = lambda n: pl.BlockSpec((tm, n), lambda i: (i, 0))
    sems = [pltpu.SemaphoreType.DMA((ncar,)), pltpu.SemaphoreType.DMA((ncar,))] if ncar else []
    return pl.pallas_call(
        body, name="ffn_bwd_act_carry" if ncar else "ffn_bwd_act", grid=(nsteps,),
        in_specs=[tok(d), tok(f), tok(f), _resident((f, d))] + [_HBM] * ncar,
        out_specs=[tok(f), tok(f)] + [_HBM] * ncar,
        out_shape=[jax.ShapeDtypeStruct((t, f), BF16)] * 2
        + [jax.ShapeDtypeStruct((c.shape[0],) + c.shape[2:], c.dtype) for c in carry],
        scratch_shapes=sems,
        compiler_params=_params("arbitrary" if ncar else "parallel"),
    )(dy, s, p, wd, *carry)


def _ffn_bwd_fused(x, dy, s, p, nw, wg, wu, wd, carry=()):
    t, d = x.shape
    f = wd.shape[0]
    tm = min(TOKEN_TILE // 2, t)
    nsteps = t // tm
    ncar = len(carry)

    def body(x_ref, dy_ref, s_ref, p_ref, nw_ref, wg_ref, wu_ref, wd_ref, *rest):
        srcs, (dx_ref, dnw_ref, h_ref, dg_ref, du_ref) = rest[:ncar], rest[ncar:ncar + 5]
        dsts, sems = rest[ncar + 5:2 * ncar + 5], rest[2 * ncar + 5:]
        if ncar:
            @pl.when(pl.program_id(0) == 0)
            def _():
                _scatter_start(srcs, dsts, *sems)

        dyv = dy_ref[...]
        dob = (0.5 * dyv).astype(BF16)
        for j in range(f // FF_CHUNK):
            sl = slice(j * FF_CHUNK, (j + 1) * FF_CHUNK)
            da = _dot_nt(dob, wd_ref[sl, :])
            dg_ref[:, sl] = (da * p_ref[:, sl].astype(F32)).astype(BF16)
            du_ref[:, sl] = (da * s_ref[:, sl].astype(F32)).astype(BF16)
        nwv = nw_ref[...]
        hf, xhat, inv = _rms_fwd(x_ref[...], nwv)
        h_ref[...] = hf.astype(BF16)
        dh = _dot_nt(dg_ref[...], wg_ref[...]) + _dot_nt(du_ref[...], wu_ref[...])
        dx, dw = _rms_bwd(dh, xhat, inv, nwv)
        dx_ref[...] = dyv + dx

        @pl.when(pl.program_id(0) == 0)
        def _():
            dnw_ref[...] = jnp.zeros_like(dnw_ref)

        dnw_ref[...] += dw

        if ncar:
            @pl.when(pl.program_id(0) == nsteps - 1)
            def _():
                _scatter_wait(srcs, dsts, *sems)

    tok = lambda n: pl.BlockSpec((tm, n), lambda i: (i, 0))
    sems = [pltpu.SemaphoreType.DMA((3 * ncar,)), pltpu.SemaphoreType.DMA((3 * ncar,))] if ncar else []
    return pl.pallas_call(
        body, name="ffn_bwd_fused_carry" if ncar else "ffn_bwd_fused", grid=(nsteps,),
        in_specs=[tok(d), tok(d), tok(f), tok(f), _resident((1, d)), _resident((d, f)), _resident((d, f)),
                  _resident((f, d))] + [_HBM] * ncar,
        out_specs=[tok(d), pl.BlockSpec((1, d), lambda i: (0, 0)), tok(d), tok(f), tok(f)] + [_HBM] * ncar,
        out_shape=[jax.ShapeDtypeStruct((t, d), F32), jax.ShapeDtypeStruct((1, d), F32),
                   jax.ShapeDtypeStruct((t, d), BF16), jax.ShapeDtypeStruct((t, f), BF16),
                   jax.ShapeDtypeStruct((t, f), BF16)]
        + [jax.ShapeDtypeStruct((3,) + c.shape[1:], c.dtype) for c in carry],
        scratch_shapes=sems,
        compiler_params=_params("arbitrary"),
    )(x, dy, s, p, nw, wg, wu, wd, *carry)


def _pick_bn(m, n, unit):
    best = unit
    for k in range(1, n // unit + 1):
        bn = k * unit
        if n % bn == 0 and m * bn * 4 <= 8 * 2**20:
            best = bn
    return best


def _matmul_tn(a, b, scale=1.0, name="wgrad", slab=None, stack=None, buf=None):
    t, m = a.shape
    n = b.shape[1]
    bt = min(WGRAD_TOKENS, t)
    bn = _pick_bn(m, n, LANES)
    nt = t // bt
    lead = tuple(slab) if slab is not None else ()

    def body(a_ref, b_ref, *rest):
        o_ref = rest[-1]

        @pl.when(pl.program_id(1) == 0)
        def _():
            o_ref[...] = jnp.zeros_like(o_ref)

        o_ref[...] += _dot_tn(a_ref[...].astype(BF16), b_ref[...].astype(BF16))
        if scale != 1.0:
            @pl.when(pl.program_id(1) == nt - 1)
            def _():
                o_ref[...] *= scale

    in_specs = [pl.BlockSpec((bt, m), lambda j, k: (k, 0)), pl.BlockSpec((bt, bn), lambda j, k: (k, j))]
    args = [a, b]
    if buf is not None:
        in_specs.append(_HBM)
        args.append(buf)
    return pl.pallas_call(
        body, name=name, grid=(n // bn, nt),
        in_specs=in_specs,
        out_specs=pl.BlockSpec((None,) * len(lead) + (m, bn), lambda j, k: lead + (0, j)),
        out_shape=jax.ShapeDtypeStruct(tuple(stack or ()) + (m, n), F32),
        input_output_aliases={2: 0} if buf is not None else {},
        compiler_params=_params("parallel", "arbitrary"),
    )(*args)


def _matmul_tn_blocked(a, b, name="wgrad_blk"):
    t, m = a.shape
    nb = b.shape[0]
    bt = min(1024, t)
    nbt = _pick_bn(m, nb * LANES, LANES) // LANES
    while nb % nbt:
        nbt -= 1

    def body(a_ref, b_ref, o_ref):
        @pl.when(pl.program_id(1) == 0)
        def _():
            o_ref[...] = jnp.zeros_like(o_ref)

        bv = jnp.concatenate([b_ref[i] for i in range(nbt)], axis=1) if nbt > 1 else b_ref[0]
        o_ref[...] += _dot_tn(a_ref[...], bv)

    return pl.pallas_call(
        body, name=name, grid=(nb // nbt, t // bt),
        in_specs=[pl.BlockSpec((bt, m), lambda j, k: (k, 0)), pl.BlockSpec((nbt, bt, LANES), lambda j, k: (j, k, 0))],
        out_specs=pl.BlockSpec((m, nbt * LANES), lambda j, k: (0, j)),
        out_shape=jax.ShapeDtypeStruct((m, nb * LANES), F32),
        compiler_params=_params("parallel", "arbitrary"),
    )(a, b)


def _norm_mm(x, nw, w):
    t, d = x.shape
    n = w.shape[1]
    tm = min(TOKEN_TILE, t)
    cn = 1024 if n % 1024 == 0 else n

    def body(x_ref, nw_ref, w_ref, o_ref):
        h = _rms_fwd(x_ref[...], nw_ref[...])[0].astype(BF16)
        for j in range(n // cn):
            sl = slice(j * cn, (j + 1) * cn)
            o_ref[:, sl] = _dot(h, w_ref[:, sl])

    return pl.pallas_call(
        body, name="norm_mm", grid=(t // tm,),
        in_specs=[pl.BlockSpec((tm, d), lambda i: (i, 0)), _resident((1, d)), _resident((d, n))],
        out_specs=pl.BlockSpec((tm, n), lambda i: (i, 0)),
        out_shape=jax.ShapeDtypeStruct((t, n), F32),
        compiler_params=_params("parallel"),
    )(x, nw, w)


def _ssd_inproj(x, nw, w):
    t, d = x.shape
    tm = min(TOKEN_TILE, t)
    nz, nx, ng = SSD_D_INNER // LANES, SSD_CONV_DIM // LANES, SSD_N_GROUPS
    cn = 1024

    def body(x_ref, nw_ref, w_ref, z_ref, xr_ref, dt_ref):
        h = _rms_fwd(x_ref[...], nw_ref[...])[0].astype(BF16)
        for j in range(-(-SSD_IN_PAD // cn)):
            lo, hi = j * cn, min((j + 1) * cn, SSD_IN_PAD)
            r = _dot(h, w_ref[:, lo:hi])
            for i in range((hi - lo) // LANES):
                blk = j * (cn // LANES) + i
                v = r[:, i * LANES:(i + 1) * LANES]
                if blk < nz:
                    z_ref[blk] = v
                elif blk < nz + nx:
                    xr_ref[blk - nz] = v
                else:
                    dt_ref[blk - nz - nx] = v

    out = lambda n: pl.BlockSpec((n, tm, LANES), lambda i: (0, i, 0))
    return pl.pallas_call(
        body, name="ssd_inproj", grid=(t // tm,),
        in_specs=[pl.BlockSpec((tm, d), lambda i: (i, 0)), _resident((1, d)), _resident((d, SSD_IN_PAD))],
        out_specs=[out(nz), out(nx), out(ng)],
        out_shape=[jax.ShapeDtypeStruct((n, t, LANES), F32) for n in (nz, nx, ng)],
        compiler_params=_params("parallel"),
    )(x, nw, w)


def _inproj_bwd(x, dy, nw, ws, pieces, carry=()):
    t, d = x.shape
    tm = min(TOKEN_TILE, t)
    nsteps = t // tm
    nws = len(ws)
    ncar = len(carry)
    flat = [p for group in pieces for p in group]

    def body(*refs):
        x_ref, dy_ref, nw_ref = refs[:3]
        w_refs = refs[3:3 + nws]
        p_refs = list(refs[3 + nws:3 + nws + len(flat)])
        rest = refs[3 + nws + len(flat):]
        srcs, (dx_ref, dnw_ref, h_ref) = rest[:ncar], rest[ncar:ncar + 3]
        dsts, sems = rest[ncar + 3:2 * ncar + 3], rest[2 * ncar + 3:]
        if ncar:
            @pl.when(pl.program_id(0) == 0)
            def _():
                _scatter_start(srcs, dsts, *sems)

        nwv = nw_ref[...]
        hf, xhat, inv = _rms_fwd(x_ref[...], nwv)
        h_ref[...] = hf.astype(BF16)
        dh = None
        for w_ref, group in zip(w_refs, pieces):
            parts = []
            for _ in group:
                p = p_refs.pop(0)
                parts += [p[i] for i in range(p.shape[0])] if len(p.shape) == 3 else [p[...]]
            dz = jnp.concatenate(parts, axis=1) if len(parts) > 1 else parts[0]
            part = _dot_nt(dz, w_ref[...])
            dh = part if dh is None else dh + part
        dx, dw = _rms_bwd(dh, xhat, inv, nwv)
        dx_ref[...] = dy_ref[...] + dx

        @pl.when(pl.program_id(0) == 0)
        def _():
            dnw_ref[...] = jnp.zeros_like(dnw_ref)

        dnw_ref[...] += dw

        if ncar:
            @pl.when(pl.program_id(0) == nsteps - 1)
            def _():
                _scatter_wait(srcs, dsts, *sems)

    tok = lambda m: pl.BlockSpec((tm, m), lambda i: (i, 0))
    p_specs = [pl.BlockSpec((p.shape[0], tm, LANES), lambda i: (0, i, 0)) if p.ndim == 3 else tok(p.shape[1])
               for p in flat]
    sems = [pltpu.SemaphoreType.DMA((3 * ncar,)), pltpu.SemaphoreType.DMA((3 * ncar,))] if ncar else []
    return pl.pallas_call(
        body, name="inproj_bwd_carry" if ncar else "inproj_bwd", grid=(nsteps,),
        in_specs=[tok(d), tok(d), _resident((1, d))] + [_resident(w.shape) for w in ws] + p_specs + [_HBM] * ncar,
        out_specs=[tok(d), pl.BlockSpec((1, d), lambda i: (0, 0)), tok(d)] + [_HBM] * ncar,
        out_shape=[jax.ShapeDtypeStruct((t, d), F32), jax.ShapeDtypeStruct((1, d), F32),
                   jax.ShapeDtypeStruct((t, d), BF16)]
        + [jax.ShapeDtypeStruct((3,) + c.shape[1:], c.dtype) for c in carry],
        scratch_shapes=sems,
        compiler_params=_params("arbitrary"),
    )(x, dy, nw, *ws, *flat, *carry)


def _shift_down(v, j, prev8):
    if j == 0:
        return v
    r = pltpu.roll(v, j, 0)
    p = pltpu.roll(prev8, j, 0)
    rows = lax.broadcasted_iota(jnp.int32, prev8.shape, 0)
    first = jnp.where(rows < j, p, r[0:SUBLANES])
    return jnp.concatenate([first, r[SUBLANES:]], axis=0)


def _shift_up(v, j, next8):
    if j == 0:
        return v
    n = v.shape[0]
    r = pltpu.roll(v, n - j, 0)
    p = pltpu.roll(next8, SUBLANES - j, 0)
    rows = lax.broadcasted_iota(jnp.int32, next8.shape, 0)
    last = jnp.where(rows >= SUBLANES - j, p, r[n - SUBLANES:])
    return jnp.concatenate([r[:n - SUBLANES], last], axis=0)


def _sc_fwd(x, bcu, cw, wo):
    t, d = x.shape
    tm = min(TOKEN_TILE, t)
    hb = tm // SUBLANES

    def body(x_ref, bcu_ref, prev_ref, cw_ref, wo_ref, o_ref):
        bg, cg, u = bcu_ref[:, 0:d], bcu_ref[:, d:2 * d], bcu_ref[:, 2 * d:3 * d]
        q = cg * u
        qp = jnp.where(pl.program_id(0) == 0, 0.0, prev_ref[:, d:2 * d] * prev_ref[:, 2 * d:3 * d])
        cwv = cw_ref[...]
        v = cwv[2:3] * q + cwv[1:2] * _shift_down(q, 1, qp) + cwv[0:1] * _shift_down(q, 2, qp)
        o_ref[...] = x_ref[...] + _dot((bg * v).astype(BF16), wo_ref[...])

    return pl.pallas_call(
        body, name="sc_fwd", grid=(t // tm,),
        in_specs=[pl.BlockSpec((tm, d), lambda i: (i, 0)), pl.BlockSpec((tm, 3 * d), lambda i: (i, 0)),
                  pl.BlockSpec((SUBLANES, 3 * d), lambda i: (jnp.maximum(i * hb - 1, 0), 0)),
                  _resident((SUBLANES, d)), _resident((d, d))],
        out_specs=pl.BlockSpec((tm, d), lambda i: (i, 0)),
        out_shape=jax.ShapeDtypeStruct((t, d), F32),
        compiler_params=_params("parallel"),
    )(x, bcu, bcu, cw, wo)


def _sc_bwd(dy, bcu, cw, wo):
    t, d = dy.shape
    tm = min(TOKEN_TILE, t)
    hb = tm // SUBLANES
    nt = t // tm

    def body(dy_ref, dyn_ref, bcu_ref, prev_ref, next_ref, cw_ref, wo_ref, dbcu_ref, p_ref, dcw_ref):
        i = pl.program_id(0)
        bg, cg, u = bcu_ref[:, 0:d], bcu_ref[:, d:2 * d], bcu_ref[:, 2 * d:3 * d]
        q = cg * u
        qp = jnp.where(i == 0, 0.0, prev_ref[:, d:2 * d] * prev_ref[:, 2 * d:3 * d])
        cwv = cw_ref[...]
        q1 = _shift_down(q, 1, qp)
        q2 = _shift_down(q, 2, qp)
        v = cwv[2:3] * q + cwv[1:2] * q1 + cwv[0:1] * q2
        p_ref[...] = (bg * v).astype(BF16)
        wov = wo_ref[...]
        dp = _dot_nt(dy_ref[...].astype(BF16), wov)
        dpn = _dot_nt(dyn_ref[...].astype(BF16), wov)
        dv = dp * bg
        dvn = jnp.where(i == nt - 1, 0.0, dpn * next_ref[:, 0:d])
        dq = cwv[2:3] * dv + cwv[1:2] * _shift_up(dv, 1, dvn) + cwv[0:1] * _shift_up(dv, 2, dvn)
        dbcu_ref[:, 0:d] = (dp * v).astype(BF16)
        dbcu_ref[:, d:2 * d] = (dq * u).astype(BF16)
        dbcu_ref[:, 2 * d:3 * d] = (dq * cg).astype(BF16)

        @pl.when(i == 0)
        def _():
            dcw_ref[...] = jnp.zeros_like(dcw_ref)

        dcw_ref[0:1, :] += jnp.sum(dv * q2, axis=0, keepdims=True)
        dcw_ref[1:2, :] += jnp.sum(dv * q1, axis=0, keepdims=True)
        dcw_ref[2:3, :] += jnp.sum(dv * q, axis=0, keepdims=True)

    last8 = t // SUBLANES - 1
    return pl.pallas_call(
        body, name="sc_bwd", grid=(nt,),
        in_specs=[pl.BlockSpec((tm, d), lambda i: (i, 0)),
                  pl.BlockSpec((SUBLANES, d), lambda i: (jnp.minimum((i + 1) * hb, last8), 0)),
                  pl.BlockSpec((tm, 3 * d), lambda i: (i, 0)),
                  pl.BlockSpec((SUBLANES, 3 * d), lambda i: (jnp.maximum(i * hb - 1, 0), 0)),
                  pl.BlockSpec((SUBLANES, 3 * d), lambda i: (jnp.minimum((i + 1) * hb, last8), 0)),
                  _resident((SUBLANES, d)), _resident((d, d))],
        out_specs=[pl.BlockSpec((tm, 3 * d), lambda i: (i, 0)), pl.BlockSpec((tm, d), lambda i: (i, 0)),
                   pl.BlockSpec((SUBLANES, d), lambda i: (0, 0))],
        out_shape=[jax.ShapeDtypeStruct((t, 3 * d), BF16), jax.ShapeDtypeStruct((t, d), BF16),
                   jax.ShapeDtypeStruct((SUBLANES, d), F32)],
        compiler_params=_params("arbitrary"),
    )(dy, dy, bcu, bcu, bcu, cw, wo)


NEG_BIG = -1e30


HEADS_PER_GROUP = SSD_N_HEADS // SSD_N_GROUPS
PAIRS_PER_GROUP = HEADS_PER_GROUP // 2
GROUPS_PER_STEP = 4


def _ssd_consts():
    r = lax.broadcasted_iota(jnp.int32, (LANES, LANES), 0)
    c = lax.broadcasted_iota(jnp.int32, (LANES, LANES), 1)
    return (c <= r).astype(BF16), (c >= r).astype(BF16)


def _ssd_decay(dtr, dtb, alog, tril):
    shape = (SSD_CHUNK, LANES)
    lanes = lax.broadcasted_iota(jnp.int32, shape, 1)
    rows = lax.broadcasted_iota(jnp.int32, shape, 0)
    pre = dtr + dtb
    valid = lanes < HEADS_PER_GROUP
    dt = jnp.where(valid, jnp.maximum(pre, 0.0) + jnp.log(1.0 + jnp.exp(-jnp.abs(pre))), 0.0)
    a = -jnp.exp(alog)
    acs = _sel_left(tril, _split3(dt * a))
    return dt, a, acs, pre, valid, rows, lanes


def _lane_col(v, j):
    return jnp.broadcast_to(v[:, j:j + 1], v.shape)


def _ssd_pair_terms(k, dt, cols, low_half, xs):
    dtp = jnp.where(low_half, _lane_col(dt, 2 * k), _lane_col(dt, 2 * k + 1))
    acsp = jnp.where(low_half, cols[2 * k], cols[2 * k + 1])
    lastp = acsp[SSD_CHUNK - 1:SSD_CHUNK, :]
    eap = jnp.exp(acsp)
    decp = jnp.exp(lastp - acsp)
    etp = jnp.exp(lastp)
    xdt = xs * dtp
    return dtp, eap, decp, etp, xdt


def _ssd_conv_taps(cwb, xr, prev8):
    sh = [_shift_down(xr, j, prev8) for j in range(SSD_CONV_W)]
    xc = cwb[4:5]
    for j in range(SSD_CONV_W):
        xc = xc + cwb[3 - j:4 - j] * sh[j]
    return xc, sh


def _ssd_specs(nc, rev):
    ch = (lambda i: nc - 1 - i) if rev else (lambda i: i)
    L = SSD_CHUNK
    gps = GROUPS_PER_STEP
    b0, c0 = N_XS_BLK // gps, (N_XS_BLK + SSD_N_GROUPS) // gps
    xs = pl.BlockSpec((4 * gps, L, LANES), lambda g, i: (g, ch(i), 0))
    bb = pl.BlockSpec((gps, L, LANES), lambda g, i: (b0 + g, ch(i), 0))
    cc = pl.BlockSpec((gps, L, LANES), lambda g, i: (c0 + g, ch(i), 0))
    dt = pl.BlockSpec((gps, L, LANES), lambda g, i: (g, ch(i), 0))
    cw_xs = pl.BlockSpec((4 * gps, SUBLANES, LANES), lambda g, i: (g, 0, 0))
    cw_b = pl.BlockSpec((gps, SUBLANES, LANES), lambda g, i: (b0 + g, 0, 0))
    cw_c = pl.BlockSpec((gps, SUBLANES, LANES), lambda g, i: (c0 + g, 0, 0))
    st = pl.BlockSpec((1, 4 * gps, LANES, LANES), lambda g, i: (ch(i), g, 0, 0))
    grp4 = pl.BlockSpec((4 * gps, L, LANES), lambda g, i: (g, ch(i), 0))
    return xs, bb, cc, dt, cw_xs, cw_b, cw_c, st, grp4


def _group_views(q, refs4, refs1, refs6=()):
    return ([r.at[pl.ds(4 * q, 4)] for r in refs4], [r.at[pl.ds(q, 1)] for r in refs1],
            [r.at[pl.ds(6 * q, 6)] for r in refs6])


def _ssd_scan_fwd(xr, dtr, cwb, dtb, alog, dskip, consts):
    t = xr.shape[1]
    L = SSD_CHUNK
    nc = t // L
    tril, _ = consts
    xs_s, b_s, c_s, dt_s, cwx_s, cwb_s, cwc_s, st_s, grp4 = _ssd_specs(nc, False)
    gps = GROUPS_PER_STEP
    grp_row = pl.BlockSpec((gps, 1, LANES), lambda g, i: (g, 0, 0))

    def body(xs_all, b_all, c_all, dtr_all, cwx_all, cwbb_all, cwc_all, dtb_all, alog_all, dsk_all,
             tril_ref, y_all, sp_all, state_all, tail_all):
        @pl.when(pl.program_id(1) == 0)
        def _():
            state_all[...] = jnp.zeros_like(state_all)
            tail_all[...] = jnp.zeros_like(tail_all)

        for q in range(gps):
            fours, ones, sixes = _group_views(q, (xs_all, cwx_all, dsk_all, y_all, sp_all.at[0], state_all),
                                              (b_all, c_all, dtr_all, cwbb_all, cwc_all, dtb_all, alog_all), (tail_all,))
            group(*fours, *ones, *sixes, tril_ref)

    def group(xs_ref, cwx_ref, dsk_ref, y_ref, sp_ref, state, b_ref, c_ref, dtr_ref, cwbb_ref, cwc_ref,
              dtb_ref, alog_ref, tail, tril_ref):
        xa = []
        for b in range(6):
            xrb = xs_ref[b] if b < 4 else (b_ref[0] if b == 4 else c_ref[0])
            cw = cwx_ref[b] if b < 4 else (cwbb_ref[0] if b == 4 else cwc_ref[0])
            xc, _ = _ssd_conv_taps(cw, xrb, tail[b])
            tail[b] = xrb[L - SUBLANES:]
            xa.append(xc * _sigmoid(xc))

        dt, a, acs, _, _, rows, lanes = _ssd_decay(dtr_ref[0], dtb_ref[0], alog_ref[0], tril_ref[...])
        acst = acs.T
        bb = xa[4].astype(BF16)
        cb_ = xa[5].astype(BF16)
        cbm = _dot_nt(cb_, bb)
        causal = rows >= lanes
        low_half = lanes < LANES // 2
        cols = [_lane_col(acs, j) for j in range(HEADS_PER_GROUP)]

        for k in range(PAIRS_PER_GROUP):
            xs = xa[k]
            dtp, eap, decp, etp, xdt = _ssd_pair_terms(k, dt, cols, low_half, xs)
            ms = []
            for j in (2 * k, 2 * k + 1):
                diff = cols[j] - jnp.broadcast_to(acst[j:j + 1, :], (L, L))
                ms.append((cbm * jnp.exp(jnp.where(causal, diff, NEG_BIG))).astype(BF16))
            xcat = jnp.concatenate([jnp.where(low_half, xdt, 0.0).astype(BF16),
                                    jnp.where(low_half, 0.0, xdt).astype(BF16)], axis=0)
            yd = _dot(jnp.concatenate(ms, axis=1), xcat)
            sp = state[k]
            yo = eap * _dot(cb_, sp.astype(BF16))
            y_ref[k] = yd + yo + dsk_ref[k][0:1] * xs
            sp_ref[k] = sp
            state[k] = etp * sp + _dot_tn(bb, (decp * xdt).astype(BF16))

    return pl.pallas_call(
        body, name="ssd_scan_fwd", grid=(SSD_N_GROUPS // gps, nc),
        in_specs=[xs_s, b_s, c_s, dt_s, cwx_s, cwb_s, cwc_s, grp_row, grp_row, cwx_s, _resident(tril.shape)],
        out_specs=[grp4, st_s],
        out_shape=[jax.ShapeDtypeStruct((N_XS_BLK, t, LANES), F32),
                   jax.ShapeDtypeStruct((nc, N_XS_BLK, LANES, LANES), F32)],
        scratch_shapes=[pltpu.VMEM((4 * gps, LANES, LANES), F32), pltpu.VMEM((6 * gps, SUBLANES, LANES), F32)],
        compiler_params=_params("arbitrary", "arbitrary"),
    )(xr, xr, xr, dtr, cwb, cwb, cwb, dtb, alog, dskip, tril)


def _ssd_scan_bwd(xr, dtr, dy, sprev, cwb, dtb, alog, dskip, consts):
    t = xr.shape[1]
    L = SSD_CHUNK
    nc = t // L
    hb = L // SUBLANES
    tril, triu = consts
    xs_s, b_s, c_s, dt_s, cwx_s, cwb_s, cwc_s, st_s, grp4 = _ssd_specs(nc, True)
    gps = GROUPS_PER_STEP
    grp_row = pl.BlockSpec((gps, 1, LANES), lambda g, i: (g, 0, 0))
    prev = lambda off: pl.BlockSpec(
        (4 * gps if off is None else gps, SUBLANES, LANES),
        (lambda g, i: (g, jnp.maximum((nc - 1 - i) * hb - 1, 0), 0)) if off is None else
        (lambda g, i: (off // gps + g, jnp.maximum((nc - 1 - i) * hb - 1, 0), 0)))
    grp1 = pl.BlockSpec((gps, L, LANES), lambda g, i: (g, nc - 1 - i, 0))
    acc4 = pl.BlockSpec((4 * gps, SUBLANES, LANES), lambda g, i: (g, 0, 0))
    acc1 = pl.BlockSpec((gps, SUBLANES, LANES), lambda g, i: (g, 0, 0))

    def body(xs_all, b_all, c_all, pxs_all, pb_all, pc_all, dtr_all, dy_all, sp_all,
             cwx_all, cwbb_all, cwc_all, dtb_all, alog_all, dsk_all, tril_ref, triu_ref,
             dxs_all, db_all, dc_all, ddtr_all, dcwx_all, dcwb_all, dcwc_all, dd_all, dsm_all,
             dstate_all, head_all):
        @pl.when(pl.program_id(1) == 0)
        def _():
            for r in (dstate_all, head_all, dcwx_all, dcwb_all, dcwc_all, dd_all, dsm_all):
                r[...] = jnp.zeros_like(r)

        for q in range(gps):
            fours, ones, sixes = _group_views(
                q, (xs_all, pxs_all, dy_all, sp_all.at[0], cwx_all, dsk_all, dxs_all, dcwx_all, dd_all, dstate_all),
                (b_all, c_all, pb_all, pc_all, dtr_all, cwbb_all, cwc_all, dtb_all, alog_all, db_all, dc_all, ddtr_all,
                 dcwb_all, dcwc_all, dsm_all), (head_all,))
            group(*fours, *ones, *sixes, tril_ref, triu_ref)

    def group(xs_ref, pxs_ref, dy_ref, sp_ref, cwx_ref, dsk_ref, dxs_ref, dcwx_ref, dd_ref, dstate,
              b_ref, c_ref, pb_ref, pc_ref, dtr_ref, cwbb_ref, cwc_ref, dtb_ref, alog_ref, db_ref, dc_ref, ddtr_ref,
              dcwb_ref, dcwc_ref, dsm_ref, head, tril_ref, triu_ref):
        first_chunk = pl.program_id(1) == nc - 1

        def blk(b):
            xrb = xs_ref[b] if b < 4 else (b_ref[0] if b == 4 else c_ref[0])
            cw = cwx_ref[b] if b < 4 else (cwbb_ref[0] if b == 4 else cwc_ref[0])
            p8 = pxs_ref[b] if b < 4 else (pb_ref[0] if b == 4 else pc_ref[0])
            return xrb, cw, jnp.where(first_chunk, 0.0, p8)

        xa, dsil = [], []
        for b in range(6):
            xrb, cw, p8 = blk(b)
            xc, _ = _ssd_conv_taps(cw, xrb, p8)
            sig = _sigmoid(xc)
            xa.append(xc * sig)
            dsil.append(sig * (1.0 + xc * (1.0 - sig)))

        dt, a, acs, pre, valid, rows, lanes = _ssd_decay(dtr_ref[0], dtb_ref[0], alog_ref[0], tril_ref[...])
        acst = acs.T
        bb = xa[4].astype(BF16)
        cb_ = xa[5].astype(BF16)
        cbm = _dot_nt(cb_, bb)
        cbmt = _dot_nt(bb, cb_)
        causal = rows >= lanes
        anti = rows <= lanes
        low_half = lanes < LANES // 2
        last_row = rows == L - 1
        cols = [_lane_col(acs, j) for j in range(HEADS_PER_GROUP)]
        zeros = jnp.zeros((L, LANES), F32)
        dcb, dcbt, dbg, dcg, dacs, dacst, ddt = zeros, zeros, zeros, zeros, zeros, zeros, zeros
        dxa = []

        for k in range(PAIRS_PER_GROUP):
            xs = xa[k]
            dtp, eap, decp, etp, xdt = _ssd_pair_terms(k, dt, cols, low_half, xs)
            xdtb = xdt.astype(BF16)
            w = decp * xdt
            wb = w.astype(BF16)
            dyv = dy_ref[k]
            sp = sp_ref[k]
            spb = sp.astype(BF16)
            dsn = dstate[k]
            dsnb = dsn.astype(BF16)
            yoff = eap * _dot(cb_, spb)
            dgb = (eap * dyv).astype(BF16)
            dcg = dcg + _dot_nt(dgb, spb)
            dstate[k] = _dot_tn(cb_, dgb) + etp * dsn
            last_lane = etp * jnp.sum(dsn * sp, axis=0, keepdims=True)
            dbg = dbg + _dot_nt(wb, dsnb)
            dw = _dot(bb, dsnb)
            t2 = dw * w
            dxdt = decp * dw
            last_lane = last_lane + jnp.sum(t2, axis=0, keepdims=True)
            lane_acc = dyv * yoff - t2 + jnp.where(last_row, last_lane, 0.0)
            for j in (2 * k, 2 * k + 1):
                diff = cols[j] - jnp.broadcast_to(acst[j:j + 1, :], (L, L))
                lm = jnp.exp(jnp.where(causal, diff, NEG_BIG))
                lmt = jnp.exp(jnp.where(anti, -diff, NEG_BIG))
                dye = jnp.where(low_half == (j % 2 == 0), dyv, 0.0).astype(BF16)
                dm = _dot_nt(dye, xdtb)
                dmt = _dot_nt(xdtb, dye)
                mt = cbmt * lmt
                seg = dmt * mt - dm * (cbm * lm)
                dacst = dacst + jnp.where(rows == j, jnp.sum(seg, axis=0, keepdims=True), 0.0)
                dcb = dcb + dm * lm
                dcbt = dcbt + dmt * lmt
                dxdt = dxdt + _dot(mt.astype(BF16), dye)
            ddt_lane = dxdt * xs
            for j, keep in ((2 * k, low_half), (2 * k + 1, jnp.logical_not(low_half))):
                dacs = dacs + jnp.where(lanes == j, jnp.sum(jnp.where(keep, lane_acc, 0.0), axis=1, keepdims=True), 0.0)
                ddt = ddt + jnp.where(lanes == j, jnp.sum(jnp.where(keep, ddt_lane, 0.0), axis=1, keepdims=True), 0.0)
            dxa.append(dsk_ref[k][0:1] * dyv + dxdt * dtp)
            dd_ref[k, 0:1, :] += jnp.sum(dyv * xs, axis=0, keepdims=True)

        dxa.append(dbg + _dot(dcbt.astype(BF16), cb_))
        dxa.append(dcg + _dot(dcb.astype(BF16), bb))
        dac = _sel_left(triu_ref[...], _split3(dacs + dacst.T))
        ddtr = jnp.where(valid, (ddt + dac * a) * _sigmoid(pre), 0.0)
        ddtr_ref[0] = ddtr.astype(BF16)
        dsm_ref[0, 0:1, :] += jnp.sum(ddtr, axis=0, keepdims=True)
        dsm_ref[0, 1:2, :] += jnp.sum(dac * dt, axis=0, keepdims=True) * a

        for b in range(6):
            xrb, cw, p8 = blk(b)
            sh = [_shift_down(xrb, j, p8) for j in range(SSD_CONV_W)]
            dxc = dxa[b] * dsil[b]
            acc = dcwx_ref.at[b] if b < 4 else (dcwb_ref.at[0] if b == 4 else dcwc_ref.at[0])
            acc[4:5, :] += jnp.sum(dxc, axis=0, keepdims=True)
            dxr = jnp.zeros_like(dxc)
            for j in range(SSD_CONV_W):
                acc[3 - j:4 - j, :] += jnp.sum(dxc * sh[j], axis=0, keepdims=True)
                dxr = dxr + cw[3 - j:4 - j] * _shift_up(dxc, j, head[b])
            head[b] = dxc[0:SUBLANES]
            out = dxs_ref.at[b] if b < 4 else (db_ref.at[0] if b == 4 else dc_ref.at[0])
            out[...] = dxr.astype(BF16)

    return pl.pallas_call(
        body, name="ssd_scan_bwd", grid=(SSD_N_GROUPS // gps, nc),
        in_specs=[xs_s, b_s, c_s, prev(None), prev(N_XS_BLK), prev(N_XS_BLK + SSD_N_GROUPS), dt_s, grp4, st_s,
                  cwx_s, cwb_s, cwc_s, grp_row, grp_row, cwx_s, _resident(tril.shape), _resident(triu.shape)],
        out_specs=[grp4, grp1, grp1, grp1, acc4, acc1, acc1, acc4, acc1],
        out_shape=[jax.ShapeDtypeStruct((N_XS_BLK, t, LANES), BF16),
                   jax.ShapeDtypeStruct((SSD_N_GROUPS, t, LANES), BF16),
                   jax.ShapeDtypeStruct((SSD_N_GROUPS, t, LANES), BF16),
                   jax.ShapeDtypeStruct((SSD_N_GROUPS, t, LANES), BF16),
                   jax.ShapeDtypeStruct((N_XS_BLK, SUBLANES, LANES), F32),
                   jax.ShapeDtypeStruct((SSD_N_GROUPS, SUBLANES, LANES), F32),
                   jax.ShapeDtypeStruct((SSD_N_GROUPS, SUBLANES, LANES), F32),
                   jax.ShapeDtypeStruct((N_XS_BLK, SUBLANES, LANES), F32),
                   jax.ShapeDtypeStruct((SSD_N_GROUPS, SUBLANES, LANES), F32)],
        scratch_shapes=[pltpu.VMEM((4 * gps, LANES, LANES), F32), pltpu.VMEM((6 * gps, SUBLANES, LANES), F32)],
        compiler_params=_params("arbitrary", "arbitrary"),
    )(xr, xr, xr, xr, xr, xr, dtr, dy, sprev, cwb, cwb, cwb, dtb, alog, dskip, tril, triu)


def _ssd_gate_fwd(x, y, z, gnw, wo):
    t, d = x.shape
    tm = min(TOKEN_TILE, t)
    nb = N_XS_BLK
    per = nb // SSD_N_GROUPS

    def body(x_ref, y_ref, z_ref, gnw_ref, wo_ref, o_ref, gn_ref):
        gs = []
        for j in range(nb):
            zv = z_ref[j]
            gs.append(y_ref[j] * (zv * _sigmoid(zv)))
        for q in range(SSD_N_GROUPS):
            ss = sum(jnp.sum(gs[j] * gs[j], axis=1, keepdims=True) for j in range(q * per, (q + 1) * per))
            inv = lax.rsqrt(ss / (per * LANES) + RMS_EPS)
            for j in range(q * per, (q + 1) * per):
                gn_ref[:, j * LANES:(j + 1) * LANES] = ((gs[j] * inv) * gnw_ref[j]).astype(BF16)
        o_ref[...] = x_ref[...] + _dot(gn_ref[...], wo_ref[...])

    blk = pl.BlockSpec((nb, tm, LANES), lambda i: (0, i, 0))
    return pl.pallas_call(
        body, name="ssd_gate_fwd", grid=(t // tm,),
        in_specs=[pl.BlockSpec((tm, d), lambda i: (i, 0)), blk, blk, _resident((nb, 1, LANES)),
                  _resident((SSD_D_INNER, d))],
        out_specs=[pl.BlockSpec((tm, d), lambda i: (i, 0)), pl.BlockSpec((tm, SSD_D_INNER), lambda i: (i, 0))],
        out_shape=[jax.ShapeDtypeStruct((t, d), F32), jax.ShapeDtypeStruct((t, SSD_D_INNER), BF16)],
        compiler_params=_params("parallel"),
    )(x, y, z, gnw, wo)


def _ssd_gate_bwd(dy, y, z, gnw, wo):
    t, d = dy.shape
    tm = min(TOKEN_TILE, t)
    nb = N_XS_BLK
    per = nb // SSD_N_GROUPS

    def body(dy_ref, y_ref, z_ref, gnw_ref, wo_ref, dys_ref, dz_ref, dgnw_ref):
        @pl.when(pl.program_id(0) == 0)
        def _():
            dgnw_ref[...] = jnp.zeros_like(dgnw_ref)

        dgn = _dot_nt(dy_ref[...].astype(BF16), wo_ref[...])
        for q in range(SSD_N_GROUPS):
            js = range(q * per, (q + 1) * per)
            gs, sil, dsil = {}, {}, {}
            for j in js:
                zv = z_ref[j]
                sig = _sigmoid(zv)
                sil[j] = zv * sig
                dsil[j] = sig * (1.0 + zv * (1.0 - sig))
                gs[j] = y_ref[j] * sil[j]
            ss = sum(jnp.sum(gs[j] * gs[j], axis=1, keepdims=True) for j in js)
            inv = lax.rsqrt(ss / (per * LANES) + RMS_EPS)
            ghat = {j: gs[j] * inv for j in js}
            dgh = {}
            for j in js:
                dj = dgn[:, j * LANES:(j + 1) * LANES]
                dgnw_ref[j] += jnp.sum(dj * ghat[j], axis=0, keepdims=True)
                dgh[j] = dj * gnw_ref[j]
            mean = sum(jnp.sum(dgh[j] * ghat[j], axis=1, keepdims=True) for j in js) / (per * LANES)
            for j in js:
                dg = inv * (dgh[j] - ghat[j] * mean)
                dys_ref[j] = dg * sil[j]
                dz_ref[j] = (dg * y_ref[j] * dsil[j]).astype(BF16)

    blk = pl.BlockSpec((nb, tm, LANES), lambda i: (0, i, 0))
    return pl.pallas_call(
        body, name="ssd_gate_bwd", grid=(t // tm,),
        in_specs=[pl.BlockSpec((tm, d), lambda i: (i, 0)), blk, blk, _resident((nb, 1, LANES)),
                  _resident((SSD_D_INNER, d))],
        out_specs=[blk, blk, pl.BlockSpec((nb, 1, LANES), lambda i: (0, 0, 0))],
        out_shape=[jax.ShapeDtypeStruct((nb, t, LANES), F32), jax.ShapeDtypeStruct((nb, t, LANES), BF16),
                   jax.ShapeDtypeStruct((nb, 1, LANES), F32)],
        compiler_params=_params("arbitrary"),
    )(dy, y, z, gnw, wo)


def _lane_blocks(v):
    r, n = v.shape[0], v.shape[1] // LANES
    return v.reshape(r, n, LANES).transpose(1, 0, 2)


def _ssd_prep(w_in, conv_w, conv_b, dt_bias, a_log, d_skip, norm_w):
    n_main = SSD_D_INNER + SSD_CONV_DIM
    w_dt = w_in[:, n_main:].reshape(-1, SSD_N_GROUPS, HEADS_PER_GROUP)
    w_dt = jnp.pad(w_dt, ((0, 0), (0, 0), (0, LANES - HEADS_PER_GROUP))).reshape(-1, SSD_N_GROUPS * LANES)
    w_in_pad = jnp.concatenate([w_in[:, :n_main], w_dt], axis=1)
    taps = jnp.concatenate([conv_w, conv_b[None], jnp.zeros((SUBLANES - SSD_CONV_W - 1, SSD_CONV_DIM), F32)], axis=0)
    cwb = _lane_blocks(taps)
    row = lambda v: jnp.pad(v.reshape(SSD_N_GROUPS, 1, HEADS_PER_GROUP), ((0, 0), (0, 0), (0, LANES - HEADS_PER_GROUP)))
    dskip = jnp.broadcast_to(jnp.repeat(d_skip, SSD_D_INNER // SSD_N_HEADS).reshape(N_XS_BLK, 1, LANES),
                             (N_XS_BLK, SUBLANES, LANES))
    gnw = norm_w.reshape(N_XS_BLK, 1, LANES)
    return w_in_pad, cwb, row(dt_bias), row(a_log), dskip, gnw


def _ssd_layer_fwd(x, nw, prm, wo, consts):
    w_in_pad, cwb, dtb, alog, dskip, gnw = prm
    z, xr, dtr = _ssd_inproj(x, nw, w_in_pad)
    y, sprev = _ssd_scan_fwd(xr, dtr, cwb, dtb, alog, dskip, consts)
    out, gn = _ssd_gate_fwd(x, y, z, gnw, wo)
    return out, (z, xr, dtr, y, sprev, gn)


def _ssd_layer_bwd(x, dy, nw, prm, wo, consts, saved):
    w_in_pad, cwb, dtb, alog, dskip, gnw = prm
    z, xr, dtr, y, sprev, gn = saved
    dys, dz, dgnw = _ssd_gate_bwd(dy, y, z, gnw, wo)
    dwo = _matmul_tn(gn, dy, name="wgrad_ssd_out")
    dxs, db, dc, ddtr, dcwx, dcwb, dcwc, dd, dsm = _ssd_scan_bwd(xr, dtr, dys, sprev, cwb, dtb, alog, dskip, consts)
    pieces = [dz, dxs, db, dc, ddtr]
    dx, dnw, h = _inproj_bwd(x, dy, nw, [w_in_pad], [pieces])
    dws = [_matmul_tn_blocked(h, p, name=f"wgrad_ssd_in{i}") for i, p in enumerate(pieces)]
    dw_dt = dws[4].reshape(-1, SSD_N_GROUPS, LANES)[:, :, :HEADS_PER_GROUP].reshape(-1, SSD_N_HEADS)
    dw_in = jnp.concatenate(dws[:4] + [dw_dt], axis=1)
    dtaps = jnp.concatenate([dcwx, dcwb, dcwc], axis=0).transpose(1, 0, 2).reshape(SUBLANES, SSD_CONV_DIM)
    by_head = lambda r: dsm[:, r, :HEADS_PER_GROUP].reshape(SSD_N_HEADS)
    d_d = jnp.sum(dd[:, 0, :].reshape(SSD_N_HEADS, SSD_D_INNER // SSD_N_HEADS), axis=1)
    return dx, (dnw, dw_in, dtaps[:SSD_CONV_W], dtaps[SSD_CONV_W], by_head(0), by_head(1),
                d_d, dgnw.reshape(SSD_D_INNER), dwo)


def _loss_head(x, fw, target):
    t, d = x.shape
    tm = min(TOKEN_TILE, t)

    def body(x_ref, fw_ref, tgt_ref, loss_ref, dx_ref, dfw_ref):
        fwv = fw_ref[...]
        y, xhat, inv = _rms_fwd(x_ref[...], fwv)
        err = y - tgt_ref[...]
        tot = jnp.sum(jnp.sum(err * err, axis=1, keepdims=True), axis=0, keepdims=True)
        dx, dw = _rms_bwd(err * (1.0 / d), xhat, inv, fwv)
        dx_ref[...] = dx

        @pl.when(pl.program_id(0) == 0)
        def _():
            loss_ref[...] = jnp.zeros_like(loss_ref)
            dfw_ref[...] = jnp.zeros_like(dfw_ref)

        loss_ref[...] += jnp.broadcast_to(tot * (0.5 / d), loss_ref.shape)
        dfw_ref[...] += dw

    tok = pl.BlockSpec((tm, d), lambda i: (i, 0))
    return pl.pallas_call(
        body, name="loss_head", grid=(t // tm,),
        in_specs=[tok, _resident((1, d)), tok],
        out_specs=[pl.BlockSpec((1, LANES), lambda i: (0, 0)), tok, pl.BlockSpec((1, d), lambda i: (0, 0))],
        out_shape=[jax.ShapeDtypeStruct((1, LANES), F32), jax.ShapeDtypeStruct((t, d), F32),
                   jax.ShapeDtypeStruct((1, d), F32)],
        compiler_params=_params("arbitrary"),
    )(x, fw, target)


def _row_tile(rows, cap):
    best = SUBLANES
    for r in range(SUBLANES, min(rows, cap) + 1, SUBLANES):
        if rows % r == 0:
            best = r
    return best


def _adamw(w, g, m, v, name):
    rows, cols = w.shape
    br = _row_tile(rows, 256)
    c1 = 1.0 - ADAM_B1 ** ADAM_STEP
    c2 = 1.0 - ADAM_B2 ** ADAM_STEP

    def body(w_ref, g_ref, m_ref, v_ref, d_ref, nm_ref, nv_ref):
        gv = g_ref[...]
        nm = ADAM_B1 * m_ref[...] + (1.0 - ADAM_B1) * gv
        nv = ADAM_B2 * v_ref[...] + (1.0 - ADAM_B2) * (gv * gv)
        nm_ref[...] = nm
        nv_ref[...] = nv
        d_ref[...] = -ADAM_LR * ((nm / c1) / (jnp.sqrt(nv / c2) + ADAM_EPS) + ADAM_WD * w_ref[...])

    blk = pl.BlockSpec((br, cols), lambda i: (i, 0))
    shp = jax.ShapeDtypeStruct((rows, cols), F32)
    return pl.pallas_call(
        body, name=name, grid=(rows // br,), in_specs=[blk] * 4, out_specs=[blk] * 3, out_shape=[shp] * 3,
        compiler_params=_params("parallel"),
    )(w, g, m, v)


def _place():
    x, y, c = lax.axis_index("x"), lax.axis_index("y"), lax.axis_index("c")
    return x, y, c, [(1 - x, y), (x, 1 - y), (1 - x, 1 - y)]


def _remote(src, dst, send_sems, recv_sems, k, to):
    return pltpu.make_async_remote_copy(src_ref=src, dst_ref=dst, send_sem=send_sems.at[k], recv_sem=recv_sems.at[k],
                                        device_id=to, device_id_type=MESH)


def _scatter_copies(srcs, dsts, send_sems, recv_sems):
    x, y, c, chips = _place()
    sends, arrivals = [], []
    for oi, (src, dst) in enumerate(zip(srcs, dsts)):
        for j, chip in enumerate(chips):
            sends.append(_remote(src.at[2 * chip[0] + chip[1]], dst.at[j], send_sems, recv_sems, 3 * oi + j, (*chip, c)))
            arrivals.append(_remote(dst.at[j], dst.at[j], send_sems, recv_sems, 3 * oi + j, (*chip, c)))
    return sends, arrivals


def _swap_copies(srcs, dsts, send_sems, recv_sems):
    x, y, c, _ = _place()
    return [_remote(src.at[:, 1 - c], dst, send_sems, recv_sems, oi, (x, y, 1 - c))
            for oi, (src, dst) in enumerate(zip(srcs, dsts))]


def _scatter_start(srcs, dsts, send_sems, recv_sems):
    for cp in _scatter_copies(srcs, dsts, send_sems, recv_sems)[0]:
        cp.start()


def _scatter_wait(srcs, dsts, send_sems, recv_sems):
    sends, arrivals = _scatter_copies(srcs, dsts, send_sems, recv_sems)
    for cp in arrivals:
        cp.wait_recv()
    for cp in sends:
        cp.wait_send()


def _all_gather_shards(arrs):
    n = len(arrs)

    def body(*refs):
        srcs, dsts = refs[:n], refs[n:2 * n]
        send_sems, recv_sems = refs[2 * n:]
        x, y, c, chips = _place()
        me = 2 * x + y
        sibling = (x, y, 1 - c)
        sent = []
        for oi, (src, dst) in enumerate(zip(srcs, dsts)):
            for j, chip in enumerate(chips):
                sent.append(_remote(src.at[c], dst.at[me, c], send_sems, recv_sems, 6 * oi + j, (*chip, c)))
                sent[-1].start()
        for oi, dst in enumerate(dsts):
            for j, chip in enumerate(chips):
                landed = dst.at[2 * chip[0] + chip[1], c]
                _remote(landed, landed, send_sems, recv_sems, 6 * oi + j, (*chip, c)).wait_recv()
                sent.append(_remote(landed, landed, send_sems, recv_sems, 6 * oi + 3 + j, sibling))
                sent[-1].start()
        for oi, dst in enumerate(dsts):
            for j, chip in enumerate(chips):
                landed = dst.at[2 * chip[0] + chip[1], 1 - c]
                _remote(landed, landed, send_sems, recv_sems, 6 * oi + 3 + j, sibling).wait_recv()
        for cp in sent:
            cp.wait_send()

    return pl.pallas_call(
        body, name="all_gather_shards",
        in_specs=[_HBM] * n, out_specs=[_HBM] * n,
        out_shape=[jax.ShapeDtypeStruct((N_SHARDS,) + a.shape, a.dtype) for a in arrs],
        scratch_shapes=[pltpu.SemaphoreType.DMA((6 * n,)), pltpu.SemaphoreType.DMA((6 * n,))],
    )(*arrs)


def _swap_halves(arrs):
    n = len(arrs)

    def body(*refs):
        srcs, dsts = refs[:n], refs[n:2 * n]
        cps = _swap_copies(srcs, dsts, *refs[2 * n:])
        for cp in cps:
            cp.start()
        for cp in cps:
            cp.wait()

    return pl.pallas_call(
        body, name="swap_halves", in_specs=[_HBM] * n, out_specs=[_HBM] * n,
        out_shape=[jax.ShapeDtypeStruct((a.shape[0],) + a.shape[2:], a.dtype) for a in arrs],
        scratch_shapes=[pltpu.SemaphoreType.DMA((n,)), pltpu.SemaphoreType.DMA((n,))],
    )(*arrs)


def _scatter_to_chips(arrs):
    n = len(arrs)

    def body(*refs):
        srcs, dsts = refs[:n], refs[n:2 * n]
        _scatter_start(srcs, dsts, *refs[2 * n:])
        _scatter_wait(srcs, dsts, *refs[2 * n:])

    return pl.pallas_call(
        body, name="scatter_to_chips", in_specs=[_HBM] * n, out_specs=[_HBM] * n,
        out_shape=[jax.ShapeDtypeStruct((3,) + a.shape[1:], a.dtype) for a in arrs],
        scratch_shapes=[pltpu.SemaphoreType.DMA((3 * n,)), pltpu.SemaphoreType.DMA((3 * n,))],
    )(*arrs)


def _join_halves(arrs):
    n = len(arrs)

    def body(*refs):
        bufs = refs[n:2 * n]
        send_sems, recv_sems = refs[2 * n:]
        x, y, c, _ = _place()
        sibling = (x, y, 1 - c)
        sent = [_remote(buf.at[c], buf.at[c], send_sems, recv_sems, oi, sibling) for oi, buf in enumerate(bufs)]
        for cp in sent:
            cp.start()
        for oi, buf in enumerate(bufs):
            _remote(buf.at[1 - c], buf.at[1 - c], send_sems, recv_sems, oi, sibling).wait_recv()
        for cp in sent:
            cp.wait_send()

    return pl.pallas_call(
        body, name="join_halves", in_specs=[_HBM] * n, out_specs=[_HBM] * n,
        out_shape=[jax.ShapeDtypeStruct(a.shape, a.dtype) for a in arrs],
        input_output_aliases={i: i for i in range(n)},
        scratch_shapes=[pltpu.SemaphoreType.DMA((n,)), pltpu.SemaphoreType.DMA((n,))],
    )(*arrs)


def _add_halves(full, recv, place):
    n, _, rows, cols = full.shape
    br = _row_tile(rows, 512)

    def body(p_ref, a_ref, b_ref, o_ref):
        o_ref[...] = (a_ref[...] + b_ref[...]).astype(BF16)

    grid_spec = pltpu.PrefetchScalarGridSpec(
        num_scalar_prefetch=1, grid=(n, rows // br),
        in_specs=[pl.BlockSpec((None, None, br, cols), lambda s, i, p_ref: (s, p_ref[1], i, 0)),
                  pl.BlockSpec((None, br, cols), lambda s, i, p_ref: (s, i, 0))],
        out_specs=pl.BlockSpec((None, br, cols), lambda s, i, p_ref: (s, i, 0)))
    return pl.pallas_call(
        body, name="add_halves", grid_spec=grid_spec, out_shape=jax.ShapeDtypeStruct((n, rows, cols), BF16),
        compiler_params=_params("parallel", "parallel"),
    )(place, full, recv)


def _sum_chips(mine, others, place):
    _, rows, cols = mine.shape
    br = _row_tile(rows, 512)
    slot_of_flip = {2: 0, 1: 1, 3: 2}

    def body(p_ref, m_ref, o_ref, out_ref):
        me = p_ref[0]
        own = m_ref[...].astype(F32)
        got = [o_ref[j].astype(F32) for j in range(3)]
        acc = None
        for s in range(N_SHARDS):
            flip = jnp.bitwise_xor(me, s)
            term = own
            for f, j in slot_of_flip.items():
                term = jnp.where(flip == f, got[j], term)
            acc = term if acc is None else acc + term
        out_ref[...] = acc

    grid_spec = pltpu.PrefetchScalarGridSpec(
        num_scalar_prefetch=1, grid=(rows // br,),
        in_specs=[pl.BlockSpec((None, br, cols), lambda i, p_ref: (p_ref[0], i, 0)),
                  pl.BlockSpec((3, br, cols), lambda i, p_ref: (0, i, 0))],
        out_specs=pl.BlockSpec((None, br, cols), lambda i, p_ref: (p_ref[1], i, 0)))
    return pl.pallas_call(
        body, name="sum_chips", grid_spec=grid_spec, out_shape=jax.ShapeDtypeStruct((2, rows, cols), F32),
        compiler_params=_params("parallel"),
    )(place, mine, others)


WEIGHTS = ("norm_w", "ffn_w_gate", "ffn_w_up", "ffn_w_down", "ssd_w_in", "ssd_conv_w", "ssd_conv_b", "ssd_dt_bias",
           "ssd_a_log", "ssd_d", "ssd_norm_w", "ssd_w_out", "sc_w_in", "sc_conv_w", "sc_w_out", "final_norm_w")
BIG = (("ffn_w_gate", 3), ("ffn_w_up", 3), ("ffn_w_down", 2), ("ssd_w_in", 2), ("ssd_w_out", 1), ("sc_w_in", 2),
       ("sc_w_out", 1))
SMALL_SHARDED = (("norm_w", 2), ("ssd_conv_w", 2), ("sc_conv_w", 2))
REPLICATED = ("ssd_conv_b", "ssd_dt_bias", "ssd_a_log", "ssd_d", "ssd_norm_w", "final_norm_w")
FLAT_COLS = 1024


def _pack(arrays, row_multiple, lead=()):
    flat = jnp.concatenate([a.reshape(lead + (-1,)) for a in arrays], axis=len(lead))
    unit = row_multiple * FLAT_COLS
    n = flat.shape[-1]
    pad = (-n) % unit
    if pad:
        flat = jnp.pad(flat, [(0, 0)] * len(lead) + [(0, pad)])
    return flat.reshape(lead + (-1, FLAT_COLS))


def _unpack(flat, shapes, lead=()):
    flat = flat.reshape(lead + (-1,))
    out, off = [], 0
    for shp in shapes:
        n = 1
        for s in shp:
            n *= s
        out.append(flat[..., off:off + n].reshape(lead + tuple(shp)))
        off += n
    return out


def _to_shards(full, axis):
    shp = full.shape
    r = full.reshape(shp[:axis] + (N_SHARDS, shp[axis] // N_SHARDS) + shp[axis + 1:])
    return jnp.moveaxis(r, axis, 0)


def _from_shards(sh, axis):
    r = jnp.moveaxis(sh, 0, axis)
    shp = r.shape
    return r.reshape(shp[:axis] + (shp[axis] * shp[axis + 1],) + shp[axis + 2:])


def _layer_shards(wl, i):
    j = i // 2
    ffn = lambda k: [(("ffn_w_gate", k), wl["ffn_w_gate"][i, k], 1), (("ffn_w_up", k), wl["ffn_w_up"][i, k], 1),
                     (("ffn_w_down", k), wl["ffn_w_down"][i, k], 0)]
    mix = "ssd" if i % 2 == 0 else "sc"
    return ffn(0), [((mix + "_w_in",), wl[mix + "_w_in"][j], 1), ((mix + "_w_out",), wl[mix + "_w_out"][j], 0)] + ffn(1)


def _assemble(group, received, chip):
    return {key: _from_shards(lax.dynamic_update_index_in_dim(r, own, chip, 0), axis)
            for (key, own, axis), r in zip(group, received)}


def _forward_backward(x, target, p, wl, layer0, place):
    chip = place[0]
    consts = _ssd_consts()
    nw = p["norm_w"]
    row = lambda v: v[None]
    full = {0: layer0}
    ffn = lambda i, k: (full[i]["ffn_w_gate", k], full[i]["ffn_w_up", k], full[i]["ffn_w_down", k])
    ssd_prm, sc_cw = {}, {}

    xin, saved, pre = [], [], {}
    for i in range(N_LAYERS):
        j = i // 2
        first, second = _layer_shards(wl, i + 1) if i + 1 < N_LAYERS else ([], [])
        xin.append(x)
        x, *rest = _ffn_fwd(x, row(nw[i, 0]), *ffn(i, 0), carry=[s[1] for s in first])
        pre[i, 0] = rest[:3]
        if first:
            full[i + 1] = _assemble(first, rest[3:], chip)
        xin.append(x)
        if i % 2 == 0:
            ssd_prm[j] = _ssd_prep(full[i]["ssd_w_in",], p["ssd_conv_w"][j], p["ssd_conv_b"][j], p["ssd_dt_bias"][j],
                                   p["ssd_a_log"][j], p["ssd_d"][j], p["ssd_norm_w"][j])
            x, sv = _ssd_layer_fwd(x, row(nw[i, 1]), ssd_prm[j], full[i]["ssd_w_out",], consts)
        else:
            sc_cw[j] = jnp.pad(p["sc_conv_w"][j], ((0, SUBLANES - SC_CONV_W), (0, 0)))
            sv = _norm_mm(x, row(nw[i, 1]), full[i]["sc_w_in",])
            x = _sc_fwd(x, sv, sc_cw[j], full[i]["sc_w_out",])
        saved.append(sv)
        xin.append(x)
        x, *rest = _ffn_fwd(x, row(nw[i, 2]), *ffn(i, 1), carry=[s[1] for s in second])
        pre[i, 1] = rest[:3]
        if second:
            full[i + 1].update(_assemble(second, rest[3:], chip))
    loss, dx, dfw = _loss_head(x, row(p["final_norm_w"]), target)

    g_nw = [[None] * 3 for _ in range(N_LAYERS)]
    g_ffn = {}
    g_ssd = [None, None]
    g_sc = [None, None]
    halved = lambda a: a.reshape((N_SHARDS, 2, -1, a.shape[-1]))

    def ffn_bwd(i, k, slot, dy, parts=(), sums=()):
        wg, wu, wd = ffn(i, k)
        a, s, p_ = pre[i, k]
        if parts:
            dg, du, *theirs = _ffn_bwd_act(dy, s, p_, wd, carry=parts)
            sums = [_add_halves(mine, got, place) for mine, got in zip(parts, theirs)]
            dxn, dnw, h, *arrived = _inproj_bwd(xin[3 * i + slot], dy, row(nw[i, slot]), [wg, wu], [[dg], [du]],
                                                carry=sums[:2])
        else:
            dxn, dnw, h, dg, du, *arrived = _ffn_bwd_fused(xin[3 * i + slot], dy, s, p_, row(nw[i, slot]), wg, wu, wd,
                                                           carry=sums)
        g_nw[i][slot] = dnw[0]
        for n, lhs, rhs, scale in (("ffn_w_gate", h, dg, 1.0), ("ffn_w_up", h, du, 1.0), ("ffn_w_down", a, dy, 0.5)):
            g_ffn[n, i] = _matmul_tn(lhs, rhs, scale=scale, name="wgrad_" + n, slab=(k,), stack=(2,),
                                     buf=g_ffn.get((n, i)))
        return dxn, arrived, sums

    def finish(sums, arrived):
        return _join_halves([_sum_chips(mine, others, place) for mine, others in zip(sums, arrived)])

    reduced = {}
    waiting = None
    for i in reversed(range(N_LAYERS)):
        j = i // 2
        dx, arrived, sums = ffn_bwd(i, 1, 2, dx, parts=waiting[1] if waiting else ())
        xm = xin[3 * i + 1]
        if i % 2 == 0:
            dx, gs = _ssd_layer_bwd(xm, dx, row(nw[i, 1]), ssd_prm[j], full[i]["ssd_w_out",], consts, saved[i])
            g_nw[i][1] = gs[0][0]
            g_ssd[j] = gs[1:]
            mixer = [(("ssd_w_in", j), _to_shards(gs[1], 1)), (("ssd_w_out", j), _to_shards(gs[-1], 0))]
        else:
            bcu = saved[i]
            dbcu, pin, dcw = _sc_bwd(dx, bcu, sc_cw[j], full[i]["sc_w_out",])
            dwo = _matmul_tn(pin, dx, name="wgrad_sc_out")
            dx, dnw, h = _inproj_bwd(xm, dx, row(nw[i, 1]), [full[i]["sc_w_in",]], [[dbcu]])
            g_nw[i][1] = dnw[0]
            dwi = _matmul_tn(h, dbcu, name="wgrad_sc_in")
            g_sc[j] = dcw[:SC_CONV_W]
            mixer = [(("sc_w_in", j), _to_shards(dwi, 1)), (("sc_w_out", j), _to_shards(dwo, 0))]
        dx, more, _ = ffn_bwd(i, 0, 0, dx, sums=sums[2:])
        if waiting:
            reduced.update(zip(waiting[0], finish(sums, arrived + more)))
        mine = [(("ffn_w_gate", i), _to_shards(g_ffn["ffn_w_gate", i], 2)), (("ffn_w_up", i), _to_shards(g_ffn["ffn_w_up", i], 2)),
                (("ffn_w_down", i), _to_shards(g_ffn["ffn_w_down", i], 1))] + mixer
        waiting = ([key for key, _ in mine], [halved(v) for _, v in mine])

    g = {"norm_w": jnp.stack([jnp.stack(r) for r in g_nw]), "final_norm_w": dfw[0]}
    for k, n in enumerate(("ssd_conv_w", "ssd_conv_b", "ssd_dt_bias", "ssd_a_log", "ssd_d", "ssd_norm_w")):
        g[n] = jnp.stack([g_ssd[0][k + 1], g_ssd[1][k + 1]])
    g["sc_conv_w"] = jnp.stack(g_sc)
    small_part = _pack([_to_shards(g[n], ax) for n, ax in SMALL_SHARDED]
                       + [jnp.broadcast_to(g[n][None], (N_SHARDS,) + g[n].shape) for n in REPLICATED],
                       4 * SUBLANES, lead=(N_SHARDS,))
    parts = waiting[1] + [halved(small_part)]
    last = [_add_halves(mine, got, place) for mine, got in zip(parts, _swap_halves(parts))]
    out = finish(last, _scatter_to_chips(last))
    reduced.update(zip(waiting[0], out[:-1]))
    return loss, dx, reduced, out[-1]


def kernel(x, norm_w, ffn_w_gate, ffn_w_up, ffn_w_down, ssd_w_in, ssd_conv_w, ssd_conv_b, ssd_dt_bias, ssd_a_log, ssd_d, ssd_norm_w, ssd_w_out, sc_w_in, sc_conv_w, sc_w_out, final_norm_w, loss_target, m_norm_w, m_ffn_w_gate, m_ffn_w_up, m_ffn_w_down, m_ssd_w_in, m_ssd_conv_w, m_ssd_conv_b, m_ssd_dt_bias, m_ssd_a_log, m_ssd_d, m_ssd_norm_w, m_ssd_w_out, m_sc_w_in, m_sc_conv_w, m_sc_w_out, m_final_norm_w, v_norm_w, v_ffn_w_gate, v_ffn_w_up, v_ffn_w_down, v_ssd_w_in, v_ssd_conv_w, v_ssd_conv_b, v_ssd_dt_bias, v_ssd_a_log, v_ssd_d, v_ssd_norm_w, v_ssd_w_out, v_sc_w_in, v_sc_conv_w, v_sc_w_out, v_final_norm_w):
    w = dict(zip(WEIGHTS, (norm_w, ffn_w_gate, ffn_w_up, ffn_w_down, ssd_w_in, ssd_conv_w, ssd_conv_b, ssd_dt_bias,
                           ssd_a_log, ssd_d, ssd_norm_w, ssd_w_out, sc_w_in, sc_conv_w, sc_w_out, final_norm_w)))
    m = dict(zip(WEIGHTS, (m_norm_w, m_ffn_w_gate, m_ffn_w_up, m_ffn_w_down, m_ssd_w_in, m_ssd_conv_w, m_ssd_conv_b,
                           m_ssd_dt_bias, m_ssd_a_log, m_ssd_d, m_ssd_norm_w, m_ssd_w_out, m_sc_w_in, m_sc_conv_w,
                           m_sc_w_out, m_final_norm_w)))
    v = dict(zip(WEIGHTS, (v_norm_w, v_ffn_w_gate, v_ffn_w_up, v_ffn_w_down, v_ssd_w_in, v_ssd_conv_w, v_ssd_conv_b,
                           v_ssd_dt_bias, v_ssd_a_log, v_ssd_d, v_ssd_norm_w, v_ssd_w_out, v_sc_w_in, v_sc_conv_w,
                           v_sc_w_out, v_final_norm_w)))
    chip = 2 * lax.axis_index("x") + lax.axis_index("y")
    place = jnp.stack([chip, lax.axis_index("c")]).astype(jnp.int32)
    big_names = [n for n, _ in BIG]
    small_names = [n for n, _ in SMALL_SHARDED] + list(REPLICATED)
    halved = lambda a, lead=(): a.reshape(lead + (2, -1, a.shape[-1]))

    wl = {n: w[n].astype(BF16) for n in big_names}
    first, second = _layer_shards(wl, 0)
    small = halved(_pack([w[n] for n, _ in SMALL_SHARDED], 2 * SUBLANES))
    received = _all_gather_shards([halved(s[1]) for s in first + second] + [small])
    layer0 = _assemble(first + second, [r.reshape((N_SHARDS,) + s[1].shape) for r, s in zip(received, first + second)],
                       chip)
    p = {n: w[n] for n in REPLICATED}
    small_full = lax.dynamic_update_index_in_dim(received[-1], small, chip, 0)
    for (n, ax), sh in zip(SMALL_SHARDED, _unpack(small_full, [w[n].shape for n, _ in SMALL_SHARDED], lead=(N_SHARDS,))):
        p[n] = _from_shards(sh, ax)

    t, d = x.shape[-2:]
    loss, dx, reduced, g_small = _forward_backward(x.reshape(t, d), loss_target.reshape(t, d), p, wl, layer0, place)

    grad = {}
    for n in big_names:
        per_layer = w[n].shape[0]
        grad[n] = jnp.stack([reduced[n, i].reshape(w[n].shape[1:]) for i in range(per_layer)])
    g_small = g_small.reshape(-1, FLAT_COLS)
    grad.update(zip(small_names, _unpack(g_small, [w[n].shape for n in small_names])))

    delta, new_m, new_v = {}, {}, {}
    for n in big_names:
        shp = w[n].shape
        as2d = lambda a: a.reshape(-1, shp[-1])
        out = _adamw(as2d(w[n]), as2d(grad[n]), as2d(m[n]), as2d(v[n]), name="adamw_" + n)
        delta[n], new_m[n], new_v[n] = (o.reshape(shp) for o in out)
    packed = [_pack([s[n] for n in small_names], 4 * SUBLANES) for s in (w, m, v)]
    out = _adamw(packed[0], g_small, packed[1], packed[2], name="adamw_small")
    shapes = [w[n].shape for n in small_names]
    for dst, o in zip((delta, new_m, new_v), out):
        dst.update(zip(small_names, _unpack(o, shapes)))

    loss = lax.psum(loss[0, 0], ("x", "y", "c"))
    return (loss, dx.reshape(x.shape), *[grad[n] for n in WEIGHTS], *[delta[n] for n in WEIGHTS],
            *[new_m[n] for n in WEIGHTS], *[new_v[n] for n in WEIGHTS])
```

```python
import functools

import jax
import jax.numpy as jnp
from jax import lax
from jax.experimental import pallas as pl
from jax.experimental.pallas import tpu as pltpu

F32 = jnp.float32
BF16 = jnp.bfloat16
MESH = pl.DeviceIdType.MESH

RMS_EPS = 1e-5
D_MODEL = 1024
D_FF = 2816
N_LAYERS = 4
SSD_D_INNER = 2048
SSD_N_HEADS = 32
SSD_N_GROUPS = 4
SSD_D_STATE = 128
SSD_CHUNK = 128
SSD_CONV_W = 4
SSD_CONV_DIM = 3072
SSD_IN_DIM = 5152
SC_CONV_W = 3
LANES = 128
SUBLANES = 8
N_XS_BLK = SSD_D_INNER // LANES
SSD_IN_PAD = SSD_D_INNER + SSD_CONV_DIM + SSD_N_GROUPS * LANES
VMEM_LIMIT = 56 * 2**20
TOKEN_TILE = 512
WGRAD_TOKENS = 2048
FF_CHUNK = 256
N_SHARDS = 4

ADAM_LR = 0.001
ADAM_B1 = 0.9
ADAM_B2 = 0.999
ADAM_EPS = 1e-08
ADAM_WD = 0.01
ADAM_STEP = 10


_HBM = pl.BlockSpec(memory_space=pl.ANY)


def _params(*sem):
    return pltpu.CompilerParams(dimension_semantics=sem if sem else None, vmem_limit_bytes=VMEM_LIMIT)


def _dot(a, b):
    return jnp.dot(a, b, preferred_element_type=F32)


def _dot_nt(a, b):
    return lax.dot_general(a, b, (((1,), (1,)), ((), ())), preferred_element_type=F32)


def _dot_tn(a, b):
    return lax.dot_general(a, b, (((0,), (0,)), ((), ())), preferred_element_type=F32)


def _resident(shape):
    n = len(shape)
    return pl.BlockSpec(shape, lambda *_: (0,) * n, pipeline_mode=pl.Buffered(1))


def _split3(v):
    hi = v.astype(BF16)
    r1 = v - hi.astype(F32)
    mid = r1.astype(BF16)
    lo = (r1 - mid.astype(F32)).astype(BF16)
    return hi, mid, lo


def _sel_left(sel, v3):
    return _dot(sel, v3[0]) + _dot(sel, v3[1]) + _dot(sel, v3[2])


def _sigmoid(v):
    return 1.0 / (1.0 + jnp.exp(-v))


def _rms_fwd(x, w):
    inv = lax.rsqrt(jnp.mean(x * x, axis=-1, keepdims=True) + RMS_EPS)
    xhat = x * inv
    return xhat * w, xhat, inv


def _rms_bwd(dh, xhat, inv, w):
    dxhat = dh * w
    dx = inv * (dxhat - xhat * jnp.mean(dxhat * xhat, axis=-1, keepdims=True))
    return dx, jnp.sum(dh * xhat, axis=0, keepdims=True)


def _ffn_fwd(x, nw, wg, wu, wd, carry=()):
    t, d = x.shape
    f = wg.shape[1]
    tm = min(TOKEN_TILE, t)
    nsteps = t // tm
    ncar = len(carry)

    def body(x_ref, nw_ref, wg_ref, wu_ref, wd_ref, *rest):
        srcs = rest[:ncar]
        o_ref, a_ref, s_ref, p_ref = rest[ncar:ncar + 4]
        dsts = rest[ncar + 4:2 * ncar + 4]
        if ncar:
            send_sems, recv_sems = rest[2 * ncar + 4:]
            x_, y_, c_, chips = _place()
            me = 2 * x_ + y_

            @pl.when(pl.program_id(0) == 0)
            def _():
                for oi, (src, dst) in enumerate(zip(srcs, dsts)):
                    for j, chip in enumerate(chips):
                        _remote(src, dst.at[me], send_sems, recv_sems, 3 * oi + j, (*chip, c_)).start()

        xv = x_ref[...]
        h = _rms_fwd(xv, nw_ref[...])[0].astype(BF16)
        for j in range(f // FF_CHUNK):
            sl = slice(j * FF_CHUNK, (j + 1) * FF_CHUNK)
            g = _dot(h, wg_ref[:, sl])
            u = _dot(h, wu_ref[:, sl])
            sig = _sigmoid(g)
            s = g * sig
            a_ref[:, sl] = (s * u).astype(BF16)
            s_ref[:, sl] = s.astype(BF16)
            p_ref[:, sl] = (u * (sig + s * (1.0 - sig))).astype(BF16)
        o_ref[...] = xv + 0.5 * _dot(a_ref[...], wd_ref[...])

        if ncar:
            @pl.when(pl.program_id(0) == nsteps - 1)
            def _():
                for oi, (src, dst) in enumerate(zip(srcs, dsts)):
                    for j, chip in enumerate(chips):
                        landed = dst.at[2 * chip[0] + chip[1]]
                        _remote(landed, landed, send_sems, recv_sems, 3 * oi + j, (*chip, c_)).wait_recv()
                for oi, (src, dst) in enumerate(zip(srcs, dsts)):
                    for j, chip in enumerate(chips):
                        _remote(src, dst.at[me], send_sems, recv_sems, 3 * oi + j, (*chip, c_)).wait_send()

    tok = lambda n: pl.BlockSpec((tm, n), lambda i: (i, 0))
    sems = [pltpu.SemaphoreType.DMA((3 * ncar,)), pltpu.SemaphoreType.DMA((3 * ncar,))] if ncar else []
    return pl.pallas_call(
        body, name="ffn_fwd_carry" if ncar else "ffn_fwd", grid=(nsteps,),
        in_specs=[tok(d), _resident((1, d)), _resident((d, f)), _resident((d, f)), _resident((f, d))] + [_HBM] * ncar,
        out_specs=[tok(d), tok(f), tok(f), tok(f)] + [_HBM] * ncar,
        out_shape=[jax.ShapeDtypeStruct((t, d), F32)] + [jax.ShapeDtypeStruct((t, f), BF16)] * 3
        + [jax.ShapeDtypeStruct((N_SHARDS,) + c.shape, c.dtype) for c in carry],
        scratch_shapes=sems,
        compiler_params=_params("arbitrary" if ncar else "parallel"),
    )(x, nw, wg, wu, wd, *carry)


def _ffn_bwd_fused(x, dy, s, p, nw, wg, wu, wd, carry=(), swap=()):
    assert not (carry and swap)
    t, d = x.shape
    f = wd.shape[0]
    tm = min(TOKEN_TILE // 2, t)
    nsteps = t // tm
    moved = list(carry) + list(swap)
    ncar = len(moved)
    per_copy = 3 if carry else 1

    def body(x_ref, dy_ref, s_ref, p_ref, nw_ref, wg_ref, wu_ref, wd_ref, *rest):
        srcs, (dx_ref, dnw_ref, h_ref, dg_ref, du_ref) = rest[:ncar], rest[ncar:ncar + 5]
        dsts, sems = rest[ncar + 5:2 * ncar + 5], rest[2 * ncar + 5:]
        if carry:
            @pl.when(pl.program_id(0) == 0)
            def _():
                _scatter_start(srcs, dsts, *sems)
        if swap:
            @pl.when(pl.program_id(0) == 0)
            def _():
                for cp in _swap_copies(srcs, dsts, *sems):
                    cp.start()

        dyv = dy_ref[...]
        dob = (0.5 * dyv).astype(BF16)
        for j in range(f // FF_CHUNK):
            sl = slice(j * FF_CHUNK, (j + 1) * FF_CHUNK)
            da = _dot_nt(dob, wd_ref[sl, :])
            dg_ref[:, sl] = (da * p_ref[:, sl].astype(F32)).astype(BF16)
            du_ref[:, sl] = (da * s_ref[:, sl].astype(F32)).astype(BF16)
        nwv = nw_ref[...]
        hf, xhat, inv = _rms_fwd(x_ref[...], nwv)
        h_ref[...] = hf.astype(BF16)
        dh = _dot_nt(dg_ref[...], wg_ref[...]) + _dot_nt(du_ref[...], wu_ref[...])
        dx, dw = _rms_bwd(dh, xhat, inv, nwv)
        dx_ref[...] = dyv + dx

        @pl.when(pl.program_id(0) == 0)
        def _():
            dnw_ref[...] = jnp.zeros_like(dnw_ref)

        dnw_ref[...] += dw

        if carry:
            @pl.when(pl.program_id(0) == nsteps - 1)
            def _():
                _scatter_wait(srcs, dsts, *sems)
        if swap:
            @pl.when(pl.program_id(0) == nsteps - 1)
            def _():
                for cp in _swap_copies(srcs, dsts, *sems):
                    cp.wait()

    tok = lambda n: pl.BlockSpec((tm, n), lambda i: (i, 0))
    nsem = per_copy * ncar
    sems = [pltpu.SemaphoreType.DMA((nsem,)), pltpu.SemaphoreType.DMA((nsem,))] if ncar else []
    name = "ffn_bwd_fused" + ("_carry" if carry else "_swap" if swap else "")
    return pl.pallas_call(
        body, name=name, grid=(nsteps,),
        in_specs=[tok(d), tok(d), tok(f), tok(f), _resident((1, d)), _resident((d, f)), _resident((d, f)),
                  _resident((f, d))] + [_HBM] * ncar,
        out_specs=[tok(d), pl.BlockSpec((1, d), lambda i: (0, 0)), tok(d), tok(f), tok(f)] + [_HBM] * ncar,
        out_shape=[jax.ShapeDtypeStruct((t, d), F32), jax.ShapeDtypeStruct((1, d), F32),
                   jax.ShapeDtypeStruct((t, d), BF16), jax.ShapeDtypeStruct((t, f), BF16),
                   jax.ShapeDtypeStruct((t, f), BF16)]
        + [jax.ShapeDtypeStruct((3,) + c.shape[1:], c.dtype) for c in carry]
        + [jax.ShapeDtypeStruct((c.shape[0],) + c.shape[2:], c.dtype) for c in swap],
        scratch_shapes=sems,
        compiler_params=_params("arbitrary"),
    )(x, dy, s, p, nw, wg, wu, wd, *moved)


def _pick_bn(m, n, unit):
    best = unit
    for k in range(1, n // unit + 1):
        bn = k * unit
        if n % bn == 0 and m * bn * 4 <= 8 * 2**20:
            best = bn
    return best


def _matmul_tn(a, b, scale=1.0, name="wgrad", slab=None, stack=None, buf=None, carry=()):
    t, m = a.shape
    n = b.shape[1]
    bt = min(WGRAD_TOKENS, t)
    bn = _pick_bn(m, n, LANES)
    nt, nj = t // bt, n // bn
    lead = tuple(slab) if slab is not None else ()
    nbuf = 0 if buf is None else 1
    ncar = len(carry)

    def body(a_ref, b_ref, *rest):
        srcs, o_ref = rest[nbuf:nbuf + ncar], rest[nbuf + ncar]
        dsts, sems = rest[nbuf + ncar + 1:nbuf + 2 * ncar + 1], rest[nbuf + 2 * ncar + 1:]
        if ncar:
            @pl.when((pl.program_id(0) == 0) & (pl.program_id(1) == 0))
            def _():
                _scatter_start(srcs, dsts, *sems)

        @pl.when(pl.program_id(1) == 0)
        def _():
            o_ref[...] = jnp.zeros_like(o_ref)

        o_ref[...] += _dot_tn(a_ref[...].astype(BF16), b_ref[...].astype(BF16))
        if scale != 1.0:
            @pl.when(pl.program_id(1) == nt - 1)
            def _():
                o_ref[...] *= scale

        if ncar:
            @pl.when((pl.program_id(0) == nj - 1) & (pl.program_id(1) == nt - 1))
            def _():
                _scatter_wait(srcs, dsts, *sems)

    in_specs = [pl.BlockSpec((bt, m), lambda j, k: (k, 0)), pl.BlockSpec((bt, bn), lambda j, k: (k, j))]
    args = [a, b]
    if buf is not None:
        in_specs.append(_HBM)
        args.append(buf)
    sems = [pltpu.SemaphoreType.DMA((3 * ncar,)), pltpu.SemaphoreType.DMA((3 * ncar,))] if ncar else []
    out = pl.pallas_call(
        body, name=name + "_carry" if ncar else name, grid=(nj, nt),
        in_specs=in_specs + [_HBM] * ncar,
        out_specs=[pl.BlockSpec((None,) * len(lead) + (m, bn), lambda j, k: lead + (0, j))] + [_HBM] * ncar,
        out_shape=[jax.ShapeDtypeStruct(tuple(stack or ()) + (m, n), F32)]
        + [jax.ShapeDtypeStruct((3,) + c.shape[1:], c.dtype) for c in carry],
        input_output_aliases={2: 0} if buf is not None else {},
        scratch_shapes=sems,
        compiler_params=_params("arbitrary" if ncar else "parallel", "arbitrary"),
    )(*args, *carry)
    return out if ncar else out[0]


def _matmul_tn_blocked(a, b, name="wgrad_blk"):
    t, m = a.shape
    nb = b.shape[0]
    bt = min(1024, t)
    nbt = _pick_bn(m, nb * LANES, LANES) // LANES
    while nb % nbt:
        nbt -= 1

    def body(a_ref, b_ref, o_ref):
        @pl.when(pl.program_id(1) == 0)
        def _():
            o_ref[...] = jnp.zeros_like(o_ref)

        bv = jnp.concatenate([b_ref[i] for i in range(nbt)], axis=1) if nbt > 1 else b_ref[0]
        o_ref[...] += _dot_tn(a_ref[...], bv)

    return pl.pallas_call(
        body, name=name, grid=(nb // nbt, t // bt),
        in_specs=[pl.BlockSpec((bt, m), lambda j, k: (k, 0)), pl.BlockSpec((nbt, bt, LANES), lambda j, k: (j, k, 0))],
        out_specs=pl.BlockSpec((m, nbt * LANES), lambda j, k: (0, j)),
        out_shape=jax.ShapeDtypeStruct((m, nb * LANES), F32),
        compiler_params=_params("parallel", "arbitrary"),
    )(a, b)


def _norm_mm(x, nw, w):
    t, d = x.shape
    n = w.shape[1]
    tm = min(TOKEN_TILE, t)
    cn = 1024 if n % 1024 == 0 else n

    def body(x_ref, nw_ref, w_ref, o_ref):
        h = _rms_fwd(x_ref[...], nw_ref[...])[0].astype(BF16)
        for j in range(n // cn):
            sl = slice(j * cn, (j + 1) * cn)
            o_ref[:, sl] = _dot(h, w_ref[:, sl])

    return pl.pallas_call(
        body, name="norm_mm", grid=(t // tm,),
        in_specs=[pl.BlockSpec((tm, d), lambda i: (i, 0)), _resident((1, d)), _resident((d, n))],
        out_specs=pl.BlockSpec((tm, n), lambda i: (i, 0)),
        out_shape=jax.ShapeDtypeStruct((t, n), F32),
        compiler_params=_params("parallel"),
    )(x, nw, w)


def _ssd_inproj(x, nw, w):
    t, d = x.shape
    tm = min(TOKEN_TILE, t)
    nz, nx, ng = SSD_D_INNER // LANES, SSD_CONV_DIM // LANES, SSD_N_GROUPS
    cn = 1024

    def body(x_ref, nw_ref, w_ref, z_ref, xr_ref, dt_ref):
        h = _rms_fwd(x_ref[...], nw_ref[...])[0].astype(BF16)
        for j in range(-(-SSD_IN_PAD // cn)):
            lo, hi = j * cn, min((j + 1) * cn, SSD_IN_PAD)
            r = _dot(h, w_ref[:, lo:hi])
            for i in range((hi - lo) // LANES):
                blk = j * (cn // LANES) + i
                v = r[:, i * LANES:(i + 1) * LANES]
                if blk < nz:
                    z_ref[blk] = v
                elif blk < nz + nx:
                    xr_ref[blk - nz] = v
                else:
                    dt_ref[blk - nz - nx] = v

    out = lambda n: pl.BlockSpec((n, tm, LANES), lambda i: (0, i, 0))
    return pl.pallas_call(
        body, name="ssd_inproj", grid=(t // tm,),
        in_specs=[pl.BlockSpec((tm, d), lambda i: (i, 0)), _resident((1, d)), _resident((d, SSD_IN_PAD))],
        out_specs=[out(nz), out(nx), out(ng)],
        out_shape=[jax.ShapeDtypeStruct((n, t, LANES), F32) for n in (nz, nx, ng)],
        compiler_params=_params("parallel"),
    )(x, nw, w)


def _inproj_bwd(x, dy, nw, ws, pieces, carry=()):
    t, d = x.shape
    tm = min(TOKEN_TILE, t)
    nsteps = t // tm
    nws = len(ws)
    ncar = len(carry)
    flat = [p for group in pieces for p in group]

    def body(*refs):
        x_ref, dy_ref, nw_ref = refs[:3]
        w_refs = refs[3:3 + nws]
        p_refs = list(refs[3 + nws:3 + nws + len(flat)])
        rest = refs[3 + nws + len(flat):]
        srcs, (dx_ref, dnw_ref, h_ref) = rest[:ncar], rest[ncar:ncar + 3]
        dsts, sems = rest[ncar + 3:2 * ncar + 3], rest[2 * ncar + 3:]
        if ncar:
            @pl.when(pl.program_id(0) == 0)
            def _():
                _scatter_start(srcs, dsts, *sems)

        nwv = nw_ref[...]
        hf, xhat, inv = _rms_fwd(x_ref[...], nwv)
        h_ref[...] = hf.astype(BF16)
        dh = None
        for w_ref, group in zip(w_refs, pieces):
            parts = []
            for _ in group:
                p = p_refs.pop(0)
                parts += [p[i] for i in range(p.shape[0])] if len(p.shape) == 3 else [p[...]]
            dz = jnp.concatenate(parts, axis=1) if len(parts) > 1 else parts[0]
            part = _dot_nt(dz, w_ref[...])
            dh = part if dh is None else dh + part
        dx, dw = _rms_bwd(dh, xhat, inv, nwv)
        dx_ref[...] = dy_ref[...] + dx

        @pl.when(pl.program_id(0) == 0)
        def _():
            dnw_ref[...] = jnp.zeros_like(dnw_ref)

        dnw_ref[...] += dw

        if ncar:
            @pl.when(pl.program_id(0) == nsteps - 1)
            def _():
                _scatter_wait(srcs, dsts, *sems)

    tok = lambda m: pl.BlockSpec((tm, m), lambda i: (i, 0))
    p_specs = [pl.BlockSpec((p.shape[0], tm, LANES), lambda i: (0, i, 0)) if p.ndim == 3 else tok(p.shape[1])
               for p in flat]
    sems = [pltpu.SemaphoreType.DMA((3 * ncar,)), pltpu.SemaphoreType.DMA((3 * ncar,))] if ncar else []
    return pl.pallas_call(
        body, name="inproj_bwd_carry" if ncar else "inproj_bwd", grid=(nsteps,),
        in_specs=[tok(d), tok(d), _resident((1, d))] + [_resident(w.shape) for w in ws] + p_specs + [_HBM] * ncar,
        out_specs=[tok(d), pl.BlockSpec((1, d), lambda i: (0, 0)), tok(d)] + [_HBM] * ncar,
        out_shape=[jax.ShapeDtypeStruct((t, d), F32), jax.ShapeDtypeStruct((1, d), F32),
                   jax.ShapeDtypeStruct((t, d), BF16)]
        + [jax.ShapeDtypeStruct((3,) + c.shape[1:], c.dtype) for c in carry],
        scratch_shapes=sems,
        compiler_params=_params("arbitrary"),
    )(x, dy, nw, *ws, *flat, *carry)


def _shift_down(v, j, prev8):
    if j == 0:
        return v
    r = pltpu.roll(v, j, 0)
    p = pltpu.roll(prev8, j, 0)
    rows = lax.broadcasted_iota(jnp.int32, prev8.shape, 0)
    first = jnp.where(rows < j, p, r[0:SUBLANES])
    return jnp.concatenate([first, r[SUBLANES:]], axis=0)


def _shift_up(v, j, next8):
    if j == 0:
        return v
    n = v.shape[0]
    r = pltpu.roll(v, n - j, 0)
    p = pltpu.roll(next8, SUBLANES - j, 0)
    rows = lax.broadcasted_iota(jnp.int32, next8.shape, 0)
    last = jnp.where(rows >= SUBLANES - j, p, r[n - SUBLANES:])
    return jnp.concatenate([r[:n - SUBLANES], last], axis=0)


def _sc_fwd(x, bcu, cw, wo):
    t, d = x.shape
    tm = min(TOKEN_TILE, t)
    hb = tm // SUBLANES

    def body(x_ref, bcu_ref, prev_ref, cw_ref, wo_ref, o_ref):
        bg, cg, u = bcu_ref[:, 0:d], bcu_ref[:, d:2 * d], bcu_ref[:, 2 * d:3 * d]
        q = cg * u
        qp = jnp.where(pl.program_id(0) == 0, 0.0, prev_ref[:, d:2 * d] * prev_ref[:, 2 * d:3 * d])
        cwv = cw_ref[...]
        v = cwv[2:3] * q + cwv[1:2] * _shift_down(q, 1, qp) + cwv[0:1] * _shift_down(q, 2, qp)
        o_ref[...] = x_ref[...] + _dot((bg * v).astype(BF16), wo_ref[...])

    return pl.pallas_call(
        body, name="sc_fwd", grid=(t // tm,),
        in_specs=[pl.BlockSpec((tm, d), lambda i: (i, 0)), pl.BlockSpec((tm, 3 * d), lambda i: (i, 0)),
                  pl.BlockSpec((SUBLANES, 3 * d), lambda i: (jnp.maximum(i * hb - 1, 0), 0)),
                  _resident((SUBLANES, d)), _resident((d, d))],
        out_specs=pl.BlockSpec((tm, d), lambda i: (i, 0)),
        out_shape=jax.ShapeDtypeStruct((t, d), F32),
        compiler_params=_params("parallel"),
    )(x, bcu, bcu, cw, wo)


def _sc_bwd(dy, bcu, cw, wo):
    t, d = dy.shape
    tm = min(TOKEN_TILE, t)
    hb = tm // SUBLANES
    nt = t // tm

    def body(dy_ref, dyn_ref, bcu_ref, prev_ref, next_ref, cw_ref, wo_ref, dbcu_ref, p_ref, dcw_ref):
        i = pl.program_id(0)
        bg, cg, u = bcu_ref[:, 0:d], bcu_ref[:, d:2 * d], bcu_ref[:, 2 * d:3 * d]
        q = cg * u
        qp = jnp.where(i == 0, 0.0, prev_ref[:, d:2 * d] * prev_ref[:, 2 * d:3 * d])
        cwv = cw_ref[...]
        q1 = _shift_down(q, 1, qp)
        q2 = _shift_down(q, 2, qp)
        v = cwv[2:3] * q + cwv[1:2] * q1 + cwv[0:1] * q2
        p_ref[...] = (bg * v).astype(BF16)
        wov = wo_ref[...]
        dp = _dot_nt(dy_ref[...].astype(BF16), wov)
        dpn = _dot_nt(dyn_ref[...].astype(BF16), wov)
        dv = dp * bg
        dvn = jnp.where(i == nt - 1, 0.0, dpn * next_ref[:, 0:d])
        dq = cwv[2:3] * dv + cwv[1:2] * _shift_up(dv, 1, dvn) + cwv[0:1] * _shift_up(dv, 2, dvn)
        dbcu_ref[:, 0:d] = (dp * v).astype(BF16)
        dbcu_ref[:, d:2 * d] = (dq * u).astype(BF16)
        dbcu_ref[:, 2 * d:3 * d] = (dq * cg).astype(BF16)

        @pl.when(i == 0)
        def _():
            dcw_ref[...] = jnp.zeros_like(dcw_ref)

        dcw_ref[0:1, :] += jnp.sum(dv * q2, axis=0, keepdims=True)
        dcw_ref[1:2, :] += jnp.sum(dv * q1, axis=0, keepdims=True)
        dcw_ref[2:3, :] += jnp.sum(dv * q, axis=0, keepdims=True)

    last8 = t // SUBLANES - 1
    return pl.pallas_call(
        body, name="sc_bwd", grid=(nt,),
        in_specs=[pl.BlockSpec((tm, d), lambda i: (i, 0)),
                  pl.BlockSpec((SUBLANES, d), lambda i: (jnp.minimum((i + 1) * hb, last8), 0)),
                  pl.BlockSpec((tm, 3 * d), lambda i: (i, 0)),
                  pl.BlockSpec((SUBLANES, 3 * d), lambda i: (jnp.maximum(i * hb - 1, 0), 0)),
                  pl.BlockSpec((SUBLANES, 3 * d), lambda i: (jnp.minimum((i + 1) * hb, last8), 0)),
                  _resident((SUBLANES, d)), _resident((d, d))],
        out_specs=[pl.BlockSpec((tm, 3 * d), lambda i: (i, 0)), pl.BlockSpec((tm, d), lambda i: (i, 0)),
                   pl.BlockSpec((SUBLANES, d), lambda i: (0, 0))],
        out_shape=[jax.ShapeDtypeStruct((t, 3 * d), BF16), jax.ShapeDtypeStruct((t, d), BF16),
                   jax.ShapeDtypeStruct((SUBLANES, d), F32)],
        compiler_params=_params("arbitrary"),
    )(dy, dy, bcu, bcu, bcu, cw, wo)


NEG_BIG = -1e30


HEADS_PER_GROUP = SSD_N_HEADS // SSD_N_GROUPS
PAIRS_PER_GROUP = HEADS_PER_GROUP // 2
GROUPS_PER_STEP = 4


def _ssd_consts():
    r = lax.broadcasted_iota(jnp.int32, (LANES, LANES), 0)
    c = lax.broadcasted_iota(jnp.int32, (LANES, LANES), 1)
    return (c <= r).astype(BF16), (c >= r).astype(BF16)


def _ssd_decay(dtr, dtb, alog, tril):
    shape = (SSD_CHUNK, LANES)
    lanes = lax.broadcasted_iota(jnp.int32, shape, 1)
    rows = lax.broadcasted_iota(jnp.int32, shape, 0)
    pre = dtr + dtb
    valid = lanes < HEADS_PER_GROUP
    dt = jnp.where(valid, jnp.maximum(pre, 0.0) + jnp.log(1.0 + jnp.exp(-jnp.abs(pre))), 0.0)
    a = -jnp.exp(alog)
    acs = _sel_left(tril, _split3(dt * a))
    return dt, a, acs, pre, valid, rows, lanes


def _lane_col(v, j):
    return jnp.broadcast_to(v[:, j:j + 1], v.shape)


def _ssd_pair_terms(k, dt, cols, low_half, xs):
    dtp = jnp.where(low_half, _lane_col(dt, 2 * k), _lane_col(dt, 2 * k + 1))
    acsp = jnp.where(low_half, cols[2 * k], cols[2 * k + 1])
    lastp = acsp[SSD_CHUNK - 1:SSD_CHUNK, :]
    eap = jnp.exp(acsp)
    decp = jnp.exp(lastp - acsp)
    etp = jnp.exp(lastp)
    xdt = xs * dtp
    return dtp, eap, decp, etp, xdt


def _ssd_conv_taps(cwb, xr, prev8):
    sh = [_shift_down(xr, j, prev8) for j in range(SSD_CONV_W)]
    xc = cwb[4:5]
    for j in range(SSD_CONV_W):
        xc = xc + cwb[3 - j:4 - j] * sh[j]
    return xc, sh


def _ssd_specs(nc, rev):
    ch = (lambda i: nc - 1 - i) if rev else (lambda i: i)
    L = SSD_CHUNK
    gps = GROUPS_PER_STEP
    b0, c0 = N_XS_BLK // gps, (N_XS_BLK + SSD_N_GROUPS) // gps
    xs = pl.BlockSpec((4 * gps, L, LANES), lambda g, i: (g, ch(i), 0))
    bb = pl.BlockSpec((gps, L, LANES), lambda g, i: (b0 + g, ch(i), 0))
    cc = pl.BlockSpec((gps, L, LANES), lambda g, i: (c0 + g, ch(i), 0))
    dt = pl.BlockSpec((gps, L, LANES), lambda g, i: (g, ch(i), 0))
    cw_xs = pl.BlockSpec((4 * gps, SUBLANES, LANES), lambda g, i: (g, 0, 0))
    cw_b = pl.BlockSpec((gps, SUBLANES, LANES), lambda g, i: (b0 + g, 0, 0))
    cw_c = pl.BlockSpec((gps, SUBLANES, LANES), lambda g, i: (c0 + g, 0, 0))
    st = pl.BlockSpec((1, 4 * gps, LANES, LANES), lambda g, i: (ch(i), g, 0, 0))
    grp4 = pl.BlockSpec((4 * gps, L, LANES), lambda g, i: (g, ch(i), 0))
    return xs, bb, cc, dt, cw_xs, cw_b, cw_c, st, grp4


def _group_views(q, refs4, refs1, refs6=()):
    return ([r.at[pl.ds(4 * q, 4)] for r in refs4], [r.at[pl.ds(q, 1)] for r in refs1],
            [r.at[pl.ds(6 * q, 6)] for r in refs6])


def _ssd_scan_fwd(xr, dtr, cwb, dtb, alog, dskip, consts):
    t = xr.shape[1]
    L = SSD_CHUNK
    nc = t // L
    tril, _ = consts
    xs_s, b_s, c_s, dt_s, cwx_s, cwb_s, cwc_s, st_s, grp4 = _ssd_specs(nc, False)
    gps = GROUPS_PER_STEP
    grp_row = pl.BlockSpec((gps, 1, LANES), lambda g, i: (g, 0, 0))

    def body(xs_all, b_all, c_all, dtr_all, cwx_all, cwbb_all, cwc_all, dtb_all, alog_all, dsk_all,
             tril_ref, y_all, sp_all, state_all, tail_all):
        @pl.when(pl.program_id(1) == 0)
        def _():
            state_all[...] = jnp.zeros_like(state_all)
            tail_all[...] = jnp.zeros_like(tail_all)

        for q in range(gps):
            fours, ones, sixes = _group_views(q, (xs_all, cwx_all, dsk_all, y_all, sp_all.at[0], state_all),
                                              (b_all, c_all, dtr_all, cwbb_all, cwc_all, dtb_all, alog_all), (tail_all,))
            group(*fours, *ones, *sixes, tril_ref)

    def group(xs_ref, cwx_ref, dsk_ref, y_ref, sp_ref, state, b_ref, c_ref, dtr_ref, cwbb_ref, cwc_ref,
              dtb_ref, alog_ref, tail, tril_ref):
        xa = []
        for b in range(6):
            xrb = xs_ref[b] if b < 4 else (b_ref[0] if b == 4 else c_ref[0])
            cw = cwx_ref[b] if b < 4 else (cwbb_ref[0] if b == 4 else cwc_ref[0])
            xc, _ = _ssd_conv_taps(cw, xrb, tail[b])
            tail[b] = xrb[L - SUBLANES:]
            xa.append(xc * _sigmoid(xc))

        dt, a, acs, _, _, rows, lanes = _ssd_decay(dtr_ref[0], dtb_ref[0], alog_ref[0], tril_ref[...])
        acst = acs.T
        bb = xa[4].astype(BF16)
        cb_ = xa[5].astype(BF16)
        cbm = _dot_nt(cb_, bb)
        causal = rows >= lanes
        low_half = lanes < LANES // 2
        cols = [_lane_col(acs, j) for j in range(HEADS_PER_GROUP)]

        for k in range(PAIRS_PER_GROUP):
            xs = xa[k]
            dtp, eap, decp, etp, xdt = _ssd_pair_terms(k, dt, cols, low_half, xs)
            ms = []
            for j in (2 * k, 2 * k + 1):
                diff = cols[j] - jnp.broadcast_to(acst[j:j + 1, :], (L, L))
                ms.append((cbm * jnp.exp(jnp.where(causal, diff, NEG_BIG))).astype(BF16))
            xcat = jnp.concatenate([jnp.where(low_half, xdt, 0.0).astype(BF16),
                                    jnp.where(low_half, 0.0, xdt).astype(BF16)], axis=0)
            yd = _dot(jnp.concatenate(ms, axis=1), xcat)
            sp = state[k]
            yo = eap * _dot(cb_, sp.astype(BF16))
            y_ref[k] = yd + yo + dsk_ref[k][0:1] * xs
            sp_ref[k] = sp
            state[k] = etp * sp + _dot_tn(bb, (decp * xdt).astype(BF16))

    return pl.pallas_call(
        body, name="ssd_scan_fwd", grid=(SSD_N_GROUPS // gps, nc),
        in_specs=[xs_s, b_s, c_s, dt_s, cwx_s, cwb_s, cwc_s, grp_row, grp_row, cwx_s, _resident(tril.shape)],
        out_specs=[grp4, st_s],
        out_shape=[jax.ShapeDtypeStruct((N_XS_BLK, t, LANES), F32),
                   jax.ShapeDtypeStruct((nc, N_XS_BLK, LANES, LANES), F32)],
        scratch_shapes=[pltpu.VMEM((4 * gps, LANES, LANES), F32), pltpu.VMEM((6 * gps, SUBLANES, LANES), F32)],
        compiler_params=_params("arbitrary", "arbitrary"),
    )(xr, xr, xr, dtr, cwb, cwb, cwb, dtb, alog, dskip, tril)


def _ssd_scan_bwd(xr, dtr, dy, sprev, cwb, dtb, alog, dskip, consts):
    t = xr.shape[1]
    L = SSD_CHUNK
    nc = t // L
    hb = L // SUBLANES
    tril, triu = consts
    xs_s, b_s, c_s, dt_s, cwx_s, cwb_s, cwc_s, st_s, grp4 = _ssd_specs(nc, True)
    gps = GROUPS_PER_STEP
    grp_row = pl.BlockSpec((gps, 1, LANES), lambda g, i: (g, 0, 0))
    prev = lambda off: pl.BlockSpec(
        (4 * gps if off is None else gps, SUBLANES, LANES),
        (lambda g, i: (g, jnp.maximum((nc - 1 - i) * hb - 1, 0), 0)) if off is None else
        (lambda g, i: (off // gps + g, jnp.maximum((nc - 1 - i) * hb - 1, 0), 0)))
    grp1 = pl.BlockSpec((gps, L, LANES), lambda g, i: (g, nc - 1 - i, 0))
    acc4 = pl.BlockSpec((4 * gps, SUBLANES, LANES), lambda g, i: (g, 0, 0))
    acc1 = pl.BlockSpec((gps, SUBLANES, LANES), lambda g, i: (g, 0, 0))

    def body(xs_all, b_all, c_all, pxs_all, pb_all, pc_all, dtr_all, dy_all, sp_all,
             cwx_all, cwbb_all, cwc_all, dtb_all, alog_all, dsk_all, tril_ref, triu_ref,
             dxs_all, db_all, dc_all, ddtr_all, dcwx_all, dcwb_all, dcwc_all, dd_all, dsm_all,
             dstate_all, head_all):
        @pl.when(pl.program_id(1) == 0)
        def _():
            for r in (dstate_all, head_all, dcwx_all, dcwb_all, dcwc_all, dd_all, dsm_all):
                r[...] = jnp.zeros_like(r)

        for q in range(gps):
            fours, ones, sixes = _group_views(
                q, (xs_all, pxs_all, dy_all, sp_all.at[0], cwx_all, dsk_all, dxs_all, dcwx_all, dd_all, dstate_all),
                (b_all, c_all, pb_all, pc_all, dtr_all, cwbb_all, cwc_all, dtb_all, alog_all, db_all, dc_all, ddtr_all,
                 dcwb_all, dcwc_all, dsm_all), (head_all,))
            group(*fours, *ones, *sixes, tril_ref, triu_ref)

    def group(xs_ref, pxs_ref, dy_ref, sp_ref, cwx_ref, dsk_ref, dxs_ref, dcwx_ref, dd_ref, dstate,
              b_ref, c_ref, pb_ref, pc_ref, dtr_ref, cwbb_ref, cwc_ref, dtb_ref, alog_ref, db_ref, dc_ref, ddtr_ref,
              dcwb_ref, dcwc_ref, dsm_ref, head, tril_ref, triu_ref):
        first_chunk = pl.program_id(1) == nc - 1

        def blk(b):
            xrb = xs_ref[b] if b < 4 else (b_ref[0] if b == 4 else c_ref[0])
            cw = cwx_ref[b] if b < 4 else (cwbb_ref[0] if b == 4 else cwc_ref[0])
            p8 = pxs_ref[b] if b < 4 else (pb_ref[0] if b == 4 else pc_ref[0])
            return xrb, cw, jnp.where(first_chunk, 0.0, p8)

        xa, dsil = [], []
        for b in range(6):
            xrb, cw, p8 = blk(b)
            xc, _ = _ssd_conv_taps(cw, xrb, p8)
            sig = _sigmoid(xc)
            xa.append(xc * sig)
            dsil.append(sig * (1.0 + xc * (1.0 - sig)))

        dt, a, acs, pre, valid, rows, lanes = _ssd_decay(dtr_ref[0], dtb_ref[0], alog_ref[0], tril_ref[...])
        acst = acs.T
        bb = xa[4].astype(BF16)
        cb_ = xa[5].astype(BF16)
        cbm = _dot_nt(cb_, bb)
        cbmt = _dot_nt(bb, cb_)
        causal = rows >= lanes
        anti = rows <= lanes
        low_half = lanes < LANES // 2
        last_row = rows == L - 1
        cols = [_lane_col(acs, j) for j in range(HEADS_PER_GROUP)]
        zeros = jnp.zeros((L, LANES), F32)
        dcb, dcbt, dbg, dcg, dacs, dacst, ddt = zeros, zeros, zeros, zeros, zeros, zeros, zeros
        dxa = []

        for k in range(PAIRS_PER_GROUP):
            xs = xa[k]
            dtp, eap, decp, etp, xdt = _ssd_pair_terms(k, dt, cols, low_half, xs)
            xdtb = xdt.astype(BF16)
            w = decp * xdt
            wb = w.astype(BF16)
            dyv = dy_ref[k]
            sp = sp_ref[k]
            spb = sp.astype(BF16)
            dsn = dstate[k]
            dsnb = dsn.astype(BF16)
            yoff = eap * _dot(cb_, spb)
            dgb = (eap * dyv).astype(BF16)
            dcg = dcg + _dot_nt(dgb, spb)
            dstate[k] = _dot_tn(cb_, dgb) + etp * dsn
            last_lane = etp * jnp.sum(dsn * sp, axis=0, keepdims=True)
            dbg = dbg + _dot_nt(wb, dsnb)
            dw = _dot(bb, dsnb)
            t2 = dw * w
            dxdt = decp * dw
            last_lane = last_lane + jnp.sum(t2, axis=0, keepdims=True)
            lane_acc = dyv * yoff - t2 + jnp.where(last_row, last_lane, 0.0)
            for j in (2 * k, 2 * k + 1):
                diff = cols[j] - jnp.broadcast_to(acst[j:j + 1, :], (L, L))
                lm = jnp.exp(jnp.where(causal, diff, NEG_BIG))
                lmt = jnp.exp(jnp.where(anti, -diff, NEG_BIG))
                dye = jnp.where(low_half == (j % 2 == 0), dyv, 0.0).astype(BF16)
                dm = _dot_nt(dye, xdtb)
                dmt = _dot_nt(xdtb, dye)
                mt = cbmt * lmt
                seg = dmt * mt - dm * (cbm * lm)
                dacst = dacst + jnp.where(rows == j, jnp.sum(seg, axis=0, keepdims=True), 0.0)
                dcb = dcb + dm * lm
                dcbt = dcbt + dmt * lmt
                dxdt = dxdt + _dot(mt.astype(BF16), dye)
            ddt_lane = dxdt * xs
            for j, keep in ((2 * k, low_half), (2 * k + 1, jnp.logical_not(low_half))):
                dacs = dacs + jnp.where(lanes == j, jnp.sum(jnp.where(keep, lane_acc, 0.0), axis=1, keepdims=True), 0.0)
                ddt = ddt + jnp.where(lanes == j, jnp.sum(jnp.where(keep, ddt_lane, 0.0), axis=1, keepdims=True), 0.0)
            dxa.append(dsk_ref[k][0:1] * dyv + dxdt * dtp)
            dd_ref[k, 0:1, :] += jnp.sum(dyv * xs, axis=0, keepdims=True)

        dxa.append(dbg + _dot(dcbt.astype(BF16), cb_))
        dxa.append(dcg + _dot(dcb.astype(BF16), bb))
        dac = _sel_left(triu_ref[...], _split3(dacs + dacst.T))
        ddtr = jnp.where(valid, (ddt + dac * a) * _sigmoid(pre), 0.0)
        ddtr_ref[0] = ddtr.astype(BF16)
        dsm_ref[0, 0:1, :] += jnp.sum(ddtr, axis=0, keepdims=True)
        dsm_ref[0, 1:2, :] += jnp.sum(dac * dt, axis=0, keepdims=True) * a

        for b in range(6):
            xrb, cw, p8 = blk(b)
            sh = [_shift_down(xrb, j, p8) for j in range(SSD_CONV_W)]
            dxc = dxa[b] * dsil[b]
            acc = dcwx_ref.at[b] if b < 4 else (dcwb_ref.at[0] if b == 4 else dcwc_ref.at[0])
            acc[4:5, :] += jnp.sum(dxc, axis=0, keepdims=True)
            dxr = jnp.zeros_like(dxc)
            for j in range(SSD_CONV_W):
                acc[3 - j:4 - j, :] += jnp.sum(dxc * sh[j], axis=0, keepdims=True)
                dxr = dxr + cw[3 - j:4 - j] * _shift_up(dxc, j, head[b])
            head[b] = dxc[0:SUBLANES]
            out = dxs_ref.at[b] if b < 4 else (db_ref.at[0] if b == 4 else dc_ref.at[0])
            out[...] = dxr.astype(BF16)

    return pl.pallas_call(
        body, name="ssd_scan_bwd", grid=(SSD_N_GROUPS // gps, nc),
        in_specs=[xs_s, b_s, c_s, prev(None), prev(N_XS_BLK), prev(N_XS_BLK + SSD_N_GROUPS), dt_s, grp4, st_s,
                  cwx_s, cwb_s, cwc_s, grp_row, grp_row, cwx_s, _resident(tril.shape), _resident(triu.shape)],
        out_specs=[grp4, grp1, grp1, grp1, acc4, acc1, acc1, acc4, acc1],
        out_shape=[jax.ShapeDtypeStruct((N_XS_BLK, t, LANES), BF16),
                   jax.ShapeDtypeStruct((SSD_N_GROUPS, t, LANES), BF16),
                   jax.ShapeDtypeStruct((SSD_N_GROUPS, t, LANES), BF16),
                   jax.ShapeDtypeStruct((SSD_N_GROUPS, t, LANES), BF16),
                   jax.ShapeDtypeStruct((N_XS_BLK, SUBLANES, LANES), F32),
                   jax.ShapeDtypeStruct((SSD_N_GROUPS, SUBLANES, LANES), F32),
                   jax.ShapeDtypeStruct((SSD_N_GROUPS, SUBLANES, LANES), F32),
                   jax.ShapeDtypeStruct((N_XS_BLK, SUBLANES, LANES), F32),
                   jax.ShapeDtypeStruct((SSD_N_GROUPS, SUBLANES, LANES), F32)],
        scratch_shapes=[pltpu.VMEM((4 * gps, LANES, LANES), F32), pltpu.VMEM((6 * gps, SUBLANES, LANES), F32)],
        compiler_params=_params("arbitrary", "arbitrary"),
    )(xr, xr, xr, xr, xr, xr, dtr, dy, sprev, cwb, cwb, cwb, dtb, alog, dskip, tril, triu)


def _ssd_gate_fwd(x, y, z, gnw, wo):
    t, d = x.shape
    tm = min(TOKEN_TILE, t)
    nb = N_XS_BLK
    per = nb // SSD_N_GROUPS

    def body(x_ref, y_ref, z_ref, gnw_ref, wo_ref, o_ref, gn_ref):
        gs = []
        for j in range(nb):
            zv = z_ref[j]
            gs.append(y_ref[j] * (zv * _sigmoid(zv)))
        for q in range(SSD_N_GROUPS):
            ss = sum(jnp.sum(gs[j] * gs[j], axis=1, keepdims=True) for j in range(q * per, (q + 1) * per))
            inv = lax.rsqrt(ss / (per * LANES) + RMS_EPS)
            for j in range(q * per, (q + 1) * per):
                gn_ref[:, j * LANES:(j + 1) * LANES] = ((gs[j] * inv) * gnw_ref[j]).astype(BF16)
        o_ref[...] = x_ref[...] + _dot(gn_ref[...], wo_ref[...])

    blk = pl.BlockSpec((nb, tm, LANES), lambda i: (0, i, 0))
    return pl.pallas_call(
        body, name="ssd_gate_fwd", grid=(t // tm,),
        in_specs=[pl.BlockSpec((tm, d), lambda i: (i, 0)), blk, blk, _resident((nb, 1, LANES)),
                  _resident((SSD_D_INNER, d))],
        out_specs=[pl.BlockSpec((tm, d), lambda i: (i, 0)), pl.BlockSpec((tm, SSD_D_INNER), lambda i: (i, 0))],
        out_shape=[jax.ShapeDtypeStruct((t, d), F32), jax.ShapeDtypeStruct((t, SSD_D_INNER), BF16)],
        compiler_params=_params("parallel"),
    )(x, y, z, gnw, wo)


def _ssd_gate_bwd(dy, y, z, gnw, wo):
    t, d = dy.shape
    tm = min(TOKEN_TILE, t)
    nb = N_XS_BLK
    per = nb // SSD_N_GROUPS

    def body(dy_ref, y_ref, z_ref, gnw_ref, wo_ref, dys_ref, dz_ref, dgnw_ref):
        @pl.when(pl.program_id(0) == 0)
        def _():
            dgnw_ref[...] = jnp.zeros_like(dgnw_ref)

        dgn = _dot_nt(dy_ref[...].astype(BF16), wo_ref[...])
        for q in range(SSD_N_GROUPS):
            js = range(q * per, (q + 1) * per)
            gs, sil, dsil = {}, {}, {}
            for j in js:
                zv = z_ref[j]
                sig = _sigmoid(zv)
                sil[j] = zv * sig
                dsil[j] = sig * (1.0 + zv * (1.0 - sig))
                gs[j] = y_ref[j] * sil[j]
            ss = sum(jnp.sum(gs[j] * gs[j], axis=1, keepdims=True) for j in js)
            inv = lax.rsqrt(ss / (per * LANES) + RMS_EPS)
            ghat = {j: gs[j] * inv for j in js}
            dgh = {}
            for j in js:
                dj = dgn[:, j * LANES:(j + 1) * LANES]
                dgnw_ref[j] += jnp.sum(dj * ghat[j], axis=0, keepdims=True)
                dgh[j] = dj * gnw_ref[j]
            mean = sum(jnp.sum(dgh[j] * ghat[j], axis=1, keepdims=True) for j in js) / (per * LANES)
            for j in js:
                dg = inv * (dgh[j] - ghat[j] * mean)
                dys_ref[j] = dg * sil[j]
                dz_ref[j] = (dg * y_ref[j] * dsil[j]).astype(BF16)

    blk = pl.BlockSpec((nb, tm, LANES), lambda i: (0, i, 0))
    return pl.pallas_call(
        body, name="ssd_gate_bwd", grid=(t // tm,),
        in_specs=[pl.BlockSpec((tm, d), lambda i: (i, 0)), blk, blk, _resident((nb, 1, LANES)),
                  _resident((SSD_D_INNER, d))],
        out_specs=[blk, blk, pl.BlockSpec((nb, 1, LANES), lambda i: (0, 0, 0))],
        out_shape=[jax.ShapeDtypeStruct((nb, t, LANES), F32), jax.ShapeDtypeStruct((nb, t, LANES), BF16),
                   jax.ShapeDtypeStruct((nb, 1, LANES), F32)],
        compiler_params=_params("arbitrary"),
    )(dy, y, z, gnw, wo)


def _lane_blocks(v):
    r, n = v.shape[0], v.shape[1] // LANES
    return v.reshape(r, n, LANES).transpose(1, 0, 2)


def _ssd_prep(w_in, conv_w, conv_b, dt_bias, a_log, d_skip, norm_w):
    n_main = SSD_D_INNER + SSD_CONV_DIM
    w_dt = w_in[:, n_main:].reshape(-1, SSD_N_GROUPS, HEADS_PER_GROUP)
    w_dt = jnp.pad(w_dt, ((0, 0), (0, 0), (0, LANES - HEADS_PER_GROUP))).reshape(-1, SSD_N_GROUPS * LANES)
    w_in_pad = jnp.concatenate([w_in[:, :n_main], w_dt], axis=1)
    taps = jnp.concatenate([conv_w, conv_b[None], jnp.zeros((SUBLANES - SSD_CONV_W - 1, SSD_CONV_DIM), F32)], axis=0)
    cwb = _lane_blocks(taps)
    row = lambda v: jnp.pad(v.reshape(SSD_N_GROUPS, 1, HEADS_PER_GROUP), ((0, 0), (0, 0), (0, LANES - HEADS_PER_GROUP)))
    dskip = jnp.broadcast_to(jnp.repeat(d_skip, SSD_D_INNER // SSD_N_HEADS).reshape(N_XS_BLK, 1, LANES),
                             (N_XS_BLK, SUBLANES, LANES))
    gnw = norm_w.reshape(N_XS_BLK, 1, LANES)
    return w_in_pad, cwb, row(dt_bias), row(a_log), dskip, gnw


def _ssd_layer_fwd(x, nw, prm, wo, consts):
    w_in_pad, cwb, dtb, alog, dskip, gnw = prm
    z, xr, dtr = _ssd_inproj(x, nw, w_in_pad)
    y, sprev = _ssd_scan_fwd(xr, dtr, cwb, dtb, alog, dskip, consts)
    out, gn = _ssd_gate_fwd(x, y, z, gnw, wo)
    return out, (z, xr, dtr, y, sprev, gn)


def _ssd_layer_bwd(x, dy, nw, prm, wo, consts, saved):
    w_in_pad, cwb, dtb, alog, dskip, gnw = prm
    z, xr, dtr, y, sprev, gn = saved
    dys, dz, dgnw = _ssd_gate_bwd(dy, y, z, gnw, wo)
    dwo = _matmul_tn(gn, dy, name="wgrad_ssd_out")
    dxs, db, dc, ddtr, dcwx, dcwb, dcwc, dd, dsm = _ssd_scan_bwd(xr, dtr, dys, sprev, cwb, dtb, alog, dskip, consts)
    pieces = [dz, dxs, db, dc, ddtr]
    dx, dnw, h = _inproj_bwd(x, dy, nw, [w_in_pad], [pieces])
    dws = [_matmul_tn_blocked(h, p, name=f"wgrad_ssd_in{i}") for i, p in enumerate(pieces)]
    dw_dt = dws[4].reshape(-1, SSD_N_GROUPS, LANES)[:, :, :HEADS_PER_GROUP].reshape(-1, SSD_N_HEADS)
    dw_in = jnp.concatenate(dws[:4] + [dw_dt], axis=1)
    dtaps = jnp.concatenate([dcwx, dcwb, dcwc], axis=0).transpose(1, 0, 2).reshape(SUBLANES, SSD_CONV_DIM)
    by_head = lambda r: dsm[:, r, :HEADS_PER_GROUP].reshape(SSD_N_HEADS)
    d_d = jnp.sum(dd[:, 0, :].reshape(SSD_N_HEADS, SSD_D_INNER // SSD_N_HEADS), axis=1)
    return dx, (dnw, dw_in, dtaps[:SSD_CONV_W], dtaps[SSD_CONV_W], by_head(0), by_head(1),
                d_d, dgnw.reshape(SSD_D_INNER), dwo)


def _loss_head(x, fw, target):
    t, d = x.shape
    tm = min(TOKEN_TILE, t)

    def body(x_ref, fw_ref, tgt_ref, loss_ref, dx_ref, dfw_ref):
        fwv = fw_ref[...]
        y, xhat, inv = _rms_fwd(x_ref[...], fwv)
        err = y - tgt_ref[...]
        tot = jnp.sum(jnp.sum(err * err, axis=1, keepdims=True), axis=0, keepdims=True)
        dx, dw = _rms_bwd(err * (1.0 / d), xhat, inv, fwv)
        dx_ref[...] = dx

        @pl.when(pl.program_id(0) == 0)
        def _():
            loss_ref[...] = jnp.zeros_like(loss_ref)
            dfw_ref[...] = jnp.zeros_like(dfw_ref)

        loss_ref[...] += jnp.broadcast_to(tot * (0.5 / d), loss_ref.shape)
        dfw_ref[...] += dw

    tok = pl.BlockSpec((tm, d), lambda i: (i, 0))
    return pl.pallas_call(
        body, name="loss_head", grid=(t // tm,),
        in_specs=[tok, _resident((1, d)), tok],
        out_specs=[pl.BlockSpec((1, LANES), lambda i: (0, 0)), tok, pl.BlockSpec((1, d), lambda i: (0, 0))],
        out_shape=[jax.ShapeDtypeStruct((1, LANES), F32), jax.ShapeDtypeStruct((t, d), F32),
                   jax.ShapeDtypeStruct((1, d), F32)],
        compiler_params=_params("arbitrary"),
    )(x, fw, target)


def _row_tile(rows, cap):
    best = SUBLANES
    for r in range(SUBLANES, min(rows, cap) + 1, SUBLANES):
        if rows % r == 0:
            best = r
    return best


def _adamw(w, g, m, v, name):
    rows, cols = w.shape
    br = _row_tile(rows, 256)
    c1 = 1.0 - ADAM_B1 ** ADAM_STEP
    c2 = 1.0 - ADAM_B2 ** ADAM_STEP

    def body(w_ref, g_ref, m_ref, v_ref, d_ref, nm_ref, nv_ref):
        gv = g_ref[...]
        nm = ADAM_B1 * m_ref[...] + (1.0 - ADAM_B1) * gv
        nv = ADAM_B2 * v_ref[...] + (1.0 - ADAM_B2) * (gv * gv)
        nm_ref[...] = nm
        nv_ref[...] = nv
        d_ref[...] = -ADAM_LR * ((nm / c1) / (jnp.sqrt(nv / c2) + ADAM_EPS) + ADAM_WD * w_ref[...])

    blk = pl.BlockSpec((br, cols), lambda i: (i, 0))
    shp = jax.ShapeDtypeStruct((rows, cols), F32)
    return pl.pallas_call(
        body, name=name, grid=(rows // br,), in_specs=[blk] * 4, out_specs=[blk] * 3, out_shape=[shp] * 3,
        compiler_params=_params("parallel"),
    )(w, g, m, v)


def _place():
    x, y, c = lax.axis_index("x"), lax.axis_index("y"), lax.axis_index("c")
    return x, y, c, [(1 - x, y), (x, 1 - y), (1 - x, 1 - y)]


def _remote(src, dst, send_sems, recv_sems, k, to):
    return pltpu.make_async_remote_copy(src_ref=src, dst_ref=dst, send_sem=send_sems.at[k], recv_sem=recv_sems.at[k],
                                        device_id=to, device_id_type=MESH)


def _scatter_copies(srcs, dsts, send_sems, recv_sems):
    x, y, c, chips = _place()
    sends, arrivals = [], []
    for oi, (src, dst) in enumerate(zip(srcs, dsts)):
        for j, chip in enumerate(chips):
            sends.append(_remote(src.at[2 * chip[0] + chip[1]], dst.at[j], send_sems, recv_sems, 3 * oi + j, (*chip, c)))
            arrivals.append(_remote(dst.at[j], dst.at[j], send_sems, recv_sems, 3 * oi + j, (*chip, c)))
    return sends, arrivals


def _swap_copies(srcs, dsts, send_sems, recv_sems):
    x, y, c, _ = _place()
    return [_remote(src.at[:, 1 - c], dst, send_sems, recv_sems, oi, (x, y, 1 - c))
            for oi, (src, dst) in enumerate(zip(srcs, dsts))]


def _scatter_start(srcs, dsts, send_sems, recv_sems):
    for cp in _scatter_copies(srcs, dsts, send_sems, recv_sems)[0]:
        cp.start()


def _scatter_wait(srcs, dsts, send_sems, recv_sems):
    sends, arrivals = _scatter_copies(srcs, dsts, send_sems, recv_sems)
    for cp in arrivals:
        cp.wait_recv()
    for cp in sends:
        cp.wait_send()


def _all_gather_shards(arrs):
    n = len(arrs)

    def body(*refs):
        srcs, dsts = refs[:n], refs[n:2 * n]
        send_sems, recv_sems = refs[2 * n:]
        x, y, c, chips = _place()
        me = 2 * x + y
        sibling = (x, y, 1 - c)
        sent = []
        for oi, (src, dst) in enumerate(zip(srcs, dsts)):
            for j, chip in enumerate(chips):
                sent.append(_remote(src.at[c], dst.at[me, c], send_sems, recv_sems, 6 * oi + j, (*chip, c)))
                sent[-1].start()
        for oi, dst in enumerate(dsts):
            for j, chip in enumerate(chips):
                landed = dst.at[2 * chip[0] + chip[1], c]
                _remote(landed, landed, send_sems, recv_sems, 6 * oi + j, (*chip, c)).wait_recv()
                sent.append(_remote(landed, landed, send_sems, recv_sems, 6 * oi + 3 + j, sibling))
                sent[-1].start()
        for oi, dst in enumerate(dsts):
            for j, chip in enumerate(chips):
                landed = dst.at[2 * chip[0] + chip[1], 1 - c]
                _remote(landed, landed, send_sems, recv_sems, 6 * oi + 3 + j, sibling).wait_recv()
        for cp in sent:
            cp.wait_send()

    return pl.pallas_call(
        body, name="all_gather_shards",
        in_specs=[_HBM] * n, out_specs=[_HBM] * n,
        out_shape=[jax.ShapeDtypeStruct((N_SHARDS,) + a.shape, a.dtype) for a in arrs],
        scratch_shapes=[pltpu.SemaphoreType.DMA((6 * n,)), pltpu.SemaphoreType.DMA((6 * n,))],
    )(*arrs)


def _swap_halves(arrs):
    n = len(arrs)

    def body(*refs):
        srcs, dsts = refs[:n], refs[n:2 * n]
        cps = _swap_copies(srcs, dsts, *refs[2 * n:])
        for cp in cps:
            cp.start()
        for cp in cps:
            cp.wait()

    return pl.pallas_call(
        body, name="swap_halves", in_specs=[_HBM] * n, out_specs=[_HBM] * n,
        out_shape=[jax.ShapeDtypeStruct((a.shape[0],) + a.shape[2:], a.dtype) for a in arrs],
        scratch_shapes=[pltpu.SemaphoreType.DMA((n,)), pltpu.SemaphoreType.DMA((n,))],
    )(*arrs)


def _scatter_to_chips(arrs):
    n = len(arrs)

    def body(*refs):
        srcs, dsts = refs[:n], refs[n:2 * n]
        _scatter_start(srcs, dsts, *refs[2 * n:])
        _scatter_wait(srcs, dsts, *refs[2 * n:])

    return pl.pallas_call(
        body, name="scatter_to_chips", in_specs=[_HBM] * n, out_specs=[_HBM] * n,
        out_shape=[jax.ShapeDtypeStruct((3,) + a.shape[1:], a.dtype) for a in arrs],
        scratch_shapes=[pltpu.SemaphoreType.DMA((3 * n,)), pltpu.SemaphoreType.DMA((3 * n,))],
    )(*arrs)


def _join_halves(arrs):
    n = len(arrs)

    def body(*refs):
        bufs = refs[n:2 * n]
        send_sems, recv_sems = refs[2 * n:]
        x, y, c, _ = _place()
        sibling = (x, y, 1 - c)
        sent = [_remote(buf.at[c], buf.at[c], send_sems, recv_sems, oi, sibling) for oi, buf in enumerate(bufs)]
        for cp in sent:
            cp.start()
        for oi, buf in enumerate(bufs):
            _remote(buf.at[1 - c], buf.at[1 - c], send_sems, recv_sems, oi, sibling).wait_recv()
        for cp in sent:
            cp.wait_send()

    return pl.pallas_call(
        body, name="join_halves", in_specs=[_HBM] * n, out_specs=[_HBM] * n,
        out_shape=[jax.ShapeDtypeStruct(a.shape, a.dtype) for a in arrs],
        input_output_aliases={i: i for i in range(n)},
        scratch_shapes=[pltpu.SemaphoreType.DMA((n,)), pltpu.SemaphoreType.DMA((n,))],
    )(*arrs)


def _add_halves(full, recv, place):
    n, _, rows, cols = full.shape
    br = _row_tile(rows, 512)

    def body(p_ref, a_ref, b_ref, o_ref):
        o_ref[...] = (a_ref[...] + b_ref[...]).astype(BF16)

    grid_spec = pltpu.PrefetchScalarGridSpec(
        num_scalar_prefetch=1, grid=(n, rows // br),
        in_specs=[pl.BlockSpec((None, None, br, cols), lambda s, i, p_ref: (s, p_ref[1], i, 0)),
                  pl.BlockSpec((None, br, cols), lambda s, i, p_ref: (s, i, 0))],
        out_specs=pl.BlockSpec((None, br, cols), lambda s, i, p_ref: (s, i, 0)))
    return pl.pallas_call(
        body, name="add_halves", grid_spec=grid_spec, out_shape=jax.ShapeDtypeStruct((n, rows, cols), BF16),
        compiler_params=_params("parallel", "parallel"),
    )(place, full, recv)


def _sum_chips(mine, others, place):
    _, rows, cols = mine.shape
    br = _row_tile(rows, 512)
    slot_of_flip = {2: 0, 1: 1, 3: 2}

    def body(p_ref, m_ref, o_ref, out_ref):
        me = p_ref[0]
        own = m_ref[...].astype(F32)
        got = [o_ref[j].astype(F32) for j in range(3)]
        acc = None
        for s in range(N_SHARDS):
            flip = jnp.bitwise_xor(me, s)
            term = own
            for f, j in slot_of_flip.items():
                term = jnp.where(flip == f, got[j], term)
            acc = term if acc is None else acc + term
        out_ref[...] = acc

    grid_spec = pltpu.PrefetchScalarGridSpec(
        num_scalar_prefetch=1, grid=(rows // br,),
        in_specs=[pl.BlockSpec((None, br, cols), lambda i, p_ref: (p_ref[0], i, 0)),
                  pl.BlockSpec((3, br, cols), lambda i, p_ref: (0, i, 0))],
        out_specs=pl.BlockSpec((None, br, cols), lambda i, p_ref: (p_ref[1], i, 0)))
    return pl.pallas_call(
        body, name="sum_chips", grid_spec=grid_spec, out_shape=jax.ShapeDtypeStruct((2, rows, cols), F32),
        compiler_params=_params("parallel"),
    )(place, mine, others)


WEIGHTS = ("norm_w", "ffn_w_gate", "ffn_w_up", "ffn_w_down", "ssd_w_in", "ssd_conv_w", "ssd_conv_b", "ssd_dt_bias",
           "ssd_a_log", "ssd_d", "ssd_norm_w", "ssd_w_out", "sc_w_in", "sc_conv_w", "sc_w_out", "final_norm_w")
BIG = (("ffn_w_gate", 3), ("ffn_w_up", 3), ("ffn_w_down", 2), ("ssd_w_in", 2), ("ssd_w_out", 1), ("sc_w_in", 2),
       ("sc_w_out", 1))
SMALL_SHARDED = (("norm_w", 2), ("ssd_conv_w", 2), ("sc_conv_w", 2))
REPLICATED = ("ssd_conv_b", "ssd_dt_bias", "ssd_a_log", "ssd_d", "ssd_norm_w", "final_norm_w")
FLAT_COLS = 1024


def _pack(arrays, row_multiple, lead=()):
    flat = jnp.concatenate([a.reshape(lead + (-1,)) for a in arrays], axis=len(lead))
    unit = row_multiple * FLAT_COLS
    n = flat.shape[-1]
    pad = (-n) % unit
    if pad:
        flat = jnp.pad(flat, [(0, 0)] * len(lead) + [(0, pad)])
    return flat.reshape(lead + (-1, FLAT_COLS))


def _unpack(flat, shapes, lead=()):
    flat = flat.reshape(lead + (-1,))
    out, off = [], 0
    for shp in shapes:
        n = 1
        for s in shp:
            n *= s
        out.append(flat[..., off:off + n].reshape(lead + tuple(shp)))
        off += n
    return out


def _to_shards(full, axis):
    shp = full.shape
    r = full.reshape(shp[:axis] + (N_SHARDS, shp[axis] // N_SHARDS) + shp[axis + 1:])
    return jnp.moveaxis(r, axis, 0)


def _from_shards(sh, axis):
    r = jnp.moveaxis(sh, 0, axis)
    shp = r.shape
    return r.reshape(shp[:axis] + (shp[axis] * shp[axis + 1],) + shp[axis + 2:])


def _layer_shards(wl, i):
    j = i // 2
    ffn = lambda k: [(("ffn_w_gate", k), wl["ffn_w_gate"][i, k], 1), (("ffn_w_up", k), wl["ffn_w_up"][i, k], 1),
                     (("ffn_w_down", k), wl["ffn_w_down"][i, k], 0)]
    mix = "ssd" if i % 2 == 0 else "sc"
    return ffn(0), [((mix + "_w_in",), wl[mix + "_w_in"][j], 1), ((mix + "_w_out",), wl[mix + "_w_out"][j], 0)] + ffn(1)


def _assemble(group, received, chip):
    return {key: _from_shards(lax.dynamic_update_index_in_dim(r, own, chip, 0), axis)
            for (key, own, axis), r in zip(group, received)}


def _forward_backward(x, target, p, wl, layer0, place):
    chip = place[0]
    consts = _ssd_consts()
    nw = p["norm_w"]
    row = lambda v: v[None]
    full = {0: layer0}
    ffn = lambda i, k: (full[i]["ffn_w_gate", k], full[i]["ffn_w_up", k], full[i]["ffn_w_down", k])
    ssd_prm, sc_cw = {}, {}

    xin, saved, pre = [], [], {}
    for i in range(N_LAYERS):
        j = i // 2
        first, second = _layer_shards(wl, i + 1) if i + 1 < N_LAYERS else ([], [])
        xin.append(x)
        x, *rest = _ffn_fwd(x, row(nw[i, 0]), *ffn(i, 0), carry=[s[1] for s in first])
        pre[i, 0] = rest[:3]
        if first:
            full[i + 1] = _assemble(first, rest[3:], chip)
        xin.append(x)
        if i % 2 == 0:
            ssd_prm[j] = _ssd_prep(full[i]["ssd_w_in",], p["ssd_conv_w"][j], p["ssd_conv_b"][j], p["ssd_dt_bias"][j],
                                   p["ssd_a_log"][j], p["ssd_d"][j], p["ssd_norm_w"][j])
            x, sv = _ssd_layer_fwd(x, row(nw[i, 1]), ssd_prm[j], full[i]["ssd_w_out",], consts)
        else:
            sc_cw[j] = jnp.pad(p["sc_conv_w"][j], ((0, SUBLANES - SC_CONV_W), (0, 0)))
            sv = _norm_mm(x, row(nw[i, 1]), full[i]["sc_w_in",])
            x = _sc_fwd(x, sv, sc_cw[j], full[i]["sc_w_out",])
        saved.append(sv)
        xin.append(x)
        x, *rest = _ffn_fwd(x, row(nw[i, 2]), *ffn(i, 1), carry=[s[1] for s in second])
        pre[i, 1] = rest[:3]
        if second:
            full[i + 1].update(_assemble(second, rest[3:], chip))
    loss, dx, dfw = _loss_head(x, row(p["final_norm_w"]), target)

    g_nw = [[None] * 3 for _ in range(N_LAYERS)]
    g_ffn = {}
    g_ssd = [None, None]
    g_sc = [None, None]
    halved = lambda a: a.reshape((N_SHARDS, 2, -1, a.shape[-1]))

    def ffn_bwd(i, k, slot, dy, parts=(), sums=()):
        wg, wu, wd = ffn(i, k)
        a, s, p_ = pre[i, k]
        riders = {}
        if parts:
            dxn, dnw, h, dg, du, *theirs = _ffn_bwd_fused(xin[3 * i + slot], dy, s, p_, row(nw[i, slot]), wg, wu, wd,
                                                          swap=parts)
            sums = [_add_halves(mine, got, place) for mine, got in zip(parts, theirs)]
            riders = {"ffn_w_gate": sums[0:1], "ffn_w_up": sums[1:2]}
            arrived = []
        else:
            dxn, dnw, h, dg, du, *arrived = _ffn_bwd_fused(xin[3 * i + slot], dy, s, p_, row(nw[i, slot]), wg, wu, wd,
                                                           carry=sums)
        g_nw[i][slot] = dnw[0]
        for n, lhs, rhs, scale in (("ffn_w_gate", h, dg, 1.0), ("ffn_w_up", h, du, 1.0), ("ffn_w_down", a, dy, 0.5)):
            out = _matmul_tn(lhs, rhs, scale=scale, name="wgrad_" + n, slab=(k,), stack=(2,), buf=g_ffn.get((n, i)),
                             carry=riders.get(n, ()))
            if n in riders:
                out, *landed = out
                arrived += landed
            g_ffn[n, i] = out
        return dxn, arrived, sums

    def finish(sums, arrived):
        return _join_halves([_sum_chips(mine, others, place) for mine, others in zip(sums, arrived)])

    reduced = {}
    waiting = None
    for i in reversed(range(N_LAYERS)):
        j = i // 2
        dx, arrived, sums = ffn_bwd(i, 1, 2, dx, parts=waiting[1] if waiting else ())
        xm = xin[3 * i + 1]
        if i % 2 == 0:
            dx, gs = _ssd_layer_bwd(xm, dx, row(nw[i, 1]), ssd_prm[j], full[i]["ssd_w_out",], consts, saved[i])
            g_nw[i][1] = gs[0][0]
            g_ssd[j] = gs[1:]
            mixer = [(("ssd_w_in", j), _to_shards(gs[1], 1)), (("ssd_w_out", j), _to_shards(gs[-1], 0))]
        else:
            bcu = saved[i]
            dbcu, pin, dcw = _sc_bwd(dx, bcu, sc_cw[j], full[i]["sc_w_out",])
            dwo = _matmul_tn(pin, dx, name="wgrad_sc_out")
            dx, dnw, h = _inproj_bwd(xm, dx, row(nw[i, 1]), [full[i]["sc_w_in",]], [[dbcu]])
            g_nw[i][1] = dnw[0]
            dwi = _matmul_tn(h, dbcu, name="wgrad_sc_in")
            g_sc[j] = dcw[:SC_CONV_W]
            mixer = [(("sc_w_in", j), _to_shards(dwi, 1)), (("sc_w_out", j), _to_shards(dwo, 0))]
        dx, more, _ = ffn_bwd(i, 0, 0, dx, sums=sums[2:])
        if waiting:
            reduced.update(zip(waiting[0], finish(sums, arrived + more)))
        mine = [(("ffn_w_gate", i), _to_shards(g_ffn["ffn_w_gate", i], 2)), (("ffn_w_up", i), _to_shards(g_ffn["ffn_w_up", i], 2)),
                (("ffn_w_down", i), _to_shards(g_ffn["ffn_w_down", i], 1))] + mixer
        waiting = ([key for key, _ in mine], [halved(v) for _, v in mine])

    g = {"norm_w": jnp.stack([jnp.stack(r) for r in g_nw]), "final_norm_w": dfw[0]}
    for k, n in enumerate(("ssd_conv_w", "ssd_conv_b", "ssd_dt_bias", "ssd_a_log", "ssd_d", "ssd_norm_w")):
        g[n] = jnp.stack([g_ssd[0][k + 1], g_ssd[1][k + 1]])
    g["sc_conv_w"] = jnp.stack(g_sc)
    small_part = _pack([_to_shards(g[n], ax) for n, ax in SMALL_SHARDED]
                       + [jnp.broadcast_to(g[n][None], (N_SHARDS,) + g[n].shape) for n in REPLICATED],
                       4 * SUBLANES, lead=(N_SHARDS,))
    parts = waiting[1] + [halved(small_part)]
    last = [_add_halves(mine, got, place) for mine, got in zip(parts, _swap_halves(parts))]
    out = finish(last, _scatter_to_chips(last))
    reduced.update(zip(waiting[0], out[:-1]))
    return loss, dx, reduced, out[-1]


def kernel(x, norm_w, ffn_w_gate, ffn_w_up, ffn_w_down, ssd_w_in, ssd_conv_w, ssd_conv_b, ssd_dt_bias, ssd_a_log, ssd_d, ssd_norm_w, ssd_w_out, sc_w_in, sc_conv_w, sc_w_out, final_norm_w, loss_target, m_norm_w, m_ffn_w_gate, m_ffn_w_up, m_ffn_w_down, m_ssd_w_in, m_ssd_conv_w, m_ssd_conv_b, m_ssd_dt_bias, m_ssd_a_log, m_ssd_d, m_ssd_norm_w, m_ssd_w_out, m_sc_w_in, m_sc_conv_w, m_sc_w_out, m_final_norm_w, v_norm_w, v_ffn_w_gate, v_ffn_w_up, v_ffn_w_down, v_ssd_w_in, v_ssd_conv_w, v_ssd_conv_b, v_ssd_dt_bias, v_ssd_a_log, v_ssd_d, v_ssd_norm_w, v_ssd_w_out, v_sc_w_in, v_sc_conv_w, v_sc_w_out, v_final_norm_w):
    w = dict(zip(WEIGHTS, (norm_w, ffn_w_gate, ffn_w_up, ffn_w_down, ssd_w_in, ssd_conv_w, ssd_conv_b, ssd_dt_bias,
                           ssd_a_log, ssd_d, ssd_norm_w, ssd_w_out, sc_w_in, sc_conv_w, sc_w_out, final_norm_w)))
    m = dict(zip(WEIGHTS, (m_norm_w, m_ffn_w_gate, m_ffn_w_up, m_ffn_w_down, m_ssd_w_in, m_ssd_conv_w, m_ssd_conv_b,
                           m_ssd_dt_bias, m_ssd_a_log, m_ssd_d, m_ssd_norm_w, m_ssd_w_out, m_sc_w_in, m_sc_conv_w,
                           m_sc_w_out, m_final_norm_w)))
    v = dict(zip(WEIGHTS, (v_norm_w, v_ffn_w_gate, v_ffn_w_up, v_ffn_w_down, v_ssd_w_in, v_ssd_conv_w, v_ssd_conv_b,
                           v_ssd_dt_bias, v_ssd_a_log, v_ssd_d, v_ssd_norm_w, v_ssd_w_out, v_sc_w_in, v_sc_conv_w,
                           v_sc_w_out, v_final_norm_w)))
    chip = 2 * lax.axis_index("x") + lax.axis_index("y")
    place = jnp.stack([chip, lax.axis_index("c")]).astype(jnp.int32)
    big_names = [n for n, _ in BIG]
    small_names = [n for n, _ in SMALL_SHARDED] + list(REPLICATED)
    halved = lambda a, lead=(): a.reshape(lead + (2, -1, a.shape[-1]))

    wl = {n: w[n].astype(BF16) for n in big_names}
    first, second = _layer_shards(wl, 0)
    small = halved(_pack([w[n] for n, _ in SMALL_SHARDED], 2 * SUBLANES))
    received = _all_gather_shards([halved(s[1]) for s in first + second] + [small])
    layer0 = _assemble(first + second, [r.reshape((N_SHARDS,) + s[1].shape) for r, s in zip(received, first + second)],
                       chip)
    p = {n: w[n] for n in REPLICATED}
    small_full = lax.dynamic_update_index_in_dim(received[-1], small, chip, 0)
    for (n, ax), sh in zip(SMALL_SHARDED, _unpack(small_full, [w[n].shape for n, _ in SMALL_SHARDED], lead=(N_SHARDS,))):
        p[n] = _from_shards(sh, ax)

    t, d = x.shape[-2:]
    loss, dx, reduced, g_small = _forward_backward(x.reshape(t, d), loss_target.reshape(t, d), p, wl, layer0, place)

    grad = {}
    for n in big_names:
        per_layer = w[n].shape[0]
        grad[n] = jnp.stack([reduced[n, i].reshape(w[n].shape[1:]) for i in range(per_layer)])
    g_small = g_small.reshape(-1, FLAT_COLS)
    grad.update(zip(small_names, _unpack(g_small, [w[n].shape for n in small_names])))

    delta, new_m, new_v = {}, {}, {}
    for n in big_names:
        shp = w[n].shape
        as2d = lambda a: a.reshape(-1, shp[-1])
        out = _adamw(as2d(w[n]), as2d(grad[n]), as2d(m[n]), as2d(v[n]), name="adamw_" + n)
        delta[n], new_m[n], new_v[n] = (o.reshape(shp) for o in out)
    packed = [_pack([s[n] for n in small_names], 4 * SUBLANES) for s in (w, m, v)]
    out = _adamw(packed[0], g_small, packed[1], packed[2], name="adamw_small")
    shapes = [w[n].shape for n in small_names]
    for dst, o in zip((delta, new_m, new_v), out):
        dst.update(zip(small_names, _unpack(o, shapes)))

    loss = lax.psum(loss[0, 0], ("x", "y", "c"))
    return (loss, dx.reshape(x.shape), *[grad[n] for n in WEIGHTS], *[delta[n] for n in WEIGHTS],
            *[new_m[n] for n in WEIGHTS], *[new_v[n] for n in WEIGHTS])
```

```python
import functools

import jax
import jax.numpy as jnp
from jax import lax
from jax.experimental import pallas as pl
from jax.experimental.pallas import tpu as pltpu

F32 = jnp.float32
BF16 = jnp.bfloat16
MESH = pl.DeviceIdType.MESH

RMS_EPS = 1e-5
D_MODEL = 1024
D_FF = 2816
N_LAYERS = 4
SSD_D_INNER = 2048
SSD_N_HEADS = 32
SSD_N_GROUPS = 4
SSD_D_STATE = 128
SSD_CHUNK = 128
SSD_CONV_W = 4
SSD_CONV_DIM = 3072
SSD_IN_DIM = 5152
SC_CONV_W = 3
LANES = 128
SUBLANES = 8
N_XS_BLK = SSD_D_INNER // LANES
SSD_IN_PAD = SSD_D_INNER + SSD_CONV_DIM + SSD_N_GROUPS * LANES
VMEM_LIMIT = 56 * 2**20
TOKEN_TILE = 512
WGRAD_TOKENS = 2048
FF_CHUNK = 256
N_SHARDS = 4

ADAM_LR = 0.001
ADAM_B1 = 0.9
ADAM_B2 = 0.999
ADAM_EPS = 1e-08
ADAM_WD = 0.01
ADAM_STEP = 10


_HBM = pl.BlockSpec(memory_space=pl.ANY)


def _params(*sem):
    return pltpu.CompilerParams(dimension_semantics=sem if sem else None, vmem_limit_bytes=VMEM_LIMIT)


def _dot(a, b):
    return jnp.dot(a, b, preferred_element_type=F32)


def _dot_nt(a, b):
    return lax.dot_general(a, b, (((1,), (1,)), ((), ())), preferred_element_type=F32)


def _dot_tn(a, b):
    return lax.dot_general(a, b, (((0,), (0,)), ((), ())), preferred_element_type=F32)


def _resident(shape):
    n = len(shape)
    return pl.BlockSpec(shape, lambda *_: (0,) * n, pipeline_mode=pl.Buffered(1))


def _split3(v):
    hi = v.astype(BF16)
    r1 = v - hi.astype(F32)
    mid = r1.astype(BF16)
    lo = (r1 - mid.astype(F32)).astype(BF16)
    return hi, mid, lo


def _sel_left(sel, v3):
    return _dot(sel, v3[0]) + _dot(sel, v3[1]) + _dot(sel, v3[2])


def _sigmoid(v):
    return 1.0 / (1.0 + jnp.exp(-v))


def _rms_fwd(x, w):
    inv = lax.rsqrt(jnp.mean(x * x, axis=-1, keepdims=True) + RMS_EPS)
    xhat = x * inv
    return xhat * w, xhat, inv


def _rms_bwd(dh, xhat, inv, w):
    dxhat = dh * w
    dx = inv * (dxhat - xhat * jnp.mean(dxhat * xhat, axis=-1, keepdims=True))
    return dx, jnp.sum(dh * xhat, axis=0, keepdims=True)


def _ffn_fwd(x, nw, wg, wu, wd, carry=()):
    t, d = x.shape
    f = wg.shape[1]
    tm = min(TOKEN_TILE, t)
    nsteps = t // tm
    ncar = len(carry)

    def body(x_ref, nw_ref, wg_ref, wu_ref, wd_ref, *rest):
        srcs = rest[:ncar]
        o_ref, a_ref, s_ref, p_ref = rest[ncar:ncar + 4]
        dsts = rest[ncar + 4:2 * ncar + 4]
        if ncar:
            send_sems, recv_sems = rest[2 * ncar + 4:]
            x_, y_, c_, chips = _place()
            me = 2 * x_ + y_

            @pl.when(pl.program_id(0) == 0)
            def _():
                for oi, (src, dst) in enumerate(zip(srcs, dsts)):
                    for j, chip in enumerate(chips):
                        _remote(src, dst.at[me], send_sems, recv_sems, 3 * oi + j, (*chip, c_)).start()

        xv = x_ref[...]
        h = _rms_fwd(xv, nw_ref[...])[0].astype(BF16)
        for j in range(f // FF_CHUNK):
            sl = slice(j * FF_CHUNK, (j + 1) * FF_CHUNK)
            g = _dot(h, wg_ref[:, sl])
            u = _dot(h, wu_ref[:, sl])
            sig = _sigmoid(g)
            s = g * sig
            a_ref[:, sl] = (s * u).astype(BF16)
            s_ref[:, sl] = s.astype(BF16)
            p_ref[:, sl] = (u * (sig + s * (1.0 - sig))).astype(BF16)
        o_ref[...] = xv + 0.5 * _dot(a_ref[...], wd_ref[...])

        if ncar:
            @pl.when(pl.program_id(0) == nsteps - 1)
            def _():
                for oi, (src, dst) in enumerate(zip(srcs, dsts)):
                    for j, chip in enumerate(chips):
                        landed = dst.at[2 * chip[0] + chip[1]]
                        _remote(landed, landed, send_sems, recv_sems, 3 * oi + j, (*chip, c_)).wait_recv()
                for oi, (src, dst) in enumerate(zip(srcs, dsts)):
                    for j, chip in enumerate(chips):
                        _remote(src, dst.at[me], send_sems, recv_sems, 3 * oi + j, (*chip, c_)).wait_send()

    tok = lambda n: pl.BlockSpec((tm, n), lambda i: (i, 0))
    sems = [pltpu.SemaphoreType.DMA((3 * ncar,)), pltpu.SemaphoreType.DMA((3 * ncar,))] if ncar else []
    return pl.pallas_call(
        body, name="ffn_fwd_carry" if ncar else "ffn_fwd", grid=(nsteps,),
        in_specs=[tok(d), _resident((1, d)), _resident((d, f)), _resident((d, f)), _resident((f, d))] + [_HBM] * ncar,
        out_specs=[tok(d), tok(f), tok(f), tok(f)] + [_HBM] * ncar,
        out_shape=[jax.ShapeDtypeStruct((t, d), F32)] + [jax.ShapeDtypeStruct((t, f), BF16)] * 3
        + [jax.ShapeDtypeStruct((N_SHARDS,) + c.shape, c.dtype) for c in carry],
        scratch_shapes=sems,
        compiler_params=_params("arbitrary" if ncar else "parallel"),
    )(x, nw, wg, wu, wd, *carry)


def _ffn_bwd_fused(x, dy, s, p, nw, wg, wu, wd, carry=(), swap=()):
    assert not (carry and swap)
    t, d = x.shape
    f = wd.shape[0]
    tm = min(TOKEN_TILE // 2, t)
    nsteps = t // tm
    moved = list(carry) + list(swap)
    ncar = len(moved)
    per_copy = 3 if carry else 1

    def body(x_ref, dy_ref, s_ref, p_ref, nw_ref, wg_ref, wu_ref, wd_ref, *rest):
        srcs, (dx_ref, dnw_ref, h_ref, dg_ref, du_ref) = rest[:ncar], rest[ncar:ncar + 5]
        dsts, sems = rest[ncar + 5:2 * ncar + 5], rest[2 * ncar + 5:]
        if carry:
            @pl.when(pl.program_id(0) == 0)
            def _():
                _scatter_start(srcs, dsts, *sems)
        if swap:
            @pl.when(pl.program_id(0) == 0)
            def _():
                for cp in _swap_copies(srcs, dsts, *sems):
                    cp.start()

        dyv = dy_ref[...]
        dob = (0.5 * dyv).astype(BF16)
        for j in range(f // FF_CHUNK):
            sl = slice(j * FF_CHUNK, (j + 1) * FF_CHUNK)
            da = _dot_nt(dob, wd_ref[sl, :])
            dg_ref[:, sl] = (da * p_ref[:, sl].astype(F32)).astype(BF16)
            du_ref[:, sl] = (da * s_ref[:, sl].astype(F32)).astype(BF16)
        nwv = nw_ref[...]
        hf, xhat, inv = _rms_fwd(x_ref[...], nwv)
        h_ref[...] = hf.astype(BF16)
        dh = _dot_nt(dg_ref[...], wg_ref[...]) + _dot_nt(du_ref[...], wu_ref[...])
        dx, dw = _rms_bwd(dh, xhat, inv, nwv)
        dx_ref[...] = dyv + dx

        @pl.when(pl.program_id(0) == 0)
        def _():
            dnw_ref[...] = jnp.zeros_like(dnw_ref)

        dnw_ref[...] += dw

        if carry:
            @pl.when(pl.program_id(0) == nsteps - 1)
            def _():
                _scatter_wait(srcs, dsts, *sems)
        if swap:
            @pl.when(pl.program_id(0) == nsteps - 1)
            def _():
                for cp in _swap_copies(srcs, dsts, *sems):
                    cp.wait()

    tok = lambda n: pl.BlockSpec((tm, n), lambda i: (i, 0))
    nsem = per_copy * ncar
    sems = [pltpu.SemaphoreType.DMA((nsem,)), pltpu.SemaphoreType.DMA((nsem,))] if ncar else []
    name = "ffn_bwd_fused" + ("_carry" if carry else "_swap" if swap else "")
    return pl.pallas_call(
        body, name=name, grid=(nsteps,),
        in_specs=[tok(d), tok(d), tok(f), tok(f), _resident((1, d)), _resident((d, f)), _resident((d, f)),
                  _resident((f, d))] + [_HBM] * ncar,
        out_specs=[tok(d), pl.BlockSpec((1, d), lambda i: (0, 0)), tok(d), tok(f), tok(f)] + [_HBM] * ncar,
        out_shape=[jax.ShapeDtypeStruct((t, d), F32), jax.ShapeDtypeStruct((1, d), F32),
                   jax.ShapeDtypeStruct((t, d), BF16), jax.ShapeDtypeStruct((t, f), BF16),
                   jax.ShapeDtypeStruct((t, f), BF16)]
        + [jax.ShapeDtypeStruct((3,) + c.shape[1:], c.dtype) for c in carry]
        + [jax.ShapeDtypeStruct((c.shape[0],) + c.shape[2:], c.dtype) for c in swap],
        scratch_shapes=sems,
        compiler_params=_params("arbitrary"),
    )(x, dy, s, p, nw, wg, wu, wd, *moved)


def _pick_bn(m, n, unit):
    best = unit
    for k in range(1, n // unit + 1):
        bn = k * unit
        if n % bn == 0 and m * bn * 4 <= 8 * 2**20:
            best = bn
    return best


def _matmul_tn(a, b, scale=1.0, name="wgrad", slab=None, stack=None, buf=None, carry=()):
    t, m = a.shape
    n = b.shape[1]
    bt = min(WGRAD_TOKENS, t)
    bn = _pick_bn(m, n, LANES)
    nt, nj = t // bt, n // bn
    lead = tuple(slab) if slab is not None else ()
    nbuf = 0 if buf is None else 1
    ncar = len(carry)

    def body(a_ref, b_ref, *rest):
        srcs, o_ref = rest[nbuf:nbuf + ncar], rest[nbuf + ncar]
        dsts, sems = rest[nbuf + ncar + 1:nbuf + 2 * ncar + 1], rest[nbuf + 2 * ncar + 1:]
        if ncar:
            @pl.when((pl.program_id(0) == 0) & (pl.program_id(1) == 0))
            def _():
                _scatter_start(srcs, dsts, *sems)

        @pl.when(pl.program_id(1) == 0)
        def _():
            o_ref[...] = jnp.zeros_like(o_ref)

        o_ref[...] += _dot_tn(a_ref[...].astype(BF16), b_ref[...].astype(BF16))
        if scale != 1.0:
            @pl.when(pl.program_id(1) == nt - 1)
            def _():
                o_ref[...] *= scale

        if ncar:
            @pl.when((pl.program_id(0) == nj - 1) & (pl.program_id(1) == nt - 1))
            def _():
                _scatter_wait(srcs, dsts, *sems)

    in_specs = [pl.BlockSpec((bt, m), lambda j, k: (k, 0)), pl.BlockSpec((bt, bn), lambda j, k: (k, j))]
    args = [a, b]
    if buf is not None:
        in_specs.append(_HBM)
        args.append(buf)
    sems = [pltpu.SemaphoreType.DMA((3 * ncar,)), pltpu.SemaphoreType.DMA((3 * ncar,))] if ncar else []
    out = pl.pallas_call(
        body, name=name + "_carry" if ncar else name, grid=(nj, nt),
        in_specs=in_specs + [_HBM] * ncar,
        out_specs=[pl.BlockSpec((None,) * len(lead) + (m, bn), lambda j, k: lead + (0, j))] + [_HBM] * ncar,
        out_shape=[jax.ShapeDtypeStruct(tuple(stack or ()) + (m, n), F32)]
        + [jax.ShapeDtypeStruct((3,) + c.shape[1:], c.dtype) for c in carry],
        input_output_aliases={2: 0} if buf is not None else {},
        scratch_shapes=sems,
        compiler_params=_params("arbitrary" if ncar else "parallel", "arbitrary"),
    )(*args, *carry)
    return out if ncar else out[0]


def _matmul_tn_blocked(a, b, name="wgrad_blk"):
    t, m = a.shape
    nb = b.shape[0]
    bt = min(1024, t)
    nbt = _pick_bn(m, nb * LANES, LANES) // LANES
    while nb % nbt:
        nbt -= 1

    def body(a_ref, b_ref, o_ref):
        @pl.when(pl.program_id(1) == 0)
        def _():
            o_ref[...] = jnp.zeros_like(o_ref)

        bv = jnp.concatenate([b_ref[i] for i in range(nbt)], axis=1) if nbt > 1 else b_ref[0]
        o_ref[...] += _dot_tn(a_ref[...], bv)

    return pl.pallas_call(
        body, name=name, grid=(nb // nbt, t // bt),
        in_specs=[pl.BlockSpec((bt, m), lambda j, k: (k, 0)), pl.BlockSpec((nbt, bt, LANES), lambda j, k: (j, k, 0))],
        out_specs=pl.BlockSpec((m, nbt * LANES), lambda j, k: (0, j)),
        out_shape=jax.ShapeDtypeStruct((m, nb * LANES), F32),
        compiler_params=_params("parallel", "arbitrary"),
    )(a, b)


def _norm_mm(x, nw, w):
    t, d = x.shape
    n = w.shape[1]
    tm = min(TOKEN_TILE, t)
    cn = 1024 if n % 1024 == 0 else n

    def body(x_ref, nw_ref, w_ref, o_ref):
        h = _rms_fwd(x_ref[...], nw_ref[...])[0].astype(BF16)
        for j in range(n // cn):
            sl = slice(j * cn, (j + 1) * cn)
            o_ref[:, sl] = _dot(h, w_ref[:, sl])

    return pl.pallas_call(
        body, name="norm_mm", grid=(t // tm,),
        in_specs=[pl.BlockSpec((tm, d), lambda i: (i, 0)), _resident((1, d)), _resident((d, n))],
        out_specs=pl.BlockSpec((tm, n), lambda i: (i, 0)),
        out_shape=jax.ShapeDtypeStruct((t, n), F32),
        compiler_params=_params("parallel"),
    )(x, nw, w)


def _ssd_inproj(x, nw, w):
    t, d = x.shape
    tm = min(TOKEN_TILE, t)
    nz, nx, ng = SSD_D_INNER // LANES, SSD_CONV_DIM // LANES, SSD_N_GROUPS
    cn = 1024

    def body(x_ref, nw_ref, w_ref, z_ref, xr_ref, dt_ref):
        h = _rms_fwd(x_ref[...], nw_ref[...])[0].astype(BF16)
        for j in range(-(-SSD_IN_PAD // cn)):
            lo, hi = j * cn, min((j + 1) * cn, SSD_IN_PAD)
            r = _dot(h, w_ref[:, lo:hi])
            for i in range((hi - lo) // LANES):
                blk = j * (cn // LANES) + i
                v = r[:, i * LANES:(i + 1) * LANES]
                if blk < nz:
                    z_ref[blk] = v
                elif blk < nz + nx:
                    xr_ref[blk - nz] = v
                else:
                    dt_ref[blk - nz - nx] = v

    out = lambda n: pl.BlockSpec((n, tm, LANES), lambda i: (0, i, 0))
    return pl.pallas_call(
        body, name="ssd_inproj", grid=(t // tm,),
        in_specs=[pl.BlockSpec((tm, d), lambda i: (i, 0)), _resident((1, d)), _resident((d, SSD_IN_PAD))],
        out_specs=[out(nz), out(nx), out(ng)],
        out_shape=[jax.ShapeDtypeStruct((n, t, LANES), F32) for n in (nz, nx, ng)],
        compiler_params=_params("parallel"),
    )(x, nw, w)


def _inproj_bwd(x, dy, nw, ws, pieces, carry=()):
    t, d = x.shape
    tm = min(TOKEN_TILE, t)
    nsteps = t // tm
    nws = len(ws)
    ncar = len(carry)
    flat = [p for group in pieces for p in group]

    def body(*refs):
        x_ref, dy_ref, nw_ref = refs[:3]
        w_refs = refs[3:3 + nws]
        p_refs = list(refs[3 + nws:3 + nws + len(flat)])
        rest = refs[3 + nws + len(flat):]
        srcs, (dx_ref, dnw_ref, h_ref) = rest[:ncar], rest[ncar:ncar + 3]
        dsts, sems = rest[ncar + 3:2 * ncar + 3], rest[2 * ncar + 3:]
        if ncar:
            @pl.when(pl.program_id(0) == 0)
            def _():
                _scatter_start(srcs, dsts, *sems)

        nwv = nw_ref[...]
        hf, xhat, inv = _rms_fwd(x_ref[...], nwv)
        h_ref[...] = hf.astype(BF16)
        dh = None
        for w_ref, group in zip(w_refs, pieces):
            parts = []
            for _ in group:
                p = p_refs.pop(0)
                parts += [p[i] for i in range(p.shape[0])] if len(p.shape) == 3 else [p[...]]
            dz = jnp.concatenate(parts, axis=1) if len(parts) > 1 else parts[0]
            part = _dot_nt(dz, w_ref[...])
            dh = part if dh is None else dh + part
        dx, dw = _rms_bwd(dh, xhat, inv, nwv)
        dx_ref[...] = dy_ref[...] + dx

        @pl.when(pl.program_id(0) == 0)
        def _():
            dnw_ref[...] = jnp.zeros_like(dnw_ref)

        dnw_ref[...] += dw

        if ncar:
            @pl.when(pl.program_id(0) == nsteps - 1)
            def _():
                _scatter_wait(srcs, dsts, *sems)

    tok = lambda m: pl.BlockSpec((tm, m), lambda i: (i, 0))
    p_specs = [pl.BlockSpec((p.shape[0], tm, LANES), lambda i: (0, i, 0)) if p.ndim == 3 else tok(p.shape[1])
               for p in flat]
    sems = [pltpu.SemaphoreType.DMA((3 * ncar,)), pltpu.SemaphoreType.DMA((3 * ncar,))] if ncar else []
    return pl.pallas_call(
        body, name="inproj_bwd_carry" if ncar else "inproj_bwd", grid=(nsteps,),
        in_specs=[tok(d), tok(d), _resident((1, d))] + [_resident(w.shape) for w in ws] + p_specs + [_HBM] * ncar,
        out_specs=[tok(d), pl.BlockSpec((1, d), lambda i: (0, 0)), tok(d)] + [_HBM] * ncar,
        out_shape=[jax.ShapeDtypeStruct((t, d), F32), jax.ShapeDtypeStruct((1, d), F32),
                   jax.ShapeDtypeStruct((t, d), BF16)]
        + [jax.ShapeDtypeStruct((3,) + c.shape[1:], c.dtype) for c in carry],
        scratch_shapes=sems,
        compiler_params=_params("arbitrary"),
    )(x, dy, nw, *ws, *flat, *carry)


def _shift_down(v, j, prev8):
    if j == 0:
        return v
    r = pltpu.roll(v, j, 0)
    p = pltpu.roll(prev8, j, 0)
    rows = lax.broadcasted_iota(jnp.int32, prev8.shape, 0)
    first = jnp.where(rows < j, p, r[0:SUBLANES])
    return jnp.concatenate([first, r[SUBLANES:]], axis=0)


def _shift_up(v, j, next8):
    if j == 0:
        return v
    n = v.shape[0]
    r = pltpu.roll(v, n - j, 0)
    p = pltpu.roll(next8, SUBLANES - j, 0)
    rows = lax.broadcasted_iota(jnp.int32, next8.shape, 0)
    last = jnp.where(rows >= SUBLANES - j, p, r[n - SUBLANES:])
    return jnp.concatenate([r[:n - SUBLANES], last], axis=0)


def _sc_fwd(x, bcu, cw, wo):
    t, d = x.shape
    tm = min(TOKEN_TILE, t)
    hb = tm // SUBLANES

    def body(x_ref, bcu_ref, prev_ref, cw_ref, wo_ref, o_ref):
        bg, cg, u = bcu_ref[:, 0:d], bcu_ref[:, d:2 * d], bcu_ref[:, 2 * d:3 * d]
        q = cg * u
        qp = jnp.where(pl.program_id(0) == 0, 0.0, prev_ref[:, d:2 * d] * prev_ref[:, 2 * d:3 * d])
        cwv = cw_ref[...]
        v = cwv[2:3] * q + cwv[1:2] * _shift_down(q, 1, qp) + cwv[0:1] * _shift_down(q, 2, qp)
        o_ref[...] = x_ref[...] + _dot((bg * v).astype(BF16), wo_ref[...])

    return pl.pallas_call(
        body, name="sc_fwd", grid=(t // tm,),
        in_specs=[pl.BlockSpec((tm, d), lambda i: (i, 0)), pl.BlockSpec((tm, 3 * d), lambda i: (i, 0)),
                  pl.BlockSpec((SUBLANES, 3 * d), lambda i: (jnp.maximum(i * hb - 1, 0), 0)),
                  _resident((SUBLANES, d)), _resident((d, d))],
        out_specs=pl.BlockSpec((tm, d), lambda i: (i, 0)),
        out_shape=jax.ShapeDtypeStruct((t, d), F32),
        compiler_params=_params("parallel"),
    )(x, bcu, bcu, cw, wo)


def _sc_bwd(dy, bcu, cw, wo):
    t, d = dy.shape
    tm = min(TOKEN_TILE, t)
    hb = tm // SUBLANES
    nt = t // tm

    def body(dy_ref, dyn_ref, bcu_ref, prev_ref, next_ref, cw_ref, wo_ref, dbcu_ref, p_ref, dcw_ref):
        i = pl.program_id(0)
        bg, cg, u = bcu_ref[:, 0:d], bcu_ref[:, d:2 * d], bcu_ref[:, 2 * d:3 * d]
        q = cg * u
        qp = jnp.where(i == 0, 0.0, prev_ref[:, d:2 * d] * prev_ref[:, 2 * d:3 * d])
        cwv = cw_ref[...]
        q1 = _shift_down(q, 1, qp)
        q2 = _shift_down(q, 2, qp)
        v = cwv[2:3] * q + cwv[1:2] * q1 + cwv[0:1] * q2
        p_ref[...] = (bg * v).astype(BF16)
        wov = wo_ref[...]
        dp = _dot_nt(dy_ref[...].astype(BF16), wov)
        dpn = _dot_nt(dyn_ref[...].astype(BF16), wov)
        dv = dp * bg
        dvn = jnp.where(i == nt - 1, 0.0, dpn * next_ref[:, 0:d])
        dq = cwv[2:3] * dv + cwv[1:2] * _shift_up(dv, 1, dvn) + cwv[0:1] * _shift_up(dv, 2, dvn)
        dbcu_ref[:, 0:d] = (dp * v).astype(BF16)
        dbcu_ref[:, d:2 * d] = (dq * u).astype(BF16)
        dbcu_ref[:, 2 * d:3 * d] = (dq * cg).astype(BF16)

        @pl.when(i == 0)
        def _():
            dcw_ref[...] = jnp.zeros_like(dcw_ref)

        dcw_ref[0:1, :] += jnp.sum(dv * q2, axis=0, keepdims=True)
        dcw_ref[1:2, :] += jnp.sum(dv * q1, axis=0, keepdims=True)
        dcw_ref[2:3, :] += jnp.sum(dv * q, axis=0, keepdims=True)

    last8 = t // SUBLANES - 1
    return pl.pallas_call(
        body, name="sc_bwd", grid=(nt,),
        in_specs=[pl.BlockSpec((tm, d), lambda i: (i, 0)),
                  pl.BlockSpec((SUBLANES, d), lambda i: (jnp.minimum((i + 1) * hb, last8), 0)),
                  pl.BlockSpec((tm, 3 * d), lambda i: (i, 0)),
                  pl.BlockSpec((SUBLANES, 3 * d), lambda i: (jnp.maximum(i * hb - 1, 0), 0)),
                  pl.BlockSpec((SUBLANES, 3 * d), lambda i: (jnp.minimum((i + 1) * hb, last8), 0)),
                  _resident((SUBLANES, d)), _resident((d, d))],
        out_specs=[pl.BlockSpec((tm, 3 * d), lambda i: (i, 0)), pl.BlockSpec((tm, d), lambda i: (i, 0)),
                   pl.BlockSpec((SUBLANES, d), lambda i: (0, 0))],
        out_shape=[jax.ShapeDtypeStruct((t, 3 * d), BF16), jax.ShapeDtypeStruct((t, d), BF16),
                   jax.ShapeDtypeStruct((SUBLANES, d), F32)],
        compiler_params=_params("arbitrary"),
    )(dy, dy, bcu, bcu, bcu, cw, wo)


NEG_BIG = -1e30


HEADS_PER_GROUP = SSD_N_HEADS // SSD_N_GROUPS
PAIRS_PER_GROUP = HEADS_PER_GROUP // 2
GROUPS_PER_STEP = 4


def _ssd_consts():
    r = lax.broadcasted_iota(jnp.int32, (LANES, LANES), 0)
    c = lax.broadcasted_iota(jnp.int32, (LANES, LANES), 1)
    return (c <= r).astype(BF16), (c >= r).astype(BF16)


def _ssd_decay(dtr, dtb, alog, tril):
    shape = (SSD_CHUNK, LANES)
    lanes = lax.broadcasted_iota(jnp.int32, shape, 1)
    rows = lax.broadcasted_iota(jnp.int32, shape, 0)
    pre = dtr + dtb
    valid = lanes < HEADS_PER_GROUP
    dt = jnp.where(valid, jnp.maximum(pre, 0.0) + jnp.log(1.0 + jnp.exp(-jnp.abs(pre))), 0.0)
    a = -jnp.exp(alog)
    acs = _sel_left(tril, _split3(dt * a))
    return dt, a, acs, pre, valid, rows, lanes


def _lane_col(v, j):
    return jnp.broadcast_to(v[:, j:j + 1], v.shape)


def _ssd_pair_terms(k, dt, cols, low_half, xs):
    dtp = jnp.where(low_half, _lane_col(dt, 2 * k), _lane_col(dt, 2 * k + 1))
    acsp = jnp.where(low_half, cols[2 * k], cols[2 * k + 1])
    lastp = acsp[SSD_CHUNK - 1:SSD_CHUNK, :]
    eap = jnp.exp(acsp)
    decp = jnp.exp(lastp - acsp)
    etp = jnp.exp(lastp)
    xdt = xs * dtp
    return dtp, eap, decp, etp, xdt


def _ssd_conv_taps(cwb, xr, prev8):
    sh = [_shift_down(xr, j, prev8) for j in range(SSD_CONV_W)]
    xc = cwb[4:5]
    for j in range(SSD_CONV_W):
        xc = xc + cwb[3 - j:4 - j] * sh[j]
    return xc, sh


def _ssd_specs(nc, rev):
    ch = (lambda i: nc - 1 - i) if rev else (lambda i: i)
    L = SSD_CHUNK
    gps = GROUPS_PER_STEP
    b0, c0 = N_XS_BLK // gps, (N_XS_BLK + SSD_N_GROUPS) // gps
    xs = pl.BlockSpec((4 * gps, L, LANES), lambda g, i: (g, ch(i), 0))
    bb = pl.BlockSpec((gps, L, LANES), lambda g, i: (b0 + g, ch(i), 0))
    cc = pl.BlockSpec((gps, L, LANES), lambda g, i: (c0 + g, ch(i), 0))
    dt = pl.BlockSpec((gps, L, LANES), lambda g, i: (g, ch(i), 0))
    cw_xs = pl.BlockSpec((4 * gps, SUBLANES, LANES), lambda g, i: (g, 0, 0))
    cw_b = pl.BlockSpec((gps, SUBLANES, LANES), lambda g, i: (b0 + g, 0, 0))
    cw_c = pl.BlockSpec((gps, SUBLANES, LANES), lambda g, i: (c0 + g, 0, 0))
    st = pl.BlockSpec((1, 4 * gps, LANES, LANES), lambda g, i: (ch(i), g, 0, 0))
    grp4 = pl.BlockSpec((4 * gps, L, LANES), lambda g, i: (g, ch(i), 0))
    return xs, bb, cc, dt, cw_xs, cw_b, cw_c, st, grp4


def _group_views(q, refs4, refs1, refs6=()):
    return ([r.at[pl.ds(4 * q, 4)] for r in refs4], [r.at[pl.ds(q, 1)] for r in refs1],
            [r.at[pl.ds(6 * q, 6)] for r in refs6])


def _ssd_scan_fwd(xr, dtr, cwb, dtb, alog, dskip, consts):
    t = xr.shape[1]
    L = SSD_CHUNK
    nc = t // L
    tril, _ = consts
    xs_s, b_s, c_s, dt_s, cwx_s, cwb_s, cwc_s, st_s, grp4 = _ssd_specs(nc, False)
    gps = GROUPS_PER_STEP
    grp_row = pl.BlockSpec((gps, 1, LANES), lambda g, i: (g, 0, 0))

    def body(xs_all, b_all, c_all, dtr_all, cwx_all, cwbb_all, cwc_all, dtb_all, alog_all, dsk_all,
             tril_ref, y_all, sp_all, state_all, tail_all):
        @pl.when(pl.program_id(1) == 0)
        def _():
            state_all[...] = jnp.zeros_like(state_all)
            tail_all[...] = jnp.zeros_like(tail_all)

        for q in range(gps):
            fours, ones, sixes = _group_views(q, (xs_all, cwx_all, dsk_all, y_all, sp_all.at[0], state_all),
                                              (b_all, c_all, dtr_all, cwbb_all, cwc_all, dtb_all, alog_all), (tail_all,))
            group(*fours, *ones, *sixes, tril_ref)

    def group(xs_ref, cwx_ref, dsk_ref, y_ref, sp_ref, state, b_ref, c_ref, dtr_ref, cwbb_ref, cwc_ref,
              dtb_ref, alog_ref, tail, tril_ref):
        xa = []
        for b in range(6):
            xrb = xs_ref[b] if b < 4 else (b_ref[0] if b == 4 else c_ref[0])
            cw = cwx_ref[b] if b < 4 else (cwbb_ref[0] if b == 4 else cwc_ref[0])
            xc, _ = _ssd_conv_taps(cw, xrb, tail[b])
            tail[b] = xrb[L - SUBLANES:]
            xa.append(xc * _sigmoid(xc))

        dt, a, acs, _, _, rows, lanes = _ssd_decay(dtr_ref[0], dtb_ref[0], alog_ref[0], tril_ref[...])
        acst = acs.T
        bb = xa[4].astype(BF16)
        cb_ = xa[5].astype(BF16)
        cbm = _dot_nt(cb_, bb)
        causal = rows >= lanes
        low_half = lanes < LANES // 2
        cols = [_lane_col(acs, j) for j in range(HEADS_PER_GROUP)]

        for k in range(PAIRS_PER_GROUP):
            xs = xa[k]
            dtp, eap, decp, etp, xdt = _ssd_pair_terms(k, dt, cols, low_half, xs)
            ms = []
            for j in (2 * k, 2 * k + 1):
                diff = cols[j] - jnp.broadcast_to(acst[j:j + 1, :], (L, L))
                ms.append((cbm * jnp.exp(jnp.where(causal, diff, NEG_BIG))).astype(BF16))
            xcat = jnp.concatenate([jnp.where(low_half, xdt, 0.0).astype(BF16),
                                    jnp.where(low_half, 0.0, xdt).astype(BF16)], axis=0)
            yd = _dot(jnp.concatenate(ms, axis=1), xcat)
            sp = state[k]
            yo = eap * _dot(cb_, sp.astype(BF16))
            y_ref[k] = yd + yo + dsk_ref[k][0:1] * xs
            sp_ref[k] = sp
            state[k] = etp * sp + _dot_tn(bb, (decp * xdt).astype(BF16))

    return pl.pallas_call(
        body, name="ssd_scan_fwd", grid=(SSD_N_GROUPS // gps, nc),
        in_specs=[xs_s, b_s, c_s, dt_s, cwx_s, cwb_s, cwc_s, grp_row, grp_row, cwx_s, _resident(tril.shape)],
        out_specs=[grp4, st_s],
        out_shape=[jax.ShapeDtypeStruct((N_XS_BLK, t, LANES), F32),
                   jax.ShapeDtypeStruct((nc, N_XS_BLK, LANES, LANES), F32)],
        scratch_shapes=[pltpu.VMEM((4 * gps, LANES, LANES), F32), pltpu.VMEM((6 * gps, SUBLANES, LANES), F32)],
        compiler_params=_params("arbitrary", "arbitrary"),
    )(xr, xr, xr, dtr, cwb, cwb, cwb, dtb, alog, dskip, tril)


def _ssd_scan_bwd(xr, dtr, dy, sprev, cwb, dtb, alog, dskip, consts):
    t = xr.shape[1]
    L = SSD_CHUNK
    nc = t // L
    hb = L // SUBLANES
    tril, triu = consts
    xs_s, b_s, c_s, dt_s, cwx_s, cwb_s, cwc_s, st_s, grp4 = _ssd_specs(nc, True)
    gps = GROUPS_PER_STEP
    grp_row = pl.BlockSpec((gps, 1, LANES), lambda g, i: (g, 0, 0))
    prev = lambda off: pl.BlockSpec(
        (4 * gps if off is None else gps, SUBLANES, LANES),
        (lambda g, i: (g, jnp.maximum((nc - 1 - i) * hb - 1, 0), 0)) if off is None else
        (lambda g, i: (off // gps + g, jnp.maximum((nc - 1 - i) * hb - 1, 0), 0)))
    grp1 = pl.BlockSpec((gps, L, LANES), lambda g, i: (g, nc - 1 - i, 0))
    acc4 = pl.BlockSpec((4 * gps, SUBLANES, LANES), lambda g, i: (g, 0, 0))
    acc1 = pl.BlockSpec((gps, SUBLANES, LANES), lambda g, i: (g, 0, 0))

    def body(xs_all, b_all, c_all, pxs_all, pb_all, pc_all, dtr_all, dy_all, sp_all,
             cwx_all, cwbb_all, cwc_all, dtb_all, alog_all, dsk_all, tril_ref, triu_ref,
             dxs_all, db_all, dc_all, ddtr_all, dcwx_all, dcwb_all, dcwc_all, dd_all, dsm_all,
             dstate_all, head_all):
        @pl.when(pl.program_id(1) == 0)
        def _():
            for r in (dstate_all, head_all, dcwx_all, dcwb_all, dcwc_all, dd_all, dsm_all):
                r[...] = jnp.zeros_like(r)

        for q in range(gps):
            fours, ones, sixes = _group_views(
                q, (xs_all, pxs_all, dy_all, sp_all.at[0], cwx_all, dsk_all, dxs_all, dcwx_all, dd_all, dstate_all),
                (b_all, c_all, pb_all, pc_all, dtr_all, cwbb_all, cwc_all, dtb_all, alog_all, db_all, dc_all, ddtr_all,
                 dcwb_all, dcwc_all, dsm_all), (head_all,))
            group(*fours, *ones, *sixes, tril_ref, triu_ref)

    def group(xs_ref, pxs_ref, dy_ref, sp_ref, cwx_ref, dsk_ref, dxs_ref, dcwx_ref, dd_ref, dstate,
              b_ref, c_ref, pb_ref, pc_ref, dtr_ref, cwbb_ref, cwc_ref, dtb_ref, alog_ref, db_ref, dc_ref, ddtr_ref,
              dcwb_ref, dcwc_ref, dsm_ref, head, tril_ref, triu_ref):
        first_chunk = pl.program_id(1) == nc - 1

        def blk(b):
            xrb = xs_ref[b] if b < 4 else (b_ref[0] if b == 4 else c_ref[0])
            cw = cwx_ref[b] if b < 4 else (cwbb_ref[0] if b == 4 else cwc_ref[0])
            p8 = pxs_ref[b] if b < 4 else (pb_ref[0] if b == 4 else pc_ref[0])
            return xrb, cw, jnp.where(first_chunk, 0.0, p8)

        xa, dsil = [], []
        for b in range(6):
            xrb, cw, p8 = blk(b)
            xc, _ = _ssd_conv_taps(cw, xrb, p8)
            sig = _sigmoid(xc)
            xa.append(xc * sig)
            dsil.append(sig * (1.0 + xc * (1.0 - sig)))

        dt, a, acs, pre, valid, rows, lanes = _ssd_decay(dtr_ref[0], dtb_ref[0], alog_ref[0], tril_ref[...])
        acst = acs.T
        bb = xa[4].astype(BF16)
        cb_ = xa[5].astype(BF16)
        cbm = _dot_nt(cb_, bb)
        cbmt = _dot_nt(bb, cb_)
        causal = rows >= lanes
        anti = rows <= lanes
        low_half = lanes < LANES // 2
        last_row = rows == L - 1
        cols = [_lane_col(acs, j) for j in range(HEADS_PER_GROUP)]
        zeros = jnp.zeros((L, LANES), F32)
        dcb, dcbt, dbg, dcg, dacs, dacst, ddt = zeros, zeros, zeros, zeros, zeros, zeros, zeros
        dxa = []

        for k in range(PAIRS_PER_GROUP):
            xs = xa[k]
            dtp, eap, decp, etp, xdt = _ssd_pair_terms(k, dt, cols, low_half, xs)
            xdtb = xdt.astype(BF16)
            w = decp * xdt
            wb = w.astype(BF16)
            dyv = dy_ref[k]
            sp = sp_ref[k]
            spb = sp.astype(BF16)
            dsn = dstate[k]
            dsnb = dsn.astype(BF16)
            yoff = eap * _dot(cb_, spb)
            dgb = (eap * dyv).astype(BF16)
            dcg = dcg + _dot_nt(dgb, spb)
            dstate[k] = _dot_tn(cb_, dgb) + etp * dsn
            last_lane = etp * jnp.sum(dsn * sp, axis=0, keepdims=True)
            dbg = dbg + _dot_nt(wb, dsnb)
            dw = _dot(bb, dsnb)
            t2 = dw * w
            dxdt = decp * dw
            last_lane = last_lane + jnp.sum(t2, axis=0, keepdims=True)
            lane_acc = dyv * yoff - t2 + jnp.where(last_row, last_lane, 0.0)
            for j in (2 * k, 2 * k + 1):
                diff = cols[j] - jnp.broadcast_to(acst[j:j + 1, :], (L, L))
                lm = jnp.exp(jnp.where(causal, diff, NEG_BIG))
                lmt = jnp.exp(jnp.where(anti, -diff, NEG_BIG))
                dye = jnp.where(low_half == (j % 2 == 0), dyv, 0.0).astype(BF16)
                dm = _dot_nt(dye, xdtb)
                dmt = _dot_nt(xdtb, dye)
                mt = cbmt * lmt
                seg = dmt * mt - dm * (cbm * lm)
                dacst = dacst + jnp.where(rows == j, jnp.sum(seg, axis=0, keepdims=True), 0.0)
                dcb = dcb + dm * lm
                dcbt = dcbt + dmt * lmt
                dxdt = dxdt + _dot(mt.astype(BF16), dye)
            ddt_lane = dxdt * xs
            for j, keep in ((2 * k, low_half), (2 * k + 1, jnp.logical_not(low_half))):
                dacs = dacs + jnp.where(lanes == j, jnp.sum(jnp.where(keep, lane_acc, 0.0), axis=1, keepdims=True), 0.0)
                ddt = ddt + jnp.where(lanes == j, jnp.sum(jnp.where(keep, ddt_lane, 0.0), axis=1, keepdims=True), 0.0)
            dxa.append(dsk_ref[k][0:1] * dyv + dxdt * dtp)
            dd_ref[k, 0:1, :] += jnp.sum(dyv * xs, axis=0, keepdims=True)

        dxa.append(dbg + _dot(dcbt.astype(BF16), cb_))
        dxa.append(dcg + _dot(dcb.astype(BF16), bb))
        dac = _sel_left(triu_ref[...], _split3(dacs + dacst.T))
        ddtr = jnp.where(valid, (ddt + dac * a) * _sigmoid(pre), 0.0)
        ddtr_ref[0] = ddtr.astype(BF16)
        dsm_ref[0, 0:1, :] += jnp.sum(ddtr, axis=0, keepdims=True)
        dsm_ref[0, 1:2, :] += jnp.sum(dac * dt, axis=0, keepdims=True) * a

        for b in range(6):
            xrb, cw, p8 = blk(b)
            sh = [_shift_down(xrb, j, p8) for j in range(SSD_CONV_W)]
            dxc = dxa[b] * dsil[b]
            acc = dcwx_ref.at[b] if b < 4 else (dcwb_ref.at[0] if b == 4 else dcwc_ref.at[0])
            acc[4:5, :] += jnp.sum(dxc, axis=0, keepdims=True)
            dxr = jnp.zeros_like(dxc)
            for j in range(SSD_CONV_W):
                acc[3 - j:4 - j, :] += jnp.sum(dxc * sh[j], axis=0, keepdims=True)
                dxr = dxr + cw[3 - j:4 - j] * _shift_up(dxc, j, head[b])
            head[b] = dxc[0:SUBLANES]
            out = dxs_ref.at[b] if b < 4 else (db_ref.at[0] if b == 4 else dc_ref.at[0])
            out[...] = dxr.astype(BF16)

    return pl.pallas_call(
        body, name="ssd_scan_bwd", grid=(SSD_N_GROUPS // gps, nc),
        in_specs=[xs_s, b_s, c_s, prev(None), prev(N_XS_BLK), prev(N_XS_BLK + SSD_N_GROUPS), dt_s, grp4, st_s,
                  cwx_s, cwb_s, cwc_s, grp_row, grp_row, cwx_s, _resident(tril.shape), _resident(triu.shape)],
        out_specs=[grp4, grp1, grp1, grp1, acc4, acc1, acc1, acc4, acc1],
        out_shape=[jax.ShapeDtypeStruct((N_XS_BLK, t, LANES), BF16),
                   jax.ShapeDtypeStruct((SSD_N_GROUPS, t, LANES), BF16),
                   jax.ShapeDtypeStruct((SSD_N_GROUPS, t, LANES), BF16),
                   jax.ShapeDtypeStruct((SSD_N_GROUPS, t, LANES), BF16),
                   jax.ShapeDtypeStruct((N_XS_BLK, SUBLANES, LANES), F32),
                   jax.ShapeDtypeStruct((SSD_N_GROUPS, SUBLANES, LANES), F32),
                   jax.ShapeDtypeStruct((SSD_N_GROUPS, SUBLANES, LANES), F32),
                   jax.ShapeDtypeStruct((N_XS_BLK, SUBLANES, LANES), F32),
                   jax.ShapeDtypeStruct((SSD_N_GROUPS, SUBLANES, LANES), F32)],
        scratch_shapes=[pltpu.VMEM((4 * gps, LANES, LANES), F32), pltpu.VMEM((6 * gps, SUBLANES, LANES), F32)],
        compiler_params=_params("arbitrary", "arbitrary"),
    )(xr, xr, xr, xr, xr, xr, dtr, dy, sprev, cwb, cwb, cwb, dtb, alog, dskip, tril, triu)


def _ssd_gate_fwd(x, y, z, gnw, wo):
    t, d = x.shape
    tm = min(TOKEN_TILE, t)
    nb = N_XS_BLK
    per = nb // SSD_N_GROUPS

    def body(x_ref, y_ref, z_ref, gnw_ref, wo_ref, o_ref, gn_ref):
        gs = []
        for j in range(nb):
            zv = z_ref[j]
            gs.append(y_ref[j] * (zv * _sigmoid(zv)))
        for q in range(SSD_N_GROUPS):
            ss = sum(jnp.sum(gs[j] * gs[j], axis=1, keepdims=True) for j in range(q * per, (q + 1) * per))
            inv = lax.rsqrt(ss / (per * LANES) + RMS_EPS)
            for j in range(q * per, (q + 1) * per):
                gn_ref[:, j * LANES:(j + 1) * LANES] = ((gs[j] * inv) * gnw_ref[j]).astype(BF16)
        o_ref[...] = x_ref[...] + _dot(gn_ref[...], wo_ref[...])

    blk = pl.BlockSpec((nb, tm, LANES), lambda i: (0, i, 0))
    return pl.pallas_call(
        body, name="ssd_gate_fwd", grid=(t // tm,),
        in_specs=[pl.BlockSpec((tm, d), lambda i: (i, 0)), blk, blk, _resident((nb, 1, LANES)),
                  _resident((SSD_D_INNER, d))],
        out_specs=[pl.BlockSpec((tm, d), lambda i: (i, 0)), pl.BlockSpec((tm, SSD_D_INNER), lambda i: (i, 0))],
        out_shape=[jax.ShapeDtypeStruct((t, d), F32), jax.ShapeDtypeStruct((t, SSD_D_INNER), BF16)],
        compiler_params=_params("parallel"),
    )(x, y, z, gnw, wo)


def _ssd_gate_bwd(dy, y, z, gnw, wo):
    t, d = dy.shape
    tm = min(TOKEN_TILE, t)
    nb = N_XS_BLK
    per = nb // SSD_N_GROUPS

    def body(dy_ref, y_ref, z_ref, gnw_ref, wo_ref, dys_ref, dz_ref, dgnw_ref):
        @pl.when(pl.program_id(0) == 0)
        def _():
            dgnw_ref[...] = jnp.zeros_like(dgnw_ref)

        dgn = _dot_nt(dy_ref[...].astype(BF16), wo_ref[...])
        for q in range(SSD_N_GROUPS):
            js = range(q * per, (q + 1) * per)
            gs, sil, dsil = {}, {}, {}
            for j in js:
                zv = z_ref[j]
                sig = _sigmoid(zv)
                sil[j] = zv * sig
                dsil[j] = sig * (1.0 + zv * (1.0 - sig))
                gs[j] = y_ref[j] * sil[j]
            ss = sum(jnp.sum(gs[j] * gs[j], axis=1, keepdims=True) for j in js)
            inv = lax.rsqrt(ss / (per * LANES) + RMS_EPS)
            ghat = {j: gs[j] * inv for j in js}
            dgh = {}
            for j in js:
                dj = dgn[:, j * LANES:(j + 1) * LANES]
                dgnw_ref[j] += jnp.sum(dj * ghat[j], axis=0, keepdims=True)
                dgh[j] = dj * gnw_ref[j]
            mean = sum(jnp.sum(dgh[j] * ghat[j], axis=1, keepdims=True) for j in js) / (per * LANES)
            for j in js:
                dg = inv * (dgh[j] - ghat[j] * mean)
                dys_ref[j] = dg * sil[j]
                dz_ref[j] = (dg * y_ref[j] * dsil[j]).astype(BF16)

    blk = pl.BlockSpec((nb, tm, LANES), lambda i: (0, i, 0))
    return pl.pallas_call(
        body, name="ssd_gate_bwd", grid=(t // tm,),
        in_specs=[pl.BlockSpec((tm, d), lambda i: (i, 0)), blk, blk, _resident((nb, 1, LANES)),
                  _resident((SSD_D_INNER, d))],
        out_specs=[blk, blk, pl.BlockSpec((nb, 1, LANES), lambda i: (0, 0, 0))],
        out_shape=[jax.ShapeDtypeStruct((nb, t, LANES), F32), jax.ShapeDtypeStruct((nb, t, LANES), BF16),
                   jax.ShapeDtypeStruct((nb, 1, LANES), F32)],
        compiler_params=_params("arbitrary"),
    )(dy, y, z, gnw, wo)


def _lane_blocks(v):
    r, n = v.shape[0], v.shape[1] // LANES
    return v.reshape(r, n, LANES).transpose(1, 0, 2)


def _ssd_prep(w_in, conv_w, conv_b, dt_bias, a_log, d_skip, norm_w):
    n_main = SSD_D_INNER + SSD_CONV_DIM
    w_dt = w_in[:, n_main:].reshape(-1, SSD_N_GROUPS, HEADS_PER_GROUP)
    w_dt = jnp.pad(w_dt, ((0, 0), (0, 0), (0, LANES - HEADS_PER_GROUP))).reshape(-1, SSD_N_GROUPS * LANES)
    w_in_pad = jnp.concatenate([w_in[:, :n_main], w_dt], axis=1)
    taps = jnp.concatenate([conv_w, conv_b[None], jnp.zeros((SUBLANES - SSD_CONV_W - 1, SSD_CONV_DIM), F32)], axis=0)
    cwb = _lane_blocks(taps)
    row = lambda v: jnp.pad(v.reshape(SSD_N_GROUPS, 1, HEADS_PER_GROUP), ((0, 0), (0, 0), (0, LANES - HEADS_PER_GROUP)))
    dskip = jnp.broadcast_to(jnp.repeat(d_skip, SSD_D_INNER // SSD_N_HEADS).reshape(N_XS_BLK, 1, LANES),
                             (N_XS_BLK, SUBLANES, LANES))
    gnw = norm_w.reshape(N_XS_BLK, 1, LANES)
    return w_in_pad, cwb, row(dt_bias), row(a_log), dskip, gnw


def _ssd_layer_fwd(x, nw, prm, wo, consts):
    w_in_pad, cwb, dtb, alog, dskip, gnw = prm
    z, xr, dtr = _ssd_inproj(x, nw, w_in_pad)
    y, sprev = _ssd_scan_fwd(xr, dtr, cwb, dtb, alog, dskip, consts)
    out, gn = _ssd_gate_fwd(x, y, z, gnw, wo)
    return out, (z, xr, dtr, y, sprev, gn)


def _ssd_layer_bwd(x, dy, nw, prm, wo, consts, saved):
    w_in_pad, cwb, dtb, alog, dskip, gnw = prm
    z, xr, dtr, y, sprev, gn = saved
    dys, dz, dgnw = _ssd_gate_bwd(dy, y, z, gnw, wo)
    dwo = _matmul_tn(gn, dy, name="wgrad_ssd_out")
    dxs, db, dc, ddtr, dcwx, dcwb, dcwc, dd, dsm = _ssd_scan_bwd(xr, dtr, dys, sprev, cwb, dtb, alog, dskip, consts)
    pieces = [dz, dxs, db, dc, ddtr]
    dx, dnw, h = _inproj_bwd(x, dy, nw, [w_in_pad], [pieces])
    dws = [_matmul_tn_blocked(h, p, name=f"wgrad_ssd_in{i}") for i, p in enumerate(pieces)]
    dw_dt = dws[4].reshape(-1, SSD_N_GROUPS, LANES)[:, :, :HEADS_PER_GROUP].reshape(-1, SSD_N_HEADS)
    dw_in = jnp.concatenate(dws[:4] + [dw_dt], axis=1)
    dtaps = jnp.concatenate([dcwx, dcwb, dcwc], axis=0).transpose(1, 0, 2).reshape(SUBLANES, SSD_CONV_DIM)
    by_head = lambda r: dsm[:, r, :HEADS_PER_GROUP].reshape(SSD_N_HEADS)
    d_d = jnp.sum(dd[:, 0, :].reshape(SSD_N_HEADS, SSD_D_INNER // SSD_N_HEADS), axis=1)
    return dx, (dnw, dw_in, dtaps[:SSD_CONV_W], dtaps[SSD_CONV_W], by_head(0), by_head(1),
                d_d, dgnw.reshape(SSD_D_INNER), dwo)


def _loss_head(x, fw, target):
    t, d = x.shape
    tm = min(TOKEN_TILE, t)

    def body(x_ref, fw_ref, tgt_ref, loss_ref, dx_ref, dfw_ref):
        fwv = fw_ref[...]
        y, xhat, inv = _rms_fwd(x_ref[...], fwv)
        err = y - tgt_ref[...]
        tot = jnp.sum(jnp.sum(err * err, axis=1, keepdims=True), axis=0, keepdims=True)
        dx, dw = _rms_bwd(err * (1.0 / d), xhat, inv, fwv)
        dx_ref[...] = dx

        @pl.when(pl.program_id(0) == 0)
        def _():
            loss_ref[...] = jnp.zeros_like(loss_ref)
            dfw_ref[...] = jnp.zeros_like(dfw_ref)

        loss_ref[...] += jnp.broadcast_to(tot * (0.5 / d), loss_ref.shape)
        dfw_ref[...] += dw

    tok = pl.BlockSpec((tm, d), lambda i: (i, 0))
    return pl.pallas_call(
        body, name="loss_head", grid=(t // tm,),
        in_specs=[tok, _resident((1, d)), tok],
        out_specs=[pl.BlockSpec((1, LANES), lambda i: (0, 0)), tok, pl.BlockSpec((1, d), lambda i: (0, 0))],
        out_shape=[jax.ShapeDtypeStruct((1, LANES), F32), jax.ShapeDtypeStruct((t, d), F32),
                   jax.ShapeDtypeStruct((1, d), F32)],
        compiler_params=_params("arbitrary"),
    )(x, fw, target)


def _row_tile(rows, cap):
    best = SUBLANES
    for r in range(SUBLANES, min(rows, cap) + 1, SUBLANES):
        if rows % r == 0:
            best = r
    return best


def _adamw(w, g, m, v, name):
    rows, cols = w.shape
    br = _row_tile(rows, 256)
    c1 = 1.0 - ADAM_B1 ** ADAM_STEP
    c2 = 1.0 - ADAM_B2 ** ADAM_STEP

    def body(w_ref, g_ref, m_ref, v_ref, d_ref, nm_ref, nv_ref):
        gv = g_ref[...]
        nm = ADAM_B1 * m_ref[...] + (1.0 - ADAM_B1) * gv
        nv = ADAM_B2 * v_ref[...] + (1.0 - ADAM_B2) * (gv * gv)
        nm_ref[...] = nm
        nv_ref[...] = nv
        d_ref[...] = -ADAM_LR * ((nm / c1) / (jnp.sqrt(nv / c2) + ADAM_EPS) + ADAM_WD * w_ref[...])

    blk = pl.BlockSpec((br, cols), lambda i: (i, 0))
    shp = jax.ShapeDtypeStruct((rows, cols), F32)
    return pl.pallas_call(
        body, name=name, grid=(rows // br,), in_specs=[blk] * 4, out_specs=[blk] * 3, out_shape=[shp] * 3,
        compiler_params=_params("parallel"),
    )(w, g, m, v)


def _place():
    x, y, c = lax.axis_index("x"), lax.axis_index("y"), lax.axis_index("c")
    return x, y, c, [(1 - x, y), (x, 1 - y), (1 - x, 1 - y)]


def _remote(src, dst, send_sems, recv_sems, k, to):
    return pltpu.make_async_remote_copy(src_ref=src, dst_ref=dst, send_sem=send_sems.at[k], recv_sem=recv_sems.at[k],
                                        device_id=to, device_id_type=MESH)


def _scatter_copies(srcs, dsts, send_sems, recv_sems):
    x, y, c, chips = _place()
    sends, arrivals = [], []
    for oi, (src, dst) in enumerate(zip(srcs, dsts)):
        for j, chip in enumerate(chips):
            sends.append(_remote(src.at[2 * chip[0] + chip[1]], dst.at[j], send_sems, recv_sems, 3 * oi + j, (*chip, c)))
            arrivals.append(_remote(dst.at[j], dst.at[j], send_sems, recv_sems, 3 * oi + j, (*chip, c)))
    return sends, arrivals


def _swap_copies(srcs, dsts, send_sems, recv_sems):
    x, y, c, _ = _place()
    return [_remote(src.at[:, 1 - c], dst, send_sems, recv_sems, oi, (x, y, 1 - c))
            for oi, (src, dst) in enumerate(zip(srcs, dsts))]


def _scatter_start(srcs, dsts, send_sems, recv_sems):
    for cp in _scatter_copies(srcs, dsts, send_sems, recv_sems)[0]:
        cp.start()


def _scatter_wait(srcs, dsts, send_sems, recv_sems):
    sends, arrivals = _scatter_copies(srcs, dsts, send_sems, recv_sems)
    for cp in arrivals:
        cp.wait_recv()
    for cp in sends:
        cp.wait_send()


def _all_gather_shards(arrs):
    n = len(arrs)

    def body(*refs):
        srcs, dsts = refs[:n], refs[n:2 * n]
        send_sems, recv_sems = refs[2 * n:]
        x, y, c, chips = _place()
        me = 2 * x + y
        sibling = (x, y, 1 - c)
        sent = []
        for oi, (src, dst) in enumerate(zip(srcs, dsts)):
            for j, chip in enumerate(chips):
                sent.append(_remote(src.at[c], dst.at[me, c], send_sems, recv_sems, 6 * oi + j, (*chip, c)))
                sent[-1].start()
        for oi, dst in enumerate(dsts):
            for j, chip in enumerate(chips):
                landed = dst.at[2 * chip[0] + chip[1], c]
                _remote(landed, landed, send_sems, recv_sems, 6 * oi + j, (*chip, c)).wait_recv()
                sent.append(_remote(landed, landed, send_sems, recv_sems, 6 * oi + 3 + j, sibling))
                sent[-1].start()
        for oi, dst in enumerate(dsts):
            for j, chip in enumerate(chips):
                landed = dst.at[2 * chip[0] + chip[1], 1 - c]
                _remote(landed, landed, send_sems, recv_sems, 6 * oi + 3 + j, sibling).wait_recv()
        for cp in sent:
            cp.wait_send()

    return pl.pallas_call(
        body, name="all_gather_shards",
        in_specs=[_HBM] * n, out_specs=[_HBM] * n,
        out_shape=[jax.ShapeDtypeStruct((N_SHARDS,) + a.shape, a.dtype) for a in arrs],
        scratch_shapes=[pltpu.SemaphoreType.DMA((6 * n,)), pltpu.SemaphoreType.DMA((6 * n,))],
    )(*arrs)


def _swap_halves(arrs):
    n = len(arrs)

    def body(*refs):
        srcs, dsts = refs[:n], refs[n:2 * n]
        cps = _swap_copies(srcs, dsts, *refs[2 * n:])
        for cp in cps:
            cp.start()
        for cp in cps:
            cp.wait()

    return pl.pallas_call(
        body, name="swap_halves", in_specs=[_HBM] * n, out_specs=[_HBM] * n,
        out_shape=[jax.ShapeDtypeStruct((a.shape[0],) + a.shape[2:], a.dtype) for a in arrs],
        scratch_shapes=[pltpu.SemaphoreType.DMA((n,)), pltpu.SemaphoreType.DMA((n,))],
    )(*arrs)


def _scatter_to_chips(arrs):
    n = len(arrs)

    def body(*refs):
        srcs, dsts = refs[:n], refs[n:2 * n]
        _scatter_start(srcs, dsts, *refs[2 * n:])
        _scatter_wait(srcs, dsts, *refs[2 * n:])

    return pl.pallas_call(
        body, name="scatter_to_chips", in_specs=[_HBM] * n, out_specs=[_HBM] * n,
        out_shape=[jax.ShapeDtypeStruct((3,) + a.shape[1:], a.dtype) for a in arrs],
        scratch_shapes=[pltpu.SemaphoreType.DMA((3 * n,)), pltpu.SemaphoreType.DMA((3 * n,))],
    )(*arrs)


def _join_halves(arrs):
    n = len(arrs)

    def body(*refs):
        bufs = refs[n:2 * n]
        send_sems, recv_sems = refs[2 * n:]
        x, y, c, _ = _place()
        sibling = (x, y, 1 - c)
        sent = [_remote(buf.at[c], buf.at[c], send_sems, recv_sems, oi, sibling) for oi, buf in enumerate(bufs)]
        for cp in sent:
            cp.start()
        for oi, buf in enumerate(bufs):
            _remote(buf.at[1 - c], buf.at[1 - c], send_sems, recv_sems, oi, sibling).wait_recv()
        for cp in sent:
            cp.wait_send()

    return pl.pallas_call(
        body, name="join_halves", in_specs=[_HBM] * n, out_specs=[_HBM] * n,
        out_shape=[jax.ShapeDtypeStruct(a.shape, a.dtype) for a in arrs],
        input_output_aliases={i: i for i in range(n)},
        scratch_shapes=[pltpu.SemaphoreType.DMA((n,)), pltpu.SemaphoreType.DMA((n,))],
    )(*arrs)


def _add_halves(full, recv, place):
    n, _, rows, cols = full.shape
    br = _row_tile(rows, 512)

    def body(p_ref, a_ref, b_ref, o_ref):
        o_ref[...] = (a_ref[...] + b_ref[...]).astype(BF16)

    grid_spec = pltpu.PrefetchScalarGridSpec(
        num_scalar_prefetch=1, grid=(n, rows // br),
        in_specs=[pl.BlockSpec((None, None, br, cols), lambda s, i, p_ref: (s, p_ref[1], i, 0)),
                  pl.BlockSpec((None, br, cols), lambda s, i, p_ref: (s, i, 0))],
        out_specs=pl.BlockSpec((None, br, cols), lambda s, i, p_ref: (s, i, 0)))
    return pl.pallas_call(
        body, name="add_halves", grid_spec=grid_spec, out_shape=jax.ShapeDtypeStruct((n, rows, cols), BF16),
        compiler_params=_params("parallel", "parallel"),
    )(place, full, recv)


def _sum_chips(mine, others, place):
    _, rows, cols = mine.shape
    br = _row_tile(rows, 512)
    slot_of_flip = {2: 0, 1: 1, 3: 2}

    def body(p_ref, m_ref, o_ref, out_ref):
        me = p_ref[0]
        own = m_ref[...].astype(F32)
        got = [o_ref[j].astype(F32) for j in range(3)]
        acc = None
        for s in range(N_SHARDS):
            flip = jnp.bitwise_xor(me, s)
            term = own
            for f, j in slot_of_flip.items():
                term = jnp.where(flip == f, got[j], term)
            acc = term if acc is None else acc + term
        out_ref[...] = acc

    grid_spec = pltpu.PrefetchScalarGridSpec(
        num_scalar_prefetch=1, grid=(rows // br,),
        in_specs=[pl.BlockSpec((None, br, cols), lambda i, p_ref: (p_ref[0], i, 0)),
                  pl.BlockSpec((3, br, cols), lambda i, p_ref: (0, i, 0))],
        out_specs=pl.BlockSpec((None, br, cols), lambda i, p_ref: (p_ref[1], i, 0)))
    return pl.pallas_call(
        body, name="sum_chips", grid_spec=grid_spec, out_shape=jax.ShapeDtypeStruct((2, rows, cols), F32),
        compiler_params=_params("parallel"),
    )(place, mine, others)


WEIGHTS = ("norm_w", "ffn_w_gate", "ffn_w_up", "ffn_w_down", "ssd_w_in", "ssd_conv_w", "ssd_conv_b", "ssd_dt_bias",
           "ssd_a_log", "ssd_d", "ssd_norm_w", "ssd_w_out", "sc_w_in", "sc_conv_w", "sc_w_out", "final_norm_w")
BIG = (("ffn_w_gate", 3), ("ffn_w_up", 3), ("ffn_w_down", 2), ("ssd_w_in", 2), ("ssd_w_out", 1), ("sc_w_in", 2),
       ("sc_w_out", 1))
SMALL_SHARDED = (("norm_w", 2), ("ssd_conv_w", 2), ("sc_conv_w", 2))
REPLICATED = ("ssd_conv_b", "ssd_dt_bias", "ssd_a_log", "ssd_d", "ssd_norm_w", "final_norm_w")
FLAT_COLS = 1024


def _pack(arrays, row_multiple, lead=()):
    flat = jnp.concatenate([a.reshape(lead + (-1,)) for a in arrays], axis=len(lead))
    unit = row_multiple * FLAT_COLS
    n = flat.shape[-1]
    pad = (-n) % unit
    if pad:
        flat = jnp.pad(flat, [(0, 0)] * len(lead) + [(0, pad)])
    return flat.reshape(lead + (-1, FLAT_COLS))


def _unpack(flat, shapes, lead=()):
    flat = flat.reshape(lead + (-1,))
    out, off = [], 0
    for shp in shapes:
        n = 1
        for s in shp:
            n *= s
        out.append(flat[..., off:off + n].reshape(lead + tuple(shp)))
        off += n
    return out


def _to_shards(full, axis):
    shp = full.shape
    r = full.reshape(shp[:axis] + (N_SHARDS, shp[axis] // N_SHARDS) + shp[axis + 1:])
    return jnp.moveaxis(r, axis, 0)


def _from_shards(sh, axis):
    r = jnp.moveaxis(sh, 0, axis)
    shp = r.shape
    return r.reshape(shp[:axis] + (shp[axis] * shp[axis + 1],) + shp[axis + 2:])


def _layer_shards(wl, i):
    j = i // 2
    ffn = lambda k: [(("ffn_w_gate", k), wl["ffn_w_gate"][i, k], 1), (("ffn_w_up", k), wl["ffn_w_up"][i, k], 1),
                     (("ffn_w_down", k), wl["ffn_w_down"][i, k], 0)]
    mix = "ssd" if i % 2 == 0 else "sc"
    return ffn(0), [((mix + "_w_in",), wl[mix + "_w_in"][j], 1), ((mix + "_w_out",), wl[mix + "_w_out"][j], 0)] + ffn(1)


def _assemble(group, received, chip):
    return {key: _from_shards(lax.dynamic_update_index_in_dim(r, own, chip, 0), axis)
            for (key, own, axis), r in zip(group, received)}


def _forward_backward(x, target, p, wl, layer0, place):
    chip = place[0]
    consts = _ssd_consts()
    nw = p["norm_w"]
    row = lambda v: v[None]
    full = {0: layer0}
    ffn = lambda i, k: (full[i]["ffn_w_gate", k], full[i]["ffn_w_up", k], full[i]["ffn_w_down", k])
    ssd_prm, sc_cw = {}, {}

    xin, saved, pre = [], [], {}
    for i in range(N_LAYERS):
        j = i // 2
        first, second = _layer_shards(wl, i + 1) if i + 1 < N_LAYERS else ([], [])
        late = _layer_shards(wl, 0)[1] if i == 0 else []
        xin.append(x)
        x, *rest = _ffn_fwd(x, row(nw[i, 0]), *ffn(i, 0), carry=[s[1] for s in late + first])
        pre[i, 0] = rest[:3]
        if late:
            full[0].update(_assemble(late, rest[3:3 + len(late)], chip))
        if first:
            full[i + 1] = _assemble(first, rest[3 + len(late):], chip)
        xin.append(x)
        if i % 2 == 0:
            ssd_prm[j] = _ssd_prep(full[i]["ssd_w_in",], p["ssd_conv_w"][j], p["ssd_conv_b"][j], p["ssd_dt_bias"][j],
                                   p["ssd_a_log"][j], p["ssd_d"][j], p["ssd_norm_w"][j])
            x, sv = _ssd_layer_fwd(x, row(nw[i, 1]), ssd_prm[j], full[i]["ssd_w_out",], consts)
        else:
            sc_cw[j] = jnp.pad(p["sc_conv_w"][j], ((0, SUBLANES - SC_CONV_W), (0, 0)))
            sv = _norm_mm(x, row(nw[i, 1]), full[i]["sc_w_in",])
            x = _sc_fwd(x, sv, sc_cw[j], full[i]["sc_w_out",])
        saved.append(sv)
        xin.append(x)
        x, *rest = _ffn_fwd(x, row(nw[i, 2]), *ffn(i, 1), carry=[s[1] for s in second])
        pre[i, 1] = rest[:3]
        if second:
            full[i + 1].update(_assemble(second, rest[3:], chip))
    loss, dx, dfw = _loss_head(x, row(p["final_norm_w"]), target)

    g_nw = [[None] * 3 for _ in range(N_LAYERS)]
    g_ffn = {}
    g_ssd = [None, None]
    g_sc = [None, None]
    halved = lambda a: a.reshape((N_SHARDS, 2, -1, a.shape[-1]))

    def ffn_bwd(i, k, slot, dy, parts=(), sums=()):
        wg, wu, wd = ffn(i, k)
        a, s, p_ = pre[i, k]
        riders = {}
        if parts:
            dxn, dnw, h, dg, du, *theirs = _ffn_bwd_fused(xin[3 * i + slot], dy, s, p_, row(nw[i, slot]), wg, wu, wd,
                                                          swap=parts)
            sums = [_add_halves(mine, got, place) for mine, got in zip(parts, theirs)]
            riders = {"ffn_w_gate": sums[0:1], "ffn_w_up": sums[1:2]}
            arrived = []
        else:
            dxn, dnw, h, dg, du, *arrived = _ffn_bwd_fused(xin[3 * i + slot], dy, s, p_, row(nw[i, slot]), wg, wu, wd,
                                                           carry=sums)
        g_nw[i][slot] = dnw[0]
        for n, lhs, rhs, scale in (("ffn_w_gate", h, dg, 1.0), ("ffn_w_up", h, du, 1.0), ("ffn_w_down", a, dy, 0.5)):
            out = _matmul_tn(lhs, rhs, scale=scale, name="wgrad_" + n, slab=(k,), stack=(2,), buf=g_ffn.get((n, i)),
                             carry=riders.get(n, ()))
            if n in riders:
                out, *landed = out
                arrived += landed
            g_ffn[n, i] = out
        return dxn, arrived, sums

    def finish(sums, arrived):
        return _join_halves([_sum_chips(mine, others, place) for mine, others in zip(sums, arrived)])

    reduced = {}
    waiting = None
    for i in reversed(range(N_LAYERS)):
        j = i // 2
        dx, arrived, sums = ffn_bwd(i, 1, 2, dx, parts=waiting[1] if waiting else ())
        xm = xin[3 * i + 1]
        if i % 2 == 0:
            dx, gs = _ssd_layer_bwd(xm, dx, row(nw[i, 1]), ssd_prm[j], full[i]["ssd_w_out",], consts, saved[i])
            g_nw[i][1] = gs[0][0]
            g_ssd[j] = gs[1:]
            mixer = [(("ssd_w_in", j), _to_shards(gs[1], 1)), (("ssd_w_out", j), _to_shards(gs[-1], 0))]
        else:
            bcu = saved[i]
            dbcu, pin, dcw = _sc_bwd(dx, bcu, sc_cw[j], full[i]["sc_w_out",])
            dwo = _matmul_tn(pin, dx, name="wgrad_sc_out")
            dx, dnw, h = _inproj_bwd(xm, dx, row(nw[i, 1]), [full[i]["sc_w_in",]], [[dbcu]])
            g_nw[i][1] = dnw[0]
            dwi = _matmul_tn(h, dbcu, name="wgrad_sc_in")
            g_sc[j] = dcw[:SC_CONV_W]
            mixer = [(("sc_w_in", j), _to_shards(dwi, 1)), (("sc_w_out", j), _to_shards(dwo, 0))]
        dx, more, _ = ffn_bwd(i, 0, 0, dx, sums=sums[2:])
        if waiting:
            reduced.update(zip(waiting[0], finish(sums, arrived + more)))
        mine = [(("ffn_w_gate", i), _to_shards(g_ffn["ffn_w_gate", i], 2)), (("ffn_w_up", i), _to_shards(g_ffn["ffn_w_up", i], 2)),
                (("ffn_w_down", i), _to_shards(g_ffn["ffn_w_down", i], 1))] + mixer
        waiting = ([key for key, _ in mine], [halved(v) for _, v in mine])

    g = {"norm_w": jnp.stack([jnp.stack(r) for r in g_nw]), "final_norm_w": dfw[0]}
    for k, n in enumerate(("ssd_conv_w", "ssd_conv_b", "ssd_dt_bias", "ssd_a_log", "ssd_d", "ssd_norm_w")):
        g[n] = jnp.stack([g_ssd[0][k + 1], g_ssd[1][k + 1]])
    g["sc_conv_w"] = jnp.stack(g_sc)
    small_part = _pack([_to_shards(g[n], ax) for n, ax in SMALL_SHARDED]
                       + [jnp.broadcast_to(g[n][None], (N_SHARDS,) + g[n].shape) for n in REPLICATED],
                       4 * SUBLANES, lead=(N_SHARDS,))
    parts = waiting[1] + [halved(small_part)]
    last = [_add_halves(mine, got, place) for mine, got in zip(parts, _swap_halves(parts))]
    out = finish(last, _scatter_to_chips(last))
    reduced.update(zip(waiting[0], out[:-1]))
    return loss, dx, reduced, out[-1]


def kernel(x, norm_w, ffn_w_gate, ffn_w_up, ffn_w_down, ssd_w_in, ssd_conv_w, ssd_conv_b, ssd_dt_bias, ssd_a_log, ssd_d, ssd_norm_w, ssd_w_out, sc_w_in, sc_conv_w, sc_w_out, final_norm_w, loss_target, m_norm_w, m_ffn_w_gate, m_ffn_w_up, m_ffn_w_down, m_ssd_w_in, m_ssd_conv_w, m_ssd_conv_b, m_ssd_dt_bias, m_ssd_a_log, m_ssd_d, m_ssd_norm_w, m_ssd_w_out, m_sc_w_in, m_sc_conv_w, m_sc_w_out, m_final_norm_w, v_norm_w, v_ffn_w_gate, v_ffn_w_up, v_ffn_w_down, v_ssd_w_in, v_ssd_conv_w, v_ssd_conv_b, v_ssd_dt_bias, v_ssd_a_log, v_ssd_d, v_ssd_norm_w, v_ssd_w_out, v_sc_w_in, v_sc_conv_w, v_sc_w_out, v_final_norm_w):
    w = dict(zip(WEIGHTS, (norm_w, ffn_w_gate, ffn_w_up, ffn_w_down, ssd_w_in, ssd_conv_w, ssd_conv_b, ssd_dt_bias,
                           ssd_a_log, ssd_d, ssd_norm_w, ssd_w_out, sc_w_in, sc_conv_w, sc_w_out, final_norm_w)))
    m = dict(zip(WEIGHTS, (m_norm_w, m_ffn_w_gate, m_ffn_w_up, m_ffn_w_down, m_ssd_w_in, m_ssd_conv_w, m_ssd_conv_b,
                           m_ssd_dt_bias, m_ssd_a_log, m_ssd_d, m_ssd_norm_w, m_ssd_w_out, m_sc_w_in, m_sc_conv_w,
                           m_sc_w_out, m_final_norm_w)))
    v = dict(zip(WEIGHTS, (v_norm_w, v_ffn_w_gate, v_ffn_w_up, v_ffn_w_down, v_ssd_w_in, v_ssd_conv_w, v_ssd_conv_b,
                           v_ssd_dt_bias, v_ssd_a_log, v_ssd_d, v_ssd_norm_w, v_ssd_w_out, v_sc_w_in, v_sc_conv_w,
                           v_sc_w_out, v_final_norm_w)))
    chip = 2 * lax.axis_index("x") + lax.axis_index("y")
    place = jnp.stack([chip, lax.axis_index("c")]).astype(jnp.int32)
    big_names = [n for n, _ in BIG]
    small_names = [n for n, _ in SMALL_SHARDED] + list(REPLICATED)
    halved = lambda a, lead=(): a.reshape(lead + (2, -1, a.shape[-1]))

    wl = {n: w[n].astype(BF16) for n in big_names}
    first, _ = _layer_shards(wl, 0)
    small = halved(_pack([w[n] for n, _ in SMALL_SHARDED], 2 * SUBLANES))
    received = _all_gather_shards([halved(s[1]) for s in first] + [small])
    layer0 = _assemble(first, [r.reshape((N_SHARDS,) + s[1].shape) for r, s in zip(received, first)], chip)
    p = {n: w[n] for n in REPLICATED}
    small_full = lax.dynamic_update_index_in_dim(received[-1], small, chip, 0)
    for (n, ax), sh in zip(SMALL_SHARDED, _unpack(small_full, [w[n].shape for n, _ in SMALL_SHARDED], lead=(N_SHARDS,))):
        p[n] = _from_shards(sh, ax)

    t, d = x.shape[-2:]
    loss, dx, reduced, g_small = _forward_backward(x.reshape(t, d), loss_target.reshape(t, d), p, wl, layer0, place)

    grad = {}
    for n in big_names:
        per_layer = w[n].shape[0]
        grad[n] = jnp.stack([reduced[n, i].reshape(w[n].shape[1:]) for i in range(per_layer)])
    g_small = g_small.reshape(-1, FLAT_COLS)
    grad.update(zip(small_names, _unpack(g_small, [w[n].shape for n in small_names])))

    delta, new_m, new_v = {}, {}, {}
    for n in big_names:
        shp = w[n].shape
        as2d = lambda a: a.reshape(-1, shp[-1])
        out = _adamw(as2d(w[n]), as2d(grad[n]), as2d(m[n]), as2d(v[n]), name="adamw_" + n)
        delta[n], new_m[n], new_v[n] = (o.reshape(shp) for o in out)
    packed = [_pack([s[n] for n in small_names], 4 * SUBLANES) for s in (w, m, v)]
    out = _adamw(packed[0], g_small, packed[1], packed[2], name="adamw_small")
    shapes = [w[n].shape for n in small_names]
    for dst, o in zip((delta, new_m, new_v), out):
        dst.update(zip(small_names, _unpack(o, shapes)))

    loss = lax.psum(loss[0, 0], ("x", "y", "c"))
    return (loss, dx.reshape(x.shape), *[grad[n] for n in WEIGHTS], *[delta[n] for n in WEIGHTS],
            *[new_m[n] for n in WEIGHTS], *[new_v[n] for n in WEIGHTS])
```
